```python
import jax, jax.numpy as jnp
from jax import lax
import numpy as np

D_MODEL = 2048
BATCH = 8
SEQ = 8192
DEPTH = 1

N_MEM = 256
EPS = 1e-5
D_MIX = D_MODEL
D_A = D_MIX // 2
D_B = D_MIX - D_A
CHUNK = 128
A_GROUPS = 8
A_GROUP_DIM = D_A // A_GROUPS
HEAD_DIM = 64
B_Q_HEADS = D_B // HEAD_DIM
B_KV_HEADS = 2
B_GROUP = B_Q_HEADS // B_KV_HEADS
WINDOW = 128
BLOCK = 128
X_HEADS = 4
X_HEAD_DIM = D_MODEL // X_HEADS
D_FF = 5632
IN_COLS = 2 * D_A + D_B + 2 * B_KV_HEADS * HEAD_DIM
NEG = -1e30

kernel_name = "hybrid_sgu_swa_sink_macaron_layer"


def rmsnorm(x, g):
    xf = x.astype(jnp.float32)
    y = xf * lax.rsqrt(jnp.mean(xf * xf, axis=-1, keepdims=True) + EPS)
    return (y * g.astype(jnp.float32)).astype(x.dtype)


def swiglu(x, w_gate, w_up, w_down):
    return (jax.nn.silu(x @ w_gate) * (x @ w_up)) @ w_down


def spatial_gating(z_uv, g_v, w_s, b_s):
    b, s, _ = z_uv.shape
    z = jax.nn.gelu(z_uv, approximate=False)
    u, v = z[..., :D_A], z[..., D_A:]
    v = rmsnorm(v, g_v)
    v = v.reshape(b, s // CHUNK, CHUNK, A_GROUPS, A_GROUP_DIM)
    causal = jnp.tril(jnp.ones((CHUNK, CHUNK), dtype=bool))
    ws = jnp.where(causal[None], w_s, jnp.zeros_like(w_s))
    sv = jnp.einsum('gts,bnsgc->bntgc', ws, v) + jnp.transpose(b_s)[None, None, :, :, None]
    return u * sv.reshape(b, s, D_A)


def window_attention_sinks(q, k, v, sinks):
    b, s, _ = q.shape
    nb = s // BLOCK
    q = q.reshape(b, nb, BLOCK, B_KV_HEADS, B_GROUP, HEAD_DIM)
    k = k.reshape(b, nb, BLOCK, B_KV_HEADS, HEAD_DIM)
    v = v.reshape(b, nb, BLOCK, B_KV_HEADS, HEAD_DIM)
    pad = ((0, 0), (1, 0), (0, 0), (0, 0), (0, 0))
    kk = jnp.concatenate([jnp.pad(k, pad)[:, :-1], k], axis=2)
    vv = jnp.concatenate([jnp.pad(v, pad)[:, :-1], v], axis=2)
    scores = jnp.einsum('bnqkgd,bnskd->bnkgqs', q, kk,
                        preferred_element_type=jnp.float32) * (HEAD_DIM ** -0.5)
    qpos = jnp.arange(BLOCK)[:, None] + BLOCK
    kpos = jnp.arange(2 * BLOCK)[None, :]
    diff = qpos - kpos
    band = (diff >= 0) & (diff < WINDOW)
    first = (jnp.arange(nb)[:, None, None] == 0) & (kpos[None] < BLOCK)
    mask = band[None] & ~first
    scores = jnp.where(mask[None, :, None, None], scores, NEG)
    sink = jnp.broadcast_to(
        sinks.astype(jnp.float32).reshape(1, 1, B_KV_HEADS, B_GROUP, 1, 1),
        scores.shape[:-1] + (1,))
    probs = jax.nn.softmax(jnp.concatenate([scores, sink], axis=-1), axis=-1)[..., :-1]
    out = jnp.einsum('bnkgqs,bnskd->bnqkgd', probs.astype(vv.dtype), vv)
    return out.reshape(b, s, D_B)


def cross_attention(hn, memn, w_q, w_kv, w_o):
    b, s, _ = hn.shape
    m = memn.shape[1]
    q = (hn @ w_q).reshape(b, s, X_HEADS, X_HEAD_DIM)
    kv = memn @ w_kv
    k = kv[..., :D_MODEL].reshape(b, m, X_HEADS, X_HEAD_DIM)
    v = kv[..., D_MODEL:].reshape(b, m, X_HEADS, X_HEAD_DIM)
    scores = jnp.einsum('bshd,bmhd->bhsm', q, k,
                        preferred_element_type=jnp.float32) * (X_HEAD_DIM ** -0.5)
    probs = jax.nn.softmax(scores, axis=-1).astype(v.dtype)
    out = jnp.einsum('bhsm,bmhd->bshd', probs, v).reshape(b, s, D_MODEL)
    return out @ w_o


def _fwd_setup_inputs(seed: int = 0) -> dict:
    key = jax.random.key(seed)
    ks = jax.random.split(key, 32)
    L, D, F = DEPTH, D_MODEL, D_FF

    def nrm(k, shape, scale):
        return jax.random.normal(k, shape, dtype=jnp.float32) * scale

    def gain(k, shape):
        return 1.0 + 0.05 * jax.random.normal(k, shape, dtype=jnp.float32)

    return {
        "x": nrm(ks[0], (BATCH, SEQ, D), 1.0),
        "mem": nrm(ks[1], (BATCH, N_MEM, D), 1.0),
        "g_ffn1": gain(ks[2], (L, D)),
        "w1_gate": nrm(ks[3], (L, D, F), D ** -0.5),
        "w1_up": nrm(ks[4], (L, D, F), D ** -0.5),
        "w1_down": nrm(ks[5], (L, F, D), F ** -0.5),
        "g_mix": gain(ks[6], (L, D)),
        "w_in": nrm(ks[7], (L, D, IN_COLS), D ** -0.5),
        "g_v": gain(ks[8], (L, D_A)),
        "w_s": nrm(ks[9], (L, A_GROUPS, CHUNK, CHUNK), 0.5 * CHUNK ** -0.5),
        "b_s": 1.0 + 0.1 * jax.random.normal(ks[10], (L, A_GROUPS, CHUNK), dtype=jnp.float32),
        "sinks": nrm(ks[11], (L, B_Q_HEADS), 0.5),
        "g_a_out": gain(ks[12], (L, D_A)),
        "g_b_out": gain(ks[13], (L, D_B)),
        "w_out": nrm(ks[14], (L, D_MIX, D), D_MIX ** -0.5),
        "g_x": gain(ks[15], (L, D)),
        "g_mem": gain(ks[16], (L, D)),
        "w_xq": nrm(ks[17], (L, D, D), D ** -0.5),
        "w_xkv": nrm(ks[18], (L, D, 2 * D), D ** -0.5),
        "w_xo": nrm(ks[19], (L, D, D), D ** -0.5),
        "g_ffn2": gain(ks[20], (L, D)),
        "w2_gate": nrm(ks[21], (L, D, F), D ** -0.5),
        "w2_up": nrm(ks[22], (L, D, F), D ** -0.5),
        "w2_down": nrm(ks[23], (L, F, D), F ** -0.5),
        "g_final": gain(ks[24], (D,)),
    }


def _fwd_reference(x, mem, g_ffn1, w1_gate, w1_up, w1_down, g_mix, w_in, g_v, w_s, b_s,
              sinks, g_a_out, g_b_out, w_out, g_x, g_mem, w_xq, w_xkv, w_xo,
              g_ffn2, w2_gate, w2_up, w2_down, g_final):
    o_q = 2 * D_A
    o_k = o_q + D_B
    o_v = o_k + B_KV_HEADS * HEAD_DIM
    h = x
    for l in range(DEPTH):
        h = h + 0.5 * swiglu(rmsnorm(h, g_ffn1[l]), w1_gate[l], w1_up[l], w1_down[l])
        z = rmsnorm(h, g_mix[l]) @ w_in[l]
        y_a = spatial_gating(z[..., :o_q], g_v[l], w_s[l], b_s[l])
        y_b = window_attention_sinks(z[..., o_q:o_k], z[..., o_k:o_v], z[..., o_v:], sinks[l])
        y = jnp.concatenate([rmsnorm(y_a, g_a_out[l]), rmsnorm(y_b, g_b_out[l])], axis=-1)
        h = h + y @ w_out[l]
        h = h + cross_attention(rmsnorm(h, g_x[l]), rmsnorm(mem, g_mem[l]),
                                w_xq[l], w_xkv[l], w_xo[l])
        h = h + 0.5 * swiglu(rmsnorm(h, g_ffn2[l]), w2_gate[l], w2_up[l], w2_down[l])
    return rmsnorm(h, g_final)


import jax as _jax
import jax.numpy as _jnp

TWIN_FORMAT = 'train_step'
FWD_PARAMS = ['x', 'mem', 'g_ffn1', 'w1_gate', 'w1_up', 'w1_down', 'g_mix', 'w_in', 'g_v', 'w_s', 'b_s', 'sinks', 'g_a_out', 'g_b_out', 'w_out', 'g_x', 'g_mem', 'w_xq', 'w_xkv', 'w_xo', 'g_ffn2', 'w2_gate', 'w2_up', 'w2_down', 'g_final']
TWIN_WEIGHTS = ['g_ffn1', 'w1_gate', 'w1_up', 'w1_down', 'g_mix', 'w_in', 'g_v', 'w_s', 'b_s', 'sinks', 'g_a_out', 'g_b_out', 'w_out', 'g_x', 'g_mem', 'w_xq', 'w_xkv', 'w_xo', 'g_ffn2', 'w2_gate', 'w2_up', 'w2_down', 'g_final']
TWIN_DIFF_INPUT = 'x'
TWIN_INPUTS = ['x', 'mem', 'g_ffn1', 'w1_gate', 'w1_up', 'w1_down', 'g_mix', 'w_in', 'g_v', 'w_s', 'b_s', 'sinks', 'g_a_out', 'g_b_out', 'w_out', 'g_x', 'g_mem', 'w_xq', 'w_xkv', 'w_xo', 'g_ffn2', 'w2_gate', 'w2_up', 'w2_down', 'g_final', 'loss_target', 'm_g_ffn1', 'm_w1_gate', 'm_w1_up', 'm_w1_down', 'm_g_mix', 'm_w_in', 'm_g_v', 'm_w_s', 'm_b_s', 'm_sinks', 'm_g_a_out', 'm_g_b_out', 'm_w_out', 'm_g_x', 'm_g_mem', 'm_w_xq', 'm_w_xkv', 'm_w_xo', 'm_g_ffn2', 'm_w2_gate', 'm_w2_up', 'm_w2_down', 'm_g_final', 'v_g_ffn1', 'v_w1_gate', 'v_w1_up', 'v_w1_down', 'v_g_mix', 'v_w_in', 'v_g_v', 'v_w_s', 'v_b_s', 'v_sinks', 'v_g_a_out', 'v_g_b_out', 'v_w_out', 'v_g_x', 'v_g_mem', 'v_w_xq', 'v_w_xkv', 'v_w_xo', 'v_g_ffn2', 'v_w2_gate', 'v_w2_up', 'v_w2_down', 'v_g_final']
TWIN_OUTPUTS = ['loss', 'grad_x', 'grad_g_ffn1', 'grad_w1_gate', 'grad_w1_up', 'grad_w1_down', 'grad_g_mix', 'grad_w_in', 'grad_g_v', 'grad_w_s', 'grad_b_s', 'grad_sinks', 'grad_g_a_out', 'grad_g_b_out', 'grad_w_out', 'grad_g_x', 'grad_g_mem', 'grad_w_xq', 'grad_w_xkv', 'grad_w_xo', 'grad_g_ffn2', 'grad_w2_gate', 'grad_w2_up', 'grad_w2_down', 'grad_g_final', 'delta_g_ffn1', 'delta_w1_gate', 'delta_w1_up', 'delta_w1_down', 'delta_g_mix', 'delta_w_in', 'delta_g_v', 'delta_w_s', 'delta_b_s', 'delta_sinks', 'delta_g_a_out', 'delta_g_b_out', 'delta_w_out', 'delta_g_x', 'delta_g_mem', 'delta_w_xq', 'delta_w_xkv', 'delta_w_xo', 'delta_g_ffn2', 'delta_w2_gate', 'delta_w2_up', 'delta_w2_down', 'delta_g_final', 'new_m_g_ffn1', 'new_m_w1_gate', 'new_m_w1_up', 'new_m_w1_down', 'new_m_g_mix', 'new_m_w_in', 'new_m_g_v', 'new_m_w_s', 'new_m_b_s', 'new_m_sinks', 'new_m_g_a_out', 'new_m_g_b_out', 'new_m_w_out', 'new_m_g_x', 'new_m_g_mem', 'new_m_w_xq', 'new_m_w_xkv', 'new_m_w_xo', 'new_m_g_ffn2', 'new_m_w2_gate', 'new_m_w2_up', 'new_m_w2_down', 'new_m_g_final', 'new_v_g_ffn1', 'new_v_w1_gate', 'new_v_w1_up', 'new_v_w1_down', 'new_v_g_mix', 'new_v_w_in', 'new_v_g_v', 'new_v_w_s', 'new_v_b_s', 'new_v_sinks', 'new_v_g_a_out', 'new_v_g_b_out', 'new_v_w_out', 'new_v_g_x', 'new_v_g_mem', 'new_v_w_xq', 'new_v_w_xkv', 'new_v_w_xo', 'new_v_g_ffn2', 'new_v_w2_gate', 'new_v_w2_up', 'new_v_w2_down', 'new_v_g_final']
TWIN_LEAF_KINDS = {'loss': 'loss', 'grad_x': 'grad_x', 'grad_g_ffn1': 'grad_w', 'grad_w1_gate': 'grad_w', 'grad_w1_up': 'grad_w', 'grad_w1_down': 'grad_w', 'grad_g_mix': 'grad_w', 'grad_w_in': 'grad_w', 'grad_g_v': 'grad_w', 'grad_w_s': 'grad_w', 'grad_b_s': 'grad_w', 'grad_sinks': 'grad_w', 'grad_g_a_out': 'grad_w', 'grad_g_b_out': 'grad_w', 'grad_w_out': 'grad_w', 'grad_g_x': 'grad_w', 'grad_g_mem': 'grad_w', 'grad_w_xq': 'grad_w', 'grad_w_xkv': 'grad_w', 'grad_w_xo': 'grad_w', 'grad_g_ffn2': 'grad_w', 'grad_w2_gate': 'grad_w', 'grad_w2_up': 'grad_w', 'grad_w2_down': 'grad_w', 'grad_g_final': 'grad_w', 'delta_g_ffn1': 'delta_w', 'delta_w1_gate': 'delta_w', 'delta_w1_up': 'delta_w', 'delta_w1_down': 'delta_w', 'delta_g_mix': 'delta_w', 'delta_w_in': 'delta_w', 'delta_g_v': 'delta_w', 'delta_w_s': 'delta_w', 'delta_b_s': 'delta_w', 'delta_sinks': 'delta_w', 'delta_g_a_out': 'delta_w', 'delta_g_b_out': 'delta_w', 'delta_w_out': 'delta_w', 'delta_g_x': 'delta_w', 'delta_g_mem': 'delta_w', 'delta_w_xq': 'delta_w', 'delta_w_xkv': 'delta_w', 'delta_w_xo': 'delta_w', 'delta_g_ffn2': 'delta_w', 'delta_w2_gate': 'delta_w', 'delta_w2_up': 'delta_w', 'delta_w2_down': 'delta_w', 'delta_g_final': 'delta_w', 'new_m_g_ffn1': 'new_m', 'new_m_w1_gate': 'new_m', 'new_m_w1_up': 'new_m', 'new_m_w1_down': 'new_m', 'new_m_g_mix': 'new_m', 'new_m_w_in': 'new_m', 'new_m_g_v': 'new_m', 'new_m_w_s': 'new_m', 'new_m_b_s': 'new_m', 'new_m_sinks': 'new_m', 'new_m_g_a_out': 'new_m', 'new_m_g_b_out': 'new_m', 'new_m_w_out': 'new_m', 'new_m_g_x': 'new_m', 'new_m_g_mem': 'new_m', 'new_m_w_xq': 'new_m', 'new_m_w_xkv': 'new_m', 'new_m_w_xo': 'new_m', 'new_m_g_ffn2': 'new_m', 'new_m_w2_gate': 'new_m', 'new_m_w2_up': 'new_m', 'new_m_w2_down': 'new_m', 'new_m_g_final': 'new_m', 'new_v_g_ffn1': 'new_v', 'new_v_w1_gate': 'new_v', 'new_v_w1_up': 'new_v', 'new_v_w1_down': 'new_v', 'new_v_g_mix': 'new_v', 'new_v_w_in': 'new_v', 'new_v_g_v': 'new_v', 'new_v_w_s': 'new_v', 'new_v_b_s': 'new_v', 'new_v_sinks': 'new_v', 'new_v_g_a_out': 'new_v', 'new_v_g_b_out': 'new_v', 'new_v_w_out': 'new_v', 'new_v_g_x': 'new_v', 'new_v_g_mem': 'new_v', 'new_v_w_xq': 'new_v', 'new_v_w_xkv': 'new_v', 'new_v_w_xo': 'new_v', 'new_v_g_ffn2': 'new_v', 'new_v_w2_gate': 'new_v', 'new_v_w2_up': 'new_v', 'new_v_w2_down': 'new_v', 'new_v_g_final': 'new_v'}


def _forward(args):
    return _fwd_reference(*[args[k] for k in FWD_PARAMS])


def _output_shape():
    def fwd():
        inp = _fwd_setup_inputs(0)
        return _fwd_reference(*[inp[k] for k in FWD_PARAMS])
    out = _jax.eval_shape(fwd)
    return out.shape, out.dtype

N_MICROBATCH = 1
ADAM_LR = 0.001
ADAM_B1 = 0.9
ADAM_B2 = 0.999
ADAM_EPS = 1e-08
ADAM_WD = 0.01
ADAM_STEP = 10
PER_EXAMPLE_BATCH_AXIS = {'x': 0, 'mem': 0, 'loss_target': 0}
SHARED_INPUTS = []
_WEIGHT_DTYPES = {'g_ffn1': _jnp.float32, 'w1_gate': _jnp.float32, 'w1_up': _jnp.float32, 'w1_down': _jnp.float32, 'g_mix': _jnp.float32, 'w_in': _jnp.float32, 'g_v': _jnp.float32, 'w_s': _jnp.float32, 'b_s': _jnp.float32, 'sinks': _jnp.float32, 'g_a_out': _jnp.float32, 'g_b_out': _jnp.float32, 'w_out': _jnp.float32, 'g_x': _jnp.float32, 'g_mem': _jnp.float32, 'w_xq': _jnp.float32, 'w_xkv': _jnp.float32, 'w_xo': _jnp.float32, 'g_ffn2': _jnp.float32, 'w2_gate': _jnp.float32, 'w2_up': _jnp.float32, 'w2_down': _jnp.float32, 'g_final': _jnp.float32}
MOMENT_SCALE = {'g_ffn1': 6.730165e-02, 'w1_gate': 2.860451e-02, 'w1_up': 2.773546e-02, 'w1_down': 4.608579e-02, 'g_mix': 1.308614e-01, 'w_in': 1.020306e-01, 'g_v': 3.020497e-02, 'w_s': 5.773288e-02, 'b_s': 8.032190e-02, 'sinks': 3.716585e-02, 'g_a_out': 1.529143e-01, 'g_b_out': 9.680052e-02, 'w_out': 1.167271e-01, 'g_x': 9.328160e-03, 'g_mem': 1.559072e-02, 'w_xq': 9.305127e-03, 'w_xkv': 1.018023e-02, 'w_xo': 1.098921e-02, 'g_ffn2': 3.983077e-02, 'w2_gate': 1.612785e-02, 'w2_up': 1.600567e-02, 'w2_down': 2.653863e-02, 'g_final': 3.208368e+01}


def _to_microbatches(a, axis):
    t = _jnp.moveaxis(a, axis, 0)
    t = t.reshape((N_MICROBATCH, t.shape[0] // N_MICROBATCH) + t.shape[1:])
    return _jnp.moveaxis(t, 1, axis + 1)


def setup_inputs(seed: int = 0) -> dict:
    inp = _fwd_setup_inputs(seed)
    key = _jax.random.fold_in(_jax.random.key(seed), 7919)
    shape, _ = _output_shape()
    out = dict(inp)
    out["loss_target"] = _jax.random.normal(_jax.random.fold_in(key, 0), shape, _jnp.float32)
    for i, name in enumerate(TWIN_WEIGHTS):
        w = inp[name].astype(_jnp.float32)
        if MOMENT_SCALE is None:
            s = _jnp.sqrt(_jnp.mean(_jnp.square(w)) + 1e-30)
        else:
            s = MOMENT_SCALE[name]
        km, kv = _jax.random.split(_jax.random.fold_in(key, i + 1))
        out[name] = w
        out["m_" + name] = s * _jax.random.normal(km, w.shape, _jnp.float32)
        out["v_" + name] = (s * s) * _jax.random.uniform(kv, w.shape, _jnp.float32, 0.5, 1.5)
    if N_MICROBATCH > 1:
        for name, axis in PER_EXAMPLE_BATCH_AXIS.items():
            out[name] = _to_microbatches(out[name], axis)
    return {'x': out['x'], 'mem': out['mem'], 'g_ffn1': out['g_ffn1'], 'w1_gate': out['w1_gate'], 'w1_up': out['w1_up'], 'w1_down': out['w1_down'], 'g_mix': out['g_mix'], 'w_in': out['w_in'], 'g_v': out['g_v'], 'w_s': out['w_s'], 'b_s': out['b_s'], 'sinks': out['sinks'], 'g_a_out': out['g_a_out'], 'g_b_out': out['g_b_out'], 'w_out': out['w_out'], 'g_x': out['g_x'], 'g_mem': out['g_mem'], 'w_xq': out['w_xq'], 'w_xkv': out['w_xkv'], 'w_xo': out['w_xo'], 'g_ffn2': out['g_ffn2'], 'w2_gate': out['w2_gate'], 'w2_up': out['w2_up'], 'w2_down': out['w2_down'], 'g_final': out['g_final'], 'loss_target': out['loss_target'], 'm_g_ffn1': out['m_g_ffn1'], 'm_w1_gate': out['m_w1_gate'], 'm_w1_up': out['m_w1_up'], 'm_w1_down': out['m_w1_down'], 'm_g_mix': out['m_g_mix'], 'm_w_in': out['m_w_in'], 'm_g_v': out['m_g_v'], 'm_w_s': out['m_w_s'], 'm_b_s': out['m_b_s'], 'm_sinks': out['m_sinks'], 'm_g_a_out': out['m_g_a_out'], 'm_g_b_out': out['m_g_b_out'], 'm_w_out': out['m_w_out'], 'm_g_x': out['m_g_x'], 'm_g_mem': out['m_g_mem'], 'm_w_xq': out['m_w_xq'], 'm_w_xkv': out['m_w_xkv'], 'm_w_xo': out['m_w_xo'], 'm_g_ffn2': out['m_g_ffn2'], 'm_w2_gate': out['m_w2_gate'], 'm_w2_up': out['m_w2_up'], 'm_w2_down': out['m_w2_down'], 'm_g_final': out['m_g_final'], 'v_g_ffn1': out['v_g_ffn1'], 'v_w1_gate': out['v_w1_gate'], 'v_w1_up': out['v_w1_up'], 'v_w1_down': out['v_w1_down'], 'v_g_mix': out['v_g_mix'], 'v_w_in': out['v_w_in'], 'v_g_v': out['v_g_v'], 'v_w_s': out['v_w_s'], 'v_b_s': out['v_b_s'], 'v_sinks': out['v_sinks'], 'v_g_a_out': out['v_g_a_out'], 'v_g_b_out': out['v_g_b_out'], 'v_w_out': out['v_w_out'], 'v_g_x': out['v_g_x'], 'v_g_mem': out['v_g_mem'], 'v_w_xq': out['v_w_xq'], 'v_w_xkv': out['v_w_xkv'], 'v_w_xo': out['v_w_xo'], 'v_g_ffn2': out['v_g_ffn2'], 'v_w2_gate': out['v_w2_gate'], 'v_w2_up': out['v_w2_up'], 'v_w2_down': out['v_w2_down'], 'v_g_final': out['v_g_final']}


def _loss(weights, diff, rest, loss_target):
    with _jax.named_scope("forward"):
        args = {**rest, TWIN_DIFF_INPUT: diff, **{k: w.astype(_WEIGHT_DTYPES[k]) for k, w in weights.items()}}
        y = _forward(args)
    with _jax.named_scope("loss_head"):
        err = _jnp.square(y.astype(_jnp.float32) - loss_target)
        return 0.5 * _jnp.sum(_jnp.mean(err, axis=-1)) if err.ndim else 0.5 * err


def _adamw(w, g, m, v):
    m = ADAM_B1 * m + (1.0 - ADAM_B1) * g
    v = ADAM_B2 * v + (1.0 - ADAM_B2) * _jnp.square(g)
    m_hat = m / (1.0 - ADAM_B1 ** ADAM_STEP)
    v_hat = v / (1.0 - ADAM_B2 ** ADAM_STEP)
    delta = -ADAM_LR * (m_hat / (_jnp.sqrt(v_hat) + ADAM_EPS) + ADAM_WD * w)
    return delta, m, v


def reference(x, mem, g_ffn1, w1_gate, w1_up, w1_down, g_mix, w_in, g_v, w_s, b_s, sinks, g_a_out, g_b_out, w_out, g_x, g_mem, w_xq, w_xkv, w_xo, g_ffn2, w2_gate, w2_up, w2_down, g_final, loss_target, m_g_ffn1, m_w1_gate, m_w1_up, m_w1_down, m_g_mix, m_w_in, m_g_v, m_w_s, m_b_s, m_sinks, m_g_a_out, m_g_b_out, m_w_out, m_g_x, m_g_mem, m_w_xq, m_w_xkv, m_w_xo, m_g_ffn2, m_w2_gate, m_w2_up, m_w2_down, m_g_final, v_g_ffn1, v_w1_gate, v_w1_up, v_w1_down, v_g_mix, v_w_in, v_g_v, v_w_s, v_b_s, v_sinks, v_g_a_out, v_g_b_out, v_w_out, v_g_x, v_g_mem, v_w_xq, v_w_xkv, v_w_xo, v_g_ffn2, v_w2_gate, v_w2_up, v_w2_down, v_g_final):
    given = dict(x=x, mem=mem, g_ffn1=g_ffn1, w1_gate=w1_gate, w1_up=w1_up, w1_down=w1_down, g_mix=g_mix, w_in=w_in, g_v=g_v, w_s=w_s, b_s=b_s, sinks=sinks, g_a_out=g_a_out, g_b_out=g_b_out, w_out=w_out, g_x=g_x, g_mem=g_mem, w_xq=w_xq, w_xkv=w_xkv, w_xo=w_xo, g_ffn2=g_ffn2, w2_gate=w2_gate, w2_up=w2_up, w2_down=w2_down, g_final=g_final, loss_target=loss_target, m_g_ffn1=m_g_ffn1, m_w1_gate=m_w1_gate, m_w1_up=m_w1_up, m_w1_down=m_w1_down, m_g_mix=m_g_mix, m_w_in=m_w_in, m_g_v=m_g_v, m_w_s=m_w_s, m_b_s=m_b_s, m_sinks=m_sinks, m_g_a_out=m_g_a_out, m_g_b_out=m_g_b_out, m_w_out=m_w_out, m_g_x=m_g_x, m_g_mem=m_g_mem, m_w_xq=m_w_xq, m_w_xkv=m_w_xkv, m_w_xo=m_w_xo, m_g_ffn2=m_g_ffn2, m_w2_gate=m_w2_gate, m_w2_up=m_w2_up, m_w2_down=m_w2_down, m_g_final=m_g_final, v_g_ffn1=v_g_ffn1, v_w1_gate=v_w1_gate, v_w1_up=v_w1_up, v_w1_down=v_w1_down, v_g_mix=v_g_mix, v_w_in=v_w_in, v_g_v=v_g_v, v_w_s=v_w_s, v_b_s=v_b_s, v_sinks=v_sinks, v_g_a_out=v_g_a_out, v_g_b_out=v_g_b_out, v_w_out=v_w_out, v_g_x=v_g_x, v_g_mem=v_g_mem, v_w_xq=v_w_xq, v_w_xkv=v_w_xkv, v_w_xo=v_w_xo, v_g_ffn2=v_g_ffn2, v_w2_gate=v_w2_gate, v_w2_up=v_w2_up, v_w2_down=v_w2_down, v_g_final=v_g_final)
    weights = {n: given[n] for n in TWIN_WEIGHTS}
    shared = {n: given[n] for n in SHARED_INPUTS}
    per_example = {n: given[n] for n in ['x', 'mem']}
    grad_fn = _jax.value_and_grad(_loss, argnums=(0, 1))

    def one_microbatch(ex, loss_target):
        ex = dict(ex)
        diff = ex.pop(TWIN_DIFF_INPUT)
        return grad_fn(weights, diff, {**shared, **ex}, loss_target)

    if N_MICROBATCH == 1:
        loss, (grad_w, grad_x) = one_microbatch(per_example, given["loss_target"])
    else:
        def body(carry, xs):
            loss_sum, grad_sum = carry
            l_k, (gw_k, gx_k) = one_microbatch(xs[0], xs[1])
            with _jax.named_scope("update"):
                return (loss_sum + l_k, _jax.tree.map(_jnp.add, grad_sum, gw_k)), gx_k

        init = (_jnp.zeros((), _jnp.float32), _jax.tree.map(_jnp.zeros_like, weights))
        (loss, grad_w), grad_x = _jax.lax.scan(body, init, (per_example, given["loss_target"]))
    with _jax.named_scope("update"):
        delta_w, new_m, new_v = {}, {}, {}
        for n in TWIN_WEIGHTS:
            delta_w[n], new_m[n], new_v[n] = _adamw(weights[n], grad_w[n], given["m_" + n], given["v_" + n])
    return (loss, grad_x, *[grad_w[n] for n in TWIN_WEIGHTS], *[delta_w[n] for n in TWIN_WEIGHTS],
            *[new_m[n] for n in TWIN_WEIGHTS], *[new_v[n] for n in TWIN_WEIGHTS])
```

```python
import jax
import jax.numpy as jnp
from jax import lax
from jax.experimental import pallas as pl
from jax.experimental.pallas import tpu as pltpu

F32 = jnp.float32
BF16 = jnp.bfloat16
SDS = jax.ShapeDtypeStruct
BS = pl.BlockSpec

N_DEV = 8
AXES = ("x", "y", "c")
EPS = 1e-5
CHUNK = 128
HEAD_DIM = 64
KV_HEADS = 2
X_HEADS = 4
NEG = -1e30
ADAM_LR = 0.001
ADAM_B1 = 0.9
ADAM_B2 = 0.999
ADAM_EPS = 1e-08
ADAM_WD = 0.01
ADAM_STEP = 10
MIB = 1 << 20
WEIGHTS = ['g_ffn1', 'w1_gate', 'w1_up', 'w1_down', 'g_mix', 'w_in', 'g_v', 'w_s', 'b_s', 'sinks', 'g_a_out',
           'g_b_out', 'w_out', 'g_x', 'g_mem', 'w_xq', 'w_xkv', 'w_xo', 'g_ffn2', 'w2_gate', 'w2_up', 'w2_down',
           'g_final']
BIG = ['w1_gate', 'w1_up', 'w1_down', 'w_in', 'w_out', 'w_xq', 'w_xkv', 'w_xo', 'w2_gate', 'w2_up', 'w2_down']
SMALL = [w for w in WEIGHTS if w not in BIG]


def _params(sem=None, vmem_mib=48):
    return pltpu.CompilerParams(dimension_semantics=sem, vmem_limit_bytes=vmem_mib * MIB)


def _dot(a, b):
    return jnp.dot(a, b, preferred_element_type=F32)


def _dot_nt(a, b):
    return lax.dot_general(a, b, (((1,), (1,)), ((), ())), preferred_element_type=F32)


def _dot_tn(a, b):
    return lax.dot_general(a, b, (((0,), (0,)), ((), ())), preferred_element_type=F32)


def _rstd(v):
    return lax.rsqrt(jnp.mean(v * v, axis=-1, keepdims=True) + EPS)


def _norm_bwd(xhat, r, g, d):
    t = d * g
    return r * (t - xhat * jnp.mean(t * xhat, axis=-1, keepdims=True))


def _gelu(v):
    return 0.5 * v * (1.0 + lax.erf(v * 0.7071067811865476))


def _gelu_grad(v):
    return 0.5 * (1.0 + lax.erf(v * 0.7071067811865476)) + v * jnp.exp(-0.5 * v * v) * 0.3989422804014327


def _mesh_pos():
    x, y, c = lax.axis_index("x"), lax.axis_index("y"), lax.axis_index("c")
    return x, y, c, 4 * x + 2 * y + c


def _peer(x, y, c, k):
    px = 1 - x if k & 4 else x
    py = 1 - y if k & 2 else y
    pc = 1 - c if k & 1 else c
    return (px, py, pc), 4 * px + 2 * py + pc


def _all_gather(shards, name):
    n = len(shards)

    def body(*refs):
        ins, outs = refs[:n], refs[n:2 * n]
        send, recv, lsem = refs[2 * n:]
        x, y, c, me = _mesh_pos()
        local = [pltpu.make_async_copy(ins[w], outs[w].at[me], lsem.at[w]) for w in range(n)]
        for cp in local:
            cp.start()
        remote = []
        for k in range(1, N_DEV):
            peer, _ = _peer(x, y, c, k)
            for w in range(n):
                cp = pltpu.make_async_remote_copy(
                    src_ref=ins[w], dst_ref=outs[w].at[me], send_sem=send.at[w, k - 1], recv_sem=recv.at[w, k - 1],
                    device_id=peer, device_id_type=pl.DeviceIdType.MESH)
                cp.start()
                remote.append(cp)
        for cp in remote:
            cp.wait()
        for cp in local:
            cp.wait()

    return pl.pallas_call(
        body, name=name,
        out_shape=[SDS((N_DEV,) + s.shape, s.dtype) for s in shards],
        in_specs=[BS(memory_space=pl.ANY)] * n, out_specs=[BS(memory_space=pl.ANY)] * n,
        scratch_shapes=[pltpu.SemaphoreType.DMA((n, N_DEV - 1)), pltpu.SemaphoreType.DMA((n, N_DEV - 1)),
                        pltpu.SemaphoreType.DMA((n,))],
        compiler_params=pltpu.CompilerParams(has_side_effects=True),
    )(*shards)


def _reduce_scatter(fulls, name):
    n = len(fulls)

    def body(*refs):
        ins, outs = refs[:n], refs[n:2 * n]
        send, recv, lsem = refs[2 * n:]
        x, y, c, me = _mesh_pos()
        local = [pltpu.make_async_copy(ins[w].at[me], outs[w].at[me], lsem.at[w]) for w in range(n)]
        for cp in local:
            cp.start()
        remote = []
        for k in range(1, N_DEV):
            peer, p = _peer(x, y, c, k)
            for w in range(n):
                cp = pltpu.make_async_remote_copy(
                    src_ref=ins[w].at[p], dst_ref=outs[w].at[me], send_sem=send.at[w, k - 1],
                    recv_sem=recv.at[w, k - 1], device_id=peer, device_id_type=pl.DeviceIdType.MESH)
                cp.start()
                remote.append(cp)
        for cp in remote:
            cp.wait()
        for cp in local:
            cp.wait()

    return pl.pallas_call(
        body, name=name,
        out_shape=[SDS(f.shape, f.dtype) for f in fulls],
        in_specs=[BS(memory_space=pl.ANY)] * n, out_specs=[BS(memory_space=pl.ANY)] * n,
        scratch_shapes=[pltpu.SemaphoreType.DMA((n, N_DEV - 1)), pltpu.SemaphoreType.DMA((n, N_DEV - 1)),
                        pltpu.SemaphoreType.DMA((n,))],
        compiler_params=pltpu.CompilerParams(has_side_effects=True),
    )(*fulls)


def _rms_fwd(h, g, name, tm=512):
    S, D = h.shape
    tm = min(tm, S)
    assert S % tm == 0, (S, tm)

    def body(h_ref, g_ref, o_ref):
        hv = h_ref[...]
        o_ref[...] = (hv * _rstd(hv) * g_ref[...]).astype(o_ref.dtype)

    return pl.pallas_call(
        body, name=name, out_shape=SDS((S, D), BF16), grid=(S // tm,),
        in_specs=[BS((tm, D), lambda i: (i, 0)), BS((1, D), lambda i: (0, 0))],
        out_specs=BS((tm, D), lambda i: (i, 0)), compiler_params=_params(("parallel",), 32),
    )(h, g)


def _rms_bwd(h, g, dn, dres, name, tm=256):
    S, D = h.shape
    tm = min(tm, S)
    assert S % tm == 0, (S, tm)
    has_res = dres is not None

    def body(*refs):
        if has_res:
            h_ref, g_ref, dn_ref, dres_ref, dh_ref, dg_ref = refs
        else:
            h_ref, g_ref, dn_ref, dh_ref, dg_ref = refs
        hv = h_ref[...]
        r = _rstd(hv)
        xh = hv * r
        d = dn_ref[...].astype(F32)

        @pl.when(pl.program_id(0) == 0)
        def _():
            dg_ref[...] = jnp.zeros_like(dg_ref)

        dg_ref[...] += jnp.sum(d * xh, axis=0, keepdims=True)
        dh = _norm_bwd(xh, r, g_ref[...], d)
        dh_ref[...] = dres_ref[...] + dh if has_res else dh

    row = BS((tm, D), lambda i: (i, 0))
    vec = BS((1, D), lambda i: (0, 0))
    args = (h, g, dn) + ((dres,) if has_res else ())
    return pl.pallas_call(
        body, name=name, out_shape=[SDS((S, D), F32), SDS((1, D), F32)], grid=(S // tm,),
        in_specs=[row, vec, row] + ([row] if has_res else []), out_specs=[row, vec],
        compiler_params=_params(("arbitrary",), 40),
    )(*args)


def _final_loss(h, g, target, name, tm=256):
    S, D = h.shape
    tm = min(tm, S)
    assert S % tm == 0, (S, tm)

    def body(h_ref, g_ref, t_ref, dh_ref, dg_ref, loss_ref):
        hv = h_ref[...]
        r = _rstd(hv)
        xh = hv * r
        gv = g_ref[...]
        e = xh * gv - t_ref[...]

        @pl.when(pl.program_id(0) == 0)
        def _():
            dg_ref[...] = jnp.zeros_like(dg_ref)
            loss_ref[...] = jnp.zeros_like(loss_ref)

        loss_ref[...] += 0.5 * jnp.sum(jnp.mean(e * e, axis=-1, keepdims=True), axis=0, keepdims=True)
        dy = e * (1.0 / D)
        dg_ref[...] += jnp.sum(dy * xh, axis=0, keepdims=True)
        dh_ref[...] = _norm_bwd(xh, r, gv, dy)

    row = BS((tm, D), lambda i: (i, 0))
    vec = BS((1, D), lambda i: (0, 0))
    return pl.pallas_call(
        body, name=name, out_shape=[SDS((S, D), F32), SDS((1, D), F32), SDS((1, 128), F32)], grid=(S // tm,),
        in_specs=[row, vec, row], out_specs=[row, vec, BS((1, 128), lambda i: (0, 0))],
        compiler_params=_params(("arbitrary",), 40),
    )(h, g, target)


def _matmul(a, b, mode, out_dtype, name, res=None, tm=512, tn=512, tk=512):
    if mode == "tn":
        K, M = a.shape
        N = b.shape[1]
        tm, tn, tk = min(tm, M), min(tn, N), min(tk, K)
        assert M % tm == 0 and N % tn == 0 and K % tk == 0, (a.shape, b.shape, tm, tn, tk)
        nk = K // tk

        def body(a_ref, b_ref, o_ref, acc_ref):
            k = pl.program_id(2)

            @pl.when(k == 0)
            def _():
                acc_ref[...] = jnp.zeros_like(acc_ref)

            acc_ref[...] += _dot_tn(a_ref[...].astype(BF16), b_ref[...].astype(BF16))

            @pl.when(k == nk - 1)
            def _():
                o_ref[...] = acc_ref[...].astype(o_ref.dtype)

        return pl.pallas_call(
            body, name=name, out_shape=SDS((M, N), out_dtype), grid=(M // tm, N // tn, nk),
            in_specs=[BS((tk, tm), lambda i, j, k: (k, i)), BS((tk, tn), lambda i, j, k: (k, j))],
            out_specs=BS((tm, tn), lambda i, j, k: (i, j)), scratch_shapes=[pltpu.VMEM((tm, tn), F32)],
            compiler_params=_params(("parallel", "parallel", "arbitrary"), 48),
        )(a, b)

    M, K = a.shape
    N = b.shape[1] if mode == "nn" else b.shape[0]
    tm, tn = min(tm, M), min(tn, N)
    assert M % tm == 0 and N % tn == 0, (a.shape, b.shape, tm, tn)
    has_res = res is not None

    def body(*refs):
        if has_res:
            a_ref, b_ref, r_ref, o_ref = refs
        else:
            a_ref, b_ref, o_ref = refs
        av, bv = a_ref[...].astype(BF16), b_ref[...].astype(BF16)
        acc = _dot(av, bv) if mode == "nn" else _dot_nt(av, bv)
        if has_res:
            acc = acc + r_ref[...]
        o_ref[...] = acc.astype(o_ref.dtype)

    b_spec = BS((K, tn), lambda i, j: (0, j)) if mode == "nn" else BS((tn, K), lambda i, j: (j, 0))
    o_spec = BS((tm, tn), lambda i, j: (i, j))
    return pl.pallas_call(
        body, name=name, out_shape=SDS((M, N), out_dtype), grid=(M // tm, N // tn),
        in_specs=[BS((tm, K), lambda i, j: (i, 0)), b_spec] + ([o_spec] if has_res else []), out_specs=o_spec,
        compiler_params=_params(("parallel", "parallel"), 48),
    )(*((a, b) + ((res,) if has_res else ())))


def _ffn_fwd(h, g, wg, wu, wd, name, tm=512):
    S, D = h.shape
    nb, _, Fs = wg.shape
    tm = min(tm, S)
    assert S % tm == 0, (S, tm)

    def body(h_ref, g_ref, wg_ref, wu_ref, wd_ref, o_ref, n_ref, a_ref, b_ref):
        j = pl.program_id(1)

        @pl.when(j == 0)
        def _():
            hv = h_ref[...]
            n_ref[...] = (hv * _rstd(hv) * g_ref[...]).astype(BF16)
            o_ref[...] = jnp.zeros_like(o_ref)

        n = n_ref[...]
        a = _dot(n, wg_ref[...]).astype(BF16)
        b = _dot(n, wu_ref[...]).astype(BF16)
        a_ref[...] = a
        b_ref[...] = b
        a, b = a.astype(F32), b.astype(F32)
        s = (a * jax.nn.sigmoid(a) * b).astype(BF16)
        o_ref[...] += _dot(s, wd_ref[...])

        @pl.when(j == nb - 1)
        def _():
            o_ref[...] = h_ref[...] + 0.5 * o_ref[...]

    row = BS((tm, D), lambda i, j: (i, 0))
    wcol = BS((None, D, Fs), lambda i, j: (j, 0, 0))
    act = BS((None, tm, Fs), lambda i, j: (j, i, 0))
    return pl.pallas_call(
        body, name=name,
        out_shape=[SDS((S, D), F32), SDS((S, D), BF16), SDS((nb, S, Fs), BF16), SDS((nb, S, Fs), BF16)],
        grid=(S // tm, nb),
        in_specs=[row, BS((1, D), lambda i, j: (0, 0)), wcol, wcol, BS((None, Fs, D), lambda i, j: (j, 0, 0))],
        out_specs=[row, row, act, act], compiler_params=_params(("parallel", "arbitrary"), 56),
    )(h, g, wg, wu, wd)


def _ffn_bwd_dx(dh, a, b, wg, wu, wd, name, tm=512):
    S, D = dh.shape
    nb, _, Fs = wg.shape
    tm = min(tm, S)
    assert S % tm == 0, (S, tm)

    def body(dh_ref, a_ref, b_ref, wg_ref, wu_ref, wd_ref, dn_ref, df_ref, da_ref, db_ref, s_ref):
        j = pl.program_id(1)

        @pl.when(j == 0)
        def _():
            df_ref[...] = (0.5 * dh_ref[...]).astype(BF16)
            dn_ref[...] = jnp.zeros_like(dn_ref)

        ds = _dot_nt(df_ref[...], wd_ref[...])
        av, bv = a_ref[...].astype(F32), b_ref[...].astype(F32)
        sig = jax.nn.sigmoid(av)
        sl = av * sig
        da = (ds * bv * (sig * (1.0 + av * (1.0 - sig)))).astype(BF16)
        db = (ds * sl).astype(BF16)
        da_ref[...] = da
        db_ref[...] = db
        s_ref[...] = (sl * bv).astype(BF16)
        dn_ref[...] += _dot_nt(da, wg_ref[...]) + _dot_nt(db, wu_ref[...])

    row = BS((tm, D), lambda i, j: (i, 0))
    wcol = BS((None, D, Fs), lambda i, j: (j, 0, 0))
    act = BS((None, tm, Fs), lambda i, j: (j, i, 0))
    return pl.pallas_call(
        body, name=name,
        out_shape=[SDS((S, D), F32), SDS((S, D), BF16)] + [SDS((nb, S, Fs), BF16)] * 3,
        grid=(S // tm, nb),
        in_specs=[row, act, act, wcol, wcol, BS((None, Fs, D), lambda i, j: (j, 0, 0))],
        out_specs=[row, row, act, act, act], compiler_params=_params(("parallel", "arbitrary"), 56),
    )(dh, a, b, wg, wu, wd)


def _ffn_bwd_dw(n, df, da, db, s, name, tk=512):
    S, D = n.shape
    nb, _, Fs = da.shape
    tk = min(tk, S)
    assert S % tk == 0, (S, tk)
    nk = S // tk

    def body(n_ref, df_ref, da_ref, db_ref, s_ref, dwg_ref, dwu_ref, dwd_ref, ag_ref, au_ref, ad_ref):
        k = pl.program_id(1)

        @pl.when(k == 0)
        def _():
            ag_ref[...] = jnp.zeros_like(ag_ref)
            au_ref[...] = jnp.zeros_like(au_ref)
            ad_ref[...] = jnp.zeros_like(ad_ref)

        nv = n_ref[...]
        ag_ref[...] += _dot_tn(nv, da_ref[...])
        au_ref[...] += _dot_tn(nv, db_ref[...])
        ad_ref[...] += _dot_tn(s_ref[...], df_ref[...])

        @pl.when(k == nk - 1)
        def _():
            dwg_ref[...] = ag_ref[...].astype(BF16)
            dwu_ref[...] = au_ref[...].astype(BF16)
            dwd_ref[...] = ad_ref[...].astype(BF16)

    row = BS((tk, D), lambda j, k: (k, 0))
    act = BS((None, tk, Fs), lambda j, k: (j, k, 0))
    wcol = BS((None, D, Fs), lambda j, k: (j, 0, 0))
    wrow = BS((None, Fs, D), lambda j, k: (j, 0, 0))
    return pl.pallas_call(
        body, name=name,
        out_shape=[SDS((nb, D, Fs), BF16), SDS((nb, D, Fs), BF16), SDS((nb, Fs, D), BF16)],
        grid=(nb, nk), in_specs=[row, row, act, act, act], out_specs=[wcol, wcol, wrow],
        scratch_shapes=[pltpu.VMEM((D, Fs), F32), pltpu.VMEM((D, Fs), F32), pltpu.VMEM((Fs, D), F32)],
        compiler_params=_params(("parallel", "arbitrary"), 56),
    )(n, df, da, db, s)


def _sgu_parts(z, gv, ws_ref, bst_ref, groups):
    da = z.shape[1] // 2
    zu, zv = z[:, :da], z[:, da:]
    u, v = _gelu(zu), _gelu(zv)
    rv = _rstd(v)
    vhat = v * rv
    vn = (vhat * gv).astype(BF16)
    tri = lax.broadcasted_iota(jnp.int32, (CHUNK, CHUNK), 0) >= lax.broadcasted_iota(jnp.int32, (CHUNK, CHUNK), 1)
    pieces = []
    for g in range(groups):
        w = jnp.where(tri, ws_ref[g], 0.0).astype(BF16)
        pieces.append(_dot(w, vn[:, g * CHUNK:(g + 1) * CHUNK]) + bst_ref[:, g:g + 1])
    sv = jnp.concatenate(pieces, axis=1)
    return dict(zu=zu, zv=zv, u=u, rv=rv, vhat=vhat, vn=vn, sv=sv, tri=tri)


def _sgu_fwd(z, g_v, w_s, b_st, g_a, name):
    S = z.shape[0]
    groups = w_s.shape[0]
    da = groups * CHUNK

    def body(z_ref, gv_ref, ws_ref, bst_ref, ga_ref, o_ref):
        p = _sgu_parts(z_ref[...], gv_ref[...], ws_ref, bst_ref, groups)
        ya = p["u"] * p["sv"]
        o_ref[...] = (ya * _rstd(ya) * ga_ref[...]).astype(BF16)

    vec = BS((1, da), lambda i: (0, 0))
    return pl.pallas_call(
        body, name=name, out_shape=SDS((S, da), BF16), grid=(S // CHUNK,),
        in_specs=[BS((CHUNK, 2 * da), lambda i: (i, 0)), vec, BS((groups, CHUNK, CHUNK), lambda i: (0, 0, 0)),
                  BS((CHUNK, groups), lambda i: (0, 0)), vec],
        out_specs=BS((CHUNK, da), lambda i: (i, 0)), compiler_params=_params(("parallel",), 32),
    )(z, g_v, w_s, b_st, g_a)


def _sgu_bwd(z, dy, g_v, w_s, w_st, b_st, g_a, name):
    S = z.shape[0]
    groups = w_s.shape[0]
    da = groups * CHUNK

    def body(z_ref, dy_ref, gv_ref, ws_ref, wst_ref, bst_ref, ga_ref, dz_ref, dws_ref, dbst_ref, dgv_ref, dga_ref):
        @pl.when(pl.program_id(0) == 0)
        def _():
            dws_ref[...] = jnp.zeros_like(dws_ref)
            dbst_ref[...] = jnp.zeros_like(dbst_ref)
            dgv_ref[...] = jnp.zeros_like(dgv_ref)
            dga_ref[...] = jnp.zeros_like(dga_ref)

        gv = gv_ref[...]
        p = _sgu_parts(z_ref[...], gv, ws_ref, bst_ref, groups)
        u, sv, tri = p["u"], p["sv"], p["tri"]
        ya = u * sv
        ra = _rstd(ya)
        yhat = ya * ra
        d = dy_ref[...]
        dga_ref[...] += jnp.sum(d * yhat, axis=0, keepdims=True)
        dya = _norm_bwd(yhat, ra, ga_ref[...], d)
        du = dya * sv
        dsv = dya * u
        dsv_b = dsv.astype(BF16)
        tri_t = (lax.broadcasted_iota(jnp.int32, (CHUNK, CHUNK), 0)
                 <= lax.broadcasted_iota(jnp.int32, (CHUNK, CHUNK), 1))
        lane = lax.broadcasted_iota(jnp.int32, (CHUNK, groups), 1)
        dvn = []
        dbs = jnp.zeros((CHUNK, groups), F32)
        for g in range(groups):
            cols = slice(g * CHUNK, (g + 1) * CHUNK)
            dbs = dbs + jnp.where(lane == g, jnp.sum(dsv[:, cols], axis=1, keepdims=True), 0.0)
            dws_ref[g] += jnp.where(tri, _dot_nt(dsv_b[:, cols], p["vn"][:, cols]), 0.0)
            wt = jnp.where(tri_t, wst_ref[g], 0.0).astype(BF16)
            dvn.append(_dot(wt, dsv_b[:, cols]))
        dbst_ref[...] += dbs
        dvn = jnp.concatenate(dvn, axis=1)
        dgv_ref[...] += jnp.sum(dvn * p["vhat"], axis=0, keepdims=True)
        dv = _norm_bwd(p["vhat"], p["rv"], gv, dvn)
        dz_ref[...] = jnp.concatenate([du * _gelu_grad(p["zu"]), dv * _gelu_grad(p["zv"])], axis=1)

    vec = BS((1, da), lambda i: (0, 0))
    wsq = BS((groups, CHUNK, CHUNK), lambda i: (0, 0, 0))
    bsq = BS((CHUNK, groups), lambda i: (0, 0))
    return pl.pallas_call(
        body, name=name,
        out_shape=[SDS((S, 2 * da), F32), SDS((groups, CHUNK, CHUNK), F32), SDS((CHUNK, groups), F32),
                   SDS((1, da), F32), SDS((1, da), F32)],
        grid=(S // CHUNK,),
        in_specs=[BS((CHUNK, 2 * da), lambda i: (i, 0)), BS((CHUNK, da), lambda i: (i, 0)), vec, wsq, wsq, bsq, vec],
        out_specs=[BS((CHUNK, 2 * da), lambda i: (i, 0)), wsq, bsq, vec, vec],
        compiler_params=_params(("arbitrary",), 32),
    )(z, dy, g_v, w_s, w_st, b_st, g_a)


def _swa_mask(i):
    row = lax.broadcasted_iota(jnp.int32, (CHUNK, 2 * CHUNK), 0)
    col = lax.broadcasted_iota(jnp.int32, (CHUNK, 2 * CHUNK), 1)
    d = row + CHUNK - col
    return (d >= 0) & (d < CHUNK) & jnp.logical_or(i > 0, col >= CHUNK)


def _swa_probs(qh, kh, sink, mask):
    s = jnp.where(mask, _dot_nt(qh, kh) * (HEAD_DIM ** -0.5), NEG)
    m = jnp.maximum(jnp.max(s, axis=-1, keepdims=True), sink)
    e = jnp.exp(s - m)
    es = jnp.exp(sink - m)
    inv = 1.0 / (jnp.sum(e, axis=-1, keepdims=True) + es)
    return e * inv, es * inv


def _swa_specs(db, nblk, clamp):
    kvw = 2 * KV_HEADS * HEAD_DIM
    cur = (lambda i: jnp.minimum(i, nblk - 1)) if clamp else (lambda i: i)
    q_spec = BS((CHUNK, db), lambda i: (cur(i), 2))
    kc_spec = BS((CHUNK, kvw), lambda i: (cur(i), 3 * db // kvw))
    kp_spec = BS((CHUNK, kvw), lambda i: (jnp.maximum(cur(i) - 1, 0), 3 * db // kvw))
    return q_spec, kc_spec, kp_spec


def _swa_fwd(z, sinks, g_b, name):
    S = z.shape[0]
    db = g_b.shape[1]
    heads = db // HEAD_DIM
    group = heads // KV_HEADS
    nblk = S // CHUNK

    def body(q_ref, kc_ref, kp_ref, sk_ref, gb_ref, yb_ref, ybn_ref):
        mask = _swa_mask(pl.program_id(0))
        q = q_ref[...].astype(BF16)
        kv = jnp.concatenate([kp_ref[...], kc_ref[...]], axis=0).astype(BF16)
        outs = []
        for h in range(heads):
            kvh = h // group
            kh = kv[:, kvh * HEAD_DIM:(kvh + 1) * HEAD_DIM]
            vh = kv[:, (KV_HEADS + kvh) * HEAD_DIM:(KV_HEADS + kvh + 1) * HEAD_DIM]
            p, _ = _swa_probs(q[:, h * HEAD_DIM:(h + 1) * HEAD_DIM], kh, sk_ref[0:1, h:h + 1], mask)
            outs.append(_dot(p.astype(BF16), vh))
        yb = jnp.concatenate(outs, axis=1)
        yb_ref[...] = yb
        ybn_ref[...] = (yb * _rstd(yb) * gb_ref[...]).astype(BF16)

    q_spec, kc_spec, kp_spec = _swa_specs(db, nblk, False)
    out = BS((CHUNK, db), lambda i: (i, 0))
    return pl.pallas_call(
        body, name=name, out_shape=[SDS((S, db), F32), SDS((S, db), BF16)], grid=(nblk,),
        in_specs=[q_spec, kc_spec, kp_spec, BS((1, heads), lambda i: (0, 0)), BS((1, db), lambda i: (0, 0))],
        out_specs=[out, out], compiler_params=_params(("parallel",), 32),
    )(z, z, z, sinks, g_b)


def _swa_bwd(z, yb, dy, sinks, g_b, name):
    S = z.shape[0]
    db = g_b.shape[1]
    heads = db // HEAD_DIM
    group = heads // KV_HEADS
    nblk = S // CHUNK
    kvw = 2 * KV_HEADS * HEAD_DIM

    def body(q_ref, kc_ref, kp_ref, yb_ref, dy_ref, sk_ref, gb_ref, dq_ref, dkv_ref, dsk_ref, dgb_ref, carry_ref):
        i = pl.program_id(0)

        @pl.when(i == 0)
        def _():
            carry_ref[...] = jnp.zeros_like(carry_ref)
            dsk_ref[...] = jnp.zeros_like(dsk_ref)
            dgb_ref[...] = jnp.zeros_like(dgb_ref)

        @pl.when(i < nblk)
        def _():
            mask = _swa_mask(i)
            yb = yb_ref[...]
            rb = _rstd(yb)
            yhat = yb * rb
            d = dy_ref[...]
            dgb_ref[...] += jnp.sum(d * yhat, axis=0, keepdims=True)
            do = _norm_bwd(yhat, rb, gb_ref[...], d).astype(BF16)
            q = q_ref[...].astype(BF16)
            kv = jnp.concatenate([kp_ref[...], kc_ref[...]], axis=0).astype(BF16)
            dqs = []
            lane = lax.broadcasted_iota(jnp.int32, (1, heads), 1)
            dsinks = jnp.zeros((1, heads), F32)
            dk = [None] * KV_HEADS
            dv = [None] * KV_HEADS
            for h in range(heads):
                kvh = h // group
                cols = slice(h * HEAD_DIM, (h + 1) * HEAD_DIM)
                kh = kv[:, kvh * HEAD_DIM:(kvh + 1) * HEAD_DIM]
                vh = kv[:, (KV_HEADS + kvh) * HEAD_DIM:(KV_HEADS + kvh + 1) * HEAD_DIM]
                p, ps = _swa_probs(q[:, cols], kh, sk_ref[0:1, h:h + 1], mask)
                dp = _dot_nt(do[:, cols], vh)
                dr = jnp.sum(p * dp, axis=-1, keepdims=True)
                ds = (p * (dp - dr) * (HEAD_DIM ** -0.5)).astype(BF16)
                dsinks = dsinks - jnp.where(lane == h, jnp.sum(ps * dr, axis=0, keepdims=True), 0.0)
                dqs.append(_dot(ds, kh))
                dkh = _dot_tn(ds, q[:, cols])
                dvh = _dot_tn(p.astype(BF16), do[:, cols])
                dk[kvh] = dkh if dk[kvh] is None else dk[kvh] + dkh
                dv[kvh] = dvh if dv[kvh] is None else dv[kvh] + dvh
            dq_ref[...] = jnp.concatenate(dqs, axis=1)
            dsk_ref[...] += dsinks
            contrib = jnp.concatenate(dk + dv, axis=1)
            dkv_ref[...] = carry_ref[...] + contrib[:CHUNK]
            carry_ref[...] = contrib[CHUNK:]

        @pl.when(i == nblk)
        def _():
            dkv_ref[...] = carry_ref[...]

    q_spec, kc_spec, kp_spec = _swa_specs(db, nblk, True)
    cur = BS((CHUNK, db), lambda i: (jnp.minimum(i, nblk - 1), 0))
    return pl.pallas_call(
        body, name=name,
        out_shape=[SDS((S, db), F32), SDS((S, kvw), F32), SDS((1, heads), F32), SDS((1, db), F32)],
        grid=(nblk + 1,),
        in_specs=[q_spec, kc_spec, kp_spec, cur, BS((CHUNK, db), lambda i: (jnp.minimum(i, nblk - 1), 1)),
                  BS((1, heads), lambda i: (0, 0)), BS((1, db), lambda i: (0, 0))],
        out_specs=[cur, BS((CHUNK, kvw), lambda i: (jnp.maximum(i - 1, 0), 0)), BS((1, heads), lambda i: (0, 0)),
                   BS((1, db), lambda i: (0, 0))],
        scratch_shapes=[pltpu.VMEM((CHUNK, kvw), F32)], compiler_params=_params(("arbitrary",), 32),
    )(z, z, z, yb, dy, sinks, g_b)


def _xattn_probs(qh, kh, hd):
    s = _dot_nt(qh, kh) * (hd ** -0.5)
    e = jnp.exp(s - jnp.max(s, axis=-1, keepdims=True))
    return e / jnp.sum(e, axis=-1, keepdims=True)


def _xattn_fwd(q, kv, name, tq=512):
    S, D = q.shape
    M = kv.shape[0]
    hd = D // X_HEADS
    tq = min(tq, S)
    assert S % tq == 0, (S, tq)

    def body(q_ref, kv_ref, o_ref):
        for h in range(X_HEADS):
            cols = slice(h * hd, (h + 1) * hd)
            p = _xattn_probs(q_ref[:, cols], kv_ref[:, cols], hd)
            o_ref[:, cols] = _dot(p.astype(BF16), kv_ref[:, D + h * hd:D + (h + 1) * hd]).astype(BF16)

    row = BS((tq, D), lambda i: (i, 0))
    return pl.pallas_call(
        body, name=name, out_shape=SDS((S, D), BF16), grid=(S // tq,),
        in_specs=[row, BS((M, 2 * D), lambda i: (0, 0))], out_specs=row, compiler_params=_params(("parallel",), 40),
    )(q, kv)


def _xattn_bwd(q, kv, do, name, tq=512):
    S, D = q.shape
    M = kv.shape[0]
    hd = D // X_HEADS
    tq = min(tq, S)
    assert S % tq == 0, (S, tq)

    def body(q_ref, kv_ref, do_ref, dq_ref, dkv_ref):
        @pl.when(pl.program_id(0) == 0)
        def _():
            dkv_ref[...] = jnp.zeros_like(dkv_ref)

        for h in range(X_HEADS):
            cols = slice(h * hd, (h + 1) * hd)
            vcols = slice(D + h * hd, D + (h + 1) * hd)
            qh, kh, vh, doh = q_ref[:, cols], kv_ref[:, cols], kv_ref[:, vcols], do_ref[:, cols]
            p = _xattn_probs(qh, kh, hd)
            dp = _dot_nt(doh, vh)
            ds = (p * (dp - jnp.sum(p * dp, axis=-1, keepdims=True)) * (hd ** -0.5)).astype(BF16)
            dq_ref[:, cols] = _dot(ds, kh).astype(BF16)
            dkv_ref[:, cols] += _dot_tn(ds, qh)
            dkv_ref[:, vcols] += _dot_tn(p.astype(BF16), doh)

    row = BS((tq, D), lambda i: (i, 0))
    full = BS((M, 2 * D), lambda i: (0, 0))
    return pl.pallas_call(
        body, name=name, out_shape=[SDS((S, D), BF16), SDS((M, 2 * D), F32)], grid=(S // tq,),
        in_specs=[row, full, row], out_specs=[row, full], compiler_params=_params(("arbitrary",), 40),
    )(q, kv, do)


def _row_tile(rows, cap):
    best = 8
    for t in range(8, min(rows, cap) + 1, 8):
        if rows % t == 0:
            best = t
    assert rows % best == 0, (rows, cap)
    return best


ADAM_TILE_ELEMS = 256 * 1024


def _adamw(w, m, v, recv, name):
    R, C = w.shape
    tr = _row_tile(R, max(8, ADAM_TILE_ELEMS // C))

    def body(w_ref, m_ref, v_ref, r_ref, g_ref, d_ref, nm_ref, nv_ref):
        g = r_ref[0].astype(F32)
        for s in range(1, N_DEV):
            g = g + r_ref[s].astype(F32)
        mn = ADAM_B1 * m_ref[...] + (1.0 - ADAM_B1) * g
        vn = ADAM_B2 * v_ref[...] + (1.0 - ADAM_B2) * jnp.square(g)
        m_hat = mn / (1.0 - ADAM_B1 ** ADAM_STEP)
        v_hat = vn / (1.0 - ADAM_B2 ** ADAM_STEP)
        g_ref[...] = g
        d_ref[...] = -ADAM_LR * (m_hat / (jnp.sqrt(v_hat) + ADAM_EPS) + ADAM_WD * w_ref[...])
        nm_ref[...] = mn
        nv_ref[...] = vn

    row = BS((tr, C), lambda i: (i, 0))
    return pl.pallas_call(
        body, name=name, out_shape=[SDS((R, C), F32)] * 4, grid=(R // tr,),
        in_specs=[row, row, row, BS((N_DEV, tr, C), lambda i: (0, i, 0))], out_specs=[row] * 4,
        compiler_params=_params(("parallel",), 40),
    )(w, m, v, recv)


def _cols_to_blocks(full):
    r, c = full.shape
    return full.reshape(r, N_DEV, c // N_DEV).transpose(1, 0, 2)


def _blocks_to_cols(blocks):
    n, r, c = blocks.shape
    return blocks.transpose(1, 0, 2).reshape(r, n * c)


def _pack(parts):
    flat = jnp.concatenate([p.reshape(-1).astype(F32) for p in parts])
    pad = (-flat.shape[0]) % (128 * 128)
    return jnp.pad(flat, (0, pad)).reshape(-1, 128)


def kernel(x, mem, g_ffn1, w1_gate, w1_up, w1_down, g_mix, w_in, g_v, w_s, b_s, sinks, g_a_out, g_b_out, w_out, g_x, g_mem, w_xq, w_xkv, w_xo, g_ffn2, w2_gate, w2_up, w2_down, g_final, loss_target, m_g_ffn1, m_w1_gate, m_w1_up, m_w1_down, m_g_mix, m_w_in, m_g_v, m_w_s, m_b_s, m_sinks, m_g_a_out, m_g_b_out, m_w_out, m_g_x, m_g_mem, m_w_xq, m_w_xkv, m_w_xo, m_g_ffn2, m_w2_gate, m_w2_up, m_w2_down, m_g_final, v_g_ffn1, v_w1_gate, v_w1_up, v_w1_down, v_g_mix, v_w_in, v_g_v, v_w_s, v_b_s, v_sinks, v_g_a_out, v_g_b_out, v_w_out, v_g_x, v_g_mem, v_w_xq, v_w_xkv, v_w_xo, v_g_ffn2, v_w2_gate, v_w2_up, v_w2_down, v_g_final):
    w = dict(g_ffn1=g_ffn1, w1_gate=w1_gate, w1_up=w1_up, w1_down=w1_down, g_mix=g_mix, w_in=w_in, g_v=g_v, w_s=w_s,
             b_s=b_s, sinks=sinks, g_a_out=g_a_out, g_b_out=g_b_out, w_out=w_out, g_x=g_x, g_mem=g_mem, w_xq=w_xq,
             w_xkv=w_xkv, w_xo=w_xo, g_ffn2=g_ffn2, w2_gate=w2_gate, w2_up=w2_up, w2_down=w2_down, g_final=g_final)
    mom = dict(g_ffn1=m_g_ffn1, w1_gate=m_w1_gate, w1_up=m_w1_up, w1_down=m_w1_down, g_mix=m_g_mix, w_in=m_w_in,
               g_v=m_g_v, w_s=m_w_s, b_s=m_b_s, sinks=m_sinks, g_a_out=m_g_a_out, g_b_out=m_g_b_out, w_out=m_w_out,
               g_x=m_g_x, g_mem=m_g_mem, w_xq=m_w_xq, w_xkv=m_w_xkv, w_xo=m_w_xo, g_ffn2=m_g_ffn2,
               w2_gate=m_w2_gate, w2_up=m_w2_up, w2_down=m_w2_down, g_final=m_g_final)
    var = dict(g_ffn1=v_g_ffn1, w1_gate=v_w1_gate, w1_up=v_w1_up, w1_down=v_w1_down, g_mix=v_g_mix, w_in=v_w_in,
               g_v=v_g_v, w_s=v_w_s, b_s=v_b_s, sinks=v_sinks, g_a_out=v_g_a_out, g_b_out=v_g_b_out, w_out=v_w_out,
               g_x=v_g_x, g_mem=v_g_mem, w_xq=v_w_xq, w_xkv=v_w_xkv, w_xo=v_w_xo, g_ffn2=v_g_ffn2,
               w2_gate=v_w2_gate, w2_up=v_w2_up, w2_down=v_w2_down, g_final=v_g_final)

    xs, ms, tgt = x[0], mem[0], loss_target[0]
    D = xs.shape[1]
    d_a = w_s.shape[1] * CHUNK
    d_b = D - d_a
    kvw = 2 * KV_HEADS * HEAD_DIM

    shard = {k: w[k][0].astype(BF16) for k in BIG}
    gathered = dict(zip(BIG, _all_gather([shard[k] for k in BIG], "ag_weights")))
    wg1, wu1, wd1 = gathered["w1_gate"], gathered["w1_up"], gathered["w1_down"]
    wg2, wu2, wd2 = gathered["w2_gate"], gathered["w2_up"], gathered["w2_down"]
    win = _blocks_to_cols(gathered["w_in"])
    wxkv = _blocks_to_cols(gathered["w_xkv"])
    wout = gathered["w_out"].reshape(D, D)
    wxq = gathered["w_xq"].reshape(D, D)
    wxo = gathered["w_xo"].reshape(D, D)
    gf = g_final.reshape(1, D)
    ws, ws_t, bs_t = w_s[0], jnp.swapaxes(w_s[0], 1, 2), b_s[0].T

    h1, n1, a1, b1 = _ffn_fwd(xs, g_ffn1, wg1, wu1, wd1, "ffn1_fwd")
    n2 = _rms_fwd(h1, g_mix, "mix_norm")
    z = _matmul(n2, win, "nn", F32, "mm_in", tn=win.shape[1] // 2)
    ya_n = _sgu_fwd(z, g_v, ws, bs_t, g_a_out, "sgu_fwd")
    yb, yb_n = _swa_fwd(z, sinks, g_b_out, "swa_fwd")
    y = jnp.concatenate([ya_n, yb_n], axis=1)
    h2 = _matmul(y, wout, "nn", F32, "mm_out", res=h1)
    hx = _rms_fwd(h2, g_x, "x_norm")
    mn = _rms_fwd(ms, g_mem, "mem_norm")
    q = _matmul(hx, wxq, "nn", BF16, "mm_xq")
    kv = _matmul(mn, wxkv, "nn", BF16, "mm_xkv")
    o = _xattn_fwd(q, kv, "xattn_fwd")
    h3 = _matmul(o, wxo, "nn", F32, "mm_xo", res=h2)
    h4, n4, a2, b2 = _ffn_fwd(h3, g_ffn2, wg2, wu2, wd2, "ffn2_fwd")
    dh4, dg_final, loss_part = _final_loss(h4, gf, tgt, "final_loss")

    grad_big, grad_small = {}, {"g_final": dg_final}
    dn4, df2, da2, db2, s2 = _ffn_bwd_dx(dh4, a2, b2, wg2, wu2, wd2, "ffn2_bwd_dx")
    dh3, grad_small["g_ffn2"] = _rms_bwd(h3, g_ffn2, dn4, dh4, "ffn2_norm_bwd")
    grad_big["w2_gate"], grad_big["w2_up"], grad_big["w2_down"] = _ffn_bwd_dw(n4, df2, da2, db2, s2, "ffn2_bwd_dw")

    do = _matmul(dh3, wxo, "nt", BF16, "mm_xo_dx")
    grad_big["w_xo"] = _matmul(o, dh3, "tn", BF16, "mm_xo_dw", tm=1024, tn=1024).reshape(N_DEV, D // N_DEV, D)
    dq, dkv = _xattn_bwd(q, kv, do, "xattn_bwd")
    dhx = _matmul(dq, wxq, "nt", F32, "mm_xq_dx")
    grad_big["w_xq"] = _matmul(hx, dq, "tn", BF16, "mm_xq_dw", tm=1024, tn=1024).reshape(N_DEV, D // N_DEV, D)
    dmn = _matmul(dkv, wxkv, "nt", F32, "mm_xkv_dx")
    grad_big["w_xkv"] = _cols_to_blocks(_matmul(mn, dkv, "tn", BF16, "mm_xkv_dw", tm=1024, tn=1024))
    dh2, grad_small["g_x"] = _rms_bwd(h2, g_x, dhx, dh3, "x_norm_bwd")
    _, grad_small["g_mem"] = _rms_bwd(ms, g_mem, dmn, None, "mem_norm_bwd")

    dy = _matmul(dh2, wout, "nt", F32, "mm_out_dx")
    grad_big["w_out"] = _matmul(y, dh2, "tn", BF16, "mm_out_dw", tm=1024, tn=1024).reshape(N_DEV, D // N_DEV, D)
    dz_uv, grad_small["w_s"], dbs_t, grad_small["g_v"], grad_small["g_a_out"] = _sgu_bwd(
        z, dy, g_v, ws, ws_t, bs_t, g_a_out, "sgu_bwd")
    grad_small["b_s"] = dbs_t.T
    dq_b, dkv_b, grad_small["sinks"], grad_small["g_b_out"] = _swa_bwd(z, yb, dy, sinks, g_b_out, "swa_bwd")
    dz = jnp.concatenate([dz_uv, dq_b, dkv_b], axis=1)
    dn2 = _matmul(dz, win, "nt", F32, "mm_in_dx")
    grad_big["w_in"] = _cols_to_blocks(_matmul(n2, dz, "tn", BF16, "mm_in_dw", tm=1024, tn=dz.shape[1] // 2))
    dh1, grad_small["g_mix"] = _rms_bwd(h1, g_mix, dn2, dh2, "mix_norm_bwd")

    dn1, df1, da1, db1, s1 = _ffn_bwd_dx(dh1, a1, b1, wg1, wu1, wd1, "ffn1_bwd_dx")
    dx, grad_small["g_ffn1"] = _rms_bwd(xs, g_ffn1, dn1, dh1, "ffn1_norm_bwd")
    grad_big["w1_gate"], grad_big["w1_up"], grad_big["w1_down"] = _ffn_bwd_dw(n1, df1, da1, db1, s1, "ffn1_bwd_dw")

    recv_big = dict(zip(BIG, _reduce_scatter([grad_big[k] for k in BIG], "rs_grads")))
    recv_small = _all_gather([_pack([grad_small[k] for k in SMALL])], "ag_small_grads")[0]

    grads, deltas, new_m, new_v = {}, {}, {}, {}
    for k in BIG:
        shp = w[k].shape
        two_d = shp[1:]
        outs = _adamw(w[k].reshape(two_d), mom[k].reshape(two_d), var[k].reshape(two_d), recv_big[k], "adamw_" + k)
        grads[k], deltas[k], new_m[k], new_v[k] = [t.reshape(shp) for t in outs]
    packed = _adamw(_pack([w[k] for k in SMALL]), _pack([mom[k] for k in SMALL]), _pack([var[k] for k in SMALL]),
                    recv_small, "adamw_small")
    off = 0
    for k in SMALL:
        shp = w[k].shape
        size = 1
        for s in shp:
            size *= s
        for dst, src in zip((grads, deltas, new_m, new_v), packed):
            dst[k] = src.reshape(-1)[off:off + size].reshape(shp)
        off += size

    loss = lax.psum(loss_part[0, 0], AXES)
    return (loss, dx[None], *[grads[k] for k in WEIGHTS], *[deltas[k] for k in WEIGHTS],
            *[new_m[k] for k in WEIGHTS], *[new_v[k] for k in WEIGHTS])
```

```python
import jax
import jax.numpy as jnp
from jax import lax
from jax.experimental import pallas as pl
from jax.experimental.pallas import tpu as pltpu

F32 = jnp.float32
BF16 = jnp.bfloat16
SDS = jax.ShapeDtypeStruct
BS = pl.BlockSpec

N_DEV = 8
AXES = ("x", "y", "c")
EPS = 1e-5
CHUNK = 128
HEAD_DIM = 64
KV_HEADS = 2
X_HEADS = 4
NEG = -1e30
ADAM_LR = 0.001
ADAM_B1 = 0.9
ADAM_B2 = 0.999
ADAM_EPS = 1e-08
ADAM_WD = 0.01
ADAM_STEP = 10
MIB = 1 << 20
WEIGHTS = ['g_ffn1', 'w1_gate', 'w1_up', 'w1_down', 'g_mix', 'w_in', 'g_v', 'w_s', 'b_s', 'sinks', 'g_a_out',
           'g_b_out', 'w_out', 'g_x', 'g_mem', 'w_xq', 'w_xkv', 'w_xo', 'g_ffn2', 'w2_gate', 'w2_up', 'w2_down',
           'g_final']
BIG = ['w1_gate', 'w1_up', 'w1_down', 'w_in', 'w_out', 'w_xq', 'w_xkv', 'w_xo', 'w2_gate', 'w2_up', 'w2_down']
SMALL = [w for w in WEIGHTS if w not in BIG]
FFN1_W = ['w1_gate', 'w1_up', 'w1_down']
FFN2_W = ['w2_gate', 'w2_up', 'w2_down']
MID_W = ['w_in', 'w_out', 'w_xq', 'w_xkv', 'w_xo']
REST_W = MID_W + FFN2_W


def _params(sem=None, vmem_mib=48):
    return pltpu.CompilerParams(dimension_semantics=sem, vmem_limit_bytes=vmem_mib * MIB)


def _dot(a, b):
    return jnp.dot(a, b, preferred_element_type=F32)


def _dot_nt(a, b):
    return lax.dot_general(a, b, (((1,), (1,)), ((), ())), preferred_element_type=F32)


def _dot_tn(a, b):
    return lax.dot_general(a, b, (((0,), (0,)), ((), ())), preferred_element_type=F32)


def _rstd(v):
    return lax.rsqrt(jnp.mean(v * v, axis=-1, keepdims=True) + EPS)


def _norm_bwd(xhat, r, g, d):
    t = d * g
    return r * (t - xhat * jnp.mean(t * xhat, axis=-1, keepdims=True))


def _gelu(v):
    return 0.5 * v * (1.0 + lax.erf(v * 0.7071067811865476))


def _gelu_grad(v):
    return 0.5 * (1.0 + lax.erf(v * 0.7071067811865476)) + v * jnp.exp(-0.5 * v * v) * 0.3989422804014327


def _mesh_pos():
    x, y, c = lax.axis_index("x"), lax.axis_index("y"), lax.axis_index("c")
    return x, y, c, 4 * x + 2 * y + c


def _peer(x, y, c, k):
    px = 1 - x if k & 4 else x
    py = 1 - y if k & 2 else y
    pc = 1 - c if k & 1 else c
    return (px, py, pc), 4 * px + 2 * py + pc


ANY = BS(memory_space=pl.ANY)
DMA_SEM = pltpu.SemaphoreType.DMA
ALL_PEERS = (1, 2, 3, 4, 5, 6, 7)
CHIP_PEERS = (2, 4, 6)
SIBLING = 1


def _remote(src, dst, send, recv, peer):
    return pltpu.make_async_remote_copy(src_ref=src, dst_ref=dst, send_sem=send, recv_sem=recv, device_id=peer,
                                        device_id_type=pl.DeviceIdType.MESH)


class _Exchange:
    def __init__(self, ins, out_shapes, sems, copies, aliases=None):
        self.ins, self.out_shapes, self.sems, self.copies, self.aliases = ins, out_shapes, sems, copies, aliases or {}


def _gather_exchange(shards, peers):
    n, m = len(shards), len(peers)

    def copies(ins, outs, sems):
        send, recv, lsem = sems
        x, y, c, me = _mesh_pos()
        cps = [pltpu.make_async_copy(ins[w], outs[w].at[me], lsem.at[w]) for w in range(n)]
        for w in range(n):
            for j, k in enumerate(peers):
                cps.append(_remote(ins[w], outs[w].at[me], send.at[w, j], recv.at[w, j], _peer(x, y, c, k)[0]))
        return cps

    return _Exchange(shards, [SDS((N_DEV,) + s.shape, s.dtype) for s in shards],
                     [DMA_SEM((n, m)), DMA_SEM((n, m)), DMA_SEM((n,))], copies)


def _sibling_exchange(gathered):
    n = len(gathered)

    def copies(ins, outs, sems):
        send, recv = sems
        x, y, c, me = _mesh_pos()
        slots = [me] + [_peer(x, y, c, k)[1] for k in CHIP_PEERS]
        sib = _peer(x, y, c, SIBLING)[0]
        return [_remote(outs[w].at[s], outs[w].at[s], send.at[w, j], recv.at[w, j], sib)
                for w in range(n) for j, s in enumerate(slots)]

    return _Exchange(gathered, [SDS(g.shape, g.dtype) for g in gathered], [DMA_SEM((n, 4)), DMA_SEM((n, 4))], copies,
                     aliases={w: w for w in range(n)})


def _scatter_exchange(fulls):
    n, m = len(fulls), len(ALL_PEERS)

    def copies(ins, outs, sems):
        send, recv, lsem = sems
        x, y, c, me = _mesh_pos()
        cps = [pltpu.make_async_copy(ins[w].at[me], outs[w].at[me], lsem.at[w]) for w in range(n)]
        for w in range(n):
            for j, k in enumerate(ALL_PEERS):
                peer, p = _peer(x, y, c, k)
                cps.append(_remote(ins[w].at[p], outs[w].at[me], send.at[w, j], recv.at[w, j], peer))
        return cps

    return _Exchange(fulls, [SDS(f.shape, f.dtype) for f in fulls],
                     [DMA_SEM((n, m)), DMA_SEM((n, m)), DMA_SEM((n,))], copies)


def _call(body, *, name, args, in_specs, out_shape, out_specs, grid, sem, vmem_mib, scratch=(), exchange=None,
          first=None, last=None):
    if exchange is None:
        return pl.pallas_call(body, name=name, out_shape=out_shape, grid=grid, in_specs=in_specs,
                              out_specs=out_specs, scratch_shapes=list(scratch),
                              compiler_params=_params(sem, vmem_mib))(*args), []
    ni, no, ns = len(args), len(out_shape), len(scratch)
    xi, xo = len(exchange.ins), len(exchange.out_shapes)

    def hosted(*refs):
        own_in, refs = refs[:ni], refs[ni:]
        x_in, refs = refs[:xi], refs[xi:]
        own_out, refs = refs[:no], refs[no:]
        x_out, refs = refs[:xo], refs[xo:]
        own_scr, x_sem = refs[:ns], refs[ns:]

        @pl.when(first())
        def _():
            for cp in exchange.copies(x_in, x_out, x_sem):
                cp.start()

        body(*own_in, *own_out, *own_scr)

        @pl.when(last())
        def _():
            for cp in exchange.copies(x_in, x_out, x_sem):
                cp.wait()

    outs = pl.pallas_call(
        hosted, name=name, out_shape=list(out_shape) + list(exchange.out_shapes), grid=grid,
        in_specs=list(in_specs) + [ANY] * xi, out_specs=list(out_specs) + [ANY] * xo,
        scratch_shapes=list(scratch) + list(exchange.sems),
        input_output_aliases={ni + a: no + b for a, b in exchange.aliases.items()},
        compiler_params=pltpu.CompilerParams(dimension_semantics=tuple("arbitrary" for _ in grid),
                                             vmem_limit_bytes=vmem_mib * MIB, has_side_effects=True),
    )(*args, *exchange.ins)
    return outs[:no], outs[no:]


def _run_exchange(exchange, name):
    xi, xo = len(exchange.ins), len(exchange.out_shapes)

    def body(*refs):
        cps = exchange.copies(refs[:xi], refs[xi:xi + xo], refs[xi + xo:])
        for cp in cps:
            cp.start()
        for cp in cps:
            cp.wait()

    return pl.pallas_call(
        body, name=name, out_shape=list(exchange.out_shapes), in_specs=[ANY] * xi, out_specs=[ANY] * xo,
        scratch_shapes=list(exchange.sems), input_output_aliases=dict(exchange.aliases),
        compiler_params=pltpu.CompilerParams(has_side_effects=True),
    )(*exchange.ins)


def _rms_fwd(h, g, name, tm=512):
    S, D = h.shape
    tm = min(tm, S)
    assert S % tm == 0, (S, tm)

    def body(h_ref, g_ref, o_ref):
        hv = h_ref[...]
        o_ref[...] = (hv * _rstd(hv) * g_ref[...]).astype(o_ref.dtype)

    return pl.pallas_call(
        body, name=name, out_shape=SDS((S, D), BF16), grid=(S // tm,),
        in_specs=[BS((tm, D), lambda i: (i, 0)), BS((1, D), lambda i: (0, 0))],
        out_specs=BS((tm, D), lambda i: (i, 0)), compiler_params=_params(("parallel",), 32),
    )(h, g)


def _rms_bwd(h, g, dn, dres, name, tm=256):
    S, D = h.shape
    tm = min(tm, S)
    assert S % tm == 0, (S, tm)
    has_res = dres is not None

    def body(*refs):
        if has_res:
            h_ref, g_ref, dn_ref, dres_ref, dh_ref, dg_ref = refs
        else:
            h_ref, g_ref, dn_ref, dh_ref, dg_ref = refs
        hv = h_ref[...]
        r = _rstd(hv)
        xh = hv * r
        d = dn_ref[...].astype(F32)

        @pl.when(pl.program_id(0) == 0)
        def _():
            dg_ref[...] = jnp.zeros_like(dg_ref)

        dg_ref[...] += jnp.sum(d * xh, axis=0, keepdims=True)
        dh = _norm_bwd(xh, r, g_ref[...], d)
        dh_ref[...] = dres_ref[...] + dh if has_res else dh

    row = BS((tm, D), lambda i: (i, 0))
    vec = BS((1, D), lambda i: (0, 0))
    args = (h, g, dn) + ((dres,) if has_res else ())
    return pl.pallas_call(
        body, name=name, out_shape=[SDS((S, D), F32), SDS((1, D), F32)], grid=(S // tm,),
        in_specs=[row, vec, row] + ([row] if has_res else []), out_specs=[row, vec],
        compiler_params=_params(("arbitrary",), 40),
    )(*args)


def _final_loss(h, g, target, name, tm=256):
    S, D = h.shape
    tm = min(tm, S)
    assert S % tm == 0, (S, tm)

    def body(h_ref, g_ref, t_ref, dh_ref, dg_ref, loss_ref):
        hv = h_ref[...]
        r = _rstd(hv)
        xh = hv * r
        gv = g_ref[...]
        e = xh * gv - t_ref[...]

        @pl.when(pl.program_id(0) == 0)
        def _():
            dg_ref[...] = jnp.zeros_like(dg_ref)
            loss_ref[...] = jnp.zeros_like(loss_ref)

        loss_ref[...] += 0.5 * jnp.sum(jnp.mean(e * e, axis=-1, keepdims=True), axis=0, keepdims=True)
        dy = e * (1.0 / D)
        dg_ref[...] += jnp.sum(dy * xh, axis=0, keepdims=True)
        dh_ref[...] = _norm_bwd(xh, r, gv, dy)

    row = BS((tm, D), lambda i: (i, 0))
    vec = BS((1, D), lambda i: (0, 0))
    return pl.pallas_call(
        body, name=name, out_shape=[SDS((S, D), F32), SDS((1, D), F32), SDS((1, 128), F32)], grid=(S // tm,),
        in_specs=[row, vec, row], out_specs=[row, vec, BS((1, 128), lambda i: (0, 0))],
        compiler_params=_params(("arbitrary",), 40),
    )(h, g, target)


def _matmul(a, b, mode, out_dtype, name, res=None, tm=512, tn=512, tk=512):
    if mode == "tn":
        K, M = a.shape
        N = b.shape[1]
        tm, tn, tk = min(tm, M), min(tn, N), min(tk, K)
        assert M % tm == 0 and N % tn == 0 and K % tk == 0, (a.shape, b.shape, tm, tn, tk)
        nk = K // tk

        def body(a_ref, b_ref, o_ref, acc_ref):
            k = pl.program_id(2)

            @pl.when(k == 0)
            def _():
                acc_ref[...] = jnp.zeros_like(acc_ref)

            acc_ref[...] += _dot_tn(a_ref[...].astype(BF16), b_ref[...].astype(BF16))

            @pl.when(k == nk - 1)
            def _():
                o_ref[...] = acc_ref[...].astype(o_ref.dtype)

        return pl.pallas_call(
            body, name=name, out_shape=SDS((M, N), out_dtype), grid=(M // tm, N // tn, nk),
            in_specs=[BS((tk, tm), lambda i, j, k: (k, i)), BS((tk, tn), lambda i, j, k: (k, j))],
            out_specs=BS((tm, tn), lambda i, j, k: (i, j)), scratch_shapes=[pltpu.VMEM((tm, tn), F32)],
            compiler_params=_params(("parallel", "parallel", "arbitrary"), 48),
        )(a, b)

    M, K = a.shape
    N = b.shape[1] if mode == "nn" else b.shape[0]
    tm, tn = min(tm, M), min(tn, N)
    assert M % tm == 0 and N % tn == 0, (a.shape, b.shape, tm, tn)
    has_res = res is not None

    def body(*refs):
        if has_res:
            a_ref, b_ref, r_ref, o_ref = refs
        else:
            a_ref, b_ref, o_ref = refs
        av, bv = a_ref[...].astype(BF16), b_ref[...].astype(BF16)
        acc = _dot(av, bv) if mode == "nn" else _dot_nt(av, bv)
        if has_res:
            acc = acc + r_ref[...]
        o_ref[...] = acc.astype(o_ref.dtype)

    b_spec = BS((K, tn), lambda i, j: (0, j)) if mode == "nn" else BS((tn, K), lambda i, j: (j, 0))
    o_spec = BS((tm, tn), lambda i, j: (i, j))
    return pl.pallas_call(
        body, name=name, out_shape=SDS((M, N), out_dtype), grid=(M // tm, N // tn),
        in_specs=[BS((tm, K), lambda i, j: (i, 0)), b_spec] + ([o_spec] if has_res else []), out_specs=o_spec,
        compiler_params=_params(("parallel", "parallel"), 48),
    )(*((a, b) + ((res,) if has_res else ())))


def _ffn_fwd(h, g, wg, wu, wd, name, tm=512, exchange=None):
    S, D = h.shape
    nb, _, Fs = wg.shape
    tm = min(tm, S)
    assert S % tm == 0, (S, tm)

    def body(h_ref, g_ref, wg_ref, wu_ref, wd_ref, o_ref, n_ref, a_ref, b_ref):
        j = pl.program_id(1)

        @pl.when(j == 0)
        def _():
            hv = h_ref[...]
            n_ref[...] = (hv * _rstd(hv) * g_ref[...]).astype(BF16)
            o_ref[...] = jnp.zeros_like(o_ref)

        n = n_ref[...]
        a = _dot(n, wg_ref[...]).astype(BF16)
        b = _dot(n, wu_ref[...]).astype(BF16)
        a_ref[...] = a
        b_ref[...] = b
        a, b = a.astype(F32), b.astype(F32)
        s = (a * jax.nn.sigmoid(a) * b).astype(BF16)
        o_ref[...] += _dot(s, wd_ref[...])

        @pl.when(j == nb - 1)
        def _():
            o_ref[...] = h_ref[...] + 0.5 * o_ref[...]

    row = BS((tm, D), lambda i, j: (i, 0))
    wcol = BS((None, D, Fs), lambda i, j: (j, 0, 0))
    act = BS((None, tm, Fs), lambda i, j: (j, i, 0))
    ni = S // tm
    return _call(
        body, name=name, args=(h, g, wg, wu, wd),
        out_shape=[SDS((S, D), F32), SDS((S, D), BF16), SDS((nb, S, Fs), BF16), SDS((nb, S, Fs), BF16)],
        grid=(ni, nb),
        in_specs=[row, BS((1, D), lambda i, j: (0, 0)), wcol, wcol, BS((None, Fs, D), lambda i, j: (j, 0, 0))],
        out_specs=[row, row, act, act], sem=("parallel", "arbitrary"), vmem_mib=56, exchange=exchange,
        first=lambda: (pl.program_id(0) == 0) & (pl.program_id(1) == 0),
        last=lambda: (pl.program_id(0) == ni - 1) & (pl.program_id(1) == nb - 1))


def _ffn_bwd_dx(dh, a, b, wg, wu, wd, name, tm=512, exchange=None):
    S, D = dh.shape
    nb, _, Fs = wg.shape
    tm = min(tm, S)
    assert S % tm == 0, (S, tm)

    def body(dh_ref, a_ref, b_ref, wg_ref, wu_ref, wd_ref, dn_ref, df_ref, da_ref, db_ref, s_ref):
        j = pl.program_id(1)

        @pl.when(j == 0)
        def _():
            df_ref[...] = (0.5 * dh_ref[...]).astype(BF16)
            dn_ref[...] = jnp.zeros_like(dn_ref)

        ds = _dot_nt(df_ref[...], wd_ref[...])
        av, bv = a_ref[...].astype(F32), b_ref[...].astype(F32)
        sig = jax.nn.sigmoid(av)
        sl = av * sig
        da = (ds * bv * (sig * (1.0 + av * (1.0 - sig)))).astype(BF16)
        db = (ds * sl).astype(BF16)
        da_ref[...] = da
        db_ref[...] = db
        s_ref[...] = (sl * bv).astype(BF16)
        dn_ref[...] += _dot_nt(da, wg_ref[...]) + _dot_nt(db, wu_ref[...])

    row = BS((tm, D), lambda i, j: (i, 0))
    wcol = BS((None, D, Fs), lambda i, j: (j, 0, 0))
    act = BS((None, tm, Fs), lambda i, j: (j, i, 0))
    ni = S // tm
    return _call(
        body, name=name, args=(dh, a, b, wg, wu, wd),
        out_shape=[SDS((S, D), F32), SDS((S, D), BF16)] + [SDS((nb, S, Fs), BF16)] * 3,
        grid=(ni, nb),
        in_specs=[row, act, act, wcol, wcol, BS((None, Fs, D), lambda i, j: (j, 0, 0))],
        out_specs=[row, row, act, act, act], sem=("parallel", "arbitrary"), vmem_mib=56, exchange=exchange,
        first=lambda: (pl.program_id(0) == 0) & (pl.program_id(1) == 0),
        last=lambda: (pl.program_id(0) == ni - 1) & (pl.program_id(1) == nb - 1))


def _ffn_bwd_dw(n, df, da, db, s, name, tk=512, exchange=None):
    S, D = n.shape
    nb, _, Fs = da.shape
    tk = min(tk, S)
    assert S % tk == 0, (S, tk)
    nk = S // tk

    def body(n_ref, df_ref, da_ref, db_ref, s_ref, dwg_ref, dwu_ref, dwd_ref, ag_ref, au_ref, ad_ref):
        k = pl.program_id(1)

        @pl.when(k == 0)
        def _():
            ag_ref[...] = jnp.zeros_like(ag_ref)
            au_ref[...] = jnp.zeros_like(au_ref)
            ad_ref[...] = jnp.zeros_like(ad_ref)

        nv = n_ref[...]
        ag_ref[...] += _dot_tn(nv, da_ref[...])
        au_ref[...] += _dot_tn(nv, db_ref[...])
        ad_ref[...] += _dot_tn(s_ref[...], df_ref[...])

        @pl.when(k == nk - 1)
        def _():
            dwg_ref[...] = ag_ref[...].astype(BF16)
            dwu_ref[...] = au_ref[...].astype(BF16)
            dwd_ref[...] = ad_ref[...].astype(BF16)

    row = BS((tk, D), lambda j, k: (k, 0))
    act = BS((None, tk, Fs), lambda j, k: (j, k, 0))
    wcol = BS((None, D, Fs), lambda j, k: (j, 0, 0))
    wrow = BS((None, Fs, D), lambda j, k: (j, 0, 0))
    return _call(
        body, name=name, args=(n, df, da, db, s),
        out_shape=[SDS((nb, D, Fs), BF16), SDS((nb, D, Fs), BF16), SDS((nb, Fs, D), BF16)],
        grid=(nb, nk), in_specs=[row, row, act, act, act], out_specs=[wcol, wcol, wrow],
        scratch=[pltpu.VMEM((D, Fs), F32), pltpu.VMEM((D, Fs), F32), pltpu.VMEM((Fs, D), F32)],
        sem=("parallel", "arbitrary"), vmem_mib=56, exchange=exchange,
        first=lambda: (pl.program_id(0) == 0) & (pl.program_id(1) == 0),
        last=lambda: (pl.program_id(0) == nb - 1) & (pl.program_id(1) == nk - 1))


def _sgu_parts(z, gv, ws_ref, bst_ref, groups):
    da = z.shape[1] // 2
    zu, zv = z[:, :da], z[:, da:]
    u, v = _gelu(zu), _gelu(zv)
    rv = _rstd(v)
    vhat = v * rv
    vn = (vhat * gv).astype(BF16)
    tri = lax.broadcasted_iota(jnp.int32, (CHUNK, CHUNK), 0) >= lax.broadcasted_iota(jnp.int32, (CHUNK, CHUNK), 1)
    pieces = []
    for g in range(groups):
        w = jnp.where(tri, ws_ref[g], 0.0).astype(BF16)
        pieces.append(_dot(w, vn[:, g * CHUNK:(g + 1) * CHUNK]) + bst_ref[:, g:g + 1])
    sv = jnp.concatenate(pieces, axis=1)
    return dict(zu=zu, zv=zv, u=u, rv=rv, vhat=vhat, vn=vn, sv=sv, tri=tri)


def _sgu_fwd(z, g_v, w_s, b_st, g_a, name):
    S = z.shape[0]
    groups = w_s.shape[0]
    da = groups * CHUNK

    def body(z_ref, gv_ref, ws_ref, bst_ref, ga_ref, o_ref):
        p = _sgu_parts(z_ref[...], gv_ref[...], ws_ref, bst_ref, groups)
        ya = p["u"] * p["sv"]
        o_ref[...] = (ya * _rstd(ya) * ga_ref[...]).astype(BF16)

    vec = BS((1, da), lambda i: (0, 0))
    return pl.pallas_call(
        body, name=name, out_shape=SDS((S, da), BF16), grid=(S // CHUNK,),
        in_specs=[BS((CHUNK, 2 * da), lambda i: (i, 0)), vec, BS((groups, CHUNK, CHUNK), lambda i: (0, 0, 0)),
                  BS((CHUNK, groups), lambda i: (0, 0)), vec],
        out_specs=BS((CHUNK, da), lambda i: (i, 0)), compiler_params=_params(("parallel",), 32),
    )(z, g_v, w_s, b_st, g_a)


def _sgu_bwd(z, dy, g_v, w_s, w_st, b_st, g_a, name):
    S = z.shape[0]
    groups = w_s.shape[0]
    da = groups * CHUNK

    def body(z_ref, dy_ref, gv_ref, ws_ref, wst_ref, bst_ref, ga_ref, dz_ref, dws_ref, dbst_ref, dgv_ref, dga_ref):
        @pl.when(pl.program_id(0) == 0)
        def _():
            dws_ref[...] = jnp.zeros_like(dws_ref)
            dbst_ref[...] = jnp.zeros_like(dbst_ref)
            dgv_ref[...] = jnp.zeros_like(dgv_ref)
            dga_ref[...] = jnp.zeros_like(dga_ref)

        gv = gv_ref[...]
        p = _sgu_parts(z_ref[...], gv, ws_ref, bst_ref, groups)
        u, sv, tri = p["u"], p["sv"], p["tri"]
        ya = u * sv
        ra = _rstd(ya)
        yhat = ya * ra
        d = dy_ref[...]
        dga_ref[...] += jnp.sum(d * yhat, axis=0, keepdims=True)
        dya = _norm_bwd(yhat, ra, ga_ref[...], d)
        du = dya * sv
        dsv = dya * u
        dsv_b = dsv.astype(BF16)
        tri_t = (lax.broadcasted_iota(jnp.int32, (CHUNK, CHUNK), 0)
                 <= lax.broadcasted_iota(jnp.int32, (CHUNK, CHUNK), 1))
        lane = lax.broadcasted_iota(jnp.int32, (CHUNK, groups), 1)
        dvn = []
        dbs = jnp.zeros((CHUNK, groups), F32)
        for g in range(groups):
            cols = slice(g * CHUNK, (g + 1) * CHUNK)
            dbs = dbs + jnp.where(lane == g, jnp.sum(dsv[:, cols], axis=1, keepdims=True), 0.0)
            dws_ref[g] += jnp.where(tri, _dot_nt(dsv_b[:, cols], p["vn"][:, cols]), 0.0)
            wt = jnp.where(tri_t, wst_ref[g], 0.0).astype(BF16)
            dvn.append(_dot(wt, dsv_b[:, cols]))
        dbst_ref[...] += dbs
        dvn = jnp.concatenate(dvn, axis=1)
        dgv_ref[...] += jnp.sum(dvn * p["vhat"], axis=0, keepdims=True)
        dv = _norm_bwd(p["vhat"], p["rv"], gv, dvn)
        dz_ref[...] = jnp.concatenate([du * _gelu_grad(p["zu"]), dv * _gelu_grad(p["zv"])], axis=1)

    vec = BS((1, da), lambda i: (0, 0))
    wsq = BS((groups, CHUNK, CHUNK), lambda i: (0, 0, 0))
    bsq = BS((CHUNK, groups), lambda i: (0, 0))
    return pl.pallas_call(
        body, name=name,
        out_shape=[SDS((S, 2 * da), F32), SDS((groups, CHUNK, CHUNK), F32), SDS((CHUNK, groups), F32),
                   SDS((1, da), F32), SDS((1, da), F32)],
        grid=(S // CHUNK,),
        in_specs=[BS((CHUNK, 2 * da), lambda i: (i, 0)), BS((CHUNK, da), lambda i: (i, 0)), vec, wsq, wsq, bsq, vec],
        out_specs=[BS((CHUNK, 2 * da), lambda i: (i, 0)), wsq, bsq, vec, vec],
        compiler_params=_params(("arbitrary",), 32),
    )(z, dy, g_v, w_s, w_st, b_st, g_a)


def _swa_mask(i, group):
    row = lax.broadcasted_iota(jnp.int32, (group * CHUNK, 2 * CHUNK), 0) & (CHUNK - 1)
    col = lax.broadcasted_iota(jnp.int32, (group * CHUNK, 2 * CHUNK), 1)
    d = row + CHUNK - col
    return (d >= 0) & (d < CHUNK) & jnp.logical_or(i > 0, col >= CHUNK)


def _stack_heads(t, g, group):
    return jnp.concatenate([t[:, h * HEAD_DIM:(h + 1) * HEAD_DIM] for h in range(g * group, (g + 1) * group)], axis=0)


def _unstack_heads(stacked, group):
    return [stacked[h * CHUNK:(h + 1) * CHUNK] for h in range(group)]


def _swa_probs(qh, kh, sink, mask):
    s = jnp.where(mask, _dot_nt(qh, kh) * (HEAD_DIM ** -0.5), NEG)
    m = jnp.maximum(jnp.max(s, axis=-1, keepdims=True), sink)
    e = jnp.exp(s - m)
    es = jnp.exp(sink - m)
    inv = 1.0 / (jnp.sum(e, axis=-1, keepdims=True) + es)
    return e * inv, es * inv


def _swa_specs(db, nblk, clamp):
    kvw = 2 * KV_HEADS * HEAD_DIM
    cur = (lambda i: jnp.minimum(i, nblk - 1)) if clamp else (lambda i: i)
    q_spec = BS((CHUNK, db), lambda i: (cur(i), 2))
    kc_spec = BS((CHUNK, kvw), lambda i: (cur(i), 3 * db // kvw))
    kp_spec = BS((CHUNK, kvw), lambda i: (jnp.maximum(cur(i) - 1, 0), 3 * db // kvw))
    return q_spec, kc_spec, kp_spec


def _swa_fwd(z, sink_rows, g_b, name):
    S = z.shape[0]
    db = g_b.shape[1]
    heads = db // HEAD_DIM
    group = heads // KV_HEADS
    nblk = S // CHUNK

    def body(q_ref, kc_ref, kp_ref, sk_ref, gb_ref, yb_ref, ybn_ref):
        mask = _swa_mask(pl.program_id(0), group)
        q = q_ref[...].astype(BF16)
        kv = jnp.concatenate([kp_ref[...], kc_ref[...]], axis=0).astype(BF16)
        outs = []
        for g in range(KV_HEADS):
            kg = kv[:, g * HEAD_DIM:(g + 1) * HEAD_DIM]
            vg = kv[:, (KV_HEADS + g) * HEAD_DIM:(KV_HEADS + g + 1) * HEAD_DIM]
            rows = slice(g * group * CHUNK, (g + 1) * group * CHUNK)
            p, _ = _swa_probs(_stack_heads(q, g, group), kg, sk_ref[rows, :], mask)
            outs += _unstack_heads(_dot(p.astype(BF16), vg), group)
        yb = jnp.concatenate(outs, axis=1)
        yb_ref[...] = yb
        ybn_ref[...] = (yb * _rstd(yb) * gb_ref[...]).astype(BF16)

    q_spec, kc_spec, kp_spec = _swa_specs(db, nblk, False)
    out = BS((CHUNK, db), lambda i: (i, 0))
    return pl.pallas_call(
        body, name=name, out_shape=[SDS((S, db), F32), SDS((S, db), BF16)], grid=(nblk,),
        in_specs=[q_spec, kc_spec, kp_spec, BS((heads * CHUNK, 1), lambda i: (0, 0)), BS((1, db), lambda i: (0, 0))],
        out_specs=[out, out], compiler_params=_params(("parallel",), 32),
    )(z, z, z, sink_rows, g_b)


def _swa_bwd(z, yb, dy, sink_rows, g_b, name):
    S = z.shape[0]
    db = g_b.shape[1]
    heads = db // HEAD_DIM
    group = heads // KV_HEADS
    nblk = S // CHUNK
    kvw = 2 * KV_HEADS * HEAD_DIM

    def body(q_ref, kc_ref, kp_ref, yb_ref, dy_ref, sk_ref, gb_ref, dq_ref, dkv_ref, dsk_ref, dgb_ref, carry_ref):
        i = pl.program_id(0)

        @pl.when(i == 0)
        def _():
            carry_ref[...] = jnp.zeros_like(carry_ref)
            dsk_ref[...] = jnp.zeros_like(dsk_ref)
            dgb_ref[...] = jnp.zeros_like(dgb_ref)

        @pl.when(i < nblk)
        def _():
            mask = _swa_mask(i, group)
            yb = yb_ref[...]
            rb = _rstd(yb)
            yhat = yb * rb
            d = dy_ref[...]
            dgb_ref[...] += jnp.sum(d * yhat, axis=0, keepdims=True)
            do = _norm_bwd(yhat, rb, gb_ref[...], d).astype(BF16)
            q = q_ref[...].astype(BF16)
            kv = jnp.concatenate([kp_ref[...], kc_ref[...]], axis=0).astype(BF16)
            dqs, dk, dv = [], [], []
            lane = lax.broadcasted_iota(jnp.int32, (1, heads), 1)
            dsinks = jnp.zeros((1, heads), F32)
            for g in range(KV_HEADS):
                kg = kv[:, g * HEAD_DIM:(g + 1) * HEAD_DIM]
                vg = kv[:, (KV_HEADS + g) * HEAD_DIM:(KV_HEADS + g + 1) * HEAD_DIM]
                rows = slice(g * group * CHUNK, (g + 1) * group * CHUNK)
                qg, dog = _stack_heads(q, g, group), _stack_heads(do, g, group)
                p, ps = _swa_probs(qg, kg, sk_ref[rows, :], mask)
                dp = _dot_nt(dog, vg)
                dr = jnp.sum(p * dp, axis=-1, keepdims=True)
                ds = (p * (dp - dr) * (HEAD_DIM ** -0.5)).astype(BF16)
                for h, t in enumerate(_unstack_heads(ps * dr, group)):
                    dsinks = dsinks - jnp.where(lane == g * group + h, jnp.sum(t, axis=0, keepdims=True), 0.0)
                dqs += _unstack_heads(_dot(ds, kg), group)
                dk.append(_dot_tn(ds, qg))
                dv.append(_dot_tn(p.astype(BF16), dog))
            dq_ref[...] = jnp.concatenate(dqs, axis=1)
            dsk_ref[...] += dsinks
            contrib = jnp.concatenate(dk + dv, axis=1)
            dkv_ref[...] = carry_ref[...] + contrib[:CHUNK]
            carry_ref[...] = contrib[CHUNK:]

        @pl.when(i == nblk)
        def _():
            dkv_ref[...] = carry_ref[...]

    q_spec, kc_spec, kp_spec = _swa_specs(db, nblk, True)
    cur = BS((CHUNK, db), lambda i: (jnp.minimum(i, nblk - 1), 0))
    return pl.pallas_call(
        body, name=name,
        out_shape=[SDS((S, db), F32), SDS((S, kvw), F32), SDS((1, heads), F32), SDS((1, db), F32)],
        grid=(nblk + 1,),
        in_specs=[q_spec, kc_spec, kp_spec, cur, BS((CHUNK, db), lambda i: (jnp.minimum(i, nblk - 1), 1)),
                  BS((heads * CHUNK, 1), lambda i: (0, 0)), BS((1, db), lambda i: (0, 0))],
        out_specs=[cur, BS((CHUNK, kvw), lambda i: (jnp.maximum(i - 1, 0), 0)), BS((1, heads), lambda i: (0, 0)),
                   BS((1, db), lambda i: (0, 0))],
        scratch_shapes=[pltpu.VMEM((CHUNK, kvw), F32)], compiler_params=_params(("arbitrary",), 32),
    )(z, z, z, yb, dy, sink_rows, g_b)


def _xattn_probs(qh, kh, hd):
    s = _dot_nt(qh, kh) * (hd ** -0.5)
    e = jnp.exp(s - jnp.max(s, axis=-1, keepdims=True))
    return e / jnp.sum(e, axis=-1, keepdims=True)


def _xattn_fwd(q, kv, name, tq=512):
    S, D = q.shape
    M = kv.shape[0]
    hd = D // X_HEADS
    tq = min(tq, S)
    assert S % tq == 0, (S, tq)

    def body(q_ref, kv_ref, o_ref):
        for h in range(X_HEADS):
            cols = slice(h * hd, (h + 1) * hd)
            p = _xattn_probs(q_ref[:, cols], kv_ref[:, cols], hd)
            o_ref[:, cols] = _dot(p.astype(BF16), kv_ref[:, D + h * hd:D + (h + 1) * hd]).astype(BF16)

    row = BS((tq, D), lambda i: (i, 0))
    return pl.pallas_call(
        body, name=name, out_shape=SDS((S, D), BF16), grid=(S // tq,),
        in_specs=[row, BS((M, 2 * D), lambda i: (0, 0))], out_specs=row, compiler_params=_params(("parallel",), 40),
    )(q, kv)


def _xattn_bwd(q, kv, do, name, tq=512):
    S, D = q.shape
    M = kv.shape[0]
    hd = D // X_HEADS
    tq = min(tq, S)
    assert S % tq == 0, (S, tq)

    def body(q_ref, kv_ref, do_ref, dq_ref, dkv_ref):
        @pl.when(pl.program_id(0) == 0)
        def _():
            dkv_ref[...] = jnp.zeros_like(dkv_ref)

        for h in range(X_HEADS):
            cols = slice(h * hd, (h + 1) * hd)
            vcols = slice(D + h * hd, D + (h + 1) * hd)
            qh, kh, vh, doh = q_ref[:, cols], kv_ref[:, cols], kv_ref[:, vcols], do_ref[:, cols]
            p = _xattn_probs(qh, kh, hd)
            dp = _dot_nt(doh, vh)
            ds = (p * (dp - jnp.sum(p * dp, axis=-1, keepdims=True)) * (hd ** -0.5)).astype(BF16)
            dq_ref[:, cols] = _dot(ds, kh).astype(BF16)
            dkv_ref[:, cols] += _dot_tn(ds, qh)
            dkv_ref[:, vcols] += _dot_tn(p.astype(BF16), doh)

    row = BS((tq, D), lambda i: (i, 0))
    full = BS((M, 2 * D), lambda i: (0, 0))
    return pl.pallas_call(
        body, name=name, out_shape=[SDS((S, D), BF16), SDS((M, 2 * D), F32)], grid=(S // tq,),
        in_specs=[row, full, row], out_specs=[row, full], compiler_params=_params(("arbitrary",), 40),
    )(q, kv, do)


def _row_tile(rows, cap):
    best = 8
    for t in range(8, min(rows, cap) + 1, 8):
        if rows % t == 0:
            best = t
    assert rows % best == 0, (rows, cap)
    return best


ADAM_TILE_ELEMS = 256 * 1024


def _adamw(w, m, v, recv, name):
    R, C = w.shape
    tr = _row_tile(R, max(8, ADAM_TILE_ELEMS // C))

    def body(w_ref, m_ref, v_ref, r_ref, g_ref, d_ref, nm_ref, nv_ref):
        g = r_ref[0].astype(F32)
        for s in range(1, N_DEV):
            g = g + r_ref[s].astype(F32)
        mn = ADAM_B1 * m_ref[...] + (1.0 - ADAM_B1) * g
        vn = ADAM_B2 * v_ref[...] + (1.0 - ADAM_B2) * jnp.square(g)
        m_hat = mn / (1.0 - ADAM_B1 ** ADAM_STEP)
        v_hat = vn / (1.0 - ADAM_B2 ** ADAM_STEP)
        g_ref[...] = g
        d_ref[...] = -ADAM_LR * (m_hat / (jnp.sqrt(v_hat) + ADAM_EPS) + ADAM_WD * w_ref[...])
        nm_ref[...] = mn
        nv_ref[...] = vn

    row = BS((tr, C), lambda i: (i, 0))
    return pl.pallas_call(
        body, name=name, out_shape=[SDS((R, C), F32)] * 4, grid=(R // tr,),
        in_specs=[row, row, row, BS((N_DEV, tr, C), lambda i: (0, i, 0))], out_specs=[row] * 4,
        compiler_params=_params(("parallel",), 40),
    )(w, m, v, recv)


def _cols_to_blocks(full):
    r, c = full.shape
    return full.reshape(r, N_DEV, c // N_DEV).transpose(1, 0, 2)


def _blocks_to_cols(blocks):
    n, r, c = blocks.shape
    return blocks.transpose(1, 0, 2).reshape(r, n * c)


def _pack(parts):
    flat = jnp.concatenate([p.reshape(-1).astype(F32) for p in parts])
    pad = (-flat.shape[0]) % (128 * 128)
    return jnp.pad(flat, (0, pad)).reshape(-1, 128)


def kernel(x, mem, g_ffn1, w1_gate, w1_up, w1_down, g_mix, w_in, g_v, w_s, b_s, sinks, g_a_out, g_b_out, w_out, g_x, g_mem, w_xq, w_xkv, w_xo, g_ffn2, w2_gate, w2_up, w2_down, g_final, loss_target, m_g_ffn1, m_w1_gate, m_w1_up, m_w1_down, m_g_mix, m_w_in, m_g_v, m_w_s, m_b_s, m_sinks, m_g_a_out, m_g_b_out, m_w_out, m_g_x, m_g_mem, m_w_xq, m_w_xkv, m_w_xo, m_g_ffn2, m_w2_gate, m_w2_up, m_w2_down, m_g_final, v_g_ffn1, v_w1_gate, v_w1_up, v_w1_down, v_g_mix, v_w_in, v_g_v, v_w_s, v_b_s, v_sinks, v_g_a_out, v_g_b_out, v_w_out, v_g_x, v_g_mem, v_w_xq, v_w_xkv, v_w_xo, v_g_ffn2, v_w2_gate, v_w2_up, v_w2_down, v_g_final):
    w = dict(g_ffn1=g_ffn1, w1_gate=w1_gate, w1_up=w1_up, w1_down=w1_down, g_mix=g_mix, w_in=w_in, g_v=g_v, w_s=w_s,
             b_s=b_s, sinks=sinks, g_a_out=g_a_out, g_b_out=g_b_out, w_out=w_out, g_x=g_x, g_mem=g_mem, w_xq=w_xq,
             w_xkv=w_xkv, w_xo=w_xo, g_ffn2=g_ffn2, w2_gate=w2_gate, w2_up=w2_up, w2_down=w2_down, g_final=g_final)
    mom = dict(g_ffn1=m_g_ffn1, w1_gate=m_w1_gate, w1_up=m_w1_up, w1_down=m_w1_down, g_mix=m_g_mix, w_in=m_w_in,
               g_v=m_g_v, w_s=m_w_s, b_s=m_b_s, sinks=m_sinks, g_a_out=m_g_a_out, g_b_out=m_g_b_out, w_out=m_w_out,
               g_x=m_g_x, g_mem=m_g_mem, w_xq=m_w_xq, w_xkv=m_w_xkv, w_xo=m_w_xo, g_ffn2=m_g_ffn2,
               w2_gate=m_w2_gate, w2_up=m_w2_up, w2_down=m_w2_down, g_final=m_g_final)
    var = dict(g_ffn1=v_g_ffn1, w1_gate=v_w1_gate, w1_up=v_w1_up, w1_down=v_w1_down, g_mix=v_g_mix, w_in=v_w_in,
               g_v=v_g_v, w_s=v_w_s, b_s=v_b_s, sinks=v_sinks, g_a_out=v_g_a_out, g_b_out=v_g_b_out, w_out=v_w_out,
               g_x=v_g_x, g_mem=v_g_mem, w_xq=v_w_xq, w_xkv=v_w_xkv, w_xo=v_w_xo, g_ffn2=v_g_ffn2,
               w2_gate=v_w2_gate, w2_up=v_w2_up, w2_down=v_w2_down, g_final=v_g_final)

    xs, ms, tgt = x[0], mem[0], loss_target[0]
    D = xs.shape[1]
    d_a = w_s.shape[1] * CHUNK
    d_b = D - d_a
    kvw = 2 * KV_HEADS * HEAD_DIM

    shard = {k: w[k][0].astype(BF16) for k in BIG}
    part = _run_exchange(_gather_exchange([shard[k] for k in FFN1_W], CHIP_PEERS), "ag_ffn1_ici")
    wg1, wu1, wd1 = _run_exchange(_sibling_exchange(part), "ag_ffn1_d2d")
    gf = g_final.reshape(1, D)
    ws, ws_t, bs_t = w_s[0], jnp.swapaxes(w_s[0], 1, 2), b_s[0].T
    sink_rows = jnp.repeat(sinks.reshape(-1), CHUNK).reshape(-1, 1)

    (h1, n1, a1, b1), part = _ffn_fwd(xs, g_ffn1, wg1, wu1, wd1, "ffn1_fwd",
                                      exchange=_gather_exchange([shard[k] for k in REST_W], CHIP_PEERS))
    gathered = dict(zip(REST_W, _run_exchange(_sibling_exchange(part), "ag_rest_d2d")))
    wg2, wu2, wd2 = gathered["w2_gate"], gathered["w2_up"], gathered["w2_down"]
    win = _blocks_to_cols(gathered["w_in"])
    wxkv = _blocks_to_cols(gathered["w_xkv"])
    wout = gathered["w_out"].reshape(D, D)
    wxq = gathered["w_xq"].reshape(D, D)
    wxo = gathered["w_xo"].reshape(D, D)
    n2 = _rms_fwd(h1, g_mix, "mix_norm")
    z = _matmul(n2, win, "nn", F32, "mm_in", tn=win.shape[1] // 2)
    ya_n = _sgu_fwd(z, g_v, ws, bs_t, g_a_out, "sgu_fwd")
    yb, yb_n = _swa_fwd(z, sink_rows, g_b_out, "swa_fwd")
    y = jnp.concatenate([ya_n, yb_n], axis=1)
    h2 = _matmul(y, wout, "nn", F32, "mm_out", res=h1)
    hx = _rms_fwd(h2, g_x, "x_norm")
    mn = _rms_fwd(ms, g_mem, "mem_norm")
    q = _matmul(hx, wxq, "nn", BF16, "mm_xq")
    kv = _matmul(mn, wxkv, "nn", BF16, "mm_xkv")
    o = _xattn_fwd(q, kv, "xattn_fwd")
    h3 = _matmul(o, wxo, "nn", F32, "mm_xo", res=h2)
    (h4, n4, a2, b2), _ = _ffn_fwd(h3, g_ffn2, wg2, wu2, wd2, "ffn2_fwd")
    dh4, dg_final, loss_part = _final_loss(h4, gf, tgt, "final_loss")

    grad_big, grad_small = {}, {"g_final": dg_final}
    (dn4, df2, da2, db2, s2), _ = _ffn_bwd_dx(dh4, a2, b2, wg2, wu2, wd2, "ffn2_bwd_dx")
    dh3, grad_small["g_ffn2"] = _rms_bwd(h3, g_ffn2, dn4, dh4, "ffn2_norm_bwd")
    (grad_big["w2_gate"], grad_big["w2_up"], grad_big["w2_down"]), _ = _ffn_bwd_dw(
        n4, df2, da2, db2, s2, "ffn2_bwd_dw")

    do = _matmul(dh3, wxo, "nt", BF16, "mm_xo_dx")
    grad_big["w_xo"] = _matmul(o, dh3, "tn", BF16, "mm_xo_dw", tm=1024, tn=1024).reshape(N_DEV, D // N_DEV, D)
    dq, dkv = _xattn_bwd(q, kv, do, "xattn_bwd")
    dhx = _matmul(dq, wxq, "nt", F32, "mm_xq_dx")
    grad_big["w_xq"] = _matmul(hx, dq, "tn", BF16, "mm_xq_dw", tm=1024, tn=1024).reshape(N_DEV, D // N_DEV, D)
    dmn = _matmul(dkv, wxkv, "nt", F32, "mm_xkv_dx")
    grad_big["w_xkv"] = _cols_to_blocks(_matmul(mn, dkv, "tn", BF16, "mm_xkv_dw", tm=1024, tn=1024))
    dh2, grad_small["g_x"] = _rms_bwd(h2, g_x, dhx, dh3, "x_norm_bwd")
    _, grad_small["g_mem"] = _rms_bwd(ms, g_mem, dmn, None, "mem_norm_bwd")

    dy = _matmul(dh2, wout, "nt", F32, "mm_out_dx")
    grad_big["w_out"] = _matmul(y, dh2, "tn", BF16, "mm_out_dw", tm=1024, tn=1024).reshape(N_DEV, D // N_DEV, D)
    dz_uv, grad_small["w_s"], dbs_t, grad_small["g_v"], grad_small["g_a_out"] = _sgu_bwd(
        z, dy, g_v, ws, ws_t, bs_t, g_a_out, "sgu_bwd")
    grad_small["b_s"] = dbs_t.T
    dq_b, dkv_b, grad_small["sinks"], grad_small["g_b_out"] = _swa_bwd(z, yb, dy, sink_rows, g_b_out, "swa_bwd")
    dz = jnp.concatenate([dz_uv, dq_b, dkv_b], axis=1)
    dn2 = _matmul(dz, win, "nt", F32, "mm_in_dx")
    grad_big["w_in"] = _cols_to_blocks(_matmul(n2, dz, "tn", BF16, "mm_in_dw", tm=1024, tn=dz.shape[1] // 2))
    dh1, grad_small["g_mix"] = _rms_bwd(h1, g_mix, dn2, dh2, "mix_norm_bwd")

    (dn1, df1, da1, db1, s1), recv_ffn2 = _ffn_bwd_dx(
        dh1, a1, b1, wg1, wu1, wd1, "ffn1_bwd_dx", exchange=_scatter_exchange([grad_big[k] for k in FFN2_W]))
    dx, grad_small["g_ffn1"] = _rms_bwd(xs, g_ffn1, dn1, dh1, "ffn1_norm_bwd")
    (grad_big["w1_gate"], grad_big["w1_up"], grad_big["w1_down"]), recv_mid = _ffn_bwd_dw(
        n1, df1, da1, db1, s1, "ffn1_bwd_dw", exchange=_scatter_exchange([grad_big[k] for k in MID_W]))

    recv_ffn1 = _run_exchange(_scatter_exchange([grad_big[k] for k in FFN1_W]), "rs_ffn1")
    recv_big = dict(zip(FFN2_W + MID_W + FFN1_W, list(recv_ffn2) + list(recv_mid) + list(recv_ffn1)))
    recv_small = _run_exchange(_gather_exchange([_pack([grad_small[k] for k in SMALL])], ALL_PEERS),
                               "ag_small_grads")[0]

    grads, deltas, new_m, new_v = {}, {}, {}, {}
    for k in BIG:
        shp = w[k].shape
        two_d = shp[1:]
        outs = _adamw(w[k].reshape(two_d), mom[k].reshape(two_d), var[k].reshape(two_d), recv_big[k], "adamw_" + k)
        grads[k], deltas[k], new_m[k], new_v[k] = [t.reshape(shp) for t in outs]
    packed = _adamw(_pack([w[k] for k in SMALL]), _pack([mom[k] for k in SMALL]), _pack([var[k] for k in SMALL]),
                    recv_small, "adamw_small")
    off = 0
    for k in SMALL:
        shp = w[k].shape
        size = 1
        for s in shp:
            size *= s
        for dst, src in zip((grads, deltas, new_m, new_v), packed):
            dst[k] = src.reshape(-1)[off:off + size].reshape(shp)
        off += size

    loss = lax.psum(loss_part[0, 0], AXES)
    return (loss, dx[None], *[grads[k] for k in WEIGHTS], *[deltas[k] for k in WEIGHTS],
            *[new_m[k] for k in WEIGHTS], *[new_v[k] for k in WEIGHTS])
```

```python
import jax
import jax.numpy as jnp
from jax import lax
from jax.experimental import pallas as pl
from jax.experimental.pallas import tpu as pltpu

F32 = jnp.float32
BF16 = jnp.bfloat16
SDS = jax.ShapeDtypeStruct
BS = pl.BlockSpec

N_DEV = 8
AXES = ("x", "y", "c")
EPS = 1e-5
CHUNK = 128
HEAD_DIM = 64
KV_HEADS = 2
X_HEADS = 4
NEG = -1e30
ADAM_LR = 0.001
ADAM_B1 = 0.9
ADAM_B2 = 0.999
ADAM_EPS = 1e-08
ADAM_WD = 0.01
ADAM_STEP = 10
MIB = 1 << 20
FFN_SUB_ROWS = 256
WEIGHTS = ['g_ffn1', 'w1_gate', 'w1_up', 'w1_down', 'g_mix', 'w_in', 'g_v', 'w_s', 'b_s', 'sinks', 'g_a_out',
           'g_b_out', 'w_out', 'g_x', 'g_mem', 'w_xq', 'w_xkv', 'w_xo', 'g_ffn2', 'w2_gate', 'w2_up', 'w2_down',
           'g_final']
BIG = ['w1_gate', 'w1_up', 'w1_down', 'w_in', 'w_out', 'w_xq', 'w_xkv', 'w_xo', 'w2_gate', 'w2_up', 'w2_down']
SMALL = [w for w in WEIGHTS if w not in BIG]
FFN1_W = ['w1_gate', 'w1_up', 'w1_down']
FFN2_W = ['w2_gate', 'w2_up', 'w2_down']
MID_W = ['w_in', 'w_out', 'w_xq', 'w_xkv', 'w_xo']
REST_W = MID_W + FFN2_W


def _params(sem=None, vmem_mib=48):
    return pltpu.CompilerParams(dimension_semantics=sem, vmem_limit_bytes=vmem_mib * MIB)


def _dot(a, b):
    return jnp.dot(a, b, preferred_element_type=F32)


def _dot_nt(a, b):
    return lax.dot_general(a, b, (((1,), (1,)), ((), ())), preferred_element_type=F32)


def _dot_tn(a, b):
    return lax.dot_general(a, b, (((0,), (0,)), ((), ())), preferred_element_type=F32)


def _rstd(v):
    return lax.rsqrt(jnp.mean(v * v, axis=-1, keepdims=True) + EPS)


def _norm_bwd(xhat, r, g, d):
    t = d * g
    return r * (t - xhat * jnp.mean(t * xhat, axis=-1, keepdims=True))


def _gelu(v):
    return 0.5 * v * (1.0 + lax.erf(v * 0.7071067811865476))


def _gelu_grad(v):
    return 0.5 * (1.0 + lax.erf(v * 0.7071067811865476)) + v * jnp.exp(-0.5 * v * v) * 0.3989422804014327


def _mesh_pos():
    x, y, c = lax.axis_index("x"), lax.axis_index("y"), lax.axis_index("c")
    return x, y, c, 4 * x + 2 * y + c


def _peer(x, y, c, k):
    px = 1 - x if k & 4 else x
    py = 1 - y if k & 2 else y
    pc = 1 - c if k & 1 else c
    return (px, py, pc), 4 * px + 2 * py + pc


ANY = BS(memory_space=pl.ANY)
DMA_SEM = pltpu.SemaphoreType.DMA
ALL_PEERS = (1, 2, 3, 4, 5, 6, 7)
CHIP_PEERS = (2, 4, 6)
SIBLING = 1


def _remote(src, dst, send, recv, peer):
    return pltpu.make_async_remote_copy(src_ref=src, dst_ref=dst, send_sem=send, recv_sem=recv, device_id=peer,
                                        device_id_type=pl.DeviceIdType.MESH)


class _Exchange:
    def __init__(self, ins, out_shapes, sems, copies, aliases=None):
        self.ins, self.out_shapes, self.sems, self.copies, self.aliases = ins, out_shapes, sems, copies, aliases or {}


def _gather_exchange(shards, peers):
    n, m = len(shards), len(peers)

    def copies(ins, outs, sems):
        send, recv, lsem = sems
        x, y, c, me = _mesh_pos()
        cps = [pltpu.make_async_copy(ins[w], outs[w].at[me], lsem.at[w]) for w in range(n)]
        for w in range(n):
            for j, k in enumerate(peers):
                cps.append(_remote(ins[w], outs[w].at[me], send.at[w, j], recv.at[w, j], _peer(x, y, c, k)[0]))
        return cps

    return _Exchange(shards, [SDS((N_DEV,) + s.shape, s.dtype) for s in shards],
                     [DMA_SEM((n, m)), DMA_SEM((n, m)), DMA_SEM((n,))], copies)


def _sibling_exchange(gathered):
    n = len(gathered)

    def copies(ins, outs, sems):
        send, recv = sems
        x, y, c, me = _mesh_pos()
        slots = [me] + [_peer(x, y, c, k)[1] for k in CHIP_PEERS]
        sib = _peer(x, y, c, SIBLING)[0]
        return [_remote(outs[w].at[s], outs[w].at[s], send.at[w, j], recv.at[w, j], sib)
                for w in range(n) for j, s in enumerate(slots)]

    return _Exchange(gathered, [SDS(g.shape, g.dtype) for g in gathered], [DMA_SEM((n, 4)), DMA_SEM((n, 4))], copies,
                     aliases={w: w for w in range(n)})


def _scatter_exchange(fulls):
    n, m = len(fulls), len(ALL_PEERS)

    def copies(ins, outs, sems):
        send, recv, lsem = sems
        x, y, c, me = _mesh_pos()
        cps = [pltpu.make_async_copy(ins[w].at[me], outs[w].at[me], lsem.at[w]) for w in range(n)]
        for w in range(n):
            for j, k in enumerate(ALL_PEERS):
                peer, p = _peer(x, y, c, k)
                cps.append(_remote(ins[w].at[p], outs[w].at[me], send.at[w, j], recv.at[w, j], peer))
        return cps

    return _Exchange(fulls, [SDS(f.shape, f.dtype) for f in fulls],
                     [DMA_SEM((n, m)), DMA_SEM((n, m)), DMA_SEM((n,))], copies)


def _call(body, *, name, args, in_specs, out_shape, out_specs, grid, sem, vmem_mib, scratch=(), exchange=None,
          first=None, last=None):
    if exchange is None:
        return pl.pallas_call(body, name=name, out_shape=out_shape, grid=grid, in_specs=in_specs,
                              out_specs=out_specs, scratch_shapes=list(scratch),
                              compiler_params=_params(sem, vmem_mib))(*args), []
    ni, no, ns = len(args), len(out_shape), len(scratch)
    xi, xo = len(exchange.ins), len(exchange.out_shapes)

    def hosted(*refs):
        own_in, refs = refs[:ni], refs[ni:]
        x_in, refs = refs[:xi], refs[xi:]
        own_out, refs = refs[:no], refs[no:]
        x_out, refs = refs[:xo], refs[xo:]
        own_scr, x_sem = refs[:ns], refs[ns:]

        @pl.when(first())
        def _():
            for cp in exchange.copies(x_in, x_out, x_sem):
                cp.start()

        body(*own_in, *own_out, *own_scr)

        @pl.when(last())
        def _():
            for cp in exchange.copies(x_in, x_out, x_sem):
                cp.wait()

    outs = pl.pallas_call(
        hosted, name=name, out_shape=list(out_shape) + list(exchange.out_shapes), grid=grid,
        in_specs=list(in_specs) + [ANY] * xi, out_specs=list(out_specs) + [ANY] * xo,
        scratch_shapes=list(scratch) + list(exchange.sems),
        input_output_aliases={ni + a: no + b for a, b in exchange.aliases.items()},
        compiler_params=pltpu.CompilerParams(dimension_semantics=tuple("arbitrary" for _ in grid),
                                             vmem_limit_bytes=vmem_mib * MIB, has_side_effects=True),
    )(*args, *exchange.ins)
    return outs[:no], outs[no:]


def _run_exchange(exchange, name):
    xi, xo = len(exchange.ins), len(exchange.out_shapes)

    def body(*refs):
        cps = exchange.copies(refs[:xi], refs[xi:xi + xo], refs[xi + xo:])
        for cp in cps:
            cp.start()
        for cp in cps:
            cp.wait()

    return pl.pallas_call(
        body, name=name, out_shape=list(exchange.out_shapes), in_specs=[ANY] * xi, out_specs=[ANY] * xo,
        scratch_shapes=list(exchange.sems), input_output_aliases=dict(exchange.aliases),
        compiler_params=pltpu.CompilerParams(has_side_effects=True),
    )(*exchange.ins)


def _rms_fwd(h, g, name, tm=512, exchange=None):
    S, D = h.shape
    tm = min(tm, S)
    assert S % tm == 0, (S, tm)
    ni = S // tm

    def body(h_ref, g_ref, o_ref):
        hv = h_ref[...]
        o_ref[...] = (hv * _rstd(hv) * g_ref[...]).astype(o_ref.dtype)

    (out,), extra = _call(
        body, name=name, args=(h, g), out_shape=[SDS((S, D), BF16)], grid=(ni,),
        in_specs=[BS((tm, D), lambda i: (i, 0)), BS((1, D), lambda i: (0, 0))],
        out_specs=[BS((tm, D), lambda i: (i, 0))], sem=("parallel",), vmem_mib=32, exchange=exchange,
        first=lambda: pl.program_id(0) == 0, last=lambda: pl.program_id(0) == ni - 1)
    return out, extra


def _rms_bwd(h, g, dn, dres, name, tm=256):
    S, D = h.shape
    tm = min(tm, S)
    assert S % tm == 0, (S, tm)
    has_res = dres is not None

    def body(*refs):
        if has_res:
            h_ref, g_ref, dn_ref, dres_ref, dh_ref, dg_ref = refs
        else:
            h_ref, g_ref, dn_ref, dh_ref, dg_ref = refs
        hv = h_ref[...]
        r = _rstd(hv)
        xh = hv * r
        d = dn_ref[...].astype(F32)

        @pl.when(pl.program_id(0) == 0)
        def _():
            dg_ref[...] = jnp.zeros_like(dg_ref)

        dg_ref[...] += jnp.sum(d * xh, axis=0, keepdims=True)
        dh = _norm_bwd(xh, r, g_ref[...], d)
        dh_ref[...] = dres_ref[...] + dh if has_res else dh

    row = BS((tm, D), lambda i: (i, 0))
    vec = BS((1, D), lambda i: (0, 0))
    args = (h, g, dn) + ((dres,) if has_res else ())
    return pl.pallas_call(
        body, name=name, out_shape=[SDS((S, D), F32), SDS((1, D), F32)], grid=(S // tm,),
        in_specs=[row, vec, row] + ([row] if has_res else []), out_specs=[row, vec],
        compiler_params=_params(("arbitrary",), 40),
    )(*args)


def _final_loss(h, g, target, name, tm=256):
    S, D = h.shape
    tm = min(tm, S)
    assert S % tm == 0, (S, tm)

    def body(h_ref, g_ref, t_ref, dh_ref, dg_ref, loss_ref):
        hv = h_ref[...]
        r = _rstd(hv)
        xh = hv * r
        gv = g_ref[...]
        e = xh * gv - t_ref[...]

        @pl.when(pl.program_id(0) == 0)
        def _():
            dg_ref[...] = jnp.zeros_like(dg_ref)
            loss_ref[...] = jnp.zeros_like(loss_ref)

        loss_ref[...] += 0.5 * jnp.sum(jnp.mean(e * e, axis=-1, keepdims=True), axis=0, keepdims=True)
        dy = e * (1.0 / D)
        dg_ref[...] += jnp.sum(dy * xh, axis=0, keepdims=True)
        dh_ref[...] = _norm_bwd(xh, r, gv, dy)

    row = BS((tm, D), lambda i: (i, 0))
    vec = BS((1, D), lambda i: (0, 0))
    return pl.pallas_call(
        body, name=name, out_shape=[SDS((S, D), F32), SDS((1, D), F32), SDS((1, 128), F32)], grid=(S // tm,),
        in_specs=[row, vec, row], out_specs=[row, vec, BS((1, 128), lambda i: (0, 0))],
        compiler_params=_params(("arbitrary",), 40),
    )(h, g, target)


def _matmul(a, b, mode, out_dtype, name, res=None, tm=512, tn=512, tk=512):
    if mode == "tn":
        K, M = a.shape
        N = b.shape[1]
        tm, tn, tk = min(tm, M), min(tn, N), min(tk, K)
        assert M % tm == 0 and N % tn == 0 and K % tk == 0, (a.shape, b.shape, tm, tn, tk)
        nk = K // tk

        def body(a_ref, b_ref, o_ref, acc_ref):
            k = pl.program_id(2)

            @pl.when(k == 0)
            def _():
                acc_ref[...] = jnp.zeros_like(acc_ref)

            acc_ref[...] += _dot_tn(a_ref[...].astype(BF16), b_ref[...].astype(BF16))

            @pl.when(k == nk - 1)
            def _():
                o_ref[...] = acc_ref[...].astype(o_ref.dtype)

        return pl.pallas_call(
            body, name=name, out_shape=SDS((M, N), out_dtype), grid=(M // tm, N // tn, nk),
            in_specs=[BS((tk, tm), lambda i, j, k: (k, i)), BS((tk, tn), lambda i, j, k: (k, j))],
            out_specs=BS((tm, tn), lambda i, j, k: (i, j)), scratch_shapes=[pltpu.VMEM((tm, tn), F32)],
            compiler_params=_params(("parallel", "parallel", "arbitrary"), 48),
        )(a, b)

    M, K = a.shape
    N = b.shape[1] if mode == "nn" else b.shape[0]
    tm, tn = min(tm, M), min(tn, N)
    assert M % tm == 0 and N % tn == 0, (a.shape, b.shape, tm, tn)
    has_res = res is not None

    def body(*refs):
        if has_res:
            a_ref, b_ref, r_ref, o_ref = refs
        else:
            a_ref, b_ref, o_ref = refs
        av, bv = a_ref[...].astype(BF16), b_ref[...].astype(BF16)
        acc = _dot(av, bv) if mode == "nn" else _dot_nt(av, bv)
        if has_res:
            acc = acc + r_ref[...]
        o_ref[...] = acc.astype(o_ref.dtype)

    b_spec = BS((K, tn), lambda i, j: (0, j)) if mode == "nn" else BS((tn, K), lambda i, j: (j, 0))
    o_spec = BS((tm, tn), lambda i, j: (i, j))
    return pl.pallas_call(
        body, name=name, out_shape=SDS((M, N), out_dtype), grid=(M // tm, N // tn),
        in_specs=[BS((tm, K), lambda i, j: (i, 0)), b_spec] + ([o_spec] if has_res else []), out_specs=o_spec,
        compiler_params=_params(("parallel", "parallel"), 48),
    )(*((a, b) + ((res,) if has_res else ())))


def _ffn_fwd(h, g, wg, wu, wd, name, tm=512, exchange=None):
    S, D = h.shape
    nb, _, Fs = wg.shape
    tm = min(tm, S)
    sub = min(FFN_SUB_ROWS, tm)
    assert S % tm == 0 and tm % sub == 0, (S, tm, sub)

    def body(h_ref, g_ref, wg_ref, wu_ref, wd_ref, o_ref, n_ref, a_ref, b_ref):
        j = pl.program_id(1)

        @pl.when(j == 0)
        def _():
            hv = h_ref[...]
            n_ref[...] = (hv * _rstd(hv) * g_ref[...]).astype(BF16)
            o_ref[...] = jnp.zeros_like(o_ref)

        for r in range(0, tm, sub):
            rows = slice(r, r + sub)
            n = n_ref[rows, :]
            a = _dot(n, wg_ref[...]).astype(BF16)
            b = _dot(n, wu_ref[...]).astype(BF16)
            a_ref[rows, :] = a
            b_ref[rows, :] = b
            a, b = a.astype(F32), b.astype(F32)
            s = (a * jax.nn.sigmoid(a) * b).astype(BF16)
            o_ref[rows, :] += _dot(s, wd_ref[...])

        @pl.when(j == nb - 1)
        def _():
            o_ref[...] = h_ref[...] + 0.5 * o_ref[...]

    row = BS((tm, D), lambda i, j: (i, 0))
    wcol = BS((None, D, Fs), lambda i, j: (j, 0, 0))
    act = BS((None, tm, Fs), lambda i, j: (j, i, 0))
    ni = S // tm
    return _call(
        body, name=name, args=(h, g, wg, wu, wd),
        out_shape=[SDS((S, D), F32), SDS((S, D), BF16), SDS((nb, S, Fs), BF16), SDS((nb, S, Fs), BF16)],
        grid=(ni, nb),
        in_specs=[row, BS((1, D), lambda i, j: (0, 0)), wcol, wcol, BS((None, Fs, D), lambda i, j: (j, 0, 0))],
        out_specs=[row, row, act, act], sem=("parallel", "arbitrary"), vmem_mib=56, exchange=exchange,
        first=lambda: (pl.program_id(0) == 0) & (pl.program_id(1) == 0),
        last=lambda: (pl.program_id(0) == ni - 1) & (pl.program_id(1) == nb - 1))


def _ffn_bwd_dx(dh, a, b, wg, wu, wd, name, tm=512, exchange=None):
    S, D = dh.shape
    nb, _, Fs = wg.shape
    tm = min(tm, S)
    sub = min(FFN_SUB_ROWS, tm)
    assert S % tm == 0 and tm % sub == 0, (S, tm, sub)

    def body(dh_ref, a_ref, b_ref, wg_ref, wu_ref, wd_ref, dn_ref, df_ref, da_ref, db_ref, s_ref):
        j = pl.program_id(1)

        @pl.when(j == 0)
        def _():
            df_ref[...] = (0.5 * dh_ref[...]).astype(BF16)
            dn_ref[...] = jnp.zeros_like(dn_ref)

        for r in range(0, tm, sub):
            rows = slice(r, r + sub)
            ds = _dot_nt(df_ref[rows, :], wd_ref[...])
            av, bv = a_ref[rows, :].astype(F32), b_ref[rows, :].astype(F32)
            sig = jax.nn.sigmoid(av)
            sl = av * sig
            da = (ds * bv * (sig * (1.0 + av * (1.0 - sig)))).astype(BF16)
            db = (ds * sl).astype(BF16)
            da_ref[rows, :] = da
            db_ref[rows, :] = db
            s_ref[rows, :] = (sl * bv).astype(BF16)
            dn_ref[rows, :] += _dot_nt(da, wg_ref[...]) + _dot_nt(db, wu_ref[...])

    row = BS((tm, D), lambda i, j: (i, 0))
    wcol = BS((None, D, Fs), lambda i, j: (j, 0, 0))
    act = BS((None, tm, Fs), lambda i, j: (j, i, 0))
    ni = S // tm
    return _call(
        body, name=name, args=(dh, a, b, wg, wu, wd),
        out_shape=[SDS((S, D), F32), SDS((S, D), BF16)] + [SDS((nb, S, Fs), BF16)] * 3,
        grid=(ni, nb),
        in_specs=[row, act, act, wcol, wcol, BS((None, Fs, D), lambda i, j: (j, 0, 0))],
        out_specs=[row, row, act, act, act], sem=("parallel", "arbitrary"), vmem_mib=56, exchange=exchange,
        first=lambda: (pl.program_id(0) == 0) & (pl.program_id(1) == 0),
        last=lambda: (pl.program_id(0) == ni - 1) & (pl.program_id(1) == nb - 1))


DW_ORDER = {0: (2, 4, 6, 3, 5, 7, 1, 0), 1: (4, 2, 6, 5, 3, 7, 1, 0)}


def _dw_block_order():
    x, y, c, me = _mesh_pos()
    flips = jnp.where(c == 0, jnp.array(DW_ORDER[0], jnp.int32), jnp.array(DW_ORDER[1], jnp.int32))
    return jnp.bitwise_xor(me.astype(jnp.int32), flips)


def _ffn_bwd_dw(n, df, da, db, s, order, name, tk=512, exchange=None):
    S, D = n.shape
    nb, _, Fs = da.shape
    assert nb == N_DEV
    tk = min(tk, S)
    assert S % tk == 0, (S, tk)
    nk = S // tk
    xi = len(exchange.ins) if exchange else 0
    xo = len(exchange.out_shapes) if exchange else 0

    def body(order_ref, n_ref, df_ref, da_ref, db_ref, s_ref, *rest):
        x_in, rest = rest[:xi], rest[xi:]
        recv_bufs, rest = rest[:3], rest[3:]
        x_out, rest = rest[:xo], rest[xo:]
        accs, stages, (send, recv, lsem), x_sem = rest[:3], rest[3:6], rest[6:9], rest[9:]
        t, k = pl.program_id(0), pl.program_id(1)
        x, y, c, me = _mesh_pos()

        if exchange:
            @pl.when((t == 0) & (k == 0))
            def _():
                for cp in exchange.copies(x_in, x_out, x_sem):
                    cp.start()

        @pl.when(k == 0)
        def _():
            for acc in accs:
                acc[...] = jnp.zeros_like(acc)

        nv = n_ref[...]
        accs[0][...] += _dot_tn(nv, da_ref[...])
        accs[1][...] += _dot_tn(nv, db_ref[...])
        accs[2][...] += _dot_tn(s_ref[...], df_ref[...])

        def block_copy(w, step):
            src, dst = stages[w].at[step % 2], recv_bufs[w].at[me]
            if step == nb - 1:
                return pltpu.make_async_copy(src, dst, lsem.at[w])
            flip = jnp.where(c == 0, DW_ORDER[0][step], DW_ORDER[1][step])
            p = jnp.bitwise_xor(me, flip)
            return _remote(src, dst, send.at[w, step], recv.at[w, step], (p >> 2, (p >> 1) & 1, p & 1))

        for step in range(nb):
            @pl.when((t == step) & (k == nk - 1))
            def _(step=step):
                for w in range(3):
                    if step >= 2:
                        block_copy(w, step - 2).wait_send()
                    stages[w][step % 2] = accs[w][...].astype(BF16)
                    block_copy(w, step).start()

        @pl.when((t == nb - 1) & (k == nk - 1))
        def _():
            for w in range(3):
                block_copy(w, nb - 2).wait_send()
                block_copy(w, nb - 1).wait()
                for step in range(nb - 1):
                    block_copy(w, step).wait_recv()
            if exchange:
                for cp in exchange.copies(x_in, x_out, x_sem):
                    cp.wait()

    row = BS((tk, D), lambda t, k, o: (k, 0))
    act = BS((None, tk, Fs), lambda t, k, o: (o[t], k, 0))
    col_shape, row_shape = (D, Fs), (Fs, D)
    grid_spec = pltpu.PrefetchScalarGridSpec(
        num_scalar_prefetch=1, grid=(nb, nk),
        in_specs=[row, row, act, act, act] + [ANY] * xi, out_specs=[ANY] * (3 + xo),
        scratch_shapes=[pltpu.VMEM(col_shape, F32), pltpu.VMEM(col_shape, F32), pltpu.VMEM(row_shape, F32),
                        pltpu.VMEM((2,) + col_shape, BF16), pltpu.VMEM((2,) + col_shape, BF16),
                        pltpu.VMEM((2,) + row_shape, BF16),
                        DMA_SEM((3, nb - 1)), DMA_SEM((3, nb - 1)), DMA_SEM((3,))]
        + (list(exchange.sems) if exchange else []))
    outs = pl.pallas_call(
        body, name=name, grid_spec=grid_spec,
        out_shape=[SDS((nb,) + col_shape, BF16), SDS((nb,) + col_shape, BF16), SDS((nb,) + row_shape, BF16)]
        + (list(exchange.out_shapes) if exchange else []),
        compiler_params=pltpu.CompilerParams(dimension_semantics=("arbitrary", "arbitrary"),
                                             vmem_limit_bytes=58 * MIB, has_side_effects=True),
    )(order, n, df, da, db, s, *(exchange.ins if exchange else ()))
    return outs[:3], outs[3:]


def _sgu_parts(z, gv, ws_ref, bst_ref, groups):
    da = z.shape[1] // 2
    zu, zv = z[:, :da], z[:, da:]
    u, v = _gelu(zu), _gelu(zv)
    rv = _rstd(v)
    vhat = v * rv
    vn = (vhat * gv).astype(BF16)
    tri = lax.broadcasted_iota(jnp.int32, (CHUNK, CHUNK), 0) >= lax.broadcasted_iota(jnp.int32, (CHUNK, CHUNK), 1)
    pieces = []
    for g in range(groups):
        w = jnp.where(tri, ws_ref[g], 0.0).astype(BF16)
        pieces.append(_dot(w, vn[:, g * CHUNK:(g + 1) * CHUNK]) + bst_ref[:, g:g + 1])
    sv = jnp.concatenate(pieces, axis=1)
    return dict(zu=zu, zv=zv, u=u, rv=rv, vhat=vhat, vn=vn, sv=sv, tri=tri)


def _sgu_fwd(z, g_v, w_s, b_st, g_a, name):
    S = z.shape[0]
    groups = w_s.shape[0]
    da = groups * CHUNK

    def body(z_ref, gv_ref, ws_ref, bst_ref, ga_ref, o_ref):
        p = _sgu_parts(z_ref[...], gv_ref[...], ws_ref, bst_ref, groups)
        ya = p["u"] * p["sv"]
        o_ref[...] = (ya * _rstd(ya) * ga_ref[...]).astype(BF16)

    vec = BS((1, da), lambda i: (0, 0))
    return pl.pallas_call(
        body, name=name, out_shape=SDS((S, da), BF16), grid=(S // CHUNK,),
        in_specs=[BS((CHUNK, 2 * da), lambda i: (i, 0)), vec, BS((groups, CHUNK, CHUNK), lambda i: (0, 0, 0)),
                  BS((CHUNK, groups), lambda i: (0, 0)), vec],
        out_specs=BS((CHUNK, da), lambda i: (i, 0)), compiler_params=_params(("parallel",), 32),
    )(z, g_v, w_s, b_st, g_a)


def _sgu_bwd(z, dy, g_v, w_s, w_st, b_st, g_a, name):
    S = z.shape[0]
    groups = w_s.shape[0]
    da = groups * CHUNK

    def body(z_ref, dy_ref, gv_ref, ws_ref, wst_ref, bst_ref, ga_ref, dz_ref, dws_ref, dbst_ref, dgv_ref, dga_ref):
        @pl.when(pl.program_id(0) == 0)
        def _():
            dws_ref[...] = jnp.zeros_like(dws_ref)
            dbst_ref[...] = jnp.zeros_like(dbst_ref)
            dgv_ref[...] = jnp.zeros_like(dgv_ref)
            dga_ref[...] = jnp.zeros_like(dga_ref)

        gv = gv_ref[...]
        p = _sgu_parts(z_ref[...], gv, ws_ref, bst_ref, groups)
        u, sv, tri = p["u"], p["sv"], p["tri"]
        ya = u * sv
        ra = _rstd(ya)
        yhat = ya * ra
        d = dy_ref[...]
        dga_ref[...] += jnp.sum(d * yhat, axis=0, keepdims=True)
        dya = _norm_bwd(yhat, ra, ga_ref[...], d)
        du = dya * sv
        dsv = dya * u
        dsv_b = dsv.astype(BF16)
        tri_t = (lax.broadcasted_iota(jnp.int32, (CHUNK, CHUNK), 0)
                 <= lax.broadcasted_iota(jnp.int32, (CHUNK, CHUNK), 1))
        lane = lax.broadcasted_iota(jnp.int32, (CHUNK, groups), 1)
        dvn = []
        dbs = jnp.zeros((CHUNK, groups), F32)
        for g in range(groups):
            cols = slice(g * CHUNK, (g + 1) * CHUNK)
            dbs = dbs + jnp.where(lane == g, jnp.sum(dsv[:, cols], axis=1, keepdims=True), 0.0)
            dws_ref[g] += jnp.where(tri, _dot_nt(dsv_b[:, cols], p["vn"][:, cols]), 0.0)
            wt = jnp.where(tri_t, wst_ref[g], 0.0).astype(BF16)
            dvn.append(_dot(wt, dsv_b[:, cols]))
        dbst_ref[...] += dbs
        dvn = jnp.concatenate(dvn, axis=1)
        dgv_ref[...] += jnp.sum(dvn * p["vhat"], axis=0, keepdims=True)
        dv = _norm_bwd(p["vhat"], p["rv"], gv, dvn)
        dz_ref[...] = jnp.concatenate([du * _gelu_grad(p["zu"]), dv * _gelu_grad(p["zv"])], axis=1)

    vec = BS((1, da), lambda i: (0, 0))
    wsq = BS((groups, CHUNK, CHUNK), lambda i: (0, 0, 0))
    bsq = BS((CHUNK, groups), lambda i: (0, 0))
    return pl.pallas_call(
        body, name=name,
        out_shape=[SDS((S, 2 * da), F32), SDS((groups, CHUNK, CHUNK), F32), SDS((CHUNK, groups), F32),
                   SDS((1, da), F32), SDS((1, da), F32)],
        grid=(S // CHUNK,),
        in_specs=[BS((CHUNK, 2 * da), lambda i: (i, 0)), BS((CHUNK, da), lambda i: (i, 0)), vec, wsq, wsq, bsq, vec],
        out_specs=[BS((CHUNK, 2 * da), lambda i: (i, 0)), wsq, bsq, vec, vec],
        compiler_params=_params(("arbitrary",), 32),
    )(z, dy, g_v, w_s, w_st, b_st, g_a)


def _swa_mask(i, group):
    row = lax.broadcasted_iota(jnp.int32, (group * CHUNK, 2 * CHUNK), 0) & (CHUNK - 1)
    col = lax.broadcasted_iota(jnp.int32, (group * CHUNK, 2 * CHUNK), 1)
    d = row + CHUNK - col
    return (d >= 0) & (d < CHUNK) & jnp.logical_or(i > 0, col >= CHUNK)


def _stack_heads(t, g, group):
    return jnp.concatenate([t[:, h * HEAD_DIM:(h + 1) * HEAD_DIM] for h in range(g * group, (g + 1) * group)], axis=0)


def _unstack_heads(stacked, group):
    return [stacked[h * CHUNK:(h + 1) * CHUNK] for h in range(group)]


def _swa_probs(qh, kh, sink, mask):
    s = jnp.where(mask, _dot_nt(qh, kh) * (HEAD_DIM ** -0.5), NEG)
    m = jnp.maximum(jnp.max(s, axis=-1, keepdims=True), sink)
    e = jnp.exp(s - m)
    es = jnp.exp(sink - m)
    inv = 1.0 / (jnp.sum(e, axis=-1, keepdims=True) + es)
    return e * inv, es * inv


def _swa_specs(db, nblk, clamp):
    kvw = 2 * KV_HEADS * HEAD_DIM
    cur = (lambda i: jnp.minimum(i, nblk - 1)) if clamp else (lambda i: i)
    q_spec = BS((CHUNK, db), lambda i: (cur(i), 2))
    kc_spec = BS((CHUNK, kvw), lambda i: (cur(i), 3 * db // kvw))
    kp_spec = BS((CHUNK, kvw), lambda i: (jnp.maximum(cur(i) - 1, 0), 3 * db // kvw))
    return q_spec, kc_spec, kp_spec


def _swa_fwd(z, sink_rows, g_b, name, exchange=None):
    S = z.shape[0]
    db = g_b.shape[1]
    heads = db // HEAD_DIM
    group = heads // KV_HEADS
    nblk = S // CHUNK

    def body(q_ref, kc_ref, kp_ref, sk_ref, gb_ref, yb_ref, ybn_ref):
        mask = _swa_mask(pl.program_id(0), group)
        q = q_ref[...].astype(BF16)
        kv = jnp.concatenate([kp_ref[...], kc_ref[...]], axis=0).astype(BF16)
        outs = []
        for g in range(KV_HEADS):
            kg = kv[:, g * HEAD_DIM:(g + 1) * HEAD_DIM]
            vg = kv[:, (KV_HEADS + g) * HEAD_DIM:(KV_HEADS + g + 1) * HEAD_DIM]
            rows = slice(g * group * CHUNK, (g + 1) * group * CHUNK)
            p, _ = _swa_probs(_stack_heads(q, g, group), kg, sk_ref[rows, :], mask)
            outs += _unstack_heads(_dot(p.astype(BF16), vg), group)
        yb = jnp.concatenate(outs, axis=1)
        yb_ref[...] = yb
        ybn_ref[...] = (yb * _rstd(yb) * gb_ref[...]).astype(BF16)

    q_spec, kc_spec, kp_spec = _swa_specs(db, nblk, False)
    out = BS((CHUNK, db), lambda i: (i, 0))
    return _call(
        body, name=name, args=(z, z, z, sink_rows, g_b), out_shape=[SDS((S, db), F32), SDS((S, db), BF16)],
        grid=(nblk,),
        in_specs=[q_spec, kc_spec, kp_spec, BS((heads * CHUNK, 1), lambda i: (0, 0)), BS((1, db), lambda i: (0, 0))],
        out_specs=[out, out], sem=("parallel",), vmem_mib=32, exchange=exchange,
        first=lambda: pl.program_id(0) == 0, last=lambda: pl.program_id(0) == nblk - 1)


def _swa_bwd(z, yb, dy, sink_rows, g_b, name):
    S = z.shape[0]
    db = g_b.shape[1]
    heads = db // HEAD_DIM
    group = heads // KV_HEADS
    nblk = S // CHUNK
    kvw = 2 * KV_HEADS * HEAD_DIM

    def body(q_ref, kc_ref, kp_ref, yb_ref, dy_ref, sk_ref, gb_ref, dq_ref, dkv_ref, dsk_ref, dgb_ref, carry_ref):
        i = pl.program_id(0)

        @pl.when(i == 0)
        def _():
            carry_ref[...] = jnp.zeros_like(carry_ref)
            dsk_ref[...] = jnp.zeros_like(dsk_ref)
            dgb_ref[...] = jnp.zeros_like(dgb_ref)

        @pl.when(i < nblk)
        def _():
            mask = _swa_mask(i, group)
            yb = yb_ref[...]
            rb = _rstd(yb)
            yhat = yb * rb
            d = dy_ref[...]
            dgb_ref[...] += jnp.sum(d * yhat, axis=0, keepdims=True)
            do = _norm_bwd(yhat, rb, gb_ref[...], d).astype(BF16)
            q = q_ref[...].astype(BF16)
            kv = jnp.concatenate([kp_ref[...], kc_ref[...]], axis=0).astype(BF16)
            dqs, dk, dv = [], [], []
            lane = lax.broadcasted_iota(jnp.int32, (1, heads), 1)
            dsinks = jnp.zeros((1, heads), F32)
            for g in range(KV_HEADS):
                kg = kv[:, g * HEAD_DIM:(g + 1) * HEAD_DIM]
                vg = kv[:, (KV_HEADS + g) * HEAD_DIM:(KV_HEADS + g + 1) * HEAD_DIM]
                rows = slice(g * group * CHUNK, (g + 1) * group * CHUNK)
                qg, dog = _stack_heads(q, g, group), _stack_heads(do, g, group)
                p, ps = _swa_probs(qg, kg, sk_ref[rows, :], mask)
                dp = _dot_nt(dog, vg)
                dr = jnp.sum(p * dp, axis=-1, keepdims=True)
                ds = (p * (dp - dr) * (HEAD_DIM ** -0.5)).astype(BF16)
                for h, t in enumerate(_unstack_heads(ps * dr, group)):
                    dsinks = dsinks - jnp.where(lane == g * group + h, jnp.sum(t, axis=0, keepdims=True), 0.0)
                dqs += _unstack_heads(_dot(ds, kg), group)
                dk.append(_dot_tn(ds, qg))
                dv.append(_dot_tn(p.astype(BF16), dog))
            dq_ref[...] = jnp.concatenate(dqs, axis=1)
            dsk_ref[...] += dsinks
            contrib = jnp.concatenate(dk + dv, axis=1)
            dkv_ref[...] = carry_ref[...] + contrib[:CHUNK]
            carry_ref[...] = contrib[CHUNK:]

        @pl.when(i == nblk)
        def _():
            dkv_ref[...] = carry_ref[...]

    q_spec, kc_spec, kp_spec = _swa_specs(db, nblk, True)
    cur = BS((CHUNK, db), lambda i: (jnp.minimum(i, nblk - 1), 0))
    return pl.pallas_call(
        body, name=name,
        out_shape=[SDS((S, db), F32), SDS((S, kvw), F32), SDS((1, heads), F32), SDS((1, db), F32)],
        grid=(nblk + 1,),
        in_specs=[q_spec, kc_spec, kp_spec, cur, BS((CHUNK, db), lambda i: (jnp.minimum(i, nblk - 1), 1)),
                  BS((heads * CHUNK, 1), lambda i: (0, 0)), BS((1, db), lambda i: (0, 0))],
        out_specs=[cur, BS((CHUNK, kvw), lambda i: (jnp.maximum(i - 1, 0), 0)), BS((1, heads), lambda i: (0, 0)),
                   BS((1, db), lambda i: (0, 0))],
        scratch_shapes=[pltpu.VMEM((CHUNK, kvw), F32)], compiler_params=_params(("arbitrary",), 32),
    )(z, z, z, yb, dy, sink_rows, g_b)


def _xattn_probs(qh, kh, hd):
    s = _dot_nt(qh, kh) * (hd ** -0.5)
    e = jnp.exp(s - jnp.max(s, axis=-1, keepdims=True))
    return e / jnp.sum(e, axis=-1, keepdims=True)


def _xattn_fwd(q, kv, name, tq=512):
    S, D = q.shape
    M = kv.shape[0]
    hd = D // X_HEADS
    tq = min(tq, S)
    assert S % tq == 0, (S, tq)

    def body(q_ref, kv_ref, o_ref):
        for h in range(X_HEADS):
            cols = slice(h * hd, (h + 1) * hd)
            p = _xattn_probs(q_ref[:, cols], kv_ref[:, cols], hd)
            o_ref[:, cols] = _dot(p.astype(BF16), kv_ref[:, D + h * hd:D + (h + 1) * hd]).astype(BF16)

    row = BS((tq, D), lambda i: (i, 0))
    return pl.pallas_call(
        body, name=name, out_shape=SDS((S, D), BF16), grid=(S // tq,),
        in_specs=[row, BS((M, 2 * D), lambda i: (0, 0))], out_specs=row, compiler_params=_params(("parallel",), 40),
    )(q, kv)


def _xattn_bwd(q, kv, do, name, tq=512):
    S, D = q.shape
    M = kv.shape[0]
    hd = D // X_HEADS
    tq = min(tq, S)
    assert S % tq == 0, (S, tq)

    def body(q_ref, kv_ref, do_ref, dq_ref, dkv_ref):
        @pl.when(pl.program_id(0) == 0)
        def _():
            dkv_ref[...] = jnp.zeros_like(dkv_ref)

        for h in range(X_HEADS):
            cols = slice(h * hd, (h + 1) * hd)
            vcols = slice(D + h * hd, D + (h + 1) * hd)
            qh, kh, vh, doh = q_ref[:, cols], kv_ref[:, cols], kv_ref[:, vcols], do_ref[:, cols]
            p = _xattn_probs(qh, kh, hd)
            dp = _dot_nt(doh, vh)
            ds = (p * (dp - jnp.sum(p * dp, axis=-1, keepdims=True)) * (hd ** -0.5)).astype(BF16)
            dq_ref[:, cols] = _dot(ds, kh).astype(BF16)
            dkv_ref[:, cols] += _dot_tn(ds, qh)
            dkv_ref[:, vcols] += _dot_tn(p.astype(BF16), doh)

    row = BS((tq, D), lambda i: (i, 0))
    full = BS((M, 2 * D), lambda i: (0, 0))
    return pl.pallas_call(
        body, name=name, out_shape=[SDS((S, D), BF16), SDS((M, 2 * D), F32)], grid=(S // tq,),
        in_specs=[row, full, row], out_specs=[row, full], compiler_params=_params(("arbitrary",), 40),
    )(q, kv, do)


def _row_tile(rows, cap):
    best = 8
    for t in range(8, min(rows, cap) + 1, 8):
        if rows % t == 0:
            best = t
    assert rows % best == 0, (rows, cap)
    return best


ADAM_TILE_ELEMS = 256 * 1024


def _adamw(w, m, v, recv, name):
    R, C = w.shape
    tr = _row_tile(R, max(8, ADAM_TILE_ELEMS // C))

    def body(w_ref, m_ref, v_ref, r_ref, g_ref, d_ref, nm_ref, nv_ref):
        g = r_ref[0].astype(F32)
        for s in range(1, N_DEV):
            g = g + r_ref[s].astype(F32)
        mn = ADAM_B1 * m_ref[...] + (1.0 - ADAM_B1) * g
        vn = ADAM_B2 * v_ref[...] + (1.0 - ADAM_B2) * jnp.square(g)
        m_hat = mn / (1.0 - ADAM_B1 ** ADAM_STEP)
        v_hat = vn / (1.0 - ADAM_B2 ** ADAM_STEP)
        g_ref[...] = g
        d_ref[...] = -ADAM_LR * (m_hat / (jnp.sqrt(v_hat) + ADAM_EPS) + ADAM_WD * w_ref[...])
        nm_ref[...] = mn
        nv_ref[...] = vn

    row = BS((tr, C), lambda i: (i, 0))
    return pl.pallas_call(
        body, name=name, out_shape=[SDS((R, C), F32)] * 4, grid=(R // tr,),
        in_specs=[row, row, row, BS((N_DEV, tr, C), lambda i: (0, i, 0))], out_specs=[row] * 4,
        compiler_params=_params(("parallel",), 40),
    )(w, m, v, recv)


def _cols_to_blocks(full):
    r, c = full.shape
    return full.reshape(r, N_DEV, c // N_DEV).transpose(1, 0, 2)


def _blocks_to_cols(blocks):
    n, r, c = blocks.shape
    return blocks.transpose(1, 0, 2).reshape(r, n * c)


def _pack(parts):
    flat = jnp.concatenate([p.reshape(-1).astype(F32) for p in parts])
    pad = (-flat.shape[0]) % (128 * 128)
    return jnp.pad(flat, (0, pad)).reshape(-1, 128)


def kernel(x, mem, g_ffn1, w1_gate, w1_up, w1_down, g_mix, w_in, g_v, w_s, b_s, sinks, g_a_out, g_b_out, w_out, g_x, g_mem, w_xq, w_xkv, w_xo, g_ffn2, w2_gate, w2_up, w2_down, g_final, loss_target, m_g_ffn1, m_w1_gate, m_w1_up, m_w1_down, m_g_mix, m_w_in, m_g_v, m_w_s, m_b_s, m_sinks, m_g_a_out, m_g_b_out, m_w_out, m_g_x, m_g_mem, m_w_xq, m_w_xkv, m_w_xo, m_g_ffn2, m_w2_gate, m_w2_up, m_w2_down, m_g_final, v_g_ffn1, v_w1_gate, v_w1_up, v_w1_down, v_g_mix, v_w_in, v_g_v, v_w_s, v_b_s, v_sinks, v_g_a_out, v_g_b_out, v_w_out, v_g_x, v_g_mem, v_w_xq, v_w_xkv, v_w_xo, v_g_ffn2, v_w2_gate, v_w2_up, v_w2_down, v_g_final):
    w = dict(g_ffn1=g_ffn1, w1_gate=w1_gate, w1_up=w1_up, w1_down=w1_down, g_mix=g_mix, w_in=w_in, g_v=g_v, w_s=w_s,
             b_s=b_s, sinks=sinks, g_a_out=g_a_out, g_b_out=g_b_out, w_out=w_out, g_x=g_x, g_mem=g_mem, w_xq=w_xq,
             w_xkv=w_xkv, w_xo=w_xo, g_ffn2=g_ffn2, w2_gate=w2_gate, w2_up=w2_up, w2_down=w2_down, g_final=g_final)
    mom = dict(g_ffn1=m_g_ffn1, w1_gate=m_w1_gate, w1_up=m_w1_up, w1_down=m_w1_down, g_mix=m_g_mix, w_in=m_w_in,
               g_v=m_g_v, w_s=m_w_s, b_s=m_b_s, sinks=m_sinks, g_a_out=m_g_a_out, g_b_out=m_g_b_out, w_out=m_w_out,
               g_x=m_g_x, g_mem=m_g_mem, w_xq=m_w_xq, w_xkv=m_w_xkv, w_xo=m_w_xo, g_ffn2=m_g_ffn2,
               w2_gate=m_w2_gate, w2_up=m_w2_up, w2_down=m_w2_down, g_final=m_g_final)
    var = dict(g_ffn1=v_g_ffn1, w1_gate=v_w1_gate, w1_up=v_w1_up, w1_down=v_w1_down, g_mix=v_g_mix, w_in=v_w_in,
               g_v=v_g_v, w_s=v_w_s, b_s=v_b_s, sinks=v_sinks, g_a_out=v_g_a_out, g_b_out=v_g_b_out, w_out=v_w_out,
               g_x=v_g_x, g_mem=v_g_mem, w_xq=v_w_xq, w_xkv=v_w_xkv, w_xo=v_w_xo, g_ffn2=v_g_ffn2,
               w2_gate=v_w2_gate, w2_up=v_w2_up, w2_down=v_w2_down, g_final=v_g_final)

    xs, ms, tgt = x[0], mem[0], loss_target[0]
    D = xs.shape[1]
    d_a = w_s.shape[1] * CHUNK
    d_b = D - d_a
    kvw = 2 * KV_HEADS * HEAD_DIM

    shard = {k: w[k][0].astype(BF16) for k in BIG}
    part = _run_exchange(_gather_exchange([shard[k] for k in FFN1_W], CHIP_PEERS), "ag_ffn1_ici")
    wg1, wu1, wd1 = _run_exchange(_sibling_exchange(part), "ag_ffn1_d2d")
    gf = g_final.reshape(1, D)
    ws, ws_t, bs_t = w_s[0], jnp.swapaxes(w_s[0], 1, 2), b_s[0].T
    sink_rows = jnp.repeat(sinks.reshape(-1), CHUNK).reshape(-1, 1)

    (h1, n1, a1, b1), part = _ffn_fwd(xs, g_ffn1, wg1, wu1, wd1, "ffn1_fwd",
                                      exchange=_gather_exchange([shard[k] for k in REST_W], CHIP_PEERS))
    part = dict(zip(REST_W, part))
    n2, (win_blocks,) = _rms_fwd(h1, g_mix, "mix_norm", exchange=_sibling_exchange([part["w_in"]]))
    win = _blocks_to_cols(win_blocks)
    z = _matmul(n2, win, "nn", F32, "mm_in", tn=win.shape[1] // 2)
    ya_n = _sgu_fwd(z, g_v, ws, bs_t, g_a_out, "sgu_fwd")
    later = [k for k in REST_W if k != "w_in"]
    (yb, yb_n), handed = _swa_fwd(z, sink_rows, g_b_out, "swa_fwd",
                                  exchange=_sibling_exchange([part[k] for k in later]))
    gathered = dict(zip(later, handed))
    wg2, wu2, wd2 = gathered["w2_gate"], gathered["w2_up"], gathered["w2_down"]
    wxkv = _blocks_to_cols(gathered["w_xkv"])
    wout = gathered["w_out"].reshape(D, D)
    wxq = gathered["w_xq"].reshape(D, D)
    wxo = gathered["w_xo"].reshape(D, D)
    y = jnp.concatenate([ya_n, yb_n], axis=1)
    h2 = _matmul(y, wout, "nn", F32, "mm_out", res=h1, tn=D)
    hx, _ = _rms_fwd(h2, g_x, "x_norm")
    mn, _ = _rms_fwd(ms, g_mem, "mem_norm")
    q = _matmul(hx, wxq, "nn", BF16, "mm_xq", tn=D)
    kv = _matmul(mn, wxkv, "nn", BF16, "mm_xkv")
    o = _xattn_fwd(q, kv, "xattn_fwd")
    h3 = _matmul(o, wxo, "nn", F32, "mm_xo", res=h2, tn=D)
    (h4, n4, a2, b2), _ = _ffn_fwd(h3, g_ffn2, wg2, wu2, wd2, "ffn2_fwd")
    dh4, dg_final, loss_part = _final_loss(h4, gf, tgt, "final_loss")

    grad_big, grad_small, recv_big = {}, {"g_final": dg_final}, {}
    order = _dw_block_order()
    (dn4, df2, da2, db2, s2), _ = _ffn_bwd_dx(dh4, a2, b2, wg2, wu2, wd2, "ffn2_bwd_dx")
    dh3, grad_small["g_ffn2"] = _rms_bwd(h3, g_ffn2, dn4, dh4, "ffn2_norm_bwd")
    (recv_big["w2_gate"], recv_big["w2_up"], recv_big["w2_down"]), _ = _ffn_bwd_dw(
        n4, df2, da2, db2, s2, order, "ffn2_bwd_dw")

    do = _matmul(dh3, wxo, "nt", BF16, "mm_xo_dx", tn=D)
    grad_big["w_xo"] = _matmul(o, dh3, "tn", BF16, "mm_xo_dw", tm=1024, tn=1024).reshape(N_DEV, D // N_DEV, D)
    dq, dkv = _xattn_bwd(q, kv, do, "xattn_bwd")
    dhx = _matmul(dq, wxq, "nt", F32, "mm_xq_dx", tn=D)
    grad_big["w_xq"] = _matmul(hx, dq, "tn", BF16, "mm_xq_dw", tm=1024, tn=1024).reshape(N_DEV, D // N_DEV, D)
    dmn = _matmul(dkv, wxkv, "nt", F32, "mm_xkv_dx")
    grad_big["w_xkv"] = _cols_to_blocks(_matmul(mn, dkv, "tn", BF16, "mm_xkv_dw", tm=1024, tn=1024))
    dh2, grad_small["g_x"] = _rms_bwd(h2, g_x, dhx, dh3, "x_norm_bwd")
    _, grad_small["g_mem"] = _rms_bwd(ms, g_mem, dmn, None, "mem_norm_bwd")

    dy = _matmul(dh2, wout, "nt", F32, "mm_out_dx", tn=D)
    grad_big["w_out"] = _matmul(y, dh2, "tn", BF16, "mm_out_dw", tm=1024, tn=1024).reshape(N_DEV, D // N_DEV, D)
    dz_uv, grad_small["w_s"], dbs_t, grad_small["g_v"], grad_small["g_a_out"] = _sgu_bwd(
        z, dy, g_v, ws, ws_t, bs_t, g_a_out, "sgu_bwd")
    grad_small["b_s"] = dbs_t.T
    dq_b, dkv_b, grad_small["sinks"], grad_small["g_b_out"] = _swa_bwd(z, yb, dy, sink_rows, g_b_out, "swa_bwd")
    dz = jnp.concatenate([dz_uv, dq_b, dkv_b], axis=1)
    dn2 = _matmul(dz, win, "nt", F32, "mm_in_dx", tn=1024)
    grad_big["w_in"] = _cols_to_blocks(_matmul(n2, dz, "tn", BF16, "mm_in_dw", tm=1024, tn=dz.shape[1] // 2))
    dh1, grad_small["g_mix"] = _rms_bwd(h1, g_mix, dn2, dh2, "mix_norm_bwd")

    (dn1, df1, da1, db1, s1), recv_mid = _ffn_bwd_dx(
        dh1, a1, b1, wg1, wu1, wd1, "ffn1_bwd_dx", exchange=_scatter_exchange([grad_big[k] for k in MID_W]))
    recv_big.update(zip(MID_W, recv_mid))
    dx, grad_small["g_ffn1"] = _rms_bwd(xs, g_ffn1, dn1, dh1, "ffn1_norm_bwd")
    (recv_big["w1_gate"], recv_big["w1_up"], recv_big["w1_down"]), (recv_small,) = _ffn_bwd_dw(
        n1, df1, da1, db1, s1, order, "ffn1_bwd_dw",
        exchange=_gather_exchange([_pack([grad_small[k] for k in SMALL])], ALL_PEERS))


    grads, deltas, new_m, new_v = {}, {}, {}, {}
    for k in BIG:
        shp = w[k].shape
        two_d = shp[1:]
        outs = _adamw(w[k].reshape(two_d), mom[k].reshape(two_d), var[k].reshape(two_d), recv_big[k], "adamw_" + k)
        grads[k], deltas[k], new_m[k], new_v[k] = [t.reshape(shp) for t in outs]
    packed = _adamw(_pack([w[k] for k in SMALL]), _pack([mom[k] for k in SMALL]), _pack([var[k] for k in SMALL]),
                    recv_small, "adamw_small")
    off = 0
    for k in SMALL:
        shp = w[k].shape
        size = 1
        for s in shp:
            size *= s
        for dst, src in zip((grads, deltas, new_m, new_v), packed):
            dst[k] = src.reshape(-1)[off:off + size].reshape(shp)
        off += size

    loss = lax.psum(loss_part[0, 0], AXES)
    return (loss, dx[None], *[grads[k] for k in WEIGHTS], *[deltas[k] for k in WEIGHTS],
            *[new_m[k] for k in WEIGHTS], *[new_v[k] for k in WEIGHTS])
```

```python
import jax
import jax.numpy as jnp
from jax import lax
from jax.experimental import pallas as pl
from jax.experimental.pallas import tpu as pltpu

F32 = jnp.float32
BF16 = jnp.bfloat16
SDS = jax.ShapeDtypeStruct
BS = pl.BlockSpec

N_DEV = 8
AXES = ("x", "y", "c")
EPS = 1e-5
CHUNK = 128
HEAD_DIM = 64
KV_HEADS = 2
X_HEADS = 4
NEG = -1e30
ADAM_LR = 0.001
ADAM_B1 = 0.9
ADAM_B2 = 0.999
ADAM_EPS = 1e-08
ADAM_WD = 0.01
ADAM_STEP = 10
MIB = 1 << 20
FFN_SUB_ROWS = 256
WEIGHTS = ['g_ffn1', 'w1_gate', 'w1_up', 'w1_down', 'g_mix', 'w_in', 'g_v', 'w_s', 'b_s', 'sinks', 'g_a_out',
           'g_b_out', 'w_out', 'g_x', 'g_mem', 'w_xq', 'w_xkv', 'w_xo', 'g_ffn2', 'w2_gate', 'w2_up', 'w2_down',
           'g_final']
BIG = ['w1_gate', 'w1_up', 'w1_down', 'w_in', 'w_out', 'w_xq', 'w_xkv', 'w_xo', 'w2_gate', 'w2_up', 'w2_down']
SMALL = [w for w in WEIGHTS if w not in BIG]
FFN1_W = ['w1_gate', 'w1_up', 'w1_down']
FFN2_W = ['w2_gate', 'w2_up', 'w2_down']
MID_W = ['w_in', 'w_out', 'w_xq', 'w_xkv', 'w_xo']
REST_W = MID_W + FFN2_W


def _params(sem=None, vmem_mib=48):
    return pltpu.CompilerParams(dimension_semantics=sem, vmem_limit_bytes=vmem_mib * MIB)


def _dot(a, b):
    return jnp.dot(a, b, preferred_element_type=F32)


def _dot_nt(a, b):
    return lax.dot_general(a, b, (((1,), (1,)), ((), ())), preferred_element_type=F32)


def _dot_tn(a, b):
    return lax.dot_general(a, b, (((0,), (0,)), ((), ())), preferred_element_type=F32)


def _rstd(v):
    return lax.rsqrt(jnp.mean(v * v, axis=-1, keepdims=True) + EPS)


def _norm_bwd(xhat, r, g, d):
    t = d * g
    return r * (t - xhat * jnp.mean(t * xhat, axis=-1, keepdims=True))


def _gelu(v):
    return 0.5 * v * (1.0 + lax.erf(v * 0.7071067811865476))


def _gelu_grad(v):
    return 0.5 * (1.0 + lax.erf(v * 0.7071067811865476)) + v * jnp.exp(-0.5 * v * v) * 0.3989422804014327


def _mesh_pos():
    x, y, c = lax.axis_index("x"), lax.axis_index("y"), lax.axis_index("c")
    return x, y, c, 4 * x + 2 * y + c


def _peer(x, y, c, k):
    px = 1 - x if k & 4 else x
    py = 1 - y if k & 2 else y
    pc = 1 - c if k & 1 else c
    return (px, py, pc), 4 * px + 2 * py + pc


ANY = BS(memory_space=pl.ANY)
DMA_SEM = pltpu.SemaphoreType.DMA
ALL_PEERS = (1, 2, 3, 4, 5, 6, 7)
CHIP_PEERS = (2, 4, 6)
SIBLING = 1


def _remote(src, dst, send, recv, peer):
    return pltpu.make_async_remote_copy(src_ref=src, dst_ref=dst, send_sem=send, recv_sem=recv, device_id=peer,
                                        device_id_type=pl.DeviceIdType.MESH)


class _Exchange:
    def __init__(self, ins, out_shapes, sems, copies, aliases=None):
        self.ins, self.out_shapes, self.sems, self.copies, self.aliases = ins, out_shapes, sems, copies, aliases or {}


def _gather_exchange(shards, peers):
    n, m = len(shards), len(peers)

    def copies(ins, outs, sems):
        send, recv, lsem = sems
        x, y, c, me = _mesh_pos()
        cps = [pltpu.make_async_copy(ins[w], outs[w].at[me], lsem.at[w]) for w in range(n)]
        for w in range(n):
            for j, k in enumerate(peers):
                cps.append(_remote(ins[w], outs[w].at[me], send.at[w, j], recv.at[w, j], _peer(x, y, c, k)[0]))
        return cps

    return _Exchange(shards, [SDS((N_DEV,) + s.shape, s.dtype) for s in shards],
                     [DMA_SEM((n, m)), DMA_SEM((n, m)), DMA_SEM((n,))], copies)


def _sibling_exchange(gathered):
    n = len(gathered)

    def copies(ins, outs, sems):
        send, recv = sems
        x, y, c, me = _mesh_pos()
        slots = [me] + [_peer(x, y, c, k)[1] for k in CHIP_PEERS]
        sib = _peer(x, y, c, SIBLING)[0]
        return [_remote(outs[w].at[s], outs[w].at[s], send.at[w, j], recv.at[w, j], sib)
                for w in range(n) for j, s in enumerate(slots)]

    return _Exchange(gathered, [SDS(g.shape, g.dtype) for g in gathered], [DMA_SEM((n, 4)), DMA_SEM((n, 4))], copies,
                     aliases={w: w for w in range(n)})


def _scatter_exchange(fulls):
    n, m = len(fulls), len(ALL_PEERS)

    def copies(ins, outs, sems):
        send, recv, lsem = sems
        x, y, c, me = _mesh_pos()
        cps = [pltpu.make_async_copy(ins[w].at[me], outs[w].at[me], lsem.at[w]) for w in range(n)]
        for w in range(n):
            for j, k in enumerate(ALL_PEERS):
                peer, p = _peer(x, y, c, k)
                cps.append(_remote(ins[w].at[p], outs[w].at[me], send.at[w, j], recv.at[w, j], peer))
        return cps

    return _Exchange(fulls, [SDS(f.shape, f.dtype) for f in fulls],
                     [DMA_SEM((n, m)), DMA_SEM((n, m)), DMA_SEM((n,))], copies)


def _call(body, *, name, args, in_specs, out_shape, out_specs, grid, sem, vmem_mib, scratch=(), exchange=None,
          first=None, last=None):
    if exchange is None:
        return pl.pallas_call(body, name=name, out_shape=out_shape, grid=grid, in_specs=in_specs,
                              out_specs=out_specs, scratch_shapes=list(scratch),
                              compiler_params=_params(sem, vmem_mib))(*args), []
    ni, no, ns = len(args), len(out_shape), len(scratch)
    xi, xo = len(exchange.ins), len(exchange.out_shapes)

    def hosted(*refs):
        own_in, refs = refs[:ni], refs[ni:]
        x_in, refs = refs[:xi], refs[xi:]
        own_out, refs = refs[:no], refs[no:]
        x_out, refs = refs[:xo], refs[xo:]
        own_scr, x_sem = refs[:ns], refs[ns:]

        @pl.when(first())
        def _():
            for cp in exchange.copies(x_in, x_out, x_sem):
                cp.start()

        body(*own_in, *own_out, *own_scr)

        @pl.when(last())
        def _():
            for cp in exchange.copies(x_in, x_out, x_sem):
                cp.wait()

    outs = pl.pallas_call(
        hosted, name=name, out_shape=list(out_shape) + list(exchange.out_shapes), grid=grid,
        in_specs=list(in_specs) + [ANY] * xi, out_specs=list(out_specs) + [ANY] * xo,
        scratch_shapes=list(scratch) + list(exchange.sems),
        input_output_aliases={ni + a: no + b for a, b in exchange.aliases.items()},
        compiler_params=pltpu.CompilerParams(dimension_semantics=tuple("arbitrary" for _ in grid),
                                             vmem_limit_bytes=vmem_mib * MIB, has_side_effects=True),
    )(*args, *exchange.ins)
    return outs[:no], outs[no:]


def _run_exchange(exchange, name):
    xi, xo = len(exchange.ins), len(exchange.out_shapes)

    def body(*refs):
        cps = exchange.copies(refs[:xi], refs[xi:xi + xo], refs[xi + xo:])
        for cp in cps:
            cp.start()
        for cp in cps:
            cp.wait()

    return pl.pallas_call(
        body, name=name, out_shape=list(exchange.out_shapes), in_specs=[ANY] * xi, out_specs=[ANY] * xo,
        scratch_shapes=list(exchange.sems), input_output_aliases=dict(exchange.aliases),
        compiler_params=pltpu.CompilerParams(has_side_effects=True),
    )(*exchange.ins)


def _rms_fwd(h, g, name, tm=512, exchange=None):
    S, D = h.shape
    tm = min(tm, S)
    assert S % tm == 0, (S, tm)
    ni = S // tm

    def body(h_ref, g_ref, o_ref):
        hv = h_ref[...]
        o_ref[...] = (hv * _rstd(hv) * g_ref[...]).astype(o_ref.dtype)

    (out,), extra = _call(
        body, name=name, args=(h, g), out_shape=[SDS((S, D), BF16)], grid=(ni,),
        in_specs=[BS((tm, D), lambda i: (i, 0)), BS((1, D), lambda i: (0, 0))],
        out_specs=[BS((tm, D), lambda i: (i, 0))], sem=("parallel",), vmem_mib=32, exchange=exchange,
        first=lambda: pl.program_id(0) == 0, last=lambda: pl.program_id(0) == ni - 1)
    return out, extra


def _rms_bwd(h, g, dn, dres, name, tm=256):
    S, D = h.shape
    tm = min(tm, S)
    assert S % tm == 0, (S, tm)
    has_res = dres is not None

    def body(*refs):
        if has_res:
            h_ref, g_ref, dn_ref, dres_ref, dh_ref, dg_ref = refs
        else:
            h_ref, g_ref, dn_ref, dh_ref, dg_ref = refs
        hv = h_ref[...]
        r = _rstd(hv)
        xh = hv * r
        d = dn_ref[...].astype(F32)

        @pl.when(pl.program_id(0) == 0)
        def _():
            dg_ref[...] = jnp.zeros_like(dg_ref)

        dg_ref[...] += jnp.sum(d * xh, axis=0, keepdims=True)
        dh = _norm_bwd(xh, r, g_ref[...], d)
        dh_ref[...] = dres_ref[...] + dh if has_res else dh

    row = BS((tm, D), lambda i: (i, 0))
    vec = BS((1, D), lambda i: (0, 0))
    args = (h, g, dn) + ((dres,) if has_res else ())
    return pl.pallas_call(
        body, name=name, out_shape=[SDS((S, D), F32), SDS((1, D), F32)], grid=(S // tm,),
        in_specs=[row, vec, row] + ([row] if has_res else []), out_specs=[row, vec],
        compiler_params=_params(("arbitrary",), 40),
    )(*args)


def _final_loss(h, g, target, name, tm=256):
    S, D = h.shape
    tm = min(tm, S)
    assert S % tm == 0, (S, tm)

    def body(h_ref, g_ref, t_ref, dh_ref, dg_ref, loss_ref):
        hv = h_ref[...]
        r = _rstd(hv)
        xh = hv * r
        gv = g_ref[...]
        e = xh * gv - t_ref[...]

        @pl.when(pl.program_id(0) == 0)
        def _():
            dg_ref[...] = jnp.zeros_like(dg_ref)
            loss_ref[...] = jnp.zeros_like(loss_ref)

        loss_ref[...] += 0.5 * jnp.sum(jnp.mean(e * e, axis=-1, keepdims=True), axis=0, keepdims=True)
        dy = e * (1.0 / D)
        dg_ref[...] += jnp.sum(dy * xh, axis=0, keepdims=True)
        dh_ref[...] = _norm_bwd(xh, r, gv, dy)

    row = BS((tm, D), lambda i: (i, 0))
    vec = BS((1, D), lambda i: (0, 0))
    return pl.pallas_call(
        body, name=name, out_shape=[SDS((S, D), F32), SDS((1, D), F32), SDS((1, 128), F32)], grid=(S // tm,),
        in_specs=[row, vec, row], out_specs=[row, vec, BS((1, 128), lambda i: (0, 0))],
        compiler_params=_params(("arbitrary",), 40),
    )(h, g, target)


def _matmul(a, b, mode, out_dtype, name, res=None, tm=512, tn=512, tk=512):
    if mode == "tn":
        K, M = a.shape
        N = b.shape[1]
        tm, tn, tk = min(tm, M), min(tn, N), min(tk, K)
        assert M % tm == 0 and N % tn == 0 and K % tk == 0, (a.shape, b.shape, tm, tn, tk)
        nk = K // tk

        def body(a_ref, b_ref, o_ref, acc_ref):
            k = pl.program_id(2)

            @pl.when(k == 0)
            def _():
                acc_ref[...] = jnp.zeros_like(acc_ref)

            acc_ref[...] += _dot_tn(a_ref[...].astype(BF16), b_ref[...].astype(BF16))

            @pl.when(k == nk - 1)
            def _():
                o_ref[...] = acc_ref[...].astype(o_ref.dtype)

        return pl.pallas_call(
            body, name=name, out_shape=SDS((M, N), out_dtype), grid=(M // tm, N // tn, nk),
            in_specs=[BS((tk, tm), lambda i, j, k: (k, i)), BS((tk, tn), lambda i, j, k: (k, j))],
            out_specs=BS((tm, tn), lambda i, j, k: (i, j)), scratch_shapes=[pltpu.VMEM((tm, tn), F32)],
            compiler_params=_params(("parallel", "parallel", "arbitrary"), 48),
        )(a, b)

    M, K = a.shape
    N = b.shape[1] if mode == "nn" else b.shape[0]
    tm, tn = min(tm, M), min(tn, N)
    assert M % tm == 0 and N % tn == 0, (a.shape, b.shape, tm, tn)
    has_res = res is not None

    def body(*refs):
        if has_res:
            a_ref, b_ref, r_ref, o_ref = refs
        else:
            a_ref, b_ref, o_ref = refs
        av, bv = a_ref[...].astype(BF16), b_ref[...].astype(BF16)
        acc = _dot(av, bv) if mode == "nn" else _dot_nt(av, bv)
        if has_res:
            acc = acc + r_ref[...]
        o_ref[...] = acc.astype(o_ref.dtype)

    b_spec = BS((K, tn), lambda i, j: (0, j)) if mode == "nn" else BS((tn, K), lambda i, j: (j, 0))
    o_spec = BS((tm, tn), lambda i, j: (i, j))
    return pl.pallas_call(
        body, name=name, out_shape=SDS((M, N), out_dtype), grid=(M // tm, N // tn),
        in_specs=[BS((tm, K), lambda i, j: (i, 0)), b_spec] + ([o_spec] if has_res else []), out_specs=o_spec,
        compiler_params=_params(("parallel", "parallel"), 48),
    )(*((a, b) + ((res,) if has_res else ())))


def _ffn_fwd(h, g, wg, wu, wd, name, tm=512, exchange=None):
    S, D = h.shape
    nb, _, Fs = wg.shape
    tm = min(tm, S)
    sub = min(FFN_SUB_ROWS, tm)
    assert S % tm == 0 and tm % sub == 0, (S, tm, sub)

    def body(h_ref, g_ref, wg_ref, wu_ref, wd_ref, o_ref, n_ref, a_ref, b_ref):
        j = pl.program_id(1)

        @pl.when(j == 0)
        def _():
            hv = h_ref[...]
            n_ref[...] = (hv * _rstd(hv) * g_ref[...]).astype(BF16)
            o_ref[...] = jnp.zeros_like(o_ref)

        for r in range(0, tm, sub):
            rows = slice(r, r + sub)
            n = n_ref[rows, :]
            a = _dot(n, wg_ref[...]).astype(BF16)
            b = _dot(n, wu_ref[...]).astype(BF16)
            a_ref[rows, :] = a
            b_ref[rows, :] = b
            a, b = a.astype(F32), b.astype(F32)
            s = (a * jax.nn.sigmoid(a) * b).astype(BF16)
            o_ref[rows, :] += _dot(s, wd_ref[...])

        @pl.when(j == nb - 1)
        def _():
            o_ref[...] = h_ref[...] + 0.5 * o_ref[...]

    row = BS((tm, D), lambda i, j: (i, 0))
    wcol = BS((None, D, Fs), lambda i, j: (j, 0, 0))
    act = BS((None, tm, Fs), lambda i, j: (j, i, 0))
    ni = S // tm
    return _call(
        body, name=name, args=(h, g, wg, wu, wd),
        out_shape=[SDS((S, D), F32), SDS((S, D), BF16), SDS((nb, S, Fs), BF16), SDS((nb, S, Fs), BF16)],
        grid=(ni, nb),
        in_specs=[row, BS((1, D), lambda i, j: (0, 0)), wcol, wcol, BS((None, Fs, D), lambda i, j: (j, 0, 0))],
        out_specs=[row, row, act, act], sem=("parallel", "arbitrary"), vmem_mib=56, exchange=exchange,
        first=lambda: (pl.program_id(0) == 0) & (pl.program_id(1) == 0),
        last=lambda: (pl.program_id(0) == ni - 1) & (pl.program_id(1) == nb - 1))


def _ffn_bwd_dx(dh, a, b, wg, wu, wd, name, tm=512, exchange=None):
    S, D = dh.shape
    nb, _, Fs = wg.shape
    tm = min(tm, S)
    sub = min(FFN_SUB_ROWS, tm)
    assert S % tm == 0 and tm % sub == 0, (S, tm, sub)

    def body(dh_ref, a_ref, b_ref, wg_ref, wu_ref, wd_ref, dn_ref, df_ref, da_ref, db_ref, s_ref):
        j = pl.program_id(1)

        @pl.when(j == 0)
        def _():
            df_ref[...] = (0.5 * dh_ref[...]).astype(BF16)
            dn_ref[...] = jnp.zeros_like(dn_ref)

        for r in range(0, tm, sub):
            rows = slice(r, r + sub)
            ds = _dot_nt(df_ref[rows, :], wd_ref[...])
            av, bv = a_ref[rows, :].astype(F32), b_ref[rows, :].astype(F32)
            sig = jax.nn.sigmoid(av)
            sl = av * sig
            da = (ds * bv * (sig * (1.0 + av * (1.0 - sig)))).astype(BF16)
            db = (ds * sl).astype(BF16)
            da_ref[rows, :] = da
            db_ref[rows, :] = db
            s_ref[rows, :] = (sl * bv).astype(BF16)
            dn_ref[rows, :] += _dot_nt(da, wg_ref[...]) + _dot_nt(db, wu_ref[...])

    row = BS((tm, D), lambda i, j: (i, 0))
    wcol = BS((None, D, Fs), lambda i, j: (j, 0, 0))
    act = BS((None, tm, Fs), lambda i, j: (j, i, 0))
    ni = S // tm
    return _call(
        body, name=name, args=(dh, a, b, wg, wu, wd),
        out_shape=[SDS((S, D), F32), SDS((S, D), BF16)] + [SDS((nb, S, Fs), BF16)] * 3,
        grid=(ni, nb),
        in_specs=[row, act, act, wcol, wcol, BS((None, Fs, D), lambda i, j: (j, 0, 0))],
        out_specs=[row, row, act, act, act], sem=("parallel", "arbitrary"), vmem_mib=56, exchange=exchange,
        first=lambda: (pl.program_id(0) == 0) & (pl.program_id(1) == 0),
        last=lambda: (pl.program_id(0) == ni - 1) & (pl.program_id(1) == nb - 1))


DW_FLIPS = {0: (6, 2, 4, 0), 1: (6, 4, 2, 0)}
N_CHIPS = N_DEV // 2


def _dw_block_order():
    x, y, c, me = _mesh_pos()
    steps = [jnp.array([v for mine, sib in zip(DW_FLIPS[core], DW_FLIPS[1 - core]) for v in (sib ^ 1, mine)], jnp.int32)
             for core in (0, 1)]
    return jnp.bitwise_xor(me.astype(jnp.int32), jnp.where(c == 0, steps[0], steps[1]))


def _ffn_bwd_dw(pairs, order, name, tk=512, exchange=None):
    npair = len(pairs)
    S = pairs[0][0].shape[-2]
    nb, half = N_DEV, N_DEV // 2
    tk = min(tk, S)
    assert S % tk == 0, (S, tk)
    nk = S // tk
    shapes = [(lhs.shape[-1], rhs.shape[-1]) for lhs, rhs in pairs]
    xi = len(exchange.ins) if exchange else 0
    xo = len(exchange.out_shapes) if exchange else 0

    def body(order_ref, *rest):
        tiles, rest = rest[:2 * npair], rest[2 * npair:]
        x_in, rest = rest[:xi], rest[xi:]
        recv_bufs, rest = rest[:npair], rest[npair:]
        x_out, rest = rest[:xo], rest[xo:]
        accs, rest = rest[:npair], rest[npair:]
        out_t, rest = rest[:npair], rest[npair:]
        out_m, rest = rest[:npair], rest[npair:]
        land, rest = rest[:npair], rest[npair:]
        (send_t, recv_t, send_m, recv_m, lsem, credit), x_sem = rest[:6], rest[6:]
        t, k = pl.program_id(0), pl.program_id(1)
        x, y, c, me = _mesh_pos()
        sibling = (x, y, 1 - c)
        chip = 2 * x + y

        if exchange:
            @pl.when((t == 0) & (k == 0))
            def _():
                for cp in exchange.copies(x_in, x_out, x_sem):
                    cp.start()

        @pl.when(k == 0)
        def _():
            for acc in accs:
                acc[...] = jnp.zeros_like(acc)

        for w in range(npair):
            accs[w][...] += _dot_tn(tiles[2 * w][...], tiles[2 * w + 1][...])

        def to_sibling(w, i):
            return _remote(out_t[w], land[w], send_t.at[w, i], recv_t.at[w, i], sibling)

        def to_owner(w, i):
            dst = recv_bufs[w].at[chip]
            if i == half - 1:
                return pltpu.make_async_copy(out_m[w], dst, lsem.at[w])
            p = jnp.bitwise_xor(me, jnp.where(c == 0, DW_FLIPS[0][i], DW_FLIPS[1][i]))
            return _remote(out_m[w], dst, send_m.at[w, i], recv_m.at[w, i], (p >> 2, (p >> 1) & 1, p & 1))

        for i in range(half):
            @pl.when((t == 2 * i) & (k == nk - 1))
            def _(i=i):
                if i >= 1:
                    pl.semaphore_wait(credit, 1)
                for w in range(npair):
                    if i >= 1:
                        to_sibling(w, i - 1).wait_send()
                    out_t[w][...] = accs[w][...].astype(BF16)
                    to_sibling(w, i).start()

            @pl.when((t == 2 * i + 1) & (k == nk - 1))
            def _(i=i):
                for w in range(npair):
                    to_sibling(w, i).wait_recv()
                    if i >= 1:
                        to_owner(w, i - 1).wait_send()
                    out_m[w][...] = (accs[w][...] + land[w][...].astype(F32)).astype(BF16)
                if i < half - 1:
                    pl.semaphore_signal(credit, inc=1, device_id=sibling, device_id_type=pl.DeviceIdType.MESH)
                for w in range(npair):
                    to_owner(w, i).start()

        @pl.when((t == nb - 1) & (k == nk - 1))
        def _():
            for w in range(npair):
                to_sibling(w, half - 1).wait_send()
                to_owner(w, half - 1).wait()
                for i in range(half - 1):
                    to_owner(w, i).wait_recv()
            if exchange:
                for cp in exchange.copies(x_in, x_out, x_sem):
                    cp.wait()

    def tile_spec(arr):
        if arr.ndim == 3:
            return BS((None, tk, arr.shape[-1]), lambda t, k, o: (o[t], k, 0))
        return BS((tk, arr.shape[-1]), lambda t, k, o: (k, 0))

    flat = [a for pair in pairs for a in pair]
    grid_spec = pltpu.PrefetchScalarGridSpec(
        num_scalar_prefetch=1, grid=(nb, nk),
        in_specs=[tile_spec(a) for a in flat] + [ANY] * xi, out_specs=[ANY] * (npair + xo),
        scratch_shapes=[pltpu.VMEM(s, F32) for s in shapes] + [pltpu.VMEM(s, BF16) for s in shapes] * 3
        + [DMA_SEM((npair, half)), DMA_SEM((npair, half)), DMA_SEM((npair, half - 1)), DMA_SEM((npair, half - 1)),
           DMA_SEM((npair,)), pltpu.SemaphoreType.REGULAR]
        + (list(exchange.sems) if exchange else []))
    outs = pl.pallas_call(
        body, name=name, grid_spec=grid_spec,
        out_shape=[SDS((N_CHIPS,) + s, BF16) for s in shapes] + (list(exchange.out_shapes) if exchange else []),
        compiler_params=pltpu.CompilerParams(dimension_semantics=("arbitrary", "arbitrary"),
                                             vmem_limit_bytes=48 * MIB, has_side_effects=True),
    )(order, *flat, *(exchange.ins if exchange else ()))
    return outs[:npair], outs[npair:]


def _sgu_parts(z, gv, ws_ref, bst_ref, groups):
    da = z.shape[1] // 2
    zu, zv = z[:, :da], z[:, da:]
    u, v = _gelu(zu), _gelu(zv)
    rv = _rstd(v)
    vhat = v * rv
    vn = (vhat * gv).astype(BF16)
    tri = lax.broadcasted_iota(jnp.int32, (CHUNK, CHUNK), 0) >= lax.broadcasted_iota(jnp.int32, (CHUNK, CHUNK), 1)
    pieces = []
    for g in range(groups):
        w = jnp.where(tri, ws_ref[g], 0.0).astype(BF16)
        pieces.append(_dot(w, vn[:, g * CHUNK:(g + 1) * CHUNK]) + bst_ref[:, g:g + 1])
    sv = jnp.concatenate(pieces, axis=1)
    return dict(zu=zu, zv=zv, u=u, rv=rv, vhat=vhat, vn=vn, sv=sv, tri=tri)


def _sgu_fwd(z, g_v, w_s, b_st, g_a, name):
    S = z.shape[0]
    groups = w_s.shape[0]
    da = groups * CHUNK

    def body(z_ref, gv_ref, ws_ref, bst_ref, ga_ref, o_ref):
        p = _sgu_parts(z_ref[...], gv_ref[...], ws_ref, bst_ref, groups)
        ya = p["u"] * p["sv"]
        o_ref[...] = (ya * _rstd(ya) * ga_ref[...]).astype(BF16)

    vec = BS((1, da), lambda i: (0, 0))
    return pl.pallas_call(
        body, name=name, out_shape=SDS((S, da), BF16), grid=(S // CHUNK,),
        in_specs=[BS((CHUNK, 2 * da), lambda i: (i, 0)), vec, BS((groups, CHUNK, CHUNK), lambda i: (0, 0, 0)),
                  BS((CHUNK, groups), lambda i: (0, 0)), vec],
        out_specs=BS((CHUNK, da), lambda i: (i, 0)), compiler_params=_params(("parallel",), 32),
    )(z, g_v, w_s, b_st, g_a)


def _sgu_bwd(z, dy, g_v, w_s, w_st, b_st, g_a, name):
    S = z.shape[0]
    groups = w_s.shape[0]
    da = groups * CHUNK

    def body(z_ref, dy_ref, gv_ref, ws_ref, wst_ref, bst_ref, ga_ref, dz_ref, dws_ref, dbst_ref, dgv_ref, dga_ref):
        @pl.when(pl.program_id(0) == 0)
        def _():
            dws_ref[...] = jnp.zeros_like(dws_ref)
            dbst_ref[...] = jnp.zeros_like(dbst_ref)
            dgv_ref[...] = jnp.zeros_like(dgv_ref)
            dga_ref[...] = jnp.zeros_like(dga_ref)

        gv = gv_ref[...]
        p = _sgu_parts(z_ref[...], gv, ws_ref, bst_ref, groups)
        u, sv, tri = p["u"], p["sv"], p["tri"]
        ya = u * sv
        ra = _rstd(ya)
        yhat = ya * ra
        d = dy_ref[...]
        dga_ref[...] += jnp.sum(d * yhat, axis=0, keepdims=True)
        dya = _norm_bwd(yhat, ra, ga_ref[...], d)
        du = dya * sv
        dsv = dya * u
        dsv_b = dsv.astype(BF16)
        tri_t = (lax.broadcasted_iota(jnp.int32, (CHUNK, CHUNK), 0)
                 <= lax.broadcasted_iota(jnp.int32, (CHUNK, CHUNK), 1))
        lane = lax.broadcasted_iota(jnp.int32, (CHUNK, groups), 1)
        dvn = []
        dbs = jnp.zeros((CHUNK, groups), F32)
        for g in range(groups):
            cols = slice(g * CHUNK, (g + 1) * CHUNK)
            dbs = dbs + jnp.where(lane == g, jnp.sum(dsv[:, cols], axis=1, keepdims=True), 0.0)
            dws_ref[g] += jnp.where(tri, _dot_nt(dsv_b[:, cols], p["vn"][:, cols]), 0.0)
            wt = jnp.where(tri_t, wst_ref[g], 0.0).astype(BF16)
            dvn.append(_dot(wt, dsv_b[:, cols]))
        dbst_ref[...] += dbs
        dvn = jnp.concatenate(dvn, axis=1)
        dgv_ref[...] += jnp.sum(dvn * p["vhat"], axis=0, keepdims=True)
        dv = _norm_bwd(p["vhat"], p["rv"], gv, dvn)
        dz_ref[...] = jnp.concatenate([du * _gelu_grad(p["zu"]), dv * _gelu_grad(p["zv"])], axis=1)

    vec = BS((1, da), lambda i: (0, 0))
    wsq = BS((groups, CHUNK, CHUNK), lambda i: (0, 0, 0))
    bsq = BS((CHUNK, groups), lambda i: (0, 0))
    return pl.pallas_call(
        body, name=name,
        out_shape=[SDS((S, 2 * da), F32), SDS((groups, CHUNK, CHUNK), F32), SDS((CHUNK, groups), F32),
                   SDS((1, da), F32), SDS((1, da), F32)],
        grid=(S // CHUNK,),
        in_specs=[BS((CHUNK, 2 * da), lambda i: (i, 0)), BS((CHUNK, da), lambda i: (i, 0)), vec, wsq, wsq, bsq, vec],
        out_specs=[BS((CHUNK, 2 * da), lambda i: (i, 0)), wsq, bsq, vec, vec],
        compiler_params=_params(("arbitrary",), 32),
    )(z, dy, g_v, w_s, w_st, b_st, g_a)


def _swa_mask(i, group):
    row = lax.broadcasted_iota(jnp.int32, (group * CHUNK, 2 * CHUNK), 0) & (CHUNK - 1)
    col = lax.broadcasted_iota(jnp.int32, (group * CHUNK, 2 * CHUNK), 1)
    d = row + CHUNK - col
    return (d >= 0) & (d < CHUNK) & jnp.logical_or(i > 0, col >= CHUNK)


def _stack_heads(t, g, group):
    return jnp.concatenate([t[:, h * HEAD_DIM:(h + 1) * HEAD_DIM] for h in range(g * group, (g + 1) * group)], axis=0)


def _unstack_heads(stacked, group):
    return [stacked[h * CHUNK:(h + 1) * CHUNK] for h in range(group)]


def _swa_probs(qh, kh, sink, mask):
    s = jnp.where(mask, _dot_nt(qh, kh) * (HEAD_DIM ** -0.5), NEG)
    m = jnp.maximum(jnp.max(s, axis=-1, keepdims=True), sink)
    e = jnp.exp(s - m)
    es = jnp.exp(sink - m)
    inv = 1.0 / (jnp.sum(e, axis=-1, keepdims=True) + es)
    return e * inv, es * inv


def _swa_specs(db, nblk, clamp):
    kvw = 2 * KV_HEADS * HEAD_DIM
    cur = (lambda i: jnp.minimum(i, nblk - 1)) if clamp else (lambda i: i)
    q_spec = BS((CHUNK, db), lambda i: (cur(i), 2))
    kc_spec = BS((CHUNK, kvw), lambda i: (cur(i), 3 * db // kvw))
    kp_spec = BS((CHUNK, kvw), lambda i: (jnp.maximum(cur(i) - 1, 0), 3 * db // kvw))
    return q_spec, kc_spec, kp_spec


def _swa_fwd(z, sink_rows, g_b, name, exchange=None):
    S = z.shape[0]
    db = g_b.shape[1]
    heads = db // HEAD_DIM
    group = heads // KV_HEADS
    nblk = S // CHUNK

    def body(q_ref, kc_ref, kp_ref, sk_ref, gb_ref, yb_ref, ybn_ref):
        mask = _swa_mask(pl.program_id(0), group)
        q = q_ref[...].astype(BF16)
        kv = jnp.concatenate([kp_ref[...], kc_ref[...]], axis=0).astype(BF16)
        outs = []
        for g in range(KV_HEADS):
            kg = kv[:, g * HEAD_DIM:(g + 1) * HEAD_DIM]
            vg = kv[:, (KV_HEADS + g) * HEAD_DIM:(KV_HEADS + g + 1) * HEAD_DIM]
            rows = slice(g * group * CHUNK, (g + 1) * group * CHUNK)
            p, _ = _swa_probs(_stack_heads(q, g, group), kg, sk_ref[rows, :], mask)
            outs += _unstack_heads(_dot(p.astype(BF16), vg), group)
        yb = jnp.concatenate(outs, axis=1)
        yb_ref[...] = yb
        ybn_ref[...] = (yb * _rstd(yb) * gb_ref[...]).astype(BF16)

    q_spec, kc_spec, kp_spec = _swa_specs(db, nblk, False)
    out = BS((CHUNK, db), lambda i: (i, 0))
    return _call(
        body, name=name, args=(z, z, z, sink_rows, g_b), out_shape=[SDS((S, db), F32), SDS((S, db), BF16)],
        grid=(nblk,),
        in_specs=[q_spec, kc_spec, kp_spec, BS((heads * CHUNK, 1), lambda i: (0, 0)), BS((1, db), lambda i: (0, 0))],
        out_specs=[out, out], sem=("parallel",), vmem_mib=32, exchange=exchange,
        first=lambda: pl.program_id(0) == 0, last=lambda: pl.program_id(0) == nblk - 1)


def _swa_bwd(z, yb, dy, sink_rows, g_b, name):
    S = z.shape[0]
    db = g_b.shape[1]
    heads = db // HEAD_DIM
    group = heads // KV_HEADS
    nblk = S // CHUNK
    kvw = 2 * KV_HEADS * HEAD_DIM

    def body(q_ref, kc_ref, kp_ref, yb_ref, dy_ref, sk_ref, gb_ref, dq_ref, dkv_ref, dsk_ref, dgb_ref, carry_ref):
        i = pl.program_id(0)

        @pl.when(i == 0)
        def _():
            carry_ref[...] = jnp.zeros_like(carry_ref)
            dsk_ref[...] = jnp.zeros_like(dsk_ref)
            dgb_ref[...] = jnp.zeros_like(dgb_ref)

        @pl.when(i < nblk)
        def _():
            mask = _swa_mask(i, group)
            yb = yb_ref[...]
            rb = _rstd(yb)
            yhat = yb * rb
            d = dy_ref[...]
            dgb_ref[...] += jnp.sum(d * yhat, axis=0, keepdims=True)
            do = _norm_bwd(yhat, rb, gb_ref[...], d).astype(BF16)
            q = q_ref[...].astype(BF16)
            kv = jnp.concatenate([kp_ref[...], kc_ref[...]], axis=0).astype(BF16)
            dqs, dk, dv = [], [], []
            lane = lax.broadcasted_iota(jnp.int32, (1, heads), 1)
            dsinks = jnp.zeros((1, heads), F32)
            for g in range(KV_HEADS):
                kg = kv[:, g * HEAD_DIM:(g + 1) * HEAD_DIM]
                vg = kv[:, (KV_HEADS + g) * HEAD_DIM:(KV_HEADS + g + 1) * HEAD_DIM]
                rows = slice(g * group * CHUNK, (g + 1) * group * CHUNK)
                qg, dog = _stack_heads(q, g, group), _stack_heads(do, g, group)
                p, ps = _swa_probs(qg, kg, sk_ref[rows, :], mask)
                dp = _dot_nt(dog, vg)
                dr = jnp.sum(p * dp, axis=-1, keepdims=True)
                ds = (p * (dp - dr) * (HEAD_DIM ** -0.5)).astype(BF16)
                for h, t in enumerate(_unstack_heads(ps * dr, group)):
                    dsinks = dsinks - jnp.where(lane == g * group + h, jnp.sum(t, axis=0, keepdims=True), 0.0)
                dqs += _unstack_heads(_dot(ds, kg), group)
                dk.append(_dot_tn(ds, qg))
                dv.append(_dot_tn(p.astype(BF16), dog))
            dq_ref[...] = jnp.concatenate(dqs, axis=1)
            dsk_ref[...] += dsinks
            contrib = jnp.concatenate(dk + dv, axis=1)
            dkv_ref[...] = carry_ref[...] + contrib[:CHUNK]
            carry_ref[...] = contrib[CHUNK:]

        @pl.when(i == nblk)
        def _():
            dkv_ref[...] = carry_ref[...]

    q_spec, kc_spec, kp_spec = _swa_specs(db, nblk, True)
    cur = BS((CHUNK, db), lambda i: (jnp.minimum(i, nblk - 1), 0))
    return pl.pallas_call(
        body, name=name,
        out_shape=[SDS((S, db), F32), SDS((S, kvw), F32), SDS((1, heads), F32), SDS((1, db), F32)],
        grid=(nblk + 1,),
        in_specs=[q_spec, kc_spec, kp_spec, cur, BS((CHUNK, db), lambda i: (jnp.minimum(i, nblk - 1), 1)),
                  BS((heads * CHUNK, 1), lambda i: (0, 0)), BS((1, db), lambda i: (0, 0))],
        out_specs=[cur, BS((CHUNK, kvw), lambda i: (jnp.maximum(i - 1, 0), 0)), BS((1, heads), lambda i: (0, 0)),
                   BS((1, db), lambda i: (0, 0))],
        scratch_shapes=[pltpu.VMEM((CHUNK, kvw), F32)], compiler_params=_params(("arbitrary",), 32),
    )(z, z, z, yb, dy, sink_rows, g_b)


def _xattn_probs(qh, kh, hd):
    s = _dot_nt(qh, kh) * (hd ** -0.5)
    e = jnp.exp(s - jnp.max(s, axis=-1, keepdims=True))
    return e / jnp.sum(e, axis=-1, keepdims=True)


def _xattn_fwd(q, kv, name, tq=512):
    S, D = q.shape
    M = kv.shape[0]
    hd = D // X_HEADS
    tq = min(tq, S)
    assert S % tq == 0, (S, tq)

    def body(q_ref, kv_ref, o_ref):
        for h in range(X_HEADS):
            cols = slice(h * hd, (h + 1) * hd)
            p = _xattn_probs(q_ref[:, cols], kv_ref[:, cols], hd)
            o_ref[:, cols] = _dot(p.astype(BF16), kv_ref[:, D + h * hd:D + (h + 1) * hd]).astype(BF16)

    row = BS((tq, D), lambda i: (i, 0))
    return pl.pallas_call(
        body, name=name, out_shape=SDS((S, D), BF16), grid=(S // tq,),
        in_specs=[row, BS((M, 2 * D), lambda i: (0, 0))], out_specs=row, compiler_params=_params(("parallel",), 40),
    )(q, kv)


def _xattn_bwd(q, kv, do, name, tq=512):
    S, D = q.shape
    M = kv.shape[0]
    hd = D // X_HEADS
    tq = min(tq, S)
    assert S % tq == 0, (S, tq)

    def body(q_ref, kv_ref, do_ref, dq_ref, dkv_ref):
        @pl.when(pl.program_id(0) == 0)
        def _():
            dkv_ref[...] = jnp.zeros_like(dkv_ref)

        for h in range(X_HEADS):
            cols = slice(h * hd, (h + 1) * hd)
            vcols = slice(D + h * hd, D + (h + 1) * hd)
            qh, kh, vh, doh = q_ref[:, cols], kv_ref[:, cols], kv_ref[:, vcols], do_ref[:, cols]
            p = _xattn_probs(qh, kh, hd)
            dp = _dot_nt(doh, vh)
            ds = (p * (dp - jnp.sum(p * dp, axis=-1, keepdims=True)) * (hd ** -0.5)).astype(BF16)
            dq_ref[:, cols] = _dot(ds, kh).astype(BF16)
            dkv_ref[:, cols] += _dot_tn(ds, qh)
            dkv_ref[:, vcols] += _dot_tn(p.astype(BF16), doh)

    row = BS((tq, D), lambda i: (i, 0))
    full = BS((M, 2 * D), lambda i: (0, 0))
    return pl.pallas_call(
        body, name=name, out_shape=[SDS((S, D), BF16), SDS((M, 2 * D), F32)], grid=(S // tq,),
        in_specs=[row, full, row], out_specs=[row, full], compiler_params=_params(("arbitrary",), 40),
    )(q, kv, do)


def _row_tile(rows, cap):
    best = 8
    for t in range(8, min(rows, cap) + 1, 8):
        if rows % t == 0:
            best = t
    assert rows % best == 0, (rows, cap)
    return best


ADAM_TILE_ELEMS = 256 * 1024


def _adamw(w, m, v, recv, name):
    R, C = w.shape
    slots = recv.shape[0]
    tr = _row_tile(R, max(8, ADAM_TILE_ELEMS // C))

    def body(w_ref, m_ref, v_ref, r_ref, g_ref, d_ref, nm_ref, nv_ref):
        g = r_ref[0].astype(F32)
        for s in range(1, slots):
            g = g + r_ref[s].astype(F32)
        mn = ADAM_B1 * m_ref[...] + (1.0 - ADAM_B1) * g
        vn = ADAM_B2 * v_ref[...] + (1.0 - ADAM_B2) * jnp.square(g)
        m_hat = mn / (1.0 - ADAM_B1 ** ADAM_STEP)
        v_hat = vn / (1.0 - ADAM_B2 ** ADAM_STEP)
        g_ref[...] = g
        d_ref[...] = -ADAM_LR * (m_hat / (jnp.sqrt(v_hat) + ADAM_EPS) + ADAM_WD * w_ref[...])
        nm_ref[...] = mn
        nv_ref[...] = vn

    row = BS((tr, C), lambda i: (i, 0))
    return pl.pallas_call(
        body, name=name, out_shape=[SDS((R, C), F32)] * 4, grid=(R // tr,),
        in_specs=[row, row, row, BS((slots, tr, C), lambda i: (0, i, 0))], out_specs=[row] * 4,
        compiler_params=_params(("parallel",), 40),
    )(w, m, v, recv)


def _cols_to_blocks(full):
    r, c = full.shape
    return full.reshape(r, N_DEV, c // N_DEV).transpose(1, 0, 2)


def _blocks_to_cols(blocks):
    n, r, c = blocks.shape
    return blocks.transpose(1, 0, 2).reshape(r, n * c)


def _pack(parts):
    flat = jnp.concatenate([p.reshape(-1).astype(F32) for p in parts])
    pad = (-flat.shape[0]) % (128 * 128)
    return jnp.pad(flat, (0, pad)).reshape(-1, 128)


def kernel(x, mem, g_ffn1, w1_gate, w1_up, w1_down, g_mix, w_in, g_v, w_s, b_s, sinks, g_a_out, g_b_out, w_out, g_x, g_mem, w_xq, w_xkv, w_xo, g_ffn2, w2_gate, w2_up, w2_down, g_final, loss_target, m_g_ffn1, m_w1_gate, m_w1_up, m_w1_down, m_g_mix, m_w_in, m_g_v, m_w_s, m_b_s, m_sinks, m_g_a_out, m_g_b_out, m_w_out, m_g_x, m_g_mem, m_w_xq, m_w_xkv, m_w_xo, m_g_ffn2, m_w2_gate, m_w2_up, m_w2_down, m_g_final, v_g_ffn1, v_w1_gate, v_w1_up, v_w1_down, v_g_mix, v_w_in, v_g_v, v_w_s, v_b_s, v_sinks, v_g_a_out, v_g_b_out, v_w_out, v_g_x, v_g_mem, v_w_xq, v_w_xkv, v_w_xo, v_g_ffn2, v_w2_gate, v_w2_up, v_w2_down, v_g_final):
    w = dict(g_ffn1=g_ffn1, w1_gate=w1_gate, w1_up=w1_up, w1_down=w1_down, g_mix=g_mix, w_in=w_in, g_v=g_v, w_s=w_s,
             b_s=b_s, sinks=sinks, g_a_out=g_a_out, g_b_out=g_b_out, w_out=w_out, g_x=g_x, g_mem=g_mem, w_xq=w_xq,
             w_xkv=w_xkv, w_xo=w_xo, g_ffn2=g_ffn2, w2_gate=w2_gate, w2_up=w2_up, w2_down=w2_down, g_final=g_final)
    mom = dict(g_ffn1=m_g_ffn1, w1_gate=m_w1_gate, w1_up=m_w1_up, w1_down=m_w1_down, g_mix=m_g_mix, w_in=m_w_in,
               g_v=m_g_v, w_s=m_w_s, b_s=m_b_s, sinks=m_sinks, g_a_out=m_g_a_out, g_b_out=m_g_b_out, w_out=m_w_out,
               g_x=m_g_x, g_mem=m_g_mem, w_xq=m_w_xq, w_xkv=m_w_xkv, w_xo=m_w_xo, g_ffn2=m_g_ffn2,
               w2_gate=m_w2_gate, w2_up=m_w2_up, w2_down=m_w2_down, g_final=m_g_final)
    var = dict(g_ffn1=v_g_ffn1, w1_gate=v_w1_gate, w1_up=v_w1_up, w1_down=v_w1_down, g_mix=v_g_mix, w_in=v_w_in,
               g_v=v_g_v, w_s=v_w_s, b_s=v_b_s, sinks=v_sinks, g_a_out=v_g_a_out, g_b_out=v_g_b_out, w_out=v_w_out,
               g_x=v_g_x, g_mem=v_g_mem, w_xq=v_w_xq, w_xkv=v_w_xkv, w_xo=v_w_xo, g_ffn2=v_g_ffn2,
               w2_gate=v_w2_gate, w2_up=v_w2_up, w2_down=v_w2_down, g_final=v_g_final)

    xs, ms, tgt = x[0], mem[0], loss_target[0]
    D = xs.shape[1]
    d_a = w_s.shape[1] * CHUNK
    d_b = D - d_a
    kvw = 2 * KV_HEADS * HEAD_DIM

    shard = {k: w[k][0].astype(BF16) for k in BIG}
    part = _run_exchange(_gather_exchange([shard[k] for k in FFN1_W], CHIP_PEERS), "ag_ffn1_ici")
    wg1, wu1, wd1 = _run_exchange(_sibling_exchange(part), "ag_ffn1_d2d")
    gf = g_final.reshape(1, D)
    ws, ws_t, bs_t = w_s[0], jnp.swapaxes(w_s[0], 1, 2), b_s[0].T
    sink_rows = jnp.repeat(sinks.reshape(-1), CHUNK).reshape(-1, 1)

    (h1, n1, a1, b1), part = _ffn_fwd(xs, g_ffn1, wg1, wu1, wd1, "ffn1_fwd",
                                      exchange=_gather_exchange([shard[k] for k in REST_W], CHIP_PEERS))
    part = dict(zip(REST_W, part))
    n2, (win_blocks,) = _rms_fwd(h1, g_mix, "mix_norm", exchange=_sibling_exchange([part["w_in"]]))
    win = _blocks_to_cols(win_blocks)
    z = _matmul(n2, win, "nn", F32, "mm_in", tn=win.shape[1] // 2)
    ya_n = _sgu_fwd(z, g_v, ws, bs_t, g_a_out, "sgu_fwd")
    later = [k for k in REST_W if k != "w_in"]
    (yb, yb_n), handed = _swa_fwd(z, sink_rows, g_b_out, "swa_fwd",
                                  exchange=_sibling_exchange([part[k] for k in later]))
    gathered = dict(zip(later, handed))
    wg2, wu2, wd2 = gathered["w2_gate"], gathered["w2_up"], gathered["w2_down"]
    wxkv = _blocks_to_cols(gathered["w_xkv"])
    wout = gathered["w_out"].reshape(D, D)
    wxq = gathered["w_xq"].reshape(D, D)
    wxo = gathered["w_xo"].reshape(D, D)
    y = jnp.concatenate([ya_n, yb_n], axis=1)
    h2 = _matmul(y, wout, "nn", F32, "mm_out", res=h1, tn=D)
    hx, _ = _rms_fwd(h2, g_x, "x_norm")
    mn, _ = _rms_fwd(ms, g_mem, "mem_norm")
    q = _matmul(hx, wxq, "nn", BF16, "mm_xq", tn=D)
    kv = _matmul(mn, wxkv, "nn", BF16, "mm_xkv")
    o = _xattn_fwd(q, kv, "xattn_fwd")
    h3 = _matmul(o, wxo, "nn", F32, "mm_xo", res=h2, tn=D)
    (h4, n4, a2, b2), _ = _ffn_fwd(h3, g_ffn2, wg2, wu2, wd2, "ffn2_fwd")
    dh4, dg_final, loss_part = _final_loss(h4, gf, tgt, "final_loss")

    grad_big, grad_small, recv_big = {}, {"g_final": dg_final}, {}
    order = _dw_block_order()
    (dn4, df2, da2, db2, s2), _ = _ffn_bwd_dx(dh4, a2, b2, wg2, wu2, wd2, "ffn2_bwd_dx")
    dh3, grad_small["g_ffn2"] = _rms_bwd(h3, g_ffn2, dn4, dh4, "ffn2_norm_bwd")
    (recv_big["w2_gate"], recv_big["w2_up"]), _ = _ffn_bwd_dw([(n4, da2), (n4, db2)], order, "ffn2_bwd_dw_gu")
    (recv_big["w2_down"],), _ = _ffn_bwd_dw([(s2, df2)], order, "ffn2_bwd_dw_d")

    do = _matmul(dh3, wxo, "nt", BF16, "mm_xo_dx", tn=D)
    grad_big["w_xo"] = _matmul(o, dh3, "tn", BF16, "mm_xo_dw", tm=1024, tn=1024).reshape(N_DEV, D // N_DEV, D)
    dq, dkv = _xattn_bwd(q, kv, do, "xattn_bwd")
    dhx = _matmul(dq, wxq, "nt", F32, "mm_xq_dx", tn=D)
    grad_big["w_xq"] = _matmul(hx, dq, "tn", BF16, "mm_xq_dw", tm=1024, tn=1024).reshape(N_DEV, D // N_DEV, D)
    dmn = _matmul(dkv, wxkv, "nt", F32, "mm_xkv_dx")
    grad_big["w_xkv"] = _cols_to_blocks(_matmul(mn, dkv, "tn", BF16, "mm_xkv_dw", tm=1024, tn=1024))
    dh2, grad_small["g_x"] = _rms_bwd(h2, g_x, dhx, dh3, "x_norm_bwd")
    _, grad_small["g_mem"] = _rms_bwd(ms, g_mem, dmn, None, "mem_norm_bwd")

    dy = _matmul(dh2, wout, "nt", F32, "mm_out_dx", tn=D)
    grad_big["w_out"] = _matmul(y, dh2, "tn", BF16, "mm_out_dw", tm=1024, tn=1024).reshape(N_DEV, D // N_DEV, D)
    dz_uv, grad_small["w_s"], dbs_t, grad_small["g_v"], grad_small["g_a_out"] = _sgu_bwd(
        z, dy, g_v, ws, ws_t, bs_t, g_a_out, "sgu_bwd")
    grad_small["b_s"] = dbs_t.T
    dq_b, dkv_b, grad_small["sinks"], grad_small["g_b_out"] = _swa_bwd(z, yb, dy, sink_rows, g_b_out, "swa_bwd")
    dz = jnp.concatenate([dz_uv, dq_b, dkv_b], axis=1)
    dn2 = _matmul(dz, win, "nt", F32, "mm_in_dx", tn=1024)
    grad_big["w_in"] = _cols_to_blocks(_matmul(n2, dz, "tn", BF16, "mm_in_dw", tm=1024, tn=dz.shape[1] // 2))
    dh1, grad_small["g_mix"] = _rms_bwd(h1, g_mix, dn2, dh2, "mix_norm_bwd")

    (dn1, df1, da1, db1, s1), recv_mid = _ffn_bwd_dx(
        dh1, a1, b1, wg1, wu1, wd1, "ffn1_bwd_dx", exchange=_scatter_exchange([grad_big[k] for k in MID_W]))
    recv_big.update(zip(MID_W, recv_mid))
    dx, grad_small["g_ffn1"] = _rms_bwd(xs, g_ffn1, dn1, dh1, "ffn1_norm_bwd")
    (recv_big["w1_gate"], recv_big["w1_up"]), (recv_small,) = _ffn_bwd_dw(
        [(n1, da1), (n1, db1)], order, "ffn1_bwd_dw_gu",
        exchange=_gather_exchange([_pack([grad_small[k] for k in SMALL])], ALL_PEERS))
    (recv_big["w1_down"],), _ = _ffn_bwd_dw([(s1, df1)], order, "ffn1_bwd_dw_d")


    grads, deltas, new_m, new_v = {}, {}, {}, {}
    for k in BIG:
        shp = w[k].shape
        two_d = shp[1:]
        outs = _adamw(w[k].reshape(two_d), mom[k].reshape(two_d), var[k].reshape(two_d), recv_big[k], "adamw_" + k)
        grads[k], deltas[k], new_m[k], new_v[k] = [t.reshape(shp) for t in outs]
    packed = _adamw(_pack([w[k] for k in SMALL]), _pack([mom[k] for k in SMALL]), _pack([var[k] for k in SMALL]),
                    recv_small, "adamw_small")
    off = 0
    for k in SMALL:
        shp = w[k].shape
        size = 1
        for s in shp:
            size *= s
        for dst, src in zip((grads, deltas, new_m, new_v), packed):
            dst[k] = src.reshape(-1)[off:off + size].reshape(shp)
        off += size

    loss = lax.psum(loss_part[0, 0], AXES)
    return (loss, dx[None], *[grads[k] for k in WEIGHTS], *[deltas[k] for k in WEIGHTS],
            *[new_m[k] for k in WEIGHTS], *[new_v[k] for k in WEIGHTS])
```

```python
import jax
import jax.numpy as jnp
from jax import lax
from jax.experimental import pallas as pl
from jax.experimental.pallas import tpu as pltpu

F32 = jnp.float32
BF16 = jnp.bfloat16
SDS = jax.ShapeDtypeStruct
BS = pl.BlockSpec

N_DEV = 8
AXES = ("x", "y", "c")
EPS = 1e-5
CHUNK = 128
HEAD_DIM = 64
KV_HEADS = 2
X_HEADS = 4
NEG = -1e30
ADAM_LR = 0.001
ADAM_B1 = 0.9
ADAM_B2 = 0.999
ADAM_EPS = 1e-08
ADAM_WD = 0.01
ADAM_STEP = 10
MIB = 1 << 20
WEIGHTS = ['g_ffn1', 'w1_gate', 'w1_up', 'w1_down', 'g_mix', 'w_in', 'g_v', 'w_s', 'b_s', 'sinks', 'g_a_out',
           'g_b_out', 'w_out', 'g_x', 'g_mem', 'w_xq', 'w_xkv', 'w_xo', 'g_ffn2', 'w2_gate', 'w2_up', 'w2_down',
           'g_final']
BIG = ['w1_gate', 'w1_up', 'w1_down', 'w_in', 'w_out', 'w_xq', 'w_xkv', 'w_xo', 'w2_gate', 'w2_up', 'w2_down']
SMALL = [w for w in WEIGHTS if w not in BIG]
FFN1_W = ['w1_gate', 'w1_up', 'w1_down']
FFN2_W = ['w2_gate', 'w2_up', 'w2_down']
MID_W = ['w_in', 'w_out', 'w_xq', 'w_xkv', 'w_xo']
REST_W = MID_W + FFN2_W


def _params(sem=None, vmem_mib=48):
    return pltpu.CompilerParams(dimension_semantics=sem, vmem_limit_bytes=vmem_mib * MIB)


def _dot(a, b):
    return jnp.dot(a, b, preferred_element_type=F32)


def _dot_nt(a, b):
    return lax.dot_general(a, b, (((1,), (1,)), ((), ())), preferred_element_type=F32)


def _dot_tn(a, b):
    return lax.dot_general(a, b, (((0,), (0,)), ((), ())), preferred_element_type=F32)


def _rstd(v):
    return lax.rsqrt(jnp.mean(v * v, axis=-1, keepdims=True) + EPS)


def _norm_bwd(xhat, r, g, d):
    t = d * g
    return r * (t - xhat * jnp.mean(t * xhat, axis=-1, keepdims=True))


def _gelu(v):
    return 0.5 * v * (1.0 + lax.erf(v * 0.7071067811865476))


def _gelu_grad(v):
    return 0.5 * (1.0 + lax.erf(v * 0.7071067811865476)) + v * jnp.exp(-0.5 * v * v) * 0.3989422804014327


def _mesh_pos():
    x, y, c = lax.axis_index("x"), lax.axis_index("y"), lax.axis_index("c")
    return x, y, c, 4 * x + 2 * y + c


def _peer(x, y, c, k):
    px = 1 - x if k & 4 else x
    py = 1 - y if k & 2 else y
    pc = 1 - c if k & 1 else c
    return (px, py, pc), 4 * px + 2 * py + pc


ANY = BS(memory_space=pl.ANY)
DMA_SEM = pltpu.SemaphoreType.DMA
ALL_PEERS = (1, 2, 3, 4, 5, 6, 7)
CHIP_PEERS = (2, 4, 6)
SIBLING = 1


def _remote(src, dst, send, recv, peer):
    return pltpu.make_async_remote_copy(src_ref=src, dst_ref=dst, send_sem=send, recv_sem=recv, device_id=peer,
                                        device_id_type=pl.DeviceIdType.MESH)


class _Exchange:
    def __init__(self, ins, out_shapes, sems, copies, aliases=None):
        self.ins, self.out_shapes, self.sems, self.copies, self.aliases = ins, out_shapes, sems, copies, aliases or {}


def _gather_exchange(shards, peers):
    n, m = len(shards), len(peers)

    def copies(ins, outs, sems):
        send, recv, lsem = sems
        x, y, c, me = _mesh_pos()
        cps = [pltpu.make_async_copy(ins[w], outs[w].at[me], lsem.at[w]) for w in range(n)]
        for w in range(n):
            for j, k in enumerate(peers):
                cps.append(_remote(ins[w], outs[w].at[me], send.at[w, j], recv.at[w, j], _peer(x, y, c, k)[0]))
        return cps

    return _Exchange(shards, [SDS((N_DEV,) + s.shape, s.dtype) for s in shards],
                     [DMA_SEM((n, m)), DMA_SEM((n, m)), DMA_SEM((n,))], copies)


def _sibling_exchange(gathered):
    n = len(gathered)

    def copies(ins, outs, sems):
        send, recv = sems
        x, y, c, me = _mesh_pos()
        slots = [me] + [_peer(x, y, c, k)[1] for k in CHIP_PEERS]
        sib = _peer(x, y, c, SIBLING)[0]
        return [_remote(outs[w].at[s], outs[w].at[s], send.at[w, j], recv.at[w, j], sib)
                for w in range(n) for j, s in enumerate(slots)]

    return _Exchange(gathered, [SDS(g.shape, g.dtype) for g in gathered], [DMA_SEM((n, 4)), DMA_SEM((n, 4))], copies,
                     aliases={w: w for w in range(n)})


def _scatter_exchange(fulls):
    n, m = len(fulls), len(ALL_PEERS)

    def copies(ins, outs, sems):
        send, recv, lsem = sems
        x, y, c, me = _mesh_pos()
        cps = [pltpu.make_async_copy(ins[w].at[me], outs[w].at[me], lsem.at[w]) for w in range(n)]
        for w in range(n):
            for j, k in enumerate(ALL_PEERS):
                peer, p = _peer(x, y, c, k)
                cps.append(_remote(ins[w].at[p], outs[w].at[me], send.at[w, j], recv.at[w, j], peer))
        return cps

    return _Exchange(fulls, [SDS(f.shape, f.dtype) for f in fulls],
                     [DMA_SEM((n, m)), DMA_SEM((n, m)), DMA_SEM((n,))], copies)


def _call(body, *, name, args, in_specs, out_shape, out_specs, grid, sem, vmem_mib, scratch=(), exchange=None,
          first=None, last=None):
    if exchange is None:
        return pl.pallas_call(body, name=name, out_shape=out_shape, grid=grid, in_specs=in_specs,
                              out_specs=out_specs, scratch_shapes=list(scratch),
                              compiler_params=_params(sem, vmem_mib))(*args), []
    ni, no, ns = len(args), len(out_shape), len(scratch)
    xi, xo = len(exchange.ins), len(exchange.out_shapes)

    def hosted(*refs):
        own_in, refs = refs[:ni], refs[ni:]
        x_in, refs = refs[:xi], refs[xi:]
        own_out, refs = refs[:no], refs[no:]
        x_out, refs = refs[:xo], refs[xo:]
        own_scr, x_sem = refs[:ns], refs[ns:]

        @pl.when(first())
        def _():
            for cp in exchange.copies(x_in, x_out, x_sem):
                cp.start()

        body(*own_in, *own_out, *own_scr)

        @pl.when(last())
        def _():
            for cp in exchange.copies(x_in, x_out, x_sem):
                cp.wait()

    outs = pl.pallas_call(
        hosted, name=name, out_shape=list(out_shape) + list(exchange.out_shapes), grid=grid,
        in_specs=list(in_specs) + [ANY] * xi, out_specs=list(out_specs) + [ANY] * xo,
        scratch_shapes=list(scratch) + list(exchange.sems),
        input_output_aliases={ni + a: no + b for a, b in exchange.aliases.items()},
        compiler_params=pltpu.CompilerParams(dimension_semantics=tuple("arbitrary" for _ in grid),
                                             vmem_limit_bytes=vmem_mib * MIB, has_side_effects=True),
    )(*args, *exchange.ins)
    return outs[:no], outs[no:]


def _run_exchange(exchange, name):
    xi, xo = len(exchange.ins), len(exchange.out_shapes)

    def body(*refs):
        cps = exchange.copies(refs[:xi], refs[xi:xi + xo], refs[xi + xo:])
        for cp in cps:
            cp.start()
        for cp in cps:
            cp.wait()

    return pl.pallas_call(
        body, name=name, out_shape=list(exchange.out_shapes), in_specs=[ANY] * xi, out_specs=[ANY] * xo,
        scratch_shapes=list(exchange.sems), input_output_aliases=dict(exchange.aliases),
        compiler_params=pltpu.CompilerParams(has_side_effects=True),
    )(*exchange.ins)


def _rms_fwd(h, g, name, tm=512, exchange=None):
    S, D = h.shape
    tm = min(tm, S)
    assert S % tm == 0, (S, tm)
    ni = S // tm

    def body(h_ref, g_ref, o_ref):
        hv = h_ref[...]
        o_ref[...] = (hv * _rstd(hv) * g_ref[...]).astype(o_ref.dtype)

    (out,), extra = _call(
        body, name=name, args=(h, g), out_shape=[SDS((S, D), BF16)], grid=(ni,),
        in_specs=[BS((tm, D), lambda i: (i, 0)), BS((1, D), lambda i: (0, 0))],
        out_specs=[BS((tm, D), lambda i: (i, 0))], sem=("parallel",), vmem_mib=32, exchange=exchange,
        first=lambda: pl.program_id(0) == 0, last=lambda: pl.program_id(0) == ni - 1)
    return out, extra


def _rms_bwd(h, g, dn, dres, name, tm=256):
    S, D = h.shape
    tm = min(tm, S)
    assert S % tm == 0, (S, tm)
    has_res = dres is not None

    def body(*refs):
        if has_res:
            h_ref, g_ref, dn_ref, dres_ref, dh_ref, dg_ref = refs
        else:
            h_ref, g_ref, dn_ref, dh_ref, dg_ref = refs
        hv = h_ref[...]
        r = _rstd(hv)
        xh = hv * r
        d = dn_ref[...].astype(F32)

        @pl.when(pl.program_id(0) == 0)
        def _():
            dg_ref[...] = jnp.zeros_like(dg_ref)

        dg_ref[...] += jnp.sum(d * xh, axis=0, keepdims=True)
        dh = _norm_bwd(xh, r, g_ref[...], d)
        dh_ref[...] = dres_ref[...] + dh if has_res else dh

    row = BS((tm, D), lambda i: (i, 0))
    vec = BS((1, D), lambda i: (0, 0))
    args = (h, g, dn) + ((dres,) if has_res else ())
    return pl.pallas_call(
        body, name=name, out_shape=[SDS((S, D), F32), SDS((1, D), F32)], grid=(S // tm,),
        in_specs=[row, vec, row] + ([row] if has_res else []), out_specs=[row, vec],
        compiler_params=_params(("arbitrary",), 40),
    )(*args)


def _final_loss(h, g, target, name, tm=256):
    S, D = h.shape
    tm = min(tm, S)
    assert S % tm == 0, (S, tm)

    def body(h_ref, g_ref, t_ref, dh_ref, dg_ref, loss_ref):
        hv = h_ref[...]
        r = _rstd(hv)
        xh = hv * r
        gv = g_ref[...]
        e = xh * gv - t_ref[...]

        @pl.when(pl.program_id(0) == 0)
        def _():
            dg_ref[...] = jnp.zeros_like(dg_ref)
            loss_ref[...] = jnp.zeros_like(loss_ref)

        loss_ref[...] += 0.5 * jnp.sum(jnp.mean(e * e, axis=-1, keepdims=True), axis=0, keepdims=True)
        dy = e * (1.0 / D)
        dg_ref[...] += jnp.sum(dy * xh, axis=0, keepdims=True)
        dh_ref[...] = _norm_bwd(xh, r, gv, dy)

    row = BS((tm, D), lambda i: (i, 0))
    vec = BS((1, D), lambda i: (0, 0))
    return pl.pallas_call(
        body, name=name, out_shape=[SDS((S, D), F32), SDS((1, D), F32), SDS((1, 128), F32)], grid=(S // tm,),
        in_specs=[row, vec, row], out_specs=[row, vec, BS((1, 128), lambda i: (0, 0))],
        compiler_params=_params(("arbitrary",), 40),
    )(h, g, target)


def _matmul(a, b, mode, out_dtype, name, res=None, tm=512, tn=512, tk=512):
    if mode == "tn":
        K, M = a.shape
        N = b.shape[1]
        tm, tn, tk = min(tm, M), min(tn, N), min(tk, K)
        assert M % tm == 0 and N % tn == 0 and K % tk == 0, (a.shape, b.shape, tm, tn, tk)
        nk = K // tk

        def body(a_ref, b_ref, o_ref, acc_ref):
            k = pl.program_id(2)

            @pl.when(k == 0)
            def _():
                acc_ref[...] = jnp.zeros_like(acc_ref)

            acc_ref[...] += _dot_tn(a_ref[...].astype(BF16), b_ref[...].astype(BF16))

            @pl.when(k == nk - 1)
            def _():
                o_ref[...] = acc_ref[...].astype(o_ref.dtype)

        return pl.pallas_call(
            body, name=name, out_shape=SDS((M, N), out_dtype), grid=(M // tm, N // tn, nk),
            in_specs=[BS((tk, tm), lambda i, j, k: (k, i)), BS((tk, tn), lambda i, j, k: (k, j))],
            out_specs=BS((tm, tn), lambda i, j, k: (i, j)), scratch_shapes=[pltpu.VMEM((tm, tn), F32)],
            compiler_params=_params(("parallel", "parallel", "arbitrary"), 48),
        )(a, b)

    M, K = a.shape
    N = b.shape[1] if mode == "nn" else b.shape[0]
    tm, tn = min(tm, M), min(tn, N)
    assert M % tm == 0 and N % tn == 0, (a.shape, b.shape, tm, tn)
    has_res = res is not None

    def body(*refs):
        if has_res:
            a_ref, b_ref, r_ref, o_ref = refs
        else:
            a_ref, b_ref, o_ref = refs
        av, bv = a_ref[...].astype(BF16), b_ref[...].astype(BF16)
        acc = _dot(av, bv) if mode == "nn" else _dot_nt(av, bv)
        if has_res:
            acc = acc + r_ref[...]
        o_ref[...] = acc.astype(o_ref.dtype)

    b_spec = BS((K, tn), lambda i, j: (0, j)) if mode == "nn" else BS((tn, K), lambda i, j: (j, 0))
    o_spec = BS((tm, tn), lambda i, j: (i, j))
    return pl.pallas_call(
        body, name=name, out_shape=SDS((M, N), out_dtype), grid=(M // tm, N // tn),
        in_specs=[BS((tm, K), lambda i, j: (i, 0)), b_spec] + ([o_spec] if has_res else []), out_specs=o_spec,
        compiler_params=_params(("parallel", "parallel"), 48),
    )(*((a, b) + ((res,) if has_res else ())))


def _ffn_fwd(h, g, wg, wu, wd, name, tm=512, exchange=None):
    S, D = h.shape
    nb, _, Fs = wg.shape
    tm = min(tm, S)
    assert S % tm == 0, (S, tm)

    def body(h_ref, g_ref, wg_ref, wu_ref, wd_ref, o_ref, n_ref, a_ref, b_ref, ak_ref, bk_ref):
        j = pl.program_id(1)

        def gate_up():
            n = n_ref[...]
            a = _dot(n, wg_ref[...]).astype(BF16)
            b = _dot(n, wu_ref[...]).astype(BF16)
            a_ref[...] = a
            b_ref[...] = b
            ak_ref[...] = a
            bk_ref[...] = b

        def activation():
            a, b = ak_ref[...].astype(F32), bk_ref[...].astype(F32)
            return (a * jax.nn.sigmoid(a) * b).astype(BF16)

        def down(s):
            o_ref[...] += _dot(s, wd_ref[...])

        @pl.when(j == 0)
        def _():
            hv = h_ref[...]
            n_ref[...] = (hv * _rstd(hv) * g_ref[...]).astype(BF16)
            o_ref[...] = jnp.zeros_like(o_ref)
            gate_up()

        @pl.when((j > 0) & (j < nb))
        def _():
            s = activation()
            gate_up()
            down(s)

        @pl.when(j == nb)
        def _():
            down(activation())
            o_ref[...] = h_ref[...] + 0.5 * o_ref[...]

    row = BS((tm, D), lambda i, j: (i, 0))
    wcol = BS((None, D, Fs), lambda i, j: (jnp.minimum(j, nb - 1), 0, 0))
    act = BS((None, tm, Fs), lambda i, j: (jnp.minimum(j, nb - 1), i, 0))
    ni = S // tm
    return _call(
        body, name=name, args=(h, g, wg, wu, wd),
        out_shape=[SDS((S, D), F32), SDS((S, D), BF16), SDS((nb, S, Fs), BF16), SDS((nb, S, Fs), BF16)],
        grid=(ni, nb + 1),
        in_specs=[row, BS((1, D), lambda i, j: (0, 0)), wcol, wcol,
                  BS((None, Fs, D), lambda i, j: (jnp.maximum(j - 1, 0), 0, 0))],
        out_specs=[row, row, act, act], scratch=[pltpu.VMEM((tm, Fs), BF16), pltpu.VMEM((tm, Fs), BF16)],
        sem=("parallel", "arbitrary"), vmem_mib=56, exchange=exchange,
        first=lambda: (pl.program_id(0) == 0) & (pl.program_id(1) == 0),
        last=lambda: (pl.program_id(0) == ni - 1) & (pl.program_id(1) == nb))


def _ffn_bwd_dx(dh, a, b, wg, wu, wd, name, tm=512, exchange=None):
    S, D = dh.shape
    nb, _, Fs = wg.shape
    tm = min(tm, S)
    assert S % tm == 0, (S, tm)

    def body(dh_ref, a_ref, b_ref, wg_ref, wu_ref, wd_ref, dn_ref, df_ref, da_ref, db_ref, s_ref, ds_ref):
        j = pl.program_id(1)

        def first_matmul():
            ds_ref[...] = _dot_nt(df_ref[...], wd_ref[...])

        def pre_activation_cotangents():
            ds = ds_ref[...]
            av, bv = a_ref[...].astype(F32), b_ref[...].astype(F32)
            sig = jax.nn.sigmoid(av)
            sl = av * sig
            da = (ds * bv * (sig * (1.0 + av * (1.0 - sig)))).astype(BF16)
            db = (ds * sl).astype(BF16)
            da_ref[...] = da
            db_ref[...] = db
            s_ref[...] = (sl * bv).astype(BF16)
            return da, db

        def last_matmuls(da, db):
            dn_ref[...] += _dot_nt(da, wg_ref[...]) + _dot_nt(db, wu_ref[...])

        @pl.when(j == 0)
        def _():
            df_ref[...] = (0.5 * dh_ref[...]).astype(BF16)
            dn_ref[...] = jnp.zeros_like(dn_ref)
            first_matmul()

        @pl.when((j > 0) & (j < nb))
        def _():
            da, db = pre_activation_cotangents()
            first_matmul()
            last_matmuls(da, db)

        @pl.when(j == nb)
        def _():
            last_matmuls(*pre_activation_cotangents())

    row = BS((tm, D), lambda i, j: (i, 0))
    prev = BS((None, D, Fs), lambda i, j: (jnp.maximum(j - 1, 0), 0, 0))
    act = BS((None, tm, Fs), lambda i, j: (jnp.maximum(j - 1, 0), i, 0))
    ni = S // tm
    return _call(
        body, name=name, args=(dh, a, b, wg, wu, wd),
        out_shape=[SDS((S, D), F32), SDS((S, D), BF16)] + [SDS((nb, S, Fs), BF16)] * 3,
        grid=(ni, nb + 1),
        in_specs=[row, act, act, prev, prev, BS((None, Fs, D), lambda i, j: (jnp.minimum(j, nb - 1), 0, 0))],
        out_specs=[row, row, act, act, act], scratch=[pltpu.VMEM((tm, Fs), F32)],
        sem=("parallel", "arbitrary"), vmem_mib=56, exchange=exchange,
        first=lambda: (pl.program_id(0) == 0) & (pl.program_id(1) == 0),
        last=lambda: (pl.program_id(0) == ni - 1) & (pl.program_id(1) == nb))


DW_FLIPS = {0: (6, 2, 4, 0), 1: (6, 4, 2, 0)}
N_CHIPS = N_DEV // 2


def _dw_block_order():
    x, y, c, me = _mesh_pos()
    steps = [jnp.array([v for mine, sib in zip(DW_FLIPS[core], DW_FLIPS[1 - core]) for v in (sib ^ 1, mine)], jnp.int32)
             for core in (0, 1)]
    return jnp.bitwise_xor(me.astype(jnp.int32), jnp.where(c == 0, steps[0], steps[1]))


def _ffn_bwd_dw(pairs, order, name, tk=512, exchange=None):
    npair = len(pairs)
    S = pairs[0][0].shape[-2]
    nb, half = N_DEV, N_DEV // 2
    tk = min(tk, S)
    assert S % tk == 0, (S, tk)
    nk = S // tk
    shapes = [(lhs.shape[-1], rhs.shape[-1]) for lhs, rhs in pairs]
    xi = len(exchange.ins) if exchange else 0
    xo = len(exchange.out_shapes) if exchange else 0

    def body(order_ref, *rest):
        tiles, rest = rest[:2 * npair], rest[2 * npair:]
        x_in, rest = rest[:xi], rest[xi:]
        recv_bufs, rest = rest[:npair], rest[npair:]
        x_out, rest = rest[:xo], rest[xo:]
        accs, rest = rest[:npair], rest[npair:]
        out_t, rest = rest[:npair], rest[npair:]
        out_m, rest = rest[:npair], rest[npair:]
        land, rest = rest[:npair], rest[npair:]
        (send_t, recv_t, send_m, recv_m, lsem, credit), x_sem = rest[:6], rest[6:]
        t, k = pl.program_id(0), pl.program_id(1)
        x, y, c, me = _mesh_pos()
        sibling = (x, y, 1 - c)
        chip = 2 * x + y

        if exchange:
            @pl.when((t == 0) & (k == 0))
            def _():
                for cp in exchange.copies(x_in, x_out, x_sem):
                    cp.start()

        @pl.when(k == 0)
        def _():
            for acc in accs:
                acc[...] = jnp.zeros_like(acc)

        for w in range(npair):
            accs[w][...] += _dot_tn(tiles[2 * w][...], tiles[2 * w + 1][...])

        def to_sibling(w, i):
            return _remote(out_t[w], land[w], send_t.at[w, i], recv_t.at[w, i], sibling)

        def to_owner(w, i):
            dst = recv_bufs[w].at[chip]
            if i == half - 1:
                return pltpu.make_async_copy(out_m[w], dst, lsem.at[w])
            p = jnp.bitwise_xor(me, jnp.where(c == 0, DW_FLIPS[0][i], DW_FLIPS[1][i]))
            return _remote(out_m[w], dst, send_m.at[w, i], recv_m.at[w, i], (p >> 2, (p >> 1) & 1, p & 1))

        for i in range(half):
            @pl.when((t == 2 * i) & (k == nk - 1))
            def _(i=i):
                if i >= 1:
                    pl.semaphore_wait(credit, 1)
                for w in range(npair):
                    if i >= 1:
                        to_sibling(w, i - 1).wait_send()
                    out_t[w][...] = accs[w][...].astype(BF16)
                    to_sibling(w, i).start()

            @pl.when((t == 2 * i + 1) & (k == nk - 1))
            def _(i=i):
                for w in range(npair):
                    to_sibling(w, i).wait_recv()
                    if i >= 1:
                        to_owner(w, i - 1).wait_send()
                    out_m[w][...] = (accs[w][...] + land[w][...].astype(F32)).astype(BF16)
                if i < half - 1:
                    pl.semaphore_signal(credit, inc=1, device_id=sibling, device_id_type=pl.DeviceIdType.MESH)
                for w in range(npair):
                    to_owner(w, i).start()

        @pl.when((t == nb - 1) & (k == nk - 1))
        def _():
            for w in range(npair):
                to_sibling(w, half - 1).wait_send()
                to_owner(w, half - 1).wait()
                for i in range(half - 1):
                    to_owner(w, i).wait_recv()
            if exchange:
                for cp in exchange.copies(x_in, x_out, x_sem):
                    cp.wait()

    def tile_spec(arr):
        if arr.ndim == 3:
            return BS((None, tk, arr.shape[-1]), lambda t, k, o: (o[t], k, 0))
        return BS((tk, arr.shape[-1]), lambda t, k, o: (k, 0))

    flat = [a for pair in pairs for a in pair]
    grid_spec = pltpu.PrefetchScalarGridSpec(
        num_scalar_prefetch=1, grid=(nb, nk),
        in_specs=[tile_spec(a) for a in flat] + [ANY] * xi, out_specs=[ANY] * (npair + xo),
        scratch_shapes=[pltpu.VMEM(s, F32) for s in shapes] + [pltpu.VMEM(s, BF16) for s in shapes] * 3
        + [DMA_SEM((npair, half)), DMA_SEM((npair, half)), DMA_SEM((npair, half - 1)), DMA_SEM((npair, half - 1)),
           DMA_SEM((npair,)), pltpu.SemaphoreType.REGULAR]
        + (list(exchange.sems) if exchange else []))
    outs = pl.pallas_call(
        body, name=name, grid_spec=grid_spec,
        out_shape=[SDS((N_CHIPS,) + s, BF16) for s in shapes] + (list(exchange.out_shapes) if exchange else []),
        compiler_params=pltpu.CompilerParams(dimension_semantics=("arbitrary", "arbitrary"),
                                             vmem_limit_bytes=48 * MIB, has_side_effects=True),
    )(order, *flat, *(exchange.ins if exchange else ()))
    return outs[:npair], outs[npair:]


def _sgu_parts(z, gv, ws_ref, bst_ref, groups):
    da = z.shape[1] // 2
    zu, zv = z[:, :da], z[:, da:]
    u, v = _gelu(zu), _gelu(zv)
    rv = _rstd(v)
    vhat = v * rv
    vn = (vhat * gv).astype(BF16)
    tri = lax.broadcasted_iota(jnp.int32, (CHUNK, CHUNK), 0) >= lax.broadcasted_iota(jnp.int32, (CHUNK, CHUNK), 1)
    pieces = []
    for g in range(groups):
        w = jnp.where(tri, ws_ref[g], 0.0).astype(BF16)
        pieces.append(_dot(w, vn[:, g * CHUNK:(g + 1) * CHUNK]) + bst_ref[:, g:g + 1])
    sv = jnp.concatenate(pieces, axis=1)
    return dict(zu=zu, zv=zv, u=u, rv=rv, vhat=vhat, vn=vn, sv=sv, tri=tri)


def _sgu_fwd(z, g_v, w_s, b_st, g_a, name):
    S = z.shape[0]
    groups = w_s.shape[0]
    da = groups * CHUNK

    def body(z_ref, gv_ref, ws_ref, bst_ref, ga_ref, o_ref):
        p = _sgu_parts(z_ref[...], gv_ref[...], ws_ref, bst_ref, groups)
        ya = p["u"] * p["sv"]
        o_ref[...] = (ya * _rstd(ya) * ga_ref[...]).astype(BF16)

    vec = BS((1, da), lambda i: (0, 0))
    return pl.pallas_call(
        body, name=name, out_shape=SDS((S, da), BF16), grid=(S // CHUNK,),
        in_specs=[BS((CHUNK, 2 * da), lambda i: (i, 0)), vec, BS((groups, CHUNK, CHUNK), lambda i: (0, 0, 0)),
                  BS((CHUNK, groups), lambda i: (0, 0)), vec],
        out_specs=BS((CHUNK, da), lambda i: (i, 0)), compiler_params=_params(("parallel",), 32),
    )(z, g_v, w_s, b_st, g_a)


def _sgu_bwd(z, dy, g_v, w_s, w_st, b_st, g_a, name):
    S = z.shape[0]
    groups = w_s.shape[0]
    da = groups * CHUNK

    def body(z_ref, dy_ref, gv_ref, ws_ref, wst_ref, bst_ref, ga_ref, dz_ref, dws_ref, dbst_ref, dgv_ref, dga_ref):
        @pl.when(pl.program_id(0) == 0)
        def _():
            dws_ref[...] = jnp.zeros_like(dws_ref)
            dbst_ref[...] = jnp.zeros_like(dbst_ref)
            dgv_ref[...] = jnp.zeros_like(dgv_ref)
            dga_ref[...] = jnp.zeros_like(dga_ref)

        gv = gv_ref[...]
        p = _sgu_parts(z_ref[...], gv, ws_ref, bst_ref, groups)
        u, sv, tri = p["u"], p["sv"], p["tri"]
        ya = u * sv
        ra = _rstd(ya)
        yhat = ya * ra
        d = dy_ref[...]
        dga_ref[...] += jnp.sum(d * yhat, axis=0, keepdims=True)
        dya = _norm_bwd(yhat, ra, ga_ref[...], d)
        du = dya * sv
        dsv = dya * u
        dsv_b = dsv.astype(BF16)
        tri_t = (lax.broadcasted_iota(jnp.int32, (CHUNK, CHUNK), 0)
                 <= lax.broadcasted_iota(jnp.int32, (CHUNK, CHUNK), 1))
        lane = lax.broadcasted_iota(jnp.int32, (CHUNK, groups), 1)
        dvn = []
        dbs = jnp.zeros((CHUNK, groups), F32)
        for g in range(groups):
            cols = slice(g * CHUNK, (g + 1) * CHUNK)
            dbs = dbs + jnp.where(lane == g, jnp.sum(dsv[:, cols], axis=1, keepdims=True), 0.0)
            dws_ref[g] += jnp.where(tri, _dot_nt(dsv_b[:, cols], p["vn"][:, cols]), 0.0)
            wt = jnp.where(tri_t, wst_ref[g], 0.0).astype(BF16)
            dvn.append(_dot(wt, dsv_b[:, cols]))
        dbst_ref[...] += dbs
        dvn = jnp.concatenate(dvn, axis=1)
        dgv_ref[...] += jnp.sum(dvn * p["vhat"], axis=0, keepdims=True)
        dv = _norm_bwd(p["vhat"], p["rv"], gv, dvn)
        dz_ref[...] = jnp.concatenate([du * _gelu_grad(p["zu"]), dv * _gelu_grad(p["zv"])], axis=1)

    vec = BS((1, da), lambda i: (0, 0))
    wsq = BS((groups, CHUNK, CHUNK), lambda i: (0, 0, 0))
    bsq = BS((CHUNK, groups), lambda i: (0, 0))
    return pl.pallas_call(
        body, name=name,
        out_shape=[SDS((S, 2 * da), F32), SDS((groups, CHUNK, CHUNK), F32), SDS((CHUNK, groups), F32),
                   SDS((1, da), F32), SDS((1, da), F32)],
        grid=(S // CHUNK,),
        in_specs=[BS((CHUNK, 2 * da), lambda i: (i, 0)), BS((CHUNK, da), lambda i: (i, 0)), vec, wsq, wsq, bsq, vec],
        out_specs=[BS((CHUNK, 2 * da), lambda i: (i, 0)), wsq, bsq, vec, vec],
        compiler_params=_params(("arbitrary",), 32),
    )(z, dy, g_v, w_s, w_st, b_st, g_a)


def _swa_mask(i, group):
    row = lax.broadcasted_iota(jnp.int32, (group * CHUNK, 2 * CHUNK), 0) & (CHUNK - 1)
    col = lax.broadcasted_iota(jnp.int32, (group * CHUNK, 2 * CHUNK), 1)
    d = row + CHUNK - col
    return (d >= 0) & (d < CHUNK) & jnp.logical_or(i > 0, col >= CHUNK)


def _stack_heads(t, g, group):
    return jnp.concatenate([t[:, h * HEAD_DIM:(h + 1) * HEAD_DIM] for h in range(g * group, (g + 1) * group)], axis=0)


def _unstack_heads(stacked, group):
    return [stacked[h * CHUNK:(h + 1) * CHUNK] for h in range(group)]


def _swa_probs(qh, kh, sink, mask):
    s = jnp.where(mask, _dot_nt(qh, kh) * (HEAD_DIM ** -0.5), NEG)
    m = jnp.maximum(jnp.max(s, axis=-1, keepdims=True), sink)
    e = jnp.exp(s - m)
    es = jnp.exp(sink - m)
    inv = 1.0 / (jnp.sum(e, axis=-1, keepdims=True) + es)
    return e * inv, es * inv


def _swa_specs(db, nblk, clamp):
    kvw = 2 * KV_HEADS * HEAD_DIM
    cur = (lambda i: jnp.minimum(i, nblk - 1)) if clamp else (lambda i: i)
    q_spec = BS((CHUNK, db), lambda i: (cur(i), 2))
    kc_spec = BS((CHUNK, kvw), lambda i: (cur(i), 3 * db // kvw))
    kp_spec = BS((CHUNK, kvw), lambda i: (jnp.maximum(cur(i) - 1, 0), 3 * db // kvw))
    return q_spec, kc_spec, kp_spec


def _swa_fwd(z, sink_rows, g_b, name, exchange=None):
    S = z.shape[0]
    db = g_b.shape[1]
    heads = db // HEAD_DIM
    group = heads // KV_HEADS
    nblk = S // CHUNK

    def body(q_ref, kc_ref, kp_ref, sk_ref, gb_ref, yb_ref, ybn_ref):
        mask = _swa_mask(pl.program_id(0), group)
        q = q_ref[...].astype(BF16)
        kv = jnp.concatenate([kp_ref[...], kc_ref[...]], axis=0).astype(BF16)
        outs = []
        for g in range(KV_HEADS):
            kg = kv[:, g * HEAD_DIM:(g + 1) * HEAD_DIM]
            vg = kv[:, (KV_HEADS + g) * HEAD_DIM:(KV_HEADS + g + 1) * HEAD_DIM]
            rows = slice(g * group * CHUNK, (g + 1) * group * CHUNK)
            p, _ = _swa_probs(_stack_heads(q, g, group), kg, sk_ref[rows, :], mask)
            outs += _unstack_heads(_dot(p.astype(BF16), vg), group)
        yb = jnp.concatenate(outs, axis=1)
        yb_ref[...] = yb
        ybn_ref[...] = (yb * _rstd(yb) * gb_ref[...]).astype(BF16)

    q_spec, kc_spec, kp_spec = _swa_specs(db, nblk, False)
    out = BS((CHUNK, db), lambda i: (i, 0))
    return _call(
        body, name=name, args=(z, z, z, sink_rows, g_b), out_shape=[SDS((S, db), F32), SDS((S, db), BF16)],
        grid=(nblk,),
        in_specs=[q_spec, kc_spec, kp_spec, BS((heads * CHUNK, 1), lambda i: (0, 0)), BS((1, db), lambda i: (0, 0))],
        out_specs=[out, out], sem=("parallel",), vmem_mib=32, exchange=exchange,
        first=lambda: pl.program_id(0) == 0, last=lambda: pl.program_id(0) == nblk - 1)


def _swa_bwd(z, yb, dy, sink_rows, g_b, name):
    S = z.shape[0]
    db = g_b.shape[1]
    heads = db // HEAD_DIM
    group = heads // KV_HEADS
    nblk = S // CHUNK
    kvw = 2 * KV_HEADS * HEAD_DIM

    def body(q_ref, kc_ref, kp_ref, yb_ref, dy_ref, sk_ref, gb_ref, dq_ref, dkv_ref, dsk_ref, dgb_ref, carry_ref):
        i = pl.program_id(0)

        @pl.when(i == 0)
        def _():
            carry_ref[...] = jnp.zeros_like(carry_ref)
            dsk_ref[...] = jnp.zeros_like(dsk_ref)
            dgb_ref[...] = jnp.zeros_like(dgb_ref)

        @pl.when(i < nblk)
        def _():
            mask = _swa_mask(i, group)
            yb = yb_ref[...]
            rb = _rstd(yb)
            yhat = yb * rb
            d = dy_ref[...]
            dgb_ref[...] += jnp.sum(d * yhat, axis=0, keepdims=True)
            do = _norm_bwd(yhat, rb, gb_ref[...], d).astype(BF16)
            q = q_ref[...].astype(BF16)
            kv = jnp.concatenate([kp_ref[...], kc_ref[...]], axis=0).astype(BF16)
            dqs, dk, dv = [], [], []
            lane = lax.broadcasted_iota(jnp.int32, (1, heads), 1)
            dsinks = jnp.zeros((1, heads), F32)
            for g in range(KV_HEADS):
                kg = kv[:, g * HEAD_DIM:(g + 1) * HEAD_DIM]
                vg = kv[:, (KV_HEADS + g) * HEAD_DIM:(KV_HEADS + g + 1) * HEAD_DIM]
                rows = slice(g * group * CHUNK, (g + 1) * group * CHUNK)
                qg, dog = _stack_heads(q, g, group), _stack_heads(do, g, group)
                p, ps = _swa_probs(qg, kg, sk_ref[rows, :], mask)
                dp = _dot_nt(dog, vg)
                dr = jnp.sum(p * dp, axis=-1, keepdims=True)
                ds = (p * (dp - dr) * (HEAD_DIM ** -0.5)).astype(BF16)
                for h, t in enumerate(_unstack_heads(ps * dr, group)):
                    dsinks = dsinks - jnp.where(lane == g * group + h, jnp.sum(t, axis=0, keepdims=True), 0.0)
                dqs += _unstack_heads(_dot(ds, kg), group)
                dk.append(_dot_tn(ds, qg))
                dv.append(_dot_tn(p.astype(BF16), dog))
            dq_ref[...] = jnp.concatenate(dqs, axis=1)
            dsk_ref[...] += dsinks
            contrib = jnp.concatenate(dk + dv, axis=1)
            dkv_ref[...] = carry_ref[...] + contrib[:CHUNK]
            carry_ref[...] = contrib[CHUNK:]

        @pl.when(i == nblk)
        def _():
            dkv_ref[...] = carry_ref[...]

    q_spec, kc_spec, kp_spec = _swa_specs(db, nblk, True)
    cur = BS((CHUNK, db), lambda i: (jnp.minimum(i, nblk - 1), 0))
    return pl.pallas_call(
        body, name=name,
        out_shape=[SDS((S, db), F32), SDS((S, kvw), F32), SDS((1, heads), F32), SDS((1, db), F32)],
        grid=(nblk + 1,),
        in_specs=[q_spec, kc_spec, kp_spec, cur, BS((CHUNK, db), lambda i: (jnp.minimum(i, nblk - 1), 1)),
                  BS((heads * CHUNK, 1), lambda i: (0, 0)), BS((1, db), lambda i: (0, 0))],
        out_specs=[cur, BS((CHUNK, kvw), lambda i: (jnp.maximum(i - 1, 0), 0)), BS((1, heads), lambda i: (0, 0)),
                   BS((1, db), lambda i: (0, 0))],
        scratch_shapes=[pltpu.VMEM((CHUNK, kvw), F32)], compiler_params=_params(("arbitrary",), 32),
    )(z, z, z, yb, dy, sink_rows, g_b)


def _xattn_probs(qh, kh, hd):
    s = _dot_nt(qh, kh) * (hd ** -0.5)
    e = jnp.exp(s - jnp.max(s, axis=-1, keepdims=True))
    return e / jnp.sum(e, axis=-1, keepdims=True)


def _xattn_fwd(q, kv, name, tq=512):
    S, D = q.shape
    M = kv.shape[0]
    hd = D // X_HEADS
    tq = min(tq, S)
    assert S % tq == 0, (S, tq)

    def body(q_ref, kv_ref, o_ref):
        for h in range(X_HEADS):
            cols = slice(h * hd, (h + 1) * hd)
            p = _xattn_probs(q_ref[:, cols], kv_ref[:, cols], hd)
            o_ref[:, cols] = _dot(p.astype(BF16), kv_ref[:, D + h * hd:D + (h + 1) * hd]).astype(BF16)

    row = BS((tq, D), lambda i: (i, 0))
    return pl.pallas_call(
        body, name=name, out_shape=SDS((S, D), BF16), grid=(S // tq,),
        in_specs=[row, BS((M, 2 * D), lambda i: (0, 0))], out_specs=row, compiler_params=_params(("parallel",), 40),
    )(q, kv)


def _xattn_bwd(q, kv, do, name, tq=512):
    S, D = q.shape
    M = kv.shape[0]
    hd = D // X_HEADS
    tq = min(tq, S)
    assert S % tq == 0, (S, tq)

    def body(q_ref, kv_ref, do_ref, dq_ref, dkv_ref):
        @pl.when(pl.program_id(0) == 0)
        def _():
            dkv_ref[...] = jnp.zeros_like(dkv_ref)

        for h in range(X_HEADS):
            cols = slice(h * hd, (h + 1) * hd)
            vcols = slice(D + h * hd, D + (h + 1) * hd)
            qh, kh, vh, doh = q_ref[:, cols], kv_ref[:, cols], kv_ref[:, vcols], do_ref[:, cols]
            p = _xattn_probs(qh, kh, hd)
            dp = _dot_nt(doh, vh)
            ds = (p * (dp - jnp.sum(p * dp, axis=-1, keepdims=True)) * (hd ** -0.5)).astype(BF16)
            dq_ref[:, cols] = _dot(ds, kh).astype(BF16)
            dkv_ref[:, cols] += _dot_tn(ds, qh)
            dkv_ref[:, vcols] += _dot_tn(p.astype(BF16), doh)

    row = BS((tq, D), lambda i: (i, 0))
    full = BS((M, 2 * D), lambda i: (0, 0))
    return pl.pallas_call(
        body, name=name, out_shape=[SDS((S, D), BF16), SDS((M, 2 * D), F32)], grid=(S // tq,),
        in_specs=[row, full, row], out_specs=[row, full], compiler_params=_params(("arbitrary",), 40),
    )(q, kv, do)


def _row_tile(rows, cap):
    best = 8
    for t in range(8, min(rows, cap) + 1, 8):
        if rows % t == 0:
            best = t
    assert rows % best == 0, (rows, cap)
    return best


ADAM_TILE_ELEMS = 256 * 1024


def _adamw(w, m, v, recv, name):
    R, C = w.shape
    slots = recv.shape[0]
    tr = _row_tile(R, max(8, ADAM_TILE_ELEMS // C))

    def body(w_ref, m_ref, v_ref, r_ref, g_ref, d_ref, nm_ref, nv_ref):
        g = r_ref[0].astype(F32)
        for s in range(1, slots):
            g = g + r_ref[s].astype(F32)
        mn = ADAM_B1 * m_ref[...] + (1.0 - ADAM_B1) * g
        vn = ADAM_B2 * v_ref[...] + (1.0 - ADAM_B2) * jnp.square(g)
        m_hat = mn / (1.0 - ADAM_B1 ** ADAM_STEP)
        v_hat = vn / (1.0 - ADAM_B2 ** ADAM_STEP)
        g_ref[...] = g
        d_ref[...] = -ADAM_LR * (m_hat / (jnp.sqrt(v_hat) + ADAM_EPS) + ADAM_WD * w_ref[...])
        nm_ref[...] = mn
        nv_ref[...] = vn

    row = BS((tr, C), lambda i: (i, 0))
    return pl.pallas_call(
        body, name=name, out_shape=[SDS((R, C), F32)] * 4, grid=(R // tr,),
        in_specs=[row, row, row, BS((slots, tr, C), lambda i: (0, i, 0))], out_specs=[row] * 4,
        compiler_params=_params(("parallel",), 40),
    )(w, m, v, recv)


def _cols_to_blocks(full):
    r, c = full.shape
    return full.reshape(r, N_DEV, c // N_DEV).transpose(1, 0, 2)


def _blocks_to_cols(blocks):
    n, r, c = blocks.shape
    return blocks.transpose(1, 0, 2).reshape(r, n * c)


def _pack(parts):
    flat = jnp.concatenate([p.reshape(-1).astype(F32) for p in parts])
    pad = (-flat.shape[0]) % (128 * 128)
    return jnp.pad(flat, (0, pad)).reshape(-1, 128)


def kernel(x, mem, g_ffn1, w1_gate, w1_up, w1_down, g_mix, w_in, g_v, w_s, b_s, sinks, g_a_out, g_b_out, w_out, g_x, g_mem, w_xq, w_xkv, w_xo, g_ffn2, w2_gate, w2_up, w2_down, g_final, loss_target, m_g_ffn1, m_w1_gate, m_w1_up, m_w1_down, m_g_mix, m_w_in, m_g_v, m_w_s, m_b_s, m_sinks, m_g_a_out, m_g_b_out, m_w_out, m_g_x, m_g_mem, m_w_xq, m_w_xkv, m_w_xo, m_g_ffn2, m_w2_gate, m_w2_up, m_w2_down, m_g_final, v_g_ffn1, v_w1_gate, v_w1_up, v_w1_down, v_g_mix, v_w_in, v_g_v, v_w_s, v_b_s, v_sinks, v_g_a_out, v_g_b_out, v_w_out, v_g_x, v_g_mem, v_w_xq, v_w_xkv, v_w_xo, v_g_ffn2, v_w2_gate, v_w2_up, v_w2_down, v_g_final):
    w = dict(g_ffn1=g_ffn1, w1_gate=w1_gate, w1_up=w1_up, w1_down=w1_down, g_mix=g_mix, w_in=w_in, g_v=g_v, w_s=w_s,
             b_s=b_s, sinks=sinks, g_a_out=g_a_out, g_b_out=g_b_out, w_out=w_out, g_x=g_x, g_mem=g_mem, w_xq=w_xq,
             w_xkv=w_xkv, w_xo=w_xo, g_ffn2=g_ffn2, w2_gate=w2_gate, w2_up=w2_up, w2_down=w2_down, g_final=g_final)
    mom = dict(g_ffn1=m_g_ffn1, w1_gate=m_w1_gate, w1_up=m_w1_up, w1_down=m_w1_down, g_mix=m_g_mix, w_in=m_w_in,
               g_v=m_g_v, w_s=m_w_s, b_s=m_b_s, sinks=m_sinks, g_a_out=m_g_a_out, g_b_out=m_g_b_out, w_out=m_w_out,
               g_x=m_g_x, g_mem=m_g_mem, w_xq=m_w_xq, w_xkv=m_w_xkv, w_xo=m_w_xo, g_ffn2=m_g_ffn2,
               w2_gate=m_w2_gate, w2_up=m_w2_up, w2_down=m_w2_down, g_final=m_g_final)
    var = dict(g_ffn1=v_g_ffn1, w1_gate=v_w1_gate, w1_up=v_w1_up, w1_down=v_w1_down, g_mix=v_g_mix, w_in=v_w_in,
               g_v=v_g_v, w_s=v_w_s, b_s=v_b_s, sinks=v_sinks, g_a_out=v_g_a_out, g_b_out=v_g_b_out, w_out=v_w_out,
               g_x=v_g_x, g_mem=v_g_mem, w_xq=v_w_xq, w_xkv=v_w_xkv, w_xo=v_w_xo, g_ffn2=v_g_ffn2,
               w2_gate=v_w2_gate, w2_up=v_w2_up, w2_down=v_w2_down, g_final=v_g_final)

    xs, ms, tgt = x[0], mem[0], loss_target[0]
    D = xs.shape[1]
    d_a = w_s.shape[1] * CHUNK
    d_b = D - d_a
    kvw = 2 * KV_HEADS * HEAD_DIM

    shard = {k: w[k][0].astype(BF16) for k in BIG}
    part = _run_exchange(_gather_exchange([shard[k] for k in FFN1_W], CHIP_PEERS), "ag_ffn1_ici")
    wg1, wu1, wd1 = _run_exchange(_sibling_exchange(part), "ag_ffn1_d2d")
    gf = g_final.reshape(1, D)
    ws, ws_t, bs_t = w_s[0], jnp.swapaxes(w_s[0], 1, 2), b_s[0].T
    sink_rows = jnp.repeat(sinks.reshape(-1), CHUNK).reshape(-1, 1)

    (h1, n1, a1, b1), part = _ffn_fwd(xs, g_ffn1, wg1, wu1, wd1, "ffn1_fwd",
                                      exchange=_gather_exchange([shard[k] for k in REST_W], CHIP_PEERS))
    part = dict(zip(REST_W, part))
    n2, (win_blocks,) = _rms_fwd(h1, g_mix, "mix_norm", exchange=_sibling_exchange([part["w_in"]]))
    win = _blocks_to_cols(win_blocks)
    z = _matmul(n2, win, "nn", F32, "mm_in", tn=win.shape[1] // 2)
    ya_n = _sgu_fwd(z, g_v, ws, bs_t, g_a_out, "sgu_fwd")
    later = [k for k in REST_W if k != "w_in"]
    (yb, yb_n), handed = _swa_fwd(z, sink_rows, g_b_out, "swa_fwd",
                                  exchange=_sibling_exchange([part[k] for k in later]))
    gathered = dict(zip(later, handed))
    wg2, wu2, wd2 = gathered["w2_gate"], gathered["w2_up"], gathered["w2_down"]
    wxkv = _blocks_to_cols(gathered["w_xkv"])
    wout = gathered["w_out"].reshape(D, D)
    wxq = gathered["w_xq"].reshape(D, D)
    wxo = gathered["w_xo"].reshape(D, D)
    y = jnp.concatenate([ya_n, yb_n], axis=1)
    h2 = _matmul(y, wout, "nn", F32, "mm_out", res=h1, tn=D)
    hx, _ = _rms_fwd(h2, g_x, "x_norm")
    mn, _ = _rms_fwd(ms, g_mem, "mem_norm")
    q = _matmul(hx, wxq, "nn", BF16, "mm_xq", tn=D)
    kv = _matmul(mn, wxkv, "nn", BF16, "mm_xkv")
    o = _xattn_fwd(q, kv, "xattn_fwd")
    h3 = _matmul(o, wxo, "nn", F32, "mm_xo", res=h2, tn=D)
    (h4, n4, a2, b2), _ = _ffn_fwd(h3, g_ffn2, wg2, wu2, wd2, "ffn2_fwd")
    dh4, dg_final, loss_part = _final_loss(h4, gf, tgt, "final_loss")

    grad_big, grad_small, recv_big = {}, {"g_final": dg_final}, {}
    order = _dw_block_order()
    (dn4, df2, da2, db2, s2), _ = _ffn_bwd_dx(dh4, a2, b2, wg2, wu2, wd2, "ffn2_bwd_dx")
    dh3, grad_small["g_ffn2"] = _rms_bwd(h3, g_ffn2, dn4, dh4, "ffn2_norm_bwd")
    (recv_big["w2_gate"], recv_big["w2_up"]), _ = _ffn_bwd_dw([(n4, da2), (n4, db2)], order, "ffn2_bwd_dw_gu")
    (recv_big["w2_down"],), _ = _ffn_bwd_dw([(s2, df2)], order, "ffn2_bwd_dw_d")

    do = _matmul(dh3, wxo, "nt", BF16, "mm_xo_dx", tn=D)
    grad_big["w_xo"] = _matmul(o, dh3, "tn", BF16, "mm_xo_dw", tm=1024, tn=1024).reshape(N_DEV, D // N_DEV, D)
    dq, dkv = _xattn_bwd(q, kv, do, "xattn_bwd")
    dhx = _matmul(dq, wxq, "nt", F32, "mm_xq_dx", tn=D)
    grad_big["w_xq"] = _matmul(hx, dq, "tn", BF16, "mm_xq_dw", tm=1024, tn=1024).reshape(N_DEV, D // N_DEV, D)
    dmn = _matmul(dkv, wxkv, "nt", F32, "mm_xkv_dx")
    grad_big["w_xkv"] = _cols_to_blocks(_matmul(mn, dkv, "tn", BF16, "mm_xkv_dw", tm=1024, tn=1024))
    dh2, grad_small["g_x"] = _rms_bwd(h2, g_x, dhx, dh3, "x_norm_bwd")
    _, grad_small["g_mem"] = _rms_bwd(ms, g_mem, dmn, None, "mem_norm_bwd")

    dy = _matmul(dh2, wout, "nt", F32, "mm_out_dx", tn=D)
    grad_big["w_out"] = _matmul(y, dh2, "tn", BF16, "mm_out_dw", tm=1024, tn=1024).reshape(N_DEV, D // N_DEV, D)
    dz_uv, grad_small["w_s"], dbs_t, grad_small["g_v"], grad_small["g_a_out"] = _sgu_bwd(
        z, dy, g_v, ws, ws_t, bs_t, g_a_out, "sgu_bwd")
    grad_small["b_s"] = dbs_t.T
    dq_b, dkv_b, grad_small["sinks"], grad_small["g_b_out"] = _swa_bwd(z, yb, dy, sink_rows, g_b_out, "swa_bwd")
    dz = jnp.concatenate([dz_uv, dq_b, dkv_b], axis=1)
    dn2 = _matmul(dz, win, "nt", F32, "mm_in_dx", tn=1024)
    grad_big["w_in"] = _cols_to_blocks(_matmul(n2, dz, "tn", BF16, "mm_in_dw", tm=1024, tn=dz.shape[1] // 2))
    dh1, grad_small["g_mix"] = _rms_bwd(h1, g_mix, dn2, dh2, "mix_norm_bwd")

    (dn1, df1, da1, db1, s1), recv_mid = _ffn_bwd_dx(
        dh1, a1, b1, wg1, wu1, wd1, "ffn1_bwd_dx", exchange=_scatter_exchange([grad_big[k] for k in MID_W]))
    recv_big.update(zip(MID_W, recv_mid))
    dx, grad_small["g_ffn1"] = _rms_bwd(xs, g_ffn1, dn1, dh1, "ffn1_norm_bwd")
    (recv_big["w1_gate"], recv_big["w1_up"]), (recv_small,) = _ffn_bwd_dw(
        [(n1, da1), (n1, db1)], order, "ffn1_bwd_dw_gu",
        exchange=_gather_exchange([_pack([grad_small[k] for k in SMALL])], ALL_PEERS))
    (recv_big["w1_down"],), _ = _ffn_bwd_dw([(s1, df1)], order, "ffn1_bwd_dw_d")


    grads, deltas, new_m, new_v = {}, {}, {}, {}
    for k in BIG:
        shp = w[k].shape
        two_d = shp[1:]
        outs = _adamw(w[k].reshape(two_d), mom[k].reshape(two_d), var[k].reshape(two_d), recv_big[k], "adamw_" + k)
        grads[k], deltas[k], new_m[k], new_v[k] = [t.reshape(shp) for t in outs]
    packed = _adamw(_pack([w[k] for k in SMALL]), _pack([mom[k] for k in SMALL]), _pack([var[k] for k in SMALL]),
                    recv_small, "adamw_small")
    off = 0
    for k in SMALL:
        shp = w[k].shape
        size = 1
        for s in shp:
            size *= s
        for dst, src in zip((grads, deltas, new_m, new_v), packed):
            dst[k] = src.reshape(-1)[off:off + size].reshape(shp)
        off += size

    loss = lax.psum(loss_part[0, 0], AXES)
    return (loss, dx[None], *[grads[k] for k in WEIGHTS], *[deltas[k] for k in WEIGHTS],
            *[new_m[k] for k in WEIGHTS], *[new_v[k] for k in WEIGHTS])
```

```python
import jax
import jax.numpy as jnp
from jax import lax
from jax.experimental import pallas as pl
from jax.experimental.pallas import tpu as pltpu

F32 = jnp.float32
BF16 = jnp.bfloat16
SDS = jax.ShapeDtypeStruct
BS = pl.BlockSpec

N_DEV = 8
AXES = ("x", "y", "c")
EPS = 1e-5
CHUNK = 128
HEAD_DIM = 64
KV_HEADS = 2
X_HEADS = 4
NEG = -1e30
ADAM_LR = 0.001
ADAM_B1 = 0.9
ADAM_B2 = 0.999
ADAM_EPS = 1e-08
ADAM_WD = 0.01
ADAM_STEP = 10
MIB = 1 << 20
FFN_SUB_ROWS = 256
WEIGHTS = ['g_ffn1', 'w1_gate', 'w1_up', 'w1_down', 'g_mix', 'w_in', 'g_v', 'w_s', 'b_s', 'sinks', 'g_a_out',
           'g_b_out', 'w_out', 'g_x', 'g_mem', 'w_xq', 'w_xkv', 'w_xo', 'g_ffn2', 'w2_gate', 'w2_up', 'w2_down',
           'g_final']
BIG = ['w1_gate', 'w1_up', 'w1_down', 'w_in', 'w_out', 'w_xq', 'w_xkv', 'w_xo', 'w2_gate', 'w2_up', 'w2_down']
SMALL = [w for w in WEIGHTS if w not in BIG]
FFN1_W = ['w1_gate', 'w1_up', 'w1_down']
FFN2_W = ['w2_gate', 'w2_up', 'w2_down']
MID_W = ['w_in', 'w_out', 'w_xq', 'w_xkv', 'w_xo']
REST_W = MID_W + FFN2_W


def _params(sem=None, vmem_mib=48):
    return pltpu.CompilerParams(dimension_semantics=sem, vmem_limit_bytes=vmem_mib * MIB)


def _dot(a, b):
    return jnp.dot(a, b, preferred_element_type=F32)


def _dot_nt(a, b):
    return lax.dot_general(a, b, (((1,), (1,)), ((), ())), preferred_element_type=F32)


def _dot_tn(a, b):
    return lax.dot_general(a, b, (((0,), (0,)), ((), ())), preferred_element_type=F32)


def _rstd(v):
    return lax.rsqrt(jnp.mean(v * v, axis=-1, keepdims=True) + EPS)


def _norm_bwd(xhat, r, g, d):
    t = d * g
    return r * (t - xhat * jnp.mean(t * xhat, axis=-1, keepdims=True))


def _gelu(v):
    return 0.5 * v * (1.0 + lax.erf(v * 0.7071067811865476))


def _gelu_grad(v):
    return 0.5 * (1.0 + lax.erf(v * 0.7071067811865476)) + v * jnp.exp(-0.5 * v * v) * 0.3989422804014327


def _mesh_pos():
    x, y, c = lax.axis_index("x"), lax.axis_index("y"), lax.axis_index("c")
    return x, y, c, 4 * x + 2 * y + c


def _peer(x, y, c, k):
    px = 1 - x if k & 4 else x
    py = 1 - y if k & 2 else y
    pc = 1 - c if k & 1 else c
    return (px, py, pc), 4 * px + 2 * py + pc


ANY = BS(memory_space=pl.ANY)
DMA_SEM = pltpu.SemaphoreType.DMA
ALL_PEERS = (1, 2, 3, 4, 5, 6, 7)
CHIP_PEERS = (2, 4, 6)
SIBLING = 1


def _remote(src, dst, send, recv, peer):
    return pltpu.make_async_remote_copy(src_ref=src, dst_ref=dst, send_sem=send, recv_sem=recv, device_id=peer,
                                        device_id_type=pl.DeviceIdType.MESH)


class _Exchange:
    def __init__(self, ins, out_shapes, sems, copies, aliases=None):
        self.ins, self.out_shapes, self.sems, self.copies, self.aliases = ins, out_shapes, sems, copies, aliases or {}


def _gather_exchange(shards, peers):
    n, m = len(shards), len(peers)

    def copies(ins, outs, sems):
        send, recv, lsem = sems
        x, y, c, me = _mesh_pos()
        cps = [pltpu.make_async_copy(ins[w], outs[w].at[me], lsem.at[w]) for w in range(n)]
        for w in range(n):
            for j, k in enumerate(peers):
                cps.append(_remote(ins[w], outs[w].at[me], send.at[w, j], recv.at[w, j], _peer(x, y, c, k)[0]))
        return cps

    return _Exchange(shards, [SDS((N_DEV,) + s.shape, s.dtype) for s in shards],
                     [DMA_SEM((n, m)), DMA_SEM((n, m)), DMA_SEM((n,))], copies)


def _sibling_exchange(gathered):
    n = len(gathered)

    def copies(ins, outs, sems):
        send, recv = sems
        x, y, c, me = _mesh_pos()
        slots = [me] + [_peer(x, y, c, k)[1] for k in CHIP_PEERS]
        sib = _peer(x, y, c, SIBLING)[0]
        return [_remote(outs[w].at[s], outs[w].at[s], send.at[w, j], recv.at[w, j], sib)
                for w in range(n) for j, s in enumerate(slots)]

    return _Exchange(gathered, [SDS(g.shape, g.dtype) for g in gathered], [DMA_SEM((n, 4)), DMA_SEM((n, 4))], copies,
                     aliases={w: w for w in range(n)})


def _scatter_exchange(fulls):
    n, m = len(fulls), len(ALL_PEERS)

    def copies(ins, outs, sems):
        send, recv, lsem = sems
        x, y, c, me = _mesh_pos()
        cps = [pltpu.make_async_copy(ins[w].at[me], outs[w].at[me], lsem.at[w]) for w in range(n)]
        for w in range(n):
            for j, k in enumerate(ALL_PEERS):
                peer, p = _peer(x, y, c, k)
                cps.append(_remote(ins[w].at[p], outs[w].at[me], send.at[w, j], recv.at[w, j], peer))
        return cps

    return _Exchange(fulls, [SDS(f.shape, f.dtype) for f in fulls],
                     [DMA_SEM((n, m)), DMA_SEM((n, m)), DMA_SEM((n,))], copies)


def _call(body, *, name, args, in_specs, out_shape, out_specs, grid, sem, vmem_mib, scratch=(), exchange=None,
          first=None, last=None):
    if exchange is None:
        return pl.pallas_call(body, name=name, out_shape=out_shape, grid=grid, in_specs=in_specs,
                              out_specs=out_specs, scratch_shapes=list(scratch),
                              compiler_params=_params(sem, vmem_mib))(*args), []
    ni, no, ns = len(args), len(out_shape), len(scratch)
    xi, xo = len(exchange.ins), len(exchange.out_shapes)

    def hosted(*refs):
        own_in, refs = refs[:ni], refs[ni:]
        x_in, refs = refs[:xi], refs[xi:]
        own_out, refs = refs[:no], refs[no:]
        x_out, refs = refs[:xo], refs[xo:]
        own_scr, x_sem = refs[:ns], refs[ns:]

        @pl.when(first())
        def _():
            for cp in exchange.copies(x_in, x_out, x_sem):
                cp.start()

        body(*own_in, *own_out, *own_scr)

        @pl.when(last())
        def _():
            for cp in exchange.copies(x_in, x_out, x_sem):
                cp.wait()

    outs = pl.pallas_call(
        hosted, name=name, out_shape=list(out_shape) + list(exchange.out_shapes), grid=grid,
        in_specs=list(in_specs) + [ANY] * xi, out_specs=list(out_specs) + [ANY] * xo,
        scratch_shapes=list(scratch) + list(exchange.sems),
        input_output_aliases={ni + a: no + b for a, b in exchange.aliases.items()},
        compiler_params=pltpu.CompilerParams(dimension_semantics=tuple("arbitrary" for _ in grid),
                                             vmem_limit_bytes=vmem_mib * MIB, has_side_effects=True),
    )(*args, *exchange.ins)
    return outs[:no], outs[no:]


def _run_exchange(exchange, name):
    xi, xo = len(exchange.ins), len(exchange.out_shapes)

    def body(*refs):
        cps = exchange.copies(refs[:xi], refs[xi:xi + xo], refs[xi + xo:])
        for cp in cps:
            cp.start()
        for cp in cps:
            cp.wait()

    return pl.pallas_call(
        body, name=name, out_shape=list(exchange.out_shapes), in_specs=[ANY] * xi, out_specs=[ANY] * xo,
        scratch_shapes=list(exchange.sems), input_output_aliases=dict(exchange.aliases),
        compiler_params=pltpu.CompilerParams(has_side_effects=True),
    )(*exchange.ins)


def _rms_fwd(h, g, name, tm=512, exchange=None):
    S, D = h.shape
    tm = min(tm, S)
    assert S % tm == 0, (S, tm)
    ni = S // tm

    def body(h_ref, g_ref, o_ref):
        hv = h_ref[...]
        o_ref[...] = (hv * _rstd(hv) * g_ref[...]).astype(o_ref.dtype)

    (out,), extra = _call(
        body, name=name, args=(h, g), out_shape=[SDS((S, D), BF16)], grid=(ni,),
        in_specs=[BS((tm, D), lambda i: (i, 0)), BS((1, D), lambda i: (0, 0))],
        out_specs=[BS((tm, D), lambda i: (i, 0))], sem=("parallel",), vmem_mib=32, exchange=exchange,
        first=lambda: pl.program_id(0) == 0, last=lambda: pl.program_id(0) == ni - 1)
    return out, extra


def _rms_bwd(h, g, dn, dres, name, tm=256):
    S, D = h.shape
    tm = min(tm, S)
    assert S % tm == 0, (S, tm)
    has_res = dres is not None

    def body(*refs):
        if has_res:
            h_ref, g_ref, dn_ref, dres_ref, dh_ref, dg_ref = refs
        else:
            h_ref, g_ref, dn_ref, dh_ref, dg_ref = refs
        hv = h_ref[...]
        r = _rstd(hv)
        xh = hv * r
        d = dn_ref[...].astype(F32)

        @pl.when(pl.program_id(0) == 0)
        def _():
            dg_ref[...] = jnp.zeros_like(dg_ref)

        dg_ref[...] += jnp.sum(d * xh, axis=0, keepdims=True)
        dh = _norm_bwd(xh, r, g_ref[...], d)
        dh_ref[...] = dres_ref[...] + dh if has_res else dh

    row = BS((tm, D), lambda i: (i, 0))
    vec = BS((1, D), lambda i: (0, 0))
    args = (h, g, dn) + ((dres,) if has_res else ())
    return pl.pallas_call(
        body, name=name, out_shape=[SDS((S, D), F32), SDS((1, D), F32)], grid=(S // tm,),
        in_specs=[row, vec, row] + ([row] if has_res else []), out_specs=[row, vec],
        compiler_params=_params(("arbitrary",), 40),
    )(*args)


def _final_loss(h, g, target, name, tm=256):
    S, D = h.shape
    tm = min(tm, S)
    assert S % tm == 0, (S, tm)

    def body(h_ref, g_ref, t_ref, dh_ref, dg_ref, loss_ref):
        hv = h_ref[...]
        r = _rstd(hv)
        xh = hv * r
        gv = g_ref[...]
        e = xh * gv - t_ref[...]

        @pl.when(pl.program_id(0) == 0)
        def _():
            dg_ref[...] = jnp.zeros_like(dg_ref)
            loss_ref[...] = jnp.zeros_like(loss_ref)

        loss_ref[...] += 0.5 * jnp.sum(jnp.mean(e * e, axis=-1, keepdims=True), axis=0, keepdims=True)
        dy = e * (1.0 / D)
        dg_ref[...] += jnp.sum(dy * xh, axis=0, keepdims=True)
        dh_ref[...] = _norm_bwd(xh, r, gv, dy)

    row = BS((tm, D), lambda i: (i, 0))
    vec = BS((1, D), lambda i: (0, 0))
    return pl.pallas_call(
        body, name=name, out_shape=[SDS((S, D), F32), SDS((1, D), F32), SDS((1, 128), F32)], grid=(S // tm,),
        in_specs=[row, vec, row], out_specs=[row, vec, BS((1, 128), lambda i: (0, 0))],
        compiler_params=_params(("arbitrary",), 40),
    )(h, g, target)


def _matmul(a, b, mode, out_dtype, name, res=None, tm=512, tn=512, tk=512):
    if mode == "tn":
        K, M = a.shape
        N = b.shape[1]
        tm, tn, tk = min(tm, M), min(tn, N), min(tk, K)
        assert M % tm == 0 and N % tn == 0 and K % tk == 0, (a.shape, b.shape, tm, tn, tk)
        nk = K // tk

        def body(a_ref, b_ref, o_ref, acc_ref):
            k = pl.program_id(2)

            @pl.when(k == 0)
            def _():
                acc_ref[...] = jnp.zeros_like(acc_ref)

            acc_ref[...] += _dot_tn(a_ref[...].astype(BF16), b_ref[...].astype(BF16))

            @pl.when(k == nk - 1)
            def _():
                o_ref[...] = acc_ref[...].astype(o_ref.dtype)

        return pl.pallas_call(
            body, name=name, out_shape=SDS((M, N), out_dtype), grid=(M // tm, N // tn, nk),
            in_specs=[BS((tk, tm), lambda i, j, k: (k, i)), BS((tk, tn), lambda i, j, k: (k, j))],
            out_specs=BS((tm, tn), lambda i, j, k: (i, j)), scratch_shapes=[pltpu.VMEM((tm, tn), F32)],
            compiler_params=_params(("parallel", "parallel", "arbitrary"), 48),
        )(a, b)

    M, K = a.shape
    N = b.shape[1] if mode == "nn" else b.shape[0]
    tm, tn = min(tm, M), min(tn, N)
    assert M % tm == 0 and N % tn == 0, (a.shape, b.shape, tm, tn)
    has_res = res is not None

    def body(*refs):
        if has_res:
            a_ref, b_ref, r_ref, o_ref = refs
        else:
            a_ref, b_ref, o_ref = refs
        av, bv = a_ref[...].astype(BF16), b_ref[...].astype(BF16)
        acc = _dot(av, bv) if mode == "nn" else _dot_nt(av, bv)
        if has_res:
            acc = acc + r_ref[...]
        o_ref[...] = acc.astype(o_ref.dtype)

    b_spec = BS((K, tn), lambda i, j: (0, j)) if mode == "nn" else BS((tn, K), lambda i, j: (j, 0))
    o_spec = BS((tm, tn), lambda i, j: (i, j))
    return pl.pallas_call(
        body, name=name, out_shape=SDS((M, N), out_dtype), grid=(M // tm, N // tn),
        in_specs=[BS((tm, K), lambda i, j: (i, 0)), b_spec] + ([o_spec] if has_res else []), out_specs=o_spec,
        compiler_params=_params(("parallel", "parallel"), 48),
    )(*((a, b) + ((res,) if has_res else ())))


def _ffn_fwd(h, g, wg, wu, wd, name, tm=512, exchange=None):
    S, D = h.shape
    nb, _, Fs = wg.shape
    tm = min(tm, S)
    sub = min(FFN_SUB_ROWS, tm)
    assert S % tm == 0 and tm % sub == 0, (S, tm, sub)

    def body(h_ref, g_ref, wg_ref, wu_ref, wd_ref, o_ref, n_ref, a_ref, b_ref):
        j = pl.program_id(1)

        @pl.when(j == 0)
        def _():
            hv = h_ref[...]
            n_ref[...] = (hv * _rstd(hv) * g_ref[...]).astype(BF16)
            o_ref[...] = jnp.zeros_like(o_ref)

        for r in range(0, tm, sub):
            rows = slice(r, r + sub)
            n = n_ref[rows, :]
            a = _dot(n, wg_ref[...]).astype(BF16)
            b = _dot(n, wu_ref[...]).astype(BF16)
            a_ref[rows, :] = a
            b_ref[rows, :] = b
            a, b = a.astype(F32), b.astype(F32)
            s = (a * jax.nn.sigmoid(a) * b).astype(BF16)
            o_ref[rows, :] += _dot(s, wd_ref[...])

        @pl.when(j == nb - 1)
        def _():
            o_ref[...] = h_ref[...] + 0.5 * o_ref[...]

    row = BS((tm, D), lambda i, j: (i, 0))
    wcol = BS((None, D, Fs), lambda i, j: (j, 0, 0))
    act = BS((None, tm, Fs), lambda i, j: (j, i, 0))
    ni = S // tm
    return _call(
        body, name=name, args=(h, g, wg, wu, wd),
        out_shape=[SDS((S, D), F32), SDS((S, D), BF16), SDS((nb, S, Fs), BF16), SDS((nb, S, Fs), BF16)],
        grid=(ni, nb),
        in_specs=[row, BS((1, D), lambda i, j: (0, 0)), wcol, wcol, BS((None, Fs, D), lambda i, j: (j, 0, 0))],
        out_specs=[row, row, act, act], sem=("parallel", "arbitrary"), vmem_mib=56, exchange=exchange,
        first=lambda: (pl.program_id(0) == 0) & (pl.program_id(1) == 0),
        last=lambda: (pl.program_id(0) == ni - 1) & (pl.program_id(1) == nb - 1))


def _ffn_bwd_dx(dh, a, b, wg, wu, wd, name, tm=512, exchange=None):
    S, D = dh.shape
    nb, _, Fs = wg.shape
    tm = min(tm, S)
    assert S % tm == 0, (S, tm)

    def body(dh_ref, a_ref, b_ref, wg_ref, wu_ref, wd_ref, dn_ref, df_ref, da_ref, db_ref, s_ref, ds_ref):
        j = pl.program_id(1)

        def first_matmul():
            ds_ref[...] = _dot_nt(df_ref[...], wd_ref[...])

        def pre_activation_cotangents():
            ds = ds_ref[...]
            av, bv = a_ref[...].astype(F32), b_ref[...].astype(F32)
            sig = jax.nn.sigmoid(av)
            sl = av * sig
            da = (ds * bv * (sig * (1.0 + av * (1.0 - sig)))).astype(BF16)
            db = (ds * sl).astype(BF16)
            da_ref[...] = da
            db_ref[...] = db
            s_ref[...] = (sl * bv).astype(BF16)
            return da, db

        def last_matmuls(da, db):
            dn_ref[...] += _dot_nt(da, wg_ref[...]) + _dot_nt(db, wu_ref[...])

        @pl.when(j == 0)
        def _():
            df_ref[...] = (0.5 * dh_ref[...]).astype(BF16)
            dn_ref[...] = jnp.zeros_like(dn_ref)
            first_matmul()

        @pl.when((j > 0) & (j < nb))
        def _():
            da, db = pre_activation_cotangents()
            first_matmul()
            last_matmuls(da, db)

        @pl.when(j == nb)
        def _():
            last_matmuls(*pre_activation_cotangents())

    row = BS((tm, D), lambda i, j: (i, 0))
    prev = BS((None, D, Fs), lambda i, j: (jnp.maximum(j - 1, 0), 0, 0))
    act = BS((None, tm, Fs), lambda i, j: (jnp.maximum(j - 1, 0), i, 0))
    ni = S // tm
    return _call(
        body, name=name, args=(dh, a, b, wg, wu, wd),
        out_shape=[SDS((S, D), F32), SDS((S, D), BF16)] + [SDS((nb, S, Fs), BF16)] * 3,
        grid=(ni, nb + 1),
        in_specs=[row, act, act, prev, prev, BS((None, Fs, D), lambda i, j: (jnp.minimum(j, nb - 1), 0, 0))],
        out_specs=[row, row, act, act, act], scratch=[pltpu.VMEM((tm, Fs), F32)],
        sem=("parallel", "arbitrary"), vmem_mib=56, exchange=exchange,
        first=lambda: (pl.program_id(0) == 0) & (pl.program_id(1) == 0),
        last=lambda: (pl.program_id(0) == ni - 1) & (pl.program_id(1) == nb))


DW_FLIPS = {0: (6, 2, 4, 0), 1: (6, 4, 2, 0)}
N_CHIPS = N_DEV // 2


def _dw_block_order():
    x, y, c, me = _mesh_pos()
    steps = [jnp.array([v for mine, sib in zip(DW_FLIPS[core], DW_FLIPS[1 - core]) for v in (sib ^ 1, mine)], jnp.int32)
             for core in (0, 1)]
    return jnp.bitwise_xor(me.astype(jnp.int32), jnp.where(c == 0, steps[0], steps[1]))


def _ffn_bwd_dw(pairs, order, name, tk=512, exchange=None):
    npair = len(pairs)
    S = pairs[0][0].shape[-2]
    nb, half = N_DEV, N_DEV // 2
    tk = min(tk, S)
    assert S % tk == 0, (S, tk)
    nk = S // tk
    shapes = [(lhs.shape[-1], rhs.shape[-1]) for lhs, rhs in pairs]
    xi = len(exchange.ins) if exchange else 0
    xo = len(exchange.out_shapes) if exchange else 0

    def body(order_ref, *rest):
        tiles, rest = rest[:2 * npair], rest[2 * npair:]
        x_in, rest = rest[:xi], rest[xi:]
        recv_bufs, rest = rest[:npair], rest[npair:]
        x_out, rest = rest[:xo], rest[xo:]
        accs, rest = rest[:npair], rest[npair:]
        out_t, rest = rest[:npair], rest[npair:]
        out_m, rest = rest[:npair], rest[npair:]
        land, rest = rest[:npair], rest[npair:]
        (send_t, recv_t, send_m, recv_m, lsem, credit), x_sem = rest[:6], rest[6:]
        t, k = pl.program_id(0), pl.program_id(1)
        x, y, c, me = _mesh_pos()
        sibling = (x, y, 1 - c)
        chip = 2 * x + y

        if exchange:
            @pl.when((t == 0) & (k == 0))
            def _():
                for cp in exchange.copies(x_in, x_out, x_sem):
                    cp.start()

        @pl.when(k == 0)
        def _():
            for acc in accs:
                acc[...] = jnp.zeros_like(acc)

        for w in range(npair):
            accs[w][...] += _dot_tn(tiles[2 * w][...], tiles[2 * w + 1][...])

        def to_sibling(w, i):
            return _remote(out_t[w], land[w], send_t.at[w, i], recv_t.at[w, i], sibling)

        def to_owner(w, i):
            dst = recv_bufs[w].at[chip]
            if i == half - 1:
                return pltpu.make_async_copy(out_m[w], dst, lsem.at[w])
            p = jnp.bitwise_xor(me, jnp.where(c == 0, DW_FLIPS[0][i], DW_FLIPS[1][i]))
            return _remote(out_m[w], dst, send_m.at[w, i], recv_m.at[w, i], (p >> 2, (p >> 1) & 1, p & 1))

        for i in range(half):
            @pl.when((t == 2 * i) & (k == nk - 1))
            def _(i=i):
                if i >= 1:
                    pl.semaphore_wait(credit, 1)
                for w in range(npair):
                    if i >= 1:
                        to_sibling(w, i - 1).wait_send()
                    out_t[w][...] = accs[w][...].astype(BF16)
                    to_sibling(w, i).start()

            @pl.when((t == 2 * i + 1) & (k == nk - 1))
            def _(i=i):
                for w in range(npair):
                    to_sibling(w, i).wait_recv()
                    if i >= 1:
                        to_owner(w, i - 1).wait_send()
                    out_m[w][...] = (accs[w][...] + land[w][...].astype(F32)).astype(BF16)
                if i < half - 1:
                    pl.semaphore_signal(credit, inc=1, device_id=sibling, device_id_type=pl.DeviceIdType.MESH)
                for w in range(npair):
                    to_owner(w, i).start()

        @pl.when((t == nb - 1) & (k == nk - 1))
        def _():
            for w in range(npair):
                to_sibling(w, half - 1).wait_send()
                to_owner(w, half - 1).wait()
                for i in range(half - 1):
                    to_owner(w, i).wait_recv()
            if exchange:
                for cp in exchange.copies(x_in, x_out, x_sem):
                    cp.wait()

    def tile_spec(arr):
        if arr.ndim == 3:
            return BS((None, tk, arr.shape[-1]), lambda t, k, o: (o[t], k, 0))
        return BS((tk, arr.shape[-1]), lambda t, k, o: (k, 0))

    flat = [a for pair in pairs for a in pair]
    grid_spec = pltpu.PrefetchScalarGridSpec(
        num_scalar_prefetch=1, grid=(nb, nk),
        in_specs=[tile_spec(a) for a in flat] + [ANY] * xi, out_specs=[ANY] * (npair + xo),
        scratch_shapes=[pltpu.VMEM(s, F32) for s in shapes] + [pltpu.VMEM(s, BF16) for s in shapes] * 3
        + [DMA_SEM((npair, half)), DMA_SEM((npair, half)), DMA_SEM((npair, half - 1)), DMA_SEM((npair, half - 1)),
           DMA_SEM((npair,)), pltpu.SemaphoreType.REGULAR]
        + (list(exchange.sems) if exchange else []))
    outs = pl.pallas_call(
        body, name=name, grid_spec=grid_spec,
        out_shape=[SDS((N_CHIPS,) + s, BF16) for s in shapes] + (list(exchange.out_shapes) if exchange else []),
        compiler_params=pltpu.CompilerParams(dimension_semantics=("arbitrary", "arbitrary"),
                                             vmem_limit_bytes=48 * MIB, has_side_effects=True),
    )(order, *flat, *(exchange.ins if exchange else ()))
    return outs[:npair], outs[npair:]


def _sgu_parts(z, gv, ws_ref, bst_ref, groups):
    da = z.shape[1] // 2
    zu, zv = z[:, :da], z[:, da:]
    u, v = _gelu(zu), _gelu(zv)
    rv = _rstd(v)
    vhat = v * rv
    vn = (vhat * gv).astype(BF16)
    tri = lax.broadcasted_iota(jnp.int32, (CHUNK, CHUNK), 0) >= lax.broadcasted_iota(jnp.int32, (CHUNK, CHUNK), 1)
    pieces = []
    for g in range(groups):
        w = jnp.where(tri, ws_ref[g], 0.0).astype(BF16)
        pieces.append(_dot(w, vn[:, g * CHUNK:(g + 1) * CHUNK]) + bst_ref[:, g:g + 1])
    sv = jnp.concatenate(pieces, axis=1)
    return dict(zu=zu, zv=zv, u=u, rv=rv, vhat=vhat, vn=vn, sv=sv, tri=tri)


def _sgu_fwd(z, g_v, w_s, b_st, g_a, name):
    S = z.shape[0]
    groups = w_s.shape[0]
    da = groups * CHUNK

    def body(z_ref, gv_ref, ws_ref, bst_ref, ga_ref, o_ref):
        p = _sgu_parts(z_ref[...], gv_ref[...], ws_ref, bst_ref, groups)
        ya = p["u"] * p["sv"]
        o_ref[...] = (ya * _rstd(ya) * ga_ref[...]).astype(BF16)

    vec = BS((1, da), lambda i: (0, 0))
    return pl.pallas_call(
        body, name=name, out_shape=SDS((S, da), BF16), grid=(S // CHUNK,),
        in_specs=[BS((CHUNK, 2 * da), lambda i: (i, 0)), vec, BS((groups, CHUNK, CHUNK), lambda i: (0, 0, 0)),
                  BS((CHUNK, groups), lambda i: (0, 0)), vec],
        out_specs=BS((CHUNK, da), lambda i: (i, 0)), compiler_params=_params(("parallel",), 32),
    )(z, g_v, w_s, b_st, g_a)


def _sgu_bwd(z, dy, g_v, w_s, w_st, b_st, g_a, name):
    S = z.shape[0]
    groups = w_s.shape[0]
    da = groups * CHUNK

    def body(z_ref, dy_ref, gv_ref, ws_ref, wst_ref, bst_ref, ga_ref, dz_ref, dws_ref, dbst_ref, dgv_ref, dga_ref):
        @pl.when(pl.program_id(0) == 0)
        def _():
            dws_ref[...] = jnp.zeros_like(dws_ref)
            dbst_ref[...] = jnp.zeros_like(dbst_ref)
            dgv_ref[...] = jnp.zeros_like(dgv_ref)
            dga_ref[...] = jnp.zeros_like(dga_ref)

        gv = gv_ref[...]
        p = _sgu_parts(z_ref[...], gv, ws_ref, bst_ref, groups)
        u, sv, tri = p["u"], p["sv"], p["tri"]
        ya = u * sv
        ra = _rstd(ya)
        yhat = ya * ra
        d = dy_ref[...]
        dga_ref[...] += jnp.sum(d * yhat, axis=0, keepdims=True)
        dya = _norm_bwd(yhat, ra, ga_ref[...], d)
        du = dya * sv
        dsv = dya * u
        dsv_b = dsv.astype(BF16)
        tri_t = (lax.broadcasted_iota(jnp.int32, (CHUNK, CHUNK), 0)
                 <= lax.broadcasted_iota(jnp.int32, (CHUNK, CHUNK), 1))
        lane = lax.broadcasted_iota(jnp.int32, (CHUNK, groups), 1)
        dvn = []
        dbs = jnp.zeros((CHUNK, groups), F32)
        for g in range(groups):
            cols = slice(g * CHUNK, (g + 1) * CHUNK)
            dbs = dbs + jnp.where(lane == g, jnp.sum(dsv[:, cols], axis=1, keepdims=True), 0.0)
            dws_ref[g] += jnp.where(tri, _dot_nt(dsv_b[:, cols], p["vn"][:, cols]), 0.0)
            wt = jnp.where(tri_t, wst_ref[g], 0.0).astype(BF16)
            dvn.append(_dot(wt, dsv_b[:, cols]))
        dbst_ref[...] += dbs
        dvn = jnp.concatenate(dvn, axis=1)
        dgv_ref[...] += jnp.sum(dvn * p["vhat"], axis=0, keepdims=True)
        dv = _norm_bwd(p["vhat"], p["rv"], gv, dvn)
        dz_ref[...] = jnp.concatenate([du * _gelu_grad(p["zu"]), dv * _gelu_grad(p["zv"])], axis=1)

    vec = BS((1, da), lambda i: (0, 0))
    wsq = BS((groups, CHUNK, CHUNK), lambda i: (0, 0, 0))
    bsq = BS((CHUNK, groups), lambda i: (0, 0))
    return pl.pallas_call(
        body, name=name,
        out_shape=[SDS((S, 2 * da), F32), SDS((groups, CHUNK, CHUNK), F32), SDS((CHUNK, groups), F32),
                   SDS((1, da), F32), SDS((1, da), F32)],
        grid=(S // CHUNK,),
        in_specs=[BS((CHUNK, 2 * da), lambda i: (i, 0)), BS((CHUNK, da), lambda i: (i, 0)), vec, wsq, wsq, bsq, vec],
        out_specs=[BS((CHUNK, 2 * da), lambda i: (i, 0)), wsq, bsq, vec, vec],
        compiler_params=_params(("arbitrary",), 32),
    )(z, dy, g_v, w_s, w_st, b_st, g_a)


def _swa_mask(i, group):
    row = lax.broadcasted_iota(jnp.int32, (group * CHUNK, 2 * CHUNK), 0) & (CHUNK - 1)
    col = lax.broadcasted_iota(jnp.int32, (group * CHUNK, 2 * CHUNK), 1)
    d = row + CHUNK - col
    return (d >= 0) & (d < CHUNK) & jnp.logical_or(i > 0, col >= CHUNK)


def _stack_heads(t, g, group):
    return jnp.concatenate([t[:, h * HEAD_DIM:(h + 1) * HEAD_DIM] for h in range(g * group, (g + 1) * group)], axis=0)


def _unstack_heads(stacked, group):
    return [stacked[h * CHUNK:(h + 1) * CHUNK] for h in range(group)]


def _swa_probs(qh, kh, sink, mask):
    s = jnp.where(mask, _dot_nt(qh, kh) * (HEAD_DIM ** -0.5), NEG)
    m = jnp.maximum(jnp.max(s, axis=-1, keepdims=True), sink)
    e = jnp.exp(s - m)
    es = jnp.exp(sink - m)
    inv = 1.0 / (jnp.sum(e, axis=-1, keepdims=True) + es)
    return e * inv, es * inv


def _swa_specs(db, nblk, clamp):
    kvw = 2 * KV_HEADS * HEAD_DIM
    cur = (lambda i: jnp.minimum(i, nblk - 1)) if clamp else (lambda i: i)
    q_spec = BS((CHUNK, db), lambda i: (cur(i), 2))
    kc_spec = BS((CHUNK, kvw), lambda i: (cur(i), 3 * db // kvw))
    kp_spec = BS((CHUNK, kvw), lambda i: (jnp.maximum(cur(i) - 1, 0), 3 * db // kvw))
    return q_spec, kc_spec, kp_spec


def _swa_fwd(z, sink_rows, g_b, name, exchange=None):
    S = z.shape[0]
    db = g_b.shape[1]
    heads = db // HEAD_DIM
    group = heads // KV_HEADS
    nblk = S // CHUNK

    def body(q_ref, kc_ref, kp_ref, sk_ref, gb_ref, yb_ref, ybn_ref):
        mask = _swa_mask(pl.program_id(0), group)
        q = q_ref[...].astype(BF16)
        kv = jnp.concatenate([kp_ref[...], kc_ref[...]], axis=0).astype(BF16)
        outs = []
        for g in range(KV_HEADS):
            kg = kv[:, g * HEAD_DIM:(g + 1) * HEAD_DIM]
            vg = kv[:, (KV_HEADS + g) * HEAD_DIM:(KV_HEADS + g + 1) * HEAD_DIM]
            rows = slice(g * group * CHUNK, (g + 1) * group * CHUNK)
            p, _ = _swa_probs(_stack_heads(q, g, group), kg, sk_ref[rows, :], mask)
            outs += _unstack_heads(_dot(p.astype(BF16), vg), group)
        yb = jnp.concatenate(outs, axis=1)
        yb_ref[...] = yb
        ybn_ref[...] = (yb * _rstd(yb) * gb_ref[...]).astype(BF16)

    q_spec, kc_spec, kp_spec = _swa_specs(db, nblk, False)
    out = BS((CHUNK, db), lambda i: (i, 0))
    return _call(
        body, name=name, args=(z, z, z, sink_rows, g_b), out_shape=[SDS((S, db), F32), SDS((S, db), BF16)],
        grid=(nblk,),
        in_specs=[q_spec, kc_spec, kp_spec, BS((heads * CHUNK, 1), lambda i: (0, 0)), BS((1, db), lambda i: (0, 0))],
        out_specs=[out, out], sem=("parallel",), vmem_mib=32, exchange=exchange,
        first=lambda: pl.program_id(0) == 0, last=lambda: pl.program_id(0) == nblk - 1)


def _swa_bwd(z, yb, dy, sink_rows, g_b, name):
    S = z.shape[0]
    db = g_b.shape[1]
    heads = db // HEAD_DIM
    group = heads // KV_HEADS
    nblk = S // CHUNK
    kvw = 2 * KV_HEADS * HEAD_DIM

    def body(q_ref, kc_ref, kp_ref, yb_ref, dy_ref, sk_ref, gb_ref, dq_ref, dkv_ref, dsk_ref, dgb_ref, carry_ref):
        i = pl.program_id(0)

        @pl.when(i == 0)
        def _():
            carry_ref[...] = jnp.zeros_like(carry_ref)
            dsk_ref[...] = jnp.zeros_like(dsk_ref)
            dgb_ref[...] = jnp.zeros_like(dgb_ref)

        @pl.when(i < nblk)
        def _():
            mask = _swa_mask(i, group)
            yb = yb_ref[...]
            rb = _rstd(yb)
            yhat = yb * rb
            d = dy_ref[...]
            dgb_ref[...] += jnp.sum(d * yhat, axis=0, keepdims=True)
            do = _norm_bwd(yhat, rb, gb_ref[...], d).astype(BF16)
            q = q_ref[...].astype(BF16)
            kv = jnp.concatenate([kp_ref[...], kc_ref[...]], axis=0).astype(BF16)
            dqs, dk, dv = [], [], []
            lane = lax.broadcasted_iota(jnp.int32, (1, heads), 1)
            dsinks = jnp.zeros((1, heads), F32)
            for g in range(KV_HEADS):
                kg = kv[:, g * HEAD_DIM:(g + 1) * HEAD_DIM]
                vg = kv[:, (KV_HEADS + g) * HEAD_DIM:(KV_HEADS + g + 1) * HEAD_DIM]
                rows = slice(g * group * CHUNK, (g + 1) * group * CHUNK)
                qg, dog = _stack_heads(q, g, group), _stack_heads(do, g, group)
                p, ps = _swa_probs(qg, kg, sk_ref[rows, :], mask)
                dp = _dot_nt(dog, vg)
                dr = jnp.sum(p * dp, axis=-1, keepdims=True)
                ds = (p * (dp - dr) * (HEAD_DIM ** -0.5)).astype(BF16)
                for h, t in enumerate(_unstack_heads(ps * dr, group)):
                    dsinks = dsinks - jnp.where(lane == g * group + h, jnp.sum(t, axis=0, keepdims=True), 0.0)
                dqs += _unstack_heads(_dot(ds, kg), group)
                dk.append(_dot_tn(ds, qg))
                dv.append(_dot_tn(p.astype(BF16), dog))
            dq_ref[...] = jnp.concatenate(dqs, axis=1)
            dsk_ref[...] += dsinks
            contrib = jnp.concatenate(dk + dv, axis=1)
            dkv_ref[...] = carry_ref[...] + contrib[:CHUNK]
            carry_ref[...] = contrib[CHUNK:]

        @pl.when(i == nblk)
        def _():
            dkv_ref[...] = carry_ref[...]

    q_spec, kc_spec, kp_spec = _swa_specs(db, nblk, True)
    cur = BS((CHUNK, db), lambda i: (jnp.minimum(i, nblk - 1), 0))
    return pl.pallas_call(
        body, name=name,
        out_shape=[SDS((S, db), F32), SDS((S, kvw), F32), SDS((1, heads), F32), SDS((1, db), F32)],
        grid=(nblk + 1,),
        in_specs=[q_spec, kc_spec, kp_spec, cur, BS((CHUNK, db), lambda i: (jnp.minimum(i, nblk - 1), 1)),
                  BS((heads * CHUNK, 1), lambda i: (0, 0)), BS((1, db), lambda i: (0, 0))],
        out_specs=[cur, BS((CHUNK, kvw), lambda i: (jnp.maximum(i - 1, 0), 0)), BS((1, heads), lambda i: (0, 0)),
                   BS((1, db), lambda i: (0, 0))],
        scratch_shapes=[pltpu.VMEM((CHUNK, kvw), F32)], compiler_params=_params(("arbitrary",), 32),
    )(z, z, z, yb, dy, sink_rows, g_b)


def _xattn_probs(qh, kh, hd):
    s = _dot_nt(qh, kh) * (hd ** -0.5)
    e = jnp.exp(s - jnp.max(s, axis=-1, keepdims=True))
    return e / jnp.sum(e, axis=-1, keepdims=True)


def _xattn_fwd(q, kv, name, tq=512):
    S, D = q.shape
    M = kv.shape[0]
    hd = D // X_HEADS
    tq = min(tq, S)
    assert S % tq == 0, (S, tq)

    def body(q_ref, kv_ref, o_ref):
        for h in range(X_HEADS):
            cols = slice(h * hd, (h + 1) * hd)
            p = _xattn_probs(q_ref[:, cols], kv_ref[:, cols], hd)
            o_ref[:, cols] = _dot(p.astype(BF16), kv_ref[:, D + h * hd:D + (h + 1) * hd]).astype(BF16)

    row = BS((tq, D), lambda i: (i, 0))
    return pl.pallas_call(
        body, name=name, out_shape=SDS((S, D), BF16), grid=(S // tq,),
        in_specs=[row, BS((M, 2 * D), lambda i: (0, 0))], out_specs=row, compiler_params=_params(("parallel",), 40),
    )(q, kv)


def _xattn_bwd(q, kv, do, name, tq=512):
    S, D = q.shape
    M = kv.shape[0]
    hd = D // X_HEADS
    tq = min(tq, S)
    assert S % tq == 0, (S, tq)

    def body(q_ref, kv_ref, do_ref, dq_ref, dkv_ref):
        @pl.when(pl.program_id(0) == 0)
        def _():
            dkv_ref[...] = jnp.zeros_like(dkv_ref)

        for h in range(X_HEADS):
            cols = slice(h * hd, (h + 1) * hd)
            vcols = slice(D + h * hd, D + (h + 1) * hd)
            qh, kh, vh, doh = q_ref[:, cols], kv_ref[:, cols], kv_ref[:, vcols], do_ref[:, cols]
            p = _xattn_probs(qh, kh, hd)
            dp = _dot_nt(doh, vh)
            ds = (p * (dp - jnp.sum(p * dp, axis=-1, keepdims=True)) * (hd ** -0.5)).astype(BF16)
            dq_ref[:, cols] = _dot(ds, kh).astype(BF16)
            dkv_ref[:, cols] += _dot_tn(ds, qh)
            dkv_ref[:, vcols] += _dot_tn(p.astype(BF16), doh)

    row = BS((tq, D), lambda i: (i, 0))
    full = BS((M, 2 * D), lambda i: (0, 0))
    return pl.pallas_call(
        body, name=name, out_shape=[SDS((S, D), BF16), SDS((M, 2 * D), F32)], grid=(S // tq,),
        in_specs=[row, full, row], out_specs=[row, full], compiler_params=_params(("arbitrary",), 40),
    )(q, kv, do)


def _row_tile(rows, cap):
    best = 8
    for t in range(8, min(rows, cap) + 1, 8):
        if rows % t == 0:
            best = t
    assert rows % best == 0, (rows, cap)
    return best


ADAM_TILE_ELEMS = 256 * 1024


def _adamw(w, m, v, recv, name):
    R, C = w.shape
    slots = recv.shape[0]
    tr = _row_tile(R, max(8, ADAM_TILE_ELEMS // C))

    def body(w_ref, m_ref, v_ref, r_ref, g_ref, d_ref, nm_ref, nv_ref):
        g = r_ref[0].astype(F32)
        for s in range(1, slots):
            g = g + r_ref[s].astype(F32)
        mn = ADAM_B1 * m_ref[...] + (1.0 - ADAM_B1) * g
        vn = ADAM_B2 * v_ref[...] + (1.0 - ADAM_B2) * jnp.square(g)
        m_hat = mn / (1.0 - ADAM_B1 ** ADAM_STEP)
        v_hat = vn / (1.0 - ADAM_B2 ** ADAM_STEP)
        g_ref[...] = g
        d_ref[...] = -ADAM_LR * (m_hat / (jnp.sqrt(v_hat) + ADAM_EPS) + ADAM_WD * w_ref[...])
        nm_ref[...] = mn
        nv_ref[...] = vn

    row = BS((tr, C), lambda i: (i, 0))
    return pl.pallas_call(
        body, name=name, out_shape=[SDS((R, C), F32)] * 4, grid=(R // tr,),
        in_specs=[row, row, row, BS((slots, tr, C), lambda i: (0, i, 0))], out_specs=[row] * 4,
        compiler_params=_params(("parallel",), 40),
    )(w, m, v, recv)


def _cols_to_blocks(full):
    r, c = full.shape
    return full.reshape(r, N_DEV, c // N_DEV).transpose(1, 0, 2)


def _blocks_to_cols(blocks):
    n, r, c = blocks.shape
    return blocks.transpose(1, 0, 2).reshape(r, n * c)


def _pack(parts):
    flat = jnp.concatenate([p.reshape(-1).astype(F32) for p in parts])
    pad = (-flat.shape[0]) % (128 * 128)
    return jnp.pad(flat, (0, pad)).reshape(-1, 128)


def kernel(x, mem, g_ffn1, w1_gate, w1_up, w1_down, g_mix, w_in, g_v, w_s, b_s, sinks, g_a_out, g_b_out, w_out, g_x, g_mem, w_xq, w_xkv, w_xo, g_ffn2, w2_gate, w2_up, w2_down, g_final, loss_target, m_g_ffn1, m_w1_gate, m_w1_up, m_w1_down, m_g_mix, m_w_in, m_g_v, m_w_s, m_b_s, m_sinks, m_g_a_out, m_g_b_out, m_w_out, m_g_x, m_g_mem, m_w_xq, m_w_xkv, m_w_xo, m_g_ffn2, m_w2_gate, m_w2_up, m_w2_down, m_g_final, v_g_ffn1, v_w1_gate, v_w1_up, v_w1_down, v_g_mix, v_w_in, v_g_v, v_w_s, v_b_s, v_sinks, v_g_a_out, v_g_b_out, v_w_out, v_g_x, v_g_mem, v_w_xq, v_w_xkv, v_w_xo, v_g_ffn2, v_w2_gate, v_w2_up, v_w2_down, v_g_final):
    w = dict(g_ffn1=g_ffn1, w1_gate=w1_gate, w1_up=w1_up, w1_down=w1_down, g_mix=g_mix, w_in=w_in, g_v=g_v, w_s=w_s,
             b_s=b_s, sinks=sinks, g_a_out=g_a_out, g_b_out=g_b_out, w_out=w_out, g_x=g_x, g_mem=g_mem, w_xq=w_xq,
             w_xkv=w_xkv, w_xo=w_xo, g_ffn2=g_ffn2, w2_gate=w2_gate, w2_up=w2_up, w2_down=w2_down, g_final=g_final)
    mom = dict(g_ffn1=m_g_ffn1, w1_gate=m_w1_gate, w1_up=m_w1_up, w1_down=m_w1_down, g_mix=m_g_mix, w_in=m_w_in,
               g_v=m_g_v, w_s=m_w_s, b_s=m_b_s, sinks=m_sinks, g_a_out=m_g_a_out, g_b_out=m_g_b_out, w_out=m_w_out,
               g_x=m_g_x, g_mem=m_g_mem, w_xq=m_w_xq, w_xkv=m_w_xkv, w_xo=m_w_xo, g_ffn2=m_g_ffn2,
               w2_gate=m_w2_gate, w2_up=m_w2_up, w2_down=m_w2_down, g_final=m_g_final)
    var = dict(g_ffn1=v_g_ffn1, w1_gate=v_w1_gate, w1_up=v_w1_up, w1_down=v_w1_down, g_mix=v_g_mix, w_in=v_w_in,
               g_v=v_g_v, w_s=v_w_s, b_s=v_b_s, sinks=v_sinks, g_a_out=v_g_a_out, g_b_out=v_g_b_out, w_out=v_w_out,
               g_x=v_g_x, g_mem=v_g_mem, w_xq=v_w_xq, w_xkv=v_w_xkv, w_xo=v_w_xo, g_ffn2=v_g_ffn2,
               w2_gate=v_w2_gate, w2_up=v_w2_up, w2_down=v_w2_down, g_final=v_g_final)

    xs, ms, tgt = x[0], mem[0], loss_target[0]
    D = xs.shape[1]
    d_a = w_s.shape[1] * CHUNK
    d_b = D - d_a
    kvw = 2 * KV_HEADS * HEAD_DIM

    shard = {k: w[k][0].astype(BF16) for k in BIG}
    part = _run_exchange(_gather_exchange([shard[k] for k in FFN1_W], CHIP_PEERS), "ag_ffn1_ici")
    wg1, wu1, wd1 = _run_exchange(_sibling_exchange(part), "ag_ffn1_d2d")
    gf = g_final.reshape(1, D)
    ws, ws_t, bs_t = w_s[0], jnp.swapaxes(w_s[0], 1, 2), b_s[0].T
    sink_rows = jnp.repeat(sinks.reshape(-1), CHUNK).reshape(-1, 1)

    (h1, n1, a1, b1), part = _ffn_fwd(xs, g_ffn1, wg1, wu1, wd1, "ffn1_fwd",
                                      exchange=_gather_exchange([shard[k] for k in REST_W], CHIP_PEERS))
    part = dict(zip(REST_W, part))
    n2, (win_blocks,) = _rms_fwd(h1, g_mix, "mix_norm", exchange=_sibling_exchange([part["w_in"]]))
    win = _blocks_to_cols(win_blocks)
    z = _matmul(n2, win, "nn", F32, "mm_in", tn=win.shape[1] // 2)
    ya_n = _sgu_fwd(z, g_v, ws, bs_t, g_a_out, "sgu_fwd")
    later = [k for k in REST_W if k != "w_in"]
    (yb, yb_n), handed = _swa_fwd(z, sink_rows, g_b_out, "swa_fwd",
                                  exchange=_sibling_exchange([part[k] for k in later]))
    gathered = dict(zip(later, handed))
    wg2, wu2, wd2 = gathered["w2_gate"], gathered["w2_up"], gathered["w2_down"]
    wxkv = _blocks_to_cols(gathered["w_xkv"])
    wout = gathered["w_out"].reshape(D, D)
    wxq = gathered["w_xq"].reshape(D, D)
    wxo = gathered["w_xo"].reshape(D, D)
    y = jnp.concatenate([ya_n, yb_n], axis=1)
    h2 = _matmul(y, wout, "nn", F32, "mm_out", res=h1, tn=D)
    hx, _ = _rms_fwd(h2, g_x, "x_norm")
    mn, _ = _rms_fwd(ms, g_mem, "mem_norm")
    q = _matmul(hx, wxq, "nn", BF16, "mm_xq", tn=D)
    kv = _matmul(mn, wxkv, "nn", BF16, "mm_xkv")
    o = _xattn_fwd(q, kv, "xattn_fwd")
    h3 = _matmul(o, wxo, "nn", F32, "mm_xo", res=h2, tn=D)
    (h4, n4, a2, b2), _ = _ffn_fwd(h3, g_ffn2, wg2, wu2, wd2, "ffn2_fwd")
    dh4, dg_final, loss_part = _final_loss(h4, gf, tgt, "final_loss")

    grad_big, grad_small, recv_big = {}, {"g_final": dg_final}, {}
    order = _dw_block_order()
    (dn4, df2, da2, db2, s2), _ = _ffn_bwd_dx(dh4, a2, b2, wg2, wu2, wd2, "ffn2_bwd_dx")
    dh3, grad_small["g_ffn2"] = _rms_bwd(h3, g_ffn2, dn4, dh4, "ffn2_norm_bwd")
    (recv_big["w2_gate"], recv_big["w2_up"]), _ = _ffn_bwd_dw([(n4, da2), (n4, db2)], order, "ffn2_bwd_dw_gu")
    (recv_big["w2_down"],), _ = _ffn_bwd_dw([(s2, df2)], order, "ffn2_bwd_dw_d")

    do = _matmul(dh3, wxo, "nt", BF16, "mm_xo_dx", tn=D)
    grad_big["w_xo"] = _matmul(o, dh3, "tn", BF16, "mm_xo_dw", tm=1024, tn=1024).reshape(N_DEV, D // N_DEV, D)
    dq, dkv = _xattn_bwd(q, kv, do, "xattn_bwd")
    dhx = _matmul(dq, wxq, "nt", F32, "mm_xq_dx", tn=D)
    grad_big["w_xq"] = _matmul(hx, dq, "tn", BF16, "mm_xq_dw", tm=1024, tn=1024).reshape(N_DEV, D // N_DEV, D)
    dmn = _matmul(dkv, wxkv, "nt", F32, "mm_xkv_dx")
    grad_big["w_xkv"] = _cols_to_blocks(_matmul(mn, dkv, "tn", BF16, "mm_xkv_dw", tm=1024, tn=1024))
    dh2, grad_small["g_x"] = _rms_bwd(h2, g_x, dhx, dh3, "x_norm_bwd")
    _, grad_small["g_mem"] = _rms_bwd(ms, g_mem, dmn, None, "mem_norm_bwd")

    dy = _matmul(dh2, wout, "nt", F32, "mm_out_dx", tn=D)
    grad_big["w_out"] = _matmul(y, dh2, "tn", BF16, "mm_out_dw", tm=1024, tn=1024).reshape(N_DEV, D // N_DEV, D)
    dz_uv, grad_small["w_s"], dbs_t, grad_small["g_v"], grad_small["g_a_out"] = _sgu_bwd(
        z, dy, g_v, ws, ws_t, bs_t, g_a_out, "sgu_bwd")
    grad_small["b_s"] = dbs_t.T
    dq_b, dkv_b, grad_small["sinks"], grad_small["g_b_out"] = _swa_bwd(z, yb, dy, sink_rows, g_b_out, "swa_bwd")
    dz = jnp.concatenate([dz_uv, dq_b, dkv_b], axis=1)
    dn2 = _matmul(dz, win, "nt", F32, "mm_in_dx", tn=1024)
    grad_big["w_in"] = _cols_to_blocks(_matmul(n2, dz, "tn", BF16, "mm_in_dw", tm=1024, tn=dz.shape[1] // 2))
    dh1, grad_small["g_mix"] = _rms_bwd(h1, g_mix, dn2, dh2, "mix_norm_bwd")

    (dn1, df1, da1, db1, s1), recv_mid = _ffn_bwd_dx(
        dh1, a1, b1, wg1, wu1, wd1, "ffn1_bwd_dx", exchange=_scatter_exchange([grad_big[k] for k in MID_W]))
    recv_big.update(zip(MID_W, recv_mid))
    dx, grad_small["g_ffn1"] = _rms_bwd(xs, g_ffn1, dn1, dh1, "ffn1_norm_bwd")
    (recv_big["w1_gate"], recv_big["w1_up"]), (recv_small,) = _ffn_bwd_dw(
        [(n1, da1), (n1, db1)], order, "ffn1_bwd_dw_gu",
        exchange=_gather_exchange([_pack([grad_small[k] for k in SMALL])], ALL_PEERS))
    (recv_big["w1_down"],), _ = _ffn_bwd_dw([(s1, df1)], order, "ffn1_bwd_dw_d")


    grads, deltas, new_m, new_v = {}, {}, {}, {}
    for k in BIG:
        shp = w[k].shape
        two_d = shp[1:]
        outs = _adamw(w[k].reshape(two_d), mom[k].reshape(two_d), var[k].reshape(two_d), recv_big[k], "adamw_" + k)
        grads[k], deltas[k], new_m[k], new_v[k] = [t.reshape(shp) for t in outs]
    packed = _adamw(_pack([w[k] for k in SMALL]), _pack([mom[k] for k in SMALL]), _pack([var[k] for k in SMALL]),
                    recv_small, "adamw_small")
    off = 0
    for k in SMALL:
        shp = w[k].shape
        size = 1
        for s in shp:
            size *= s
        for dst, src in zip((grads, deltas, new_m, new_v), packed):
            dst[k] = src.reshape(-1)[off:off + size].reshape(shp)
        off += size

    loss = lax.psum(loss_part[0, 0], AXES)
    return (loss, dx[None], *[grads[k] for k in WEIGHTS], *[deltas[k] for k in WEIGHTS],
            *[new_m[k] for k in WEIGHTS], *[new_v[k] for k in WEIGHTS])
```

```python
import jax
import jax.numpy as jnp
from jax import lax
from jax.experimental import pallas as pl
from jax.experimental.pallas import tpu as pltpu

F32 = jnp.float32
BF16 = jnp.bfloat16
SDS = jax.ShapeDtypeStruct
BS = pl.BlockSpec

N_DEV = 8
AXES = ("x", "y", "c")
EPS = 1e-5
CHUNK = 128
HEAD_DIM = 64
KV_HEADS = 2
X_HEADS = 4
NEG = -1e30
ADAM_LR = 0.001
ADAM_B1 = 0.9
ADAM_B2 = 0.999
ADAM_EPS = 1e-08
ADAM_WD = 0.01
ADAM_STEP = 10
MIB = 1 << 20
FFN_SUB_ROWS = 256
WEIGHTS = ['g_ffn1', 'w1_gate', 'w1_up', 'w1_down', 'g_mix', 'w_in', 'g_v', 'w_s', 'b_s', 'sinks', 'g_a_out',
           'g_b_out', 'w_out', 'g_x', 'g_mem', 'w_xq', 'w_xkv', 'w_xo', 'g_ffn2', 'w2_gate', 'w2_up', 'w2_down',
           'g_final']
BIG = ['w1_gate', 'w1_up', 'w1_down', 'w_in', 'w_out', 'w_xq', 'w_xkv', 'w_xo', 'w2_gate', 'w2_up', 'w2_down']
SMALL = [w for w in WEIGHTS if w not in BIG]
FFN1_W = ['w1_gate', 'w1_up', 'w1_down']
FFN2_W = ['w2_gate', 'w2_up', 'w2_down']
MID_W = ['w_in', 'w_out', 'w_xq', 'w_xkv', 'w_xo']
REST_W = MID_W + FFN2_W


def _params(sem=None, vmem_mib=48):
    return pltpu.CompilerParams(dimension_semantics=sem, vmem_limit_bytes=vmem_mib * MIB)


def _dot(a, b):
    return jnp.dot(a, b, preferred_element_type=F32)


def _dot_nt(a, b):
    return lax.dot_general(a, b, (((1,), (1,)), ((), ())), preferred_element_type=F32)


def _dot_tn(a, b):
    return lax.dot_general(a, b, (((0,), (0,)), ((), ())), preferred_element_type=F32)


def _rstd(v):
    return lax.rsqrt(jnp.mean(v * v, axis=-1, keepdims=True) + EPS)


def _norm_bwd(xhat, r, g, d):
    t = d * g
    return r * (t - xhat * jnp.mean(t * xhat, axis=-1, keepdims=True))


def _gelu(v):
    return 0.5 * v * (1.0 + lax.erf(v * 0.7071067811865476))


def _gelu_grad(v):
    return 0.5 * (1.0 + lax.erf(v * 0.7071067811865476)) + v * jnp.exp(-0.5 * v * v) * 0.3989422804014327


def _mesh_pos():
    x, y, c = lax.axis_index("x"), lax.axis_index("y"), lax.axis_index("c")
    return x, y, c, 4 * x + 2 * y + c


def _peer(x, y, c, k):
    px = 1 - x if k & 4 else x
    py = 1 - y if k & 2 else y
    pc = 1 - c if k & 1 else c
    return (px, py, pc), 4 * px + 2 * py + pc


ANY = BS(memory_space=pl.ANY)
DMA_SEM = pltpu.SemaphoreType.DMA
ALL_PEERS = (1, 2, 3, 4, 5, 6, 7)
CHIP_PEERS = (2, 4, 6)
SIBLING = 1


def _remote(src, dst, send, recv, peer):
    return pltpu.make_async_remote_copy(src_ref=src, dst_ref=dst, send_sem=send, recv_sem=recv, device_id=peer,
                                        device_id_type=pl.DeviceIdType.MESH)


class _Exchange:
    def __init__(self, ins, out_shapes, sems, copies, aliases=None):
        self.ins, self.out_shapes, self.sems, self.copies, self.aliases = ins, out_shapes, sems, copies, aliases or {}


def _gather_exchange(shards, peers):
    n, m = len(shards), len(peers)

    def copies(ins, outs, sems):
        send, recv, lsem = sems
        x, y, c, me = _mesh_pos()
        cps = [pltpu.make_async_copy(ins[w], outs[w].at[me], lsem.at[w]) for w in range(n)]
        for w in range(n):
            for j, k in enumerate(peers):
                cps.append(_remote(ins[w], outs[w].at[me], send.at[w, j], recv.at[w, j], _peer(x, y, c, k)[0]))
        return cps

    return _Exchange(shards, [SDS((N_DEV,) + s.shape, s.dtype) for s in shards],
                     [DMA_SEM((n, m)), DMA_SEM((n, m)), DMA_SEM((n,))], copies)


def _sibling_exchange(gathered):
    n = len(gathered)

    def copies(ins, outs, sems):
        send, recv = sems
        x, y, c, me = _mesh_pos()
        slots = [me] + [_peer(x, y, c, k)[1] for k in CHIP_PEERS]
        sib = _peer(x, y, c, SIBLING)[0]
        return [_remote(outs[w].at[s], outs[w].at[s], send.at[w, j], recv.at[w, j], sib)
                for w in range(n) for j, s in enumerate(slots)]

    return _Exchange(gathered, [SDS(g.shape, g.dtype) for g in gathered], [DMA_SEM((n, 4)), DMA_SEM((n, 4))], copies,
                     aliases={w: w for w in range(n)})


def _scatter_exchange(fulls):
    n, m = len(fulls), len(ALL_PEERS)

    def copies(ins, outs, sems):
        send, recv, lsem = sems
        x, y, c, me = _mesh_pos()
        cps = [pltpu.make_async_copy(ins[w].at[me], outs[w].at[me], lsem.at[w]) for w in range(n)]
        for w in range(n):
            for j, k in enumerate(ALL_PEERS):
                peer, p = _peer(x, y, c, k)
                cps.append(_remote(ins[w].at[p], outs[w].at[me], send.at[w, j], recv.at[w, j], peer))
        return cps

    return _Exchange(fulls, [SDS(f.shape, f.dtype) for f in fulls],
                     [DMA_SEM((n, m)), DMA_SEM((n, m)), DMA_SEM((n,))], copies)


def _call(body, *, name, args, in_specs, out_shape, out_specs, grid, sem, vmem_mib, scratch=(), exchange=None,
          first=None, last=None):
    if exchange is None:
        return pl.pallas_call(body, name=name, out_shape=out_shape, grid=grid, in_specs=in_specs,
                              out_specs=out_specs, scratch_shapes=list(scratch),
                              compiler_params=_params(sem, vmem_mib))(*args), []
    ni, no, ns = len(args), len(out_shape), len(scratch)
    xi, xo = len(exchange.ins), len(exchange.out_shapes)

    def hosted(*refs):
        own_in, refs = refs[:ni], refs[ni:]
        x_in, refs = refs[:xi], refs[xi:]
        own_out, refs = refs[:no], refs[no:]
        x_out, refs = refs[:xo], refs[xo:]
        own_scr, x_sem = refs[:ns], refs[ns:]

        @pl.when(first())
        def _():
            for cp in exchange.copies(x_in, x_out, x_sem):
                cp.start()

        body(*own_in, *own_out, *own_scr)

        @pl.when(last())
        def _():
            for cp in exchange.copies(x_in, x_out, x_sem):
                cp.wait()

    outs = pl.pallas_call(
        hosted, name=name, out_shape=list(out_shape) + list(exchange.out_shapes), grid=grid,
        in_specs=list(in_specs) + [ANY] * xi, out_specs=list(out_specs) + [ANY] * xo,
        scratch_shapes=list(scratch) + list(exchange.sems),
        input_output_aliases={ni + a: no + b for a, b in exchange.aliases.items()},
        compiler_params=pltpu.CompilerParams(dimension_semantics=tuple("arbitrary" for _ in grid),
                                             vmem_limit_bytes=vmem_mib * MIB, has_side_effects=True),
    )(*args, *exchange.ins)
    return outs[:no], outs[no:]


def _run_exchange(exchange, name):
    xi, xo = len(exchange.ins), len(exchange.out_shapes)

    def body(*refs):
        cps = exchange.copies(refs[:xi], refs[xi:xi + xo], refs[xi + xo:])
        for cp in cps:
            cp.start()
        for cp in cps:
            cp.wait()

    return pl.pallas_call(
        body, name=name, out_shape=list(exchange.out_shapes), in_specs=[ANY] * xi, out_specs=[ANY] * xo,
        scratch_shapes=list(exchange.sems), input_output_aliases=dict(exchange.aliases),
        compiler_params=pltpu.CompilerParams(has_side_effects=True),
    )(*exchange.ins)


def _rms_fwd(h, g, name, tm=512, exchange=None):
    S, D = h.shape
    tm = min(tm, S)
    assert S % tm == 0, (S, tm)
    ni = S // tm

    def body(h_ref, g_ref, o_ref):
        hv = h_ref[...]
        o_ref[...] = (hv * _rstd(hv) * g_ref[...]).astype(o_ref.dtype)

    (out,), extra = _call(
        body, name=name, args=(h, g), out_shape=[SDS((S, D), BF16)], grid=(ni,),
        in_specs=[BS((tm, D), lambda i: (i, 0)), BS((1, D), lambda i: (0, 0))],
        out_specs=[BS((tm, D), lambda i: (i, 0))], sem=("parallel",), vmem_mib=32, exchange=exchange,
        first=lambda: pl.program_id(0) == 0, last=lambda: pl.program_id(0) == ni - 1)
    return out, extra


def _rms_bwd(h, g, dn, dres, name, tm=256):
    S, D = h.shape
    tm = min(tm, S)
    assert S % tm == 0, (S, tm)
    has_res = dres is not None

    def body(*refs):
        if has_res:
            h_ref, g_ref, dn_ref, dres_ref, dh_ref, dg_ref = refs
        else:
            h_ref, g_ref, dn_ref, dh_ref, dg_ref = refs
        hv = h_ref[...]
        r = _rstd(hv)
        xh = hv * r
        d = dn_ref[...].astype(F32)

        @pl.when(pl.program_id(0) == 0)
        def _():
            dg_ref[...] = jnp.zeros_like(dg_ref)

        dg_ref[...] += jnp.sum(d * xh, axis=0, keepdims=True)
        dh = _norm_bwd(xh, r, g_ref[...], d)
        dh_ref[...] = dres_ref[...] + dh if has_res else dh

    row = BS((tm, D), lambda i: (i, 0))
    vec = BS((1, D), lambda i: (0, 0))
    args = (h, g, dn) + ((dres,) if has_res else ())
    return pl.pallas_call(
        body, name=name, out_shape=[SDS((S, D), F32), SDS((1, D), F32)], grid=(S // tm,),
        in_specs=[row, vec, row] + ([row] if has_res else []), out_specs=[row, vec],
        compiler_params=_params(("arbitrary",), 40),
    )(*args)


def _final_loss(h, g, target, name, tm=256):
    S, D = h.shape
    tm = min(tm, S)
    assert S % tm == 0, (S, tm)

    def body(h_ref, g_ref, t_ref, dh_ref, dg_ref, loss_ref):
        hv = h_ref[...]
        r = _rstd(hv)
        xh = hv * r
        gv = g_ref[...]
        e = xh * gv - t_ref[...]

        @pl.when(pl.program_id(0) == 0)
        def _():
            dg_ref[...] = jnp.zeros_like(dg_ref)
            loss_ref[...] = jnp.zeros_like(loss_ref)

        loss_ref[...] += 0.5 * jnp.sum(jnp.mean(e * e, axis=-1, keepdims=True), axis=0, keepdims=True)
        dy = e * (1.0 / D)
        dg_ref[...] += jnp.sum(dy * xh, axis=0, keepdims=True)
        dh_ref[...] = _norm_bwd(xh, r, gv, dy)

    row = BS((tm, D), lambda i: (i, 0))
    vec = BS((1, D), lambda i: (0, 0))
    return pl.pallas_call(
        body, name=name, out_shape=[SDS((S, D), F32), SDS((1, D), F32), SDS((1, 128), F32)], grid=(S // tm,),
        in_specs=[row, vec, row], out_specs=[row, vec, BS((1, 128), lambda i: (0, 0))],
        compiler_params=_params(("arbitrary",), 40),
    )(h, g, target)


def _matmul(a, b, mode, out_dtype, name, res=None, tm=512, tn=512, tk=512):
    if mode == "tn":
        K, M = a.shape
        N = b.shape[1]
        tm, tn, tk = min(tm, M), min(tn, N), min(tk, K)
        assert M % tm == 0 and N % tn == 0 and K % tk == 0, (a.shape, b.shape, tm, tn, tk)
        nk = K // tk

        def body(a_ref, b_ref, o_ref, acc_ref):
            k = pl.program_id(2)

            @pl.when(k == 0)
            def _():
                acc_ref[...] = jnp.zeros_like(acc_ref)

            acc_ref[...] += _dot_tn(a_ref[...].astype(BF16), b_ref[...].astype(BF16))

            @pl.when(k == nk - 1)
            def _():
                o_ref[...] = acc_ref[...].astype(o_ref.dtype)

        return pl.pallas_call(
            body, name=name, out_shape=SDS((M, N), out_dtype), grid=(M // tm, N // tn, nk),
            in_specs=[BS((tk, tm), lambda i, j, k: (k, i)), BS((tk, tn), lambda i, j, k: (k, j))],
            out_specs=BS((tm, tn), lambda i, j, k: (i, j)), scratch_shapes=[pltpu.VMEM((tm, tn), F32)],
            compiler_params=_params(("parallel", "parallel", "arbitrary"), 48),
        )(a, b)

    M, K = a.shape
    N = b.shape[1] if mode == "nn" else b.shape[0]
    tm, tn = min(tm, M), min(tn, N)
    assert M % tm == 0 and N % tn == 0, (a.shape, b.shape, tm, tn)
    has_res = res is not None

    def body(*refs):
        if has_res:
            a_ref, b_ref, r_ref, o_ref = refs
        else:
            a_ref, b_ref, o_ref = refs
        av, bv = a_ref[...].astype(BF16), b_ref[...].astype(BF16)
        acc = _dot(av, bv) if mode == "nn" else _dot_nt(av, bv)
        if has_res:
            acc = acc + r_ref[...]
        o_ref[...] = acc.astype(o_ref.dtype)

    b_spec = BS((K, tn), lambda i, j: (0, j)) if mode == "nn" else BS((tn, K), lambda i, j: (j, 0))
    o_spec = BS((tm, tn), lambda i, j: (i, j))
    return pl.pallas_call(
        body, name=name, out_shape=SDS((M, N), out_dtype), grid=(M // tm, N // tn),
        in_specs=[BS((tm, K), lambda i, j: (i, 0)), b_spec] + ([o_spec] if has_res else []), out_specs=o_spec,
        compiler_params=_params(("parallel", "parallel"), 48),
    )(*((a, b) + ((res,) if has_res else ())))


def _ffn_fwd(h, g, wg, wu, wd, name, tm=512, exchange=None):
    S, D = h.shape
    nb, _, Fs = wg.shape
    tm = min(tm, S)
    sub = min(FFN_SUB_ROWS, tm)
    assert S % tm == 0 and tm % sub == 0, (S, tm, sub)

    def body(h_ref, g_ref, wg_ref, wu_ref, wd_ref, o_ref, n_ref, a_ref, b_ref):
        j = pl.program_id(1)

        @pl.when(j == 0)
        def _():
            hv = h_ref[...]
            n_ref[...] = (hv * _rstd(hv) * g_ref[...]).astype(BF16)
            o_ref[...] = jnp.zeros_like(o_ref)

        for r in range(0, tm, sub):
            rows = slice(r, r + sub)
            n = n_ref[rows, :]
            a = _dot(n, wg_ref[...]).astype(BF16)
            b = _dot(n, wu_ref[...]).astype(BF16)
            a_ref[rows, :] = a
            b_ref[rows, :] = b
            a, b = a.astype(F32), b.astype(F32)
            s = (a * jax.nn.sigmoid(a) * b).astype(BF16)
            o_ref[rows, :] += _dot(s, wd_ref[...])

        @pl.when(j == nb - 1)
        def _():
            o_ref[...] = h_ref[...] + 0.5 * o_ref[...]

    row = BS((tm, D), lambda i, j: (i, 0))
    wcol = BS((None, D, Fs), lambda i, j: (j, 0, 0))
    act = BS((None, tm, Fs), lambda i, j: (j, i, 0))
    ni = S // tm
    return _call(
        body, name=name, args=(h, g, wg, wu, wd),
        out_shape=[SDS((S, D), F32), SDS((S, D), BF16), SDS((nb, S, Fs), BF16), SDS((nb, S, Fs), BF16)],
        grid=(ni, nb),
        in_specs=[row, BS((1, D), lambda i, j: (0, 0)), wcol, wcol, BS((None, Fs, D), lambda i, j: (j, 0, 0))],
        out_specs=[row, row, act, act], sem=("parallel", "arbitrary"), vmem_mib=56, exchange=exchange,
        first=lambda: (pl.program_id(0) == 0) & (pl.program_id(1) == 0),
        last=lambda: (pl.program_id(0) == ni - 1) & (pl.program_id(1) == nb - 1))


def _ffn_bwd_dx(dh, a, b, wg, wu, wd, name, tm=512, exchange=None):
    S, D = dh.shape
    nb, _, Fs = wg.shape
    tm = min(tm, S)
    assert S % tm == 0, (S, tm)

    def body(dh_ref, a_ref, b_ref, wg_ref, wu_ref, wd_ref, dn_ref, df_ref, da_ref, db_ref, s_ref, ds_ref):
        j = pl.program_id(1)

        def first_matmul():
            ds_ref[...] = _dot_nt(df_ref[...], wd_ref[...])

        def pre_activation_cotangents():
            ds = ds_ref[...]
            av, bv = a_ref[...].astype(F32), b_ref[...].astype(F32)
            sig = jax.nn.sigmoid(av)
            sl = av * sig
            da = (ds * bv * (sig * (1.0 + av * (1.0 - sig)))).astype(BF16)
            db = (ds * sl).astype(BF16)
            da_ref[...] = da
            db_ref[...] = db
            s_ref[...] = (sl * bv).astype(BF16)
            return da, db

        def last_matmuls(da, db):
            dn_ref[...] += _dot_nt(da, wg_ref[...]) + _dot_nt(db, wu_ref[...])

        @pl.when(j == 0)
        def _():
            df_ref[...] = (0.5 * dh_ref[...]).astype(BF16)
            dn_ref[...] = jnp.zeros_like(dn_ref)
            first_matmul()

        @pl.when((j > 0) & (j < nb))
        def _():
            da, db = pre_activation_cotangents()
            first_matmul()
            last_matmuls(da, db)

        @pl.when(j == nb)
        def _():
            last_matmuls(*pre_activation_cotangents())

    row = BS((tm, D), lambda i, j: (i, 0))
    prev = BS((None, D, Fs), lambda i, j: (jnp.maximum(j - 1, 0), 0, 0))
    act = BS((None, tm, Fs), lambda i, j: (jnp.maximum(j - 1, 0), i, 0))
    ni = S // tm
    return _call(
        body, name=name, args=(dh, a, b, wg, wu, wd),
        out_shape=[SDS((S, D), F32), SDS((S, D), BF16)] + [SDS((nb, S, Fs), BF16)] * 3,
        grid=(ni, nb + 1),
        in_specs=[row, act, act, prev, prev, BS((None, Fs, D), lambda i, j: (jnp.minimum(j, nb - 1), 0, 0))],
        out_specs=[row, row, act, act, act], scratch=[pltpu.VMEM((tm, Fs), F32)],
        sem=("parallel", "arbitrary"), vmem_mib=56, exchange=exchange,
        first=lambda: (pl.program_id(0) == 0) & (pl.program_id(1) == 0),
        last=lambda: (pl.program_id(0) == ni - 1) & (pl.program_id(1) == nb))


DW_FLIPS = {0: (6, 2, 4, 0), 1: (6, 4, 2, 0)}
N_CHIPS = N_DEV // 2


def _dw_block_order():
    x, y, c, me = _mesh_pos()
    steps = [jnp.array([v for mine, sib in zip(DW_FLIPS[core], DW_FLIPS[1 - core]) for v in (sib ^ 1, mine)], jnp.int32)
             for core in (0, 1)]
    return jnp.bitwise_xor(me.astype(jnp.int32), jnp.where(c == 0, steps[0], steps[1]))


def _ffn_bwd_dw(pairs, order, name, tk=512, exchange=None):
    npair = len(pairs)
    S = pairs[0][0].shape[-2]
    nb, half = N_DEV, N_DEV // 2
    tk = min(tk, S)
    assert S % tk == 0, (S, tk)
    nk = S // tk
    shapes = [(lhs.shape[-1], rhs.shape[-1]) for lhs, rhs in pairs]
    xi = len(exchange.ins) if exchange else 0
    xo = len(exchange.out_shapes) if exchange else 0

    def body(order_ref, *rest):
        tiles, rest = rest[:2 * npair], rest[2 * npair:]
        x_in, rest = rest[:xi], rest[xi:]
        recv_bufs, rest = rest[:npair], rest[npair:]
        x_out, rest = rest[:xo], rest[xo:]
        accs, rest = rest[:npair], rest[npair:]
        out_t, rest = rest[:npair], rest[npair:]
        out_m, rest = rest[:npair], rest[npair:]
        land, rest = rest[:npair], rest[npair:]
        (send_t, recv_t, send_m, recv_m, lsem, credit), x_sem = rest[:6], rest[6:]
        t, k = pl.program_id(0), pl.program_id(1)
        x, y, c, me = _mesh_pos()
        sibling = (x, y, 1 - c)
        chip = 2 * x + y

        if exchange:
            @pl.when((t == 0) & (k == 0))
            def _():
                for cp in exchange.copies(x_in, x_out, x_sem):
                    cp.start()

        @pl.when(k == 0)
        def _():
            for acc in accs:
                acc[...] = jnp.zeros_like(acc)

        for w in range(npair):
            accs[w][...] += _dot_tn(tiles[2 * w][...], tiles[2 * w + 1][...])

        def to_sibling(w, i):
            return _remote(out_t[w], land[w], send_t.at[w, i], recv_t.at[w, i], sibling)

        def to_owner(w, i):
            dst = recv_bufs[w].at[chip]
            if i == half - 1:
                return pltpu.make_async_copy(out_m[w], dst, lsem.at[w])
            p = jnp.bitwise_xor(me, jnp.where(c == 0, DW_FLIPS[0][i], DW_FLIPS[1][i]))
            return _remote(out_m[w], dst, send_m.at[w, i], recv_m.at[w, i], (p >> 2, (p >> 1) & 1, p & 1))

        for i in range(half):
            @pl.when((t == 2 * i) & (k == nk - 1))
            def _(i=i):
                if i >= 1:
                    pl.semaphore_wait(credit, 1)
                for w in range(npair):
                    if i >= 1:
                        to_sibling(w, i - 1).wait_send()
                    out_t[w][...] = accs[w][...].astype(BF16)
                    to_sibling(w, i).start()

            @pl.when((t == 2 * i + 1) & (k == nk - 1))
            def _(i=i):
                for w in range(npair):
                    to_sibling(w, i).wait_recv()
                    if i >= 1:
                        to_owner(w, i - 1).wait_send()
                    out_m[w][...] = (accs[w][...] + land[w][...].astype(F32)).astype(BF16)
                if i < half - 1:
                    pl.semaphore_signal(credit, inc=1, device_id=sibling, device_id_type=pl.DeviceIdType.MESH)
                for w in range(npair):
                    to_owner(w, i).start()

        @pl.when((t == nb - 1) & (k == nk - 1))
        def _():
            for w in range(npair):
                to_sibling(w, half - 1).wait_send()
                to_owner(w, half - 1).wait()
                for i in range(half - 1):
                    to_owner(w, i).wait_recv()
            if exchange:
                for cp in exchange.copies(x_in, x_out, x_sem):
                    cp.wait()

    def tile_spec(arr):
        if arr.ndim == 3:
            return BS((None, tk, arr.shape[-1]), lambda t, k, o: (o[t], k, 0))
        return BS((tk, arr.shape[-1]), lambda t, k, o: (k, 0))

    flat = [a for pair in pairs for a in pair]
    grid_spec = pltpu.PrefetchScalarGridSpec(
        num_scalar_prefetch=1, grid=(nb, nk),
        in_specs=[tile_spec(a) for a in flat] + [ANY] * xi, out_specs=[ANY] * (npair + xo),
        scratch_shapes=[pltpu.VMEM(s, F32) for s in shapes] + [pltpu.VMEM(s, BF16) for s in shapes] * 3
        + [DMA_SEM((npair, half)), DMA_SEM((npair, half)), DMA_SEM((npair, half - 1)), DMA_SEM((npair, half - 1)),
           DMA_SEM((npair,)), pltpu.SemaphoreType.REGULAR]
        + (list(exchange.sems) if exchange else []))
    outs = pl.pallas_call(
        body, name=name, grid_spec=grid_spec,
        out_shape=[SDS((N_CHIPS,) + s, BF16) for s in shapes] + (list(exchange.out_shapes) if exchange else []),
        compiler_params=pltpu.CompilerParams(dimension_semantics=("arbitrary", "arbitrary"),
                                             vmem_limit_bytes=54 * MIB, has_side_effects=True),
    )(order, *flat, *(exchange.ins if exchange else ()))
    return outs[:npair], outs[npair:]


def _sgu_parts(z, gv, ws_ref, bst_ref, groups):
    da = z.shape[1] // 2
    zu, zv = z[:, :da], z[:, da:]
    u, v = _gelu(zu), _gelu(zv)
    rv = _rstd(v)
    vhat = v * rv
    vn = (vhat * gv).astype(BF16)
    tri = lax.broadcasted_iota(jnp.int32, (CHUNK, CHUNK), 0) >= lax.broadcasted_iota(jnp.int32, (CHUNK, CHUNK), 1)
    pieces = []
    for g in range(groups):
        w = jnp.where(tri, ws_ref[g], 0.0).astype(BF16)
        pieces.append(_dot(w, vn[:, g * CHUNK:(g + 1) * CHUNK]) + bst_ref[:, g:g + 1])
    sv = jnp.concatenate(pieces, axis=1)
    return dict(zu=zu, zv=zv, u=u, rv=rv, vhat=vhat, vn=vn, sv=sv, tri=tri)


def _sgu_fwd(z, g_v, w_s, b_st, g_a, name):
    S = z.shape[0]
    groups = w_s.shape[0]
    da = groups * CHUNK

    def body(z_ref, gv_ref, ws_ref, bst_ref, ga_ref, o_ref):
        p = _sgu_parts(z_ref[...], gv_ref[...], ws_ref, bst_ref, groups)
        ya = p["u"] * p["sv"]
        o_ref[...] = (ya * _rstd(ya) * ga_ref[...]).astype(BF16)

    vec = BS((1, da), lambda i: (0, 0))
    return pl.pallas_call(
        body, name=name, out_shape=SDS((S, da), BF16), grid=(S // CHUNK,),
        in_specs=[BS((CHUNK, 2 * da), lambda i: (i, 0)), vec, BS((groups, CHUNK, CHUNK), lambda i: (0, 0, 0)),
                  BS((CHUNK, groups), lambda i: (0, 0)), vec],
        out_specs=BS((CHUNK, da), lambda i: (i, 0)), compiler_params=_params(("parallel",), 32),
    )(z, g_v, w_s, b_st, g_a)


def _sgu_bwd(z, dy, g_v, w_s, w_st, b_st, g_a, name):
    S = z.shape[0]
    groups = w_s.shape[0]
    da = groups * CHUNK

    def body(z_ref, dy_ref, gv_ref, ws_ref, wst_ref, bst_ref, ga_ref, dz_ref, dws_ref, dbst_ref, dgv_ref, dga_ref):
        @pl.when(pl.program_id(0) == 0)
        def _():
            dws_ref[...] = jnp.zeros_like(dws_ref)
            dbst_ref[...] = jnp.zeros_like(dbst_ref)
            dgv_ref[...] = jnp.zeros_like(dgv_ref)
            dga_ref[...] = jnp.zeros_like(dga_ref)

        gv = gv_ref[...]
        p = _sgu_parts(z_ref[...], gv, ws_ref, bst_ref, groups)
        u, sv, tri = p["u"], p["sv"], p["tri"]
        ya = u * sv
        ra = _rstd(ya)
        yhat = ya * ra
        d = dy_ref[...]
        dga_ref[...] += jnp.sum(d * yhat, axis=0, keepdims=True)
        dya = _norm_bwd(yhat, ra, ga_ref[...], d)
        du = dya * sv
        dsv = dya * u
        dsv_b = dsv.astype(BF16)
        tri_t = (lax.broadcasted_iota(jnp.int32, (CHUNK, CHUNK), 0)
                 <= lax.broadcasted_iota(jnp.int32, (CHUNK, CHUNK), 1))
        lane = lax.broadcasted_iota(jnp.int32, (CHUNK, groups), 1)
        dvn = []
        dbs = jnp.zeros((CHUNK, groups), F32)
        for g in range(groups):
            cols = slice(g * CHUNK, (g + 1) * CHUNK)
            dbs = dbs + jnp.where(lane == g, jnp.sum(dsv[:, cols], axis=1, keepdims=True), 0.0)
            dws_ref[g] += jnp.where(tri, _dot_nt(dsv_b[:, cols], p["vn"][:, cols]), 0.0)
            wt = jnp.where(tri_t, wst_ref[g], 0.0).astype(BF16)
            dvn.append(_dot(wt, dsv_b[:, cols]))
        dbst_ref[...] += dbs
        dvn = jnp.concatenate(dvn, axis=1)
        dgv_ref[...] += jnp.sum(dvn * p["vhat"], axis=0, keepdims=True)
        dv = _norm_bwd(p["vhat"], p["rv"], gv, dvn)
        dz_ref[...] = jnp.concatenate([du * _gelu_grad(p["zu"]), dv * _gelu_grad(p["zv"])], axis=1)

    vec = BS((1, da), lambda i: (0, 0))
    wsq = BS((groups, CHUNK, CHUNK), lambda i: (0, 0, 0))
    bsq = BS((CHUNK, groups), lambda i: (0, 0))
    return pl.pallas_call(
        body, name=name,
        out_shape=[SDS((S, 2 * da), F32), SDS((groups, CHUNK, CHUNK), F32), SDS((CHUNK, groups), F32),
                   SDS((1, da), F32), SDS((1, da), F32)],
        grid=(S // CHUNK,),
        in_specs=[BS((CHUNK, 2 * da), lambda i: (i, 0)), BS((CHUNK, da), lambda i: (i, 0)), vec, wsq, wsq, bsq, vec],
        out_specs=[BS((CHUNK, 2 * da), lambda i: (i, 0)), wsq, bsq, vec, vec],
        compiler_params=_params(("arbitrary",), 32),
    )(z, dy, g_v, w_s, w_st, b_st, g_a)


def _swa_mask(i, group):
    row = lax.broadcasted_iota(jnp.int32, (group * CHUNK, 2 * CHUNK), 0) & (CHUNK - 1)
    col = lax.broadcasted_iota(jnp.int32, (group * CHUNK, 2 * CHUNK), 1)
    d = row + CHUNK - col
    return (d >= 0) & (d < CHUNK) & jnp.logical_or(i > 0, col >= CHUNK)


def _stack_heads(t, g, group):
    return jnp.concatenate([t[:, h * HEAD_DIM:(h + 1) * HEAD_DIM] for h in range(g * group, (g + 1) * group)], axis=0)


def _unstack_heads(stacked, group):
    return [stacked[h * CHUNK:(h + 1) * CHUNK] for h in range(group)]


def _swa_probs(qh, kh, sink, mask):
    s = jnp.where(mask, _dot_nt(qh, kh) * (HEAD_DIM ** -0.5), NEG)
    m = jnp.maximum(jnp.max(s, axis=-1, keepdims=True), sink)
    e = jnp.exp(s - m)
    es = jnp.exp(sink - m)
    inv = 1.0 / (jnp.sum(e, axis=-1, keepdims=True) + es)
    return e * inv, es * inv


def _swa_specs(db, nblk, clamp):
    kvw = 2 * KV_HEADS * HEAD_DIM
    cur = (lambda i: jnp.minimum(i, nblk - 1)) if clamp else (lambda i: i)
    q_spec = BS((CHUNK, db), lambda i: (cur(i), 2))
    kc_spec = BS((CHUNK, kvw), lambda i: (cur(i), 3 * db // kvw))
    kp_spec = BS((CHUNK, kvw), lambda i: (jnp.maximum(cur(i) - 1, 0), 3 * db // kvw))
    return q_spec, kc_spec, kp_spec


def _swa_fwd(z, sink_rows, g_b, name, exchange=None):
    S = z.shape[0]
    db = g_b.shape[1]
    heads = db // HEAD_DIM
    group = heads // KV_HEADS
    nblk = S // CHUNK

    def body(q_ref, kc_ref, kp_ref, sk_ref, gb_ref, yb_ref, ybn_ref):
        mask = _swa_mask(pl.program_id(0), group)
        q = q_ref[...].astype(BF16)
        kv = jnp.concatenate([kp_ref[...], kc_ref[...]], axis=0).astype(BF16)
        outs = []
        for g in range(KV_HEADS):
            kg = kv[:, g * HEAD_DIM:(g + 1) * HEAD_DIM]
            vg = kv[:, (KV_HEADS + g) * HEAD_DIM:(KV_HEADS + g + 1) * HEAD_DIM]
            rows = slice(g * group * CHUNK, (g + 1) * group * CHUNK)
            p, _ = _swa_probs(_stack_heads(q, g, group), kg, sk_ref[rows, :], mask)
            outs += _unstack_heads(_dot(p.astype(BF16), vg), group)
        yb = jnp.concatenate(outs, axis=1)
        yb_ref[...] = yb
        ybn_ref[...] = (yb * _rstd(yb) * gb_ref[...]).astype(BF16)

    q_spec, kc_spec, kp_spec = _swa_specs(db, nblk, False)
    out = BS((CHUNK, db), lambda i: (i, 0))
    return _call(
        body, name=name, args=(z, z, z, sink_rows, g_b), out_shape=[SDS((S, db), F32), SDS((S, db), BF16)],
        grid=(nblk,),
        in_specs=[q_spec, kc_spec, kp_spec, BS((heads * CHUNK, 1), lambda i: (0, 0)), BS((1, db), lambda i: (0, 0))],
        out_specs=[out, out], sem=("parallel",), vmem_mib=32, exchange=exchange,
        first=lambda: pl.program_id(0) == 0, last=lambda: pl.program_id(0) == nblk - 1)


def _swa_bwd(z, yb, dy, sink_rows, g_b, name):
    S = z.shape[0]
    db = g_b.shape[1]
    heads = db // HEAD_DIM
    group = heads // KV_HEADS
    nblk = S // CHUNK
    kvw = 2 * KV_HEADS * HEAD_DIM

    def body(q_ref, kc_ref, kp_ref, yb_ref, dy_ref, sk_ref, gb_ref, dq_ref, dkv_ref, dsk_ref, dgb_ref, carry_ref):
        i = pl.program_id(0)

        @pl.when(i == 0)
        def _():
            carry_ref[...] = jnp.zeros_like(carry_ref)
            dsk_ref[...] = jnp.zeros_like(dsk_ref)
            dgb_ref[...] = jnp.zeros_like(dgb_ref)

        @pl.when(i < nblk)
        def _():
            mask = _swa_mask(i, group)
            yb = yb_ref[...]
            rb = _rstd(yb)
            yhat = yb * rb
            d = dy_ref[...]
            dgb_ref[...] += jnp.sum(d * yhat, axis=0, keepdims=True)
            do = _norm_bwd(yhat, rb, gb_ref[...], d).astype(BF16)
            q = q_ref[...].astype(BF16)
            kv = jnp.concatenate([kp_ref[...], kc_ref[...]], axis=0).astype(BF16)
            dqs, dk, dv = [], [], []
            lane = lax.broadcasted_iota(jnp.int32, (1, heads), 1)
            dsinks = jnp.zeros((1, heads), F32)
            for g in range(KV_HEADS):
                kg = kv[:, g * HEAD_DIM:(g + 1) * HEAD_DIM]
                vg = kv[:, (KV_HEADS + g) * HEAD_DIM:(KV_HEADS + g + 1) * HEAD_DIM]
                rows = slice(g * group * CHUNK, (g + 1) * group * CHUNK)
                qg, dog = _stack_heads(q, g, group), _stack_heads(do, g, group)
                p, ps = _swa_probs(qg, kg, sk_ref[rows, :], mask)
                dp = _dot_nt(dog, vg)
                dr = jnp.sum(p * dp, axis=-1, keepdims=True)
                ds = (p * (dp - dr) * (HEAD_DIM ** -0.5)).astype(BF16)
                for h, t in enumerate(_unstack_heads(ps * dr, group)):
                    dsinks = dsinks - jnp.where(lane == g * group + h, jnp.sum(t, axis=0, keepdims=True), 0.0)
                dqs += _unstack_heads(_dot(ds, kg), group)
                dk.append(_dot_tn(ds, qg))
                dv.append(_dot_tn(p.astype(BF16), dog))
            dq_ref[...] = jnp.concatenate(dqs, axis=1)
            dsk_ref[...] += dsinks
            contrib = jnp.concatenate(dk + dv, axis=1)
            dkv_ref[...] = carry_ref[...] + contrib[:CHUNK]
            carry_ref[...] = contrib[CHUNK:]

        @pl.when(i == nblk)
        def _():
            dkv_ref[...] = carry_ref[...]

    q_spec, kc_spec, kp_spec = _swa_specs(db, nblk, True)
    cur = BS((CHUNK, db), lambda i: (jnp.minimum(i, nblk - 1), 0))
    return pl.pallas_call(
        body, name=name,
        out_shape=[SDS((S, db), F32), SDS((S, kvw), F32), SDS((1, heads), F32), SDS((1, db), F32)],
        grid=(nblk + 1,),
        in_specs=[q_spec, kc_spec, kp_spec, cur, BS((CHUNK, db), lambda i: (jnp.minimum(i, nblk - 1), 1)),
                  BS((heads * CHUNK, 1), lambda i: (0, 0)), BS((1, db), lambda i: (0, 0))],
        out_specs=[cur, BS((CHUNK, kvw), lambda i: (jnp.maximum(i - 1, 0), 0)), BS((1, heads), lambda i: (0, 0)),
                   BS((1, db), lambda i: (0, 0))],
        scratch_shapes=[pltpu.VMEM((CHUNK, kvw), F32)], compiler_params=_params(("arbitrary",), 32),
    )(z, z, z, yb, dy, sink_rows, g_b)


def _xattn_probs(qh, kh, hd):
    s = _dot_nt(qh, kh) * (hd ** -0.5)
    e = jnp.exp(s - jnp.max(s, axis=-1, keepdims=True))
    return e / jnp.sum(e, axis=-1, keepdims=True)


def _xattn_fwd(q, kv, name, tq=512):
    S, D = q.shape
    M = kv.shape[0]
    hd = D // X_HEADS
    tq = min(tq, S)
    assert S % tq == 0, (S, tq)

    def body(q_ref, kv_ref, o_ref):
        for h in range(X_HEADS):
            cols = slice(h * hd, (h + 1) * hd)
            p = _xattn_probs(q_ref[:, cols], kv_ref[:, cols], hd)
            o_ref[:, cols] = _dot(p.astype(BF16), kv_ref[:, D + h * hd:D + (h + 1) * hd]).astype(BF16)

    row = BS((tq, D), lambda i: (i, 0))
    return pl.pallas_call(
        body, name=name, out_shape=SDS((S, D), BF16), grid=(S // tq,),
        in_specs=[row, BS((M, 2 * D), lambda i: (0, 0))], out_specs=row, compiler_params=_params(("parallel",), 40),
    )(q, kv)


def _xattn_bwd(q, kv, do, name, tq=512):
    S, D = q.shape
    M = kv.shape[0]
    hd = D // X_HEADS
    tq = min(tq, S)
    assert S % tq == 0, (S, tq)

    def body(q_ref, kv_ref, do_ref, dq_ref, dkv_ref):
        @pl.when(pl.program_id(0) == 0)
        def _():
            dkv_ref[...] = jnp.zeros_like(dkv_ref)

        for h in range(X_HEADS):
            cols = slice(h * hd, (h + 1) * hd)
            vcols = slice(D + h * hd, D + (h + 1) * hd)
            qh, kh, vh, doh = q_ref[:, cols], kv_ref[:, cols], kv_ref[:, vcols], do_ref[:, cols]
            p = _xattn_probs(qh, kh, hd)
            dp = _dot_nt(doh, vh)
            ds = (p * (dp - jnp.sum(p * dp, axis=-1, keepdims=True)) * (hd ** -0.5)).astype(BF16)
            dq_ref[:, cols] = _dot(ds, kh).astype(BF16)
            dkv_ref[:, cols] += _dot_tn(ds, qh)
            dkv_ref[:, vcols] += _dot_tn(p.astype(BF16), doh)

    row = BS((tq, D), lambda i: (i, 0))
    full = BS((M, 2 * D), lambda i: (0, 0))
    return pl.pallas_call(
        body, name=name, out_shape=[SDS((S, D), BF16), SDS((M, 2 * D), F32)], grid=(S // tq,),
        in_specs=[row, full, row], out_specs=[row, full], compiler_params=_params(("arbitrary",), 40),
    )(q, kv, do)


def _row_tile(rows, cap):
    best = 8
    for t in range(8, min(rows, cap) + 1, 8):
        if rows % t == 0:
            best = t
    assert rows % best == 0, (rows, cap)
    return best


ADAM_TILE_ELEMS = 256 * 1024


def _adamw(w, m, v, recv, name):
    R, C = w.shape
    slots = recv.shape[0]
    tr = _row_tile(R, max(8, ADAM_TILE_ELEMS // C))

    def body(w_ref, m_ref, v_ref, r_ref, g_ref, d_ref, nm_ref, nv_ref):
        g = r_ref[0].astype(F32)
        for s in range(1, slots):
            g = g + r_ref[s].astype(F32)
        mn = ADAM_B1 * m_ref[...] + (1.0 - ADAM_B1) * g
        vn = ADAM_B2 * v_ref[...] + (1.0 - ADAM_B2) * jnp.square(g)
        m_hat = mn / (1.0 - ADAM_B1 ** ADAM_STEP)
        v_hat = vn / (1.0 - ADAM_B2 ** ADAM_STEP)
        g_ref[...] = g
        d_ref[...] = -ADAM_LR * (m_hat / (jnp.sqrt(v_hat) + ADAM_EPS) + ADAM_WD * w_ref[...])
        nm_ref[...] = mn
        nv_ref[...] = vn

    row = BS((tr, C), lambda i: (i, 0))
    return pl.pallas_call(
        body, name=name, out_shape=[SDS((R, C), F32)] * 4, grid=(R // tr,),
        in_specs=[row, row, row, BS((slots, tr, C), lambda i: (0, i, 0))], out_specs=[row] * 4,
        compiler_params=_params(("parallel",), 40),
    )(w, m, v, recv)


def _cols_to_blocks(full):
    r, c = full.shape
    return full.reshape(r, N_DEV, c // N_DEV).transpose(1, 0, 2)


def _blocks_to_cols(blocks):
    n, r, c = blocks.shape
    return blocks.transpose(1, 0, 2).reshape(r, n * c)


def _pack(parts):
    flat = jnp.concatenate([p.reshape(-1).astype(F32) for p in parts])
    pad = (-flat.shape[0]) % (128 * 128)
    return jnp.pad(flat, (0, pad)).reshape(-1, 128)


def kernel(x, mem, g_ffn1, w1_gate, w1_up, w1_down, g_mix, w_in, g_v, w_s, b_s, sinks, g_a_out, g_b_out, w_out, g_x, g_mem, w_xq, w_xkv, w_xo, g_ffn2, w2_gate, w2_up, w2_down, g_final, loss_target, m_g_ffn1, m_w1_gate, m_w1_up, m_w1_down, m_g_mix, m_w_in, m_g_v, m_w_s, m_b_s, m_sinks, m_g_a_out, m_g_b_out, m_w_out, m_g_x, m_g_mem, m_w_xq, m_w_xkv, m_w_xo, m_g_ffn2, m_w2_gate, m_w2_up, m_w2_down, m_g_final, v_g_ffn1, v_w1_gate, v_w1_up, v_w1_down, v_g_mix, v_w_in, v_g_v, v_w_s, v_b_s, v_sinks, v_g_a_out, v_g_b_out, v_w_out, v_g_x, v_g_mem, v_w_xq, v_w_xkv, v_w_xo, v_g_ffn2, v_w2_gate, v_w2_up, v_w2_down, v_g_final):
    w = dict(g_ffn1=g_ffn1, w1_gate=w1_gate, w1_up=w1_up, w1_down=w1_down, g_mix=g_mix, w_in=w_in, g_v=g_v, w_s=w_s,
             b_s=b_s, sinks=sinks, g_a_out=g_a_out, g_b_out=g_b_out, w_out=w_out, g_x=g_x, g_mem=g_mem, w_xq=w_xq,
             w_xkv=w_xkv, w_xo=w_xo, g_ffn2=g_ffn2, w2_gate=w2_gate, w2_up=w2_up, w2_down=w2_down, g_final=g_final)
    mom = dict(g_ffn1=m_g_ffn1, w1_gate=m_w1_gate, w1_up=m_w1_up, w1_down=m_w1_down, g_mix=m_g_mix, w_in=m_w_in,
               g_v=m_g_v, w_s=m_w_s, b_s=m_b_s, sinks=m_sinks, g_a_out=m_g_a_out, g_b_out=m_g_b_out, w_out=m_w_out,
               g_x=m_g_x, g_mem=m_g_mem, w_xq=m_w_xq, w_xkv=m_w_xkv, w_xo=m_w_xo, g_ffn2=m_g_ffn2,
               w2_gate=m_w2_gate, w2_up=m_w2_up, w2_down=m_w2_down, g_final=m_g_final)
    var = dict(g_ffn1=v_g_ffn1, w1_gate=v_w1_gate, w1_up=v_w1_up, w1_down=v_w1_down, g_mix=v_g_mix, w_in=v_w_in,
               g_v=v_g_v, w_s=v_w_s, b_s=v_b_s, sinks=v_sinks, g_a_out=v_g_a_out, g_b_out=v_g_b_out, w_out=v_w_out,
               g_x=v_g_x, g_mem=v_g_mem, w_xq=v_w_xq, w_xkv=v_w_xkv, w_xo=v_w_xo, g_ffn2=v_g_ffn2,
               w2_gate=v_w2_gate, w2_up=v_w2_up, w2_down=v_w2_down, g_final=v_g_final)

    xs, ms, tgt = x[0], mem[0], loss_target[0]
    D = xs.shape[1]
    d_a = w_s.shape[1] * CHUNK
    d_b = D - d_a
    kvw = 2 * KV_HEADS * HEAD_DIM

    shard = {k: w[k][0].astype(BF16) for k in BIG}
    part = _run_exchange(_gather_exchange([shard[k] for k in FFN1_W], CHIP_PEERS), "ag_ffn1_ici")
    wg1, wu1, wd1 = _run_exchange(_sibling_exchange(part), "ag_ffn1_d2d")
    gf = g_final.reshape(1, D)
    ws, ws_t, bs_t = w_s[0], jnp.swapaxes(w_s[0], 1, 2), b_s[0].T
    sink_rows = jnp.repeat(sinks.reshape(-1), CHUNK).reshape(-1, 1)

    (h1, n1, a1, b1), part = _ffn_fwd(xs, g_ffn1, wg1, wu1, wd1, "ffn1_fwd",
                                      exchange=_gather_exchange([shard[k] for k in REST_W], CHIP_PEERS))
    part = dict(zip(REST_W, part))
    n2, (win_blocks,) = _rms_fwd(h1, g_mix, "mix_norm", exchange=_sibling_exchange([part["w_in"]]))
    win = _blocks_to_cols(win_blocks)
    z = _matmul(n2, win, "nn", F32, "mm_in", tn=win.shape[1] // 2)
    ya_n = _sgu_fwd(z, g_v, ws, bs_t, g_a_out, "sgu_fwd")
    later = [k for k in REST_W if k != "w_in"]
    (yb, yb_n), handed = _swa_fwd(z, sink_rows, g_b_out, "swa_fwd",
                                  exchange=_sibling_exchange([part[k] for k in later]))
    gathered = dict(zip(later, handed))
    wg2, wu2, wd2 = gathered["w2_gate"], gathered["w2_up"], gathered["w2_down"]
    wxkv = _blocks_to_cols(gathered["w_xkv"])
    wout = gathered["w_out"].reshape(D, D)
    wxq = gathered["w_xq"].reshape(D, D)
    wxo = gathered["w_xo"].reshape(D, D)
    y = jnp.concatenate([ya_n, yb_n], axis=1)
    h2 = _matmul(y, wout, "nn", F32, "mm_out", res=h1, tn=D)
    hx, _ = _rms_fwd(h2, g_x, "x_norm")
    mn, _ = _rms_fwd(ms, g_mem, "mem_norm")
    q = _matmul(hx, wxq, "nn", BF16, "mm_xq", tn=D)
    kv = _matmul(mn, wxkv, "nn", BF16, "mm_xkv")
    o = _xattn_fwd(q, kv, "xattn_fwd")
    h3 = _matmul(o, wxo, "nn", F32, "mm_xo", res=h2, tn=D)
    (h4, n4, a2, b2), _ = _ffn_fwd(h3, g_ffn2, wg2, wu2, wd2, "ffn2_fwd")
    dh4, dg_final, loss_part = _final_loss(h4, gf, tgt, "final_loss")

    grad_big, grad_small, recv_big = {}, {"g_final": dg_final}, {}
    order = _dw_block_order()
    (dn4, df2, da2, db2, s2), _ = _ffn_bwd_dx(dh4, a2, b2, wg2, wu2, wd2, "ffn2_bwd_dx")
    dh3, grad_small["g_ffn2"] = _rms_bwd(h3, g_ffn2, dn4, dh4, "ffn2_norm_bwd")
    (recv_big["w2_gate"], recv_big["w2_up"]), _ = _ffn_bwd_dw([(n4, da2), (n4, db2)], order, "ffn2_bwd_dw_gu")
    (recv_big["w2_down"],), _ = _ffn_bwd_dw([(s2, df2)], order, "ffn2_bwd_dw_d", tk=1024)

    do = _matmul(dh3, wxo, "nt", BF16, "mm_xo_dx", tn=D)
    grad_big["w_xo"] = _matmul(o, dh3, "tn", BF16, "mm_xo_dw", tm=D, tn=1024).reshape(N_DEV, D // N_DEV, D)
    dq, dkv = _xattn_bwd(q, kv, do, "xattn_bwd")
    dhx = _matmul(dq, wxq, "nt", BF16, "mm_xq_dx", tn=D)
    grad_big["w_xq"] = _matmul(hx, dq, "tn", BF16, "mm_xq_dw", tm=D, tn=1024).reshape(N_DEV, D // N_DEV, D)
    dmn = _matmul(dkv, wxkv, "nt", F32, "mm_xkv_dx")
    grad_big["w_xkv"] = _cols_to_blocks(_matmul(mn, dkv, "tn", BF16, "mm_xkv_dw", tm=1024, tn=1024))
    dh2, grad_small["g_x"] = _rms_bwd(h2, g_x, dhx, dh3, "x_norm_bwd")
    _, grad_small["g_mem"] = _rms_bwd(ms, g_mem, dmn, None, "mem_norm_bwd")

    dy = _matmul(dh2, wout, "nt", F32, "mm_out_dx", tn=D)
    grad_big["w_out"] = _matmul(y, dh2, "tn", BF16, "mm_out_dw", tm=D, tn=1024).reshape(N_DEV, D // N_DEV, D)
    dz_uv, grad_small["w_s"], dbs_t, grad_small["g_v"], grad_small["g_a_out"] = _sgu_bwd(
        z, dy, g_v, ws, ws_t, bs_t, g_a_out, "sgu_bwd")
    grad_small["b_s"] = dbs_t.T
    dq_b, dkv_b, grad_small["sinks"], grad_small["g_b_out"] = _swa_bwd(z, yb, dy, sink_rows, g_b_out, "swa_bwd")
    dz = jnp.concatenate([dz_uv, dq_b, dkv_b], axis=1)
    dn2 = _matmul(dz, win, "nt", BF16, "mm_in_dx", tn=1024)
    grad_big["w_in"] = _cols_to_blocks(_matmul(n2, dz, "tn", BF16, "mm_in_dw", tm=D, tn=dz.shape[1] // 2))
    dh1, grad_small["g_mix"] = _rms_bwd(h1, g_mix, dn2, dh2, "mix_norm_bwd")

    (dn1, df1, da1, db1, s1), recv_mid = _ffn_bwd_dx(
        dh1, a1, b1, wg1, wu1, wd1, "ffn1_bwd_dx", exchange=_scatter_exchange([grad_big[k] for k in MID_W]))
    recv_big.update(zip(MID_W, recv_mid))
    dx, grad_small["g_ffn1"] = _rms_bwd(xs, g_ffn1, dn1, dh1, "ffn1_norm_bwd")
    (recv_big["w1_gate"], recv_big["w1_up"]), (recv_small,) = _ffn_bwd_dw(
        [(n1, da1), (n1, db1)], order, "ffn1_bwd_dw_gu",
        exchange=_gather_exchange([_pack([grad_small[k] for k in SMALL])], ALL_PEERS))
    (recv_big["w1_down"],), _ = _ffn_bwd_dw([(s1, df1)], order, "ffn1_bwd_dw_d", tk=1024)


    grads, deltas, new_m, new_v = {}, {}, {}, {}
    for k in BIG:
        shp = w[k].shape
        two_d = shp[1:]
        outs = _adamw(w[k].reshape(two_d), mom[k].reshape(two_d), var[k].reshape(two_d), recv_big[k], "adamw_" + k)
        grads[k], deltas[k], new_m[k], new_v[k] = [t.reshape(shp) for t in outs]
    packed = _adamw(_pack([w[k] for k in SMALL]), _pack([mom[k] for k in SMALL]), _pack([var[k] for k in SMALL]),
                    recv_small, "adamw_small")
    off = 0
    for k in SMALL:
        shp = w[k].shape
        size = 1
        for s in shp:
            size *= s
        for dst, src in zip((grads, deltas, new_m, new_v), packed):
            dst[k] = src.reshape(-1)[off:off + size].reshape(shp)
        off += size

    loss = lax.psum(loss_part[0, 0], AXES)
    return (loss, dx[None], *[grads[k] for k in WEIGHTS], *[deltas[k] for k in WEIGHTS],
            *[new_m[k] for k in WEIGHTS], *[new_v[k] for k in WEIGHTS])
```

```python
import jax
import jax.numpy as jnp
from jax import lax
from jax.experimental import pallas as pl
from jax.experimental.pallas import tpu as pltpu

F32 = jnp.float32
BF16 = jnp.bfloat16
SDS = jax.ShapeDtypeStruct
BS = pl.BlockSpec

N_DEV = 8
AXES = ("x", "y", "c")
EPS = 1e-5
CHUNK = 128
HEAD_DIM = 64
KV_HEADS = 2
X_HEADS = 4
NEG = -1e30
ADAM_LR = 0.001
ADAM_B1 = 0.9
ADAM_B2 = 0.999
ADAM_EPS = 1e-08
ADAM_WD = 0.01
ADAM_STEP = 10
MIB = 1 << 20
FFN_SUB_ROWS = 256
WEIGHTS = ['g_ffn1', 'w1_gate', 'w1_up', 'w1_down', 'g_mix', 'w_in', 'g_v', 'w_s', 'b_s', 'sinks', 'g_a_out',
           'g_b_out', 'w_out', 'g_x', 'g_mem', 'w_xq', 'w_xkv', 'w_xo', 'g_ffn2', 'w2_gate', 'w2_up', 'w2_down',
           'g_final']
BIG = ['w1_gate', 'w1_up', 'w1_down', 'w_in', 'w_out', 'w_xq', 'w_xkv', 'w_xo', 'w2_gate', 'w2_up', 'w2_down']
SMALL = [w for w in WEIGHTS if w not in BIG]
FFN1_W = ['w1_gate', 'w1_up', 'w1_down']
FFN2_W = ['w2_gate', 'w2_up', 'w2_down']
MID_W = ['w_in', 'w_out', 'w_xq', 'w_xkv', 'w_xo']
REST_W = MID_W + FFN2_W


def _params(sem=None, vmem_mib=48):
    return pltpu.CompilerParams(dimension_semantics=sem, vmem_limit_bytes=vmem_mib * MIB)


def _dot(a, b):
    return jnp.dot(a, b, preferred_element_type=F32)


def _dot_nt(a, b):
    return lax.dot_general(a, b, (((1,), (1,)), ((), ())), preferred_element_type=F32)


def _dot_tn(a, b):
    return lax.dot_general(a, b, (((0,), (0,)), ((), ())), preferred_element_type=F32)


def _rstd(v):
    return lax.rsqrt(jnp.mean(v * v, axis=-1, keepdims=True) + EPS)


def _norm_bwd(xhat, r, g, d):
    t = d * g
    return r * (t - xhat * jnp.mean(t * xhat, axis=-1, keepdims=True))


def _gelu(v):
    return 0.5 * v * (1.0 + lax.erf(v * 0.7071067811865476))


def _gelu_grad(v):
    return 0.5 * (1.0 + lax.erf(v * 0.7071067811865476)) + v * jnp.exp(-0.5 * v * v) * 0.3989422804014327


def _mesh_pos():
    x, y, c = lax.axis_index("x"), lax.axis_index("y"), lax.axis_index("c")
    return x, y, c, 4 * x + 2 * y + c


def _peer(x, y, c, k):
    px = 1 - x if k & 4 else x
    py = 1 - y if k & 2 else y
    pc = 1 - c if k & 1 else c
    return (px, py, pc), 4 * px + 2 * py + pc


ANY = BS(memory_space=pl.ANY)
DMA_SEM = pltpu.SemaphoreType.DMA
ALL_PEERS = (1, 2, 3, 4, 5, 6, 7)
CHIP_PEERS = (2, 4, 6)
SIBLING = 1


def _remote(src, dst, send, recv, peer):
    return pltpu.make_async_remote_copy(src_ref=src, dst_ref=dst, send_sem=send, recv_sem=recv, device_id=peer,
                                        device_id_type=pl.DeviceIdType.MESH)


class _Exchange:
    def __init__(self, ins, out_shapes, sems, copies, aliases=None):
        self.ins, self.out_shapes, self.sems, self.copies, self.aliases = ins, out_shapes, sems, copies, aliases or {}


def _gather_exchange(shards, peers):
    n, m = len(shards), len(peers)

    def copies(ins, outs, sems):
        send, recv, lsem = sems
        x, y, c, me = _mesh_pos()
        cps = [pltpu.make_async_copy(ins[w], outs[w].at[me], lsem.at[w]) for w in range(n)]
        for w in range(n):
            for j, k in enumerate(peers):
                cps.append(_remote(ins[w], outs[w].at[me], send.at[w, j], recv.at[w, j], _peer(x, y, c, k)[0]))
        return cps

    return _Exchange(shards, [SDS((N_DEV,) + s.shape, s.dtype) for s in shards],
                     [DMA_SEM((n, m)), DMA_SEM((n, m)), DMA_SEM((n,))], copies)


def _sibling_exchange(gathered):
    n = len(gathered)

    def copies(ins, outs, sems):
        send, recv = sems
        x, y, c, me = _mesh_pos()
        slots = [me] + [_peer(x, y, c, k)[1] for k in CHIP_PEERS]
        sib = _peer(x, y, c, SIBLING)[0]
        return [_remote(outs[w].at[s], outs[w].at[s], send.at[w, j], recv.at[w, j], sib)
                for w in range(n) for j, s in enumerate(slots)]

    return _Exchange(gathered, [SDS(g.shape, g.dtype) for g in gathered], [DMA_SEM((n, 4)), DMA_SEM((n, 4))], copies,
                     aliases={w: w for w in range(n)})


def _scatter_exchange(fulls):
    n, m = len(fulls), len(ALL_PEERS)

    def copies(ins, outs, sems):
        send, recv, lsem = sems
        x, y, c, me = _mesh_pos()
        cps = [pltpu.make_async_copy(ins[w].at[me], outs[w].at[me], lsem.at[w]) for w in range(n)]
        for w in range(n):
            for j, k in enumerate(ALL_PEERS):
                peer, p = _peer(x, y, c, k)
                cps.append(_remote(ins[w].at[p], outs[w].at[me], send.at[w, j], recv.at[w, j], peer))
        return cps

    return _Exchange(fulls, [SDS(f.shape, f.dtype) for f in fulls],
                     [DMA_SEM((n, m)), DMA_SEM((n, m)), DMA_SEM((n,))], copies)


def _call(body, *, name, args, in_specs, out_shape, out_specs, grid, sem, vmem_mib, scratch=(), exchange=None,
          first=None, last=None):
    if exchange is None:
        return pl.pallas_call(body, name=name, out_shape=out_shape, grid=grid, in_specs=in_specs,
                              out_specs=out_specs, scratch_shapes=list(scratch),
                              compiler_params=_params(sem, vmem_mib))(*args), []
    ni, no, ns = len(args), len(out_shape), len(scratch)
    xi, xo = len(exchange.ins), len(exchange.out_shapes)

    def hosted(*refs):
        own_in, refs = refs[:ni], refs[ni:]
        x_in, refs = refs[:xi], refs[xi:]
        own_out, refs = refs[:no], refs[no:]
        x_out, refs = refs[:xo], refs[xo:]
        own_scr, x_sem = refs[:ns], refs[ns:]

        @pl.when(first())
        def _():
            for cp in exchange.copies(x_in, x_out, x_sem):
                cp.start()

        body(*own_in, *own_out, *own_scr)

        @pl.when(last())
        def _():
            for cp in exchange.copies(x_in, x_out, x_sem):
                cp.wait()

    outs = pl.pallas_call(
        hosted, name=name, out_shape=list(out_shape) + list(exchange.out_shapes), grid=grid,
        in_specs=list(in_specs) + [ANY] * xi, out_specs=list(out_specs) + [ANY] * xo,
        scratch_shapes=list(scratch) + list(exchange.sems),
        input_output_aliases={ni + a: no + b for a, b in exchange.aliases.items()},
        compiler_params=pltpu.CompilerParams(dimension_semantics=tuple("arbitrary" for _ in grid),
                                             vmem_limit_bytes=vmem_mib * MIB, has_side_effects=True),
    )(*args, *exchange.ins)
    return outs[:no], outs[no:]


def _run_exchange(exchange, name):
    xi, xo = len(exchange.ins), len(exchange.out_shapes)

    def body(*refs):
        cps = exchange.copies(refs[:xi], refs[xi:xi + xo], refs[xi + xo:])
        for cp in cps:
            cp.start()
        for cp in cps:
            cp.wait()

    return pl.pallas_call(
        body, name=name, out_shape=list(exchange.out_shapes), in_specs=[ANY] * xi, out_specs=[ANY] * xo,
        scratch_shapes=list(exchange.sems), input_output_aliases=dict(exchange.aliases),
        compiler_params=pltpu.CompilerParams(has_side_effects=True),
    )(*exchange.ins)


def _rms_fwd(h, g, name, tm=512, exchange=None):
    S, D = h.shape
    tm = min(tm, S)
    assert S % tm == 0, (S, tm)
    ni = S // tm

    def body(h_ref, g_ref, o_ref):
        hv = h_ref[...]
        o_ref[...] = (hv * _rstd(hv) * g_ref[...]).astype(o_ref.dtype)

    (out,), extra = _call(
        body, name=name, args=(h, g), out_shape=[SDS((S, D), BF16)], grid=(ni,),
        in_specs=[BS((tm, D), lambda i: (i, 0)), BS((1, D), lambda i: (0, 0))],
        out_specs=[BS((tm, D), lambda i: (i, 0))], sem=("parallel",), vmem_mib=32, exchange=exchange,
        first=lambda: pl.program_id(0) == 0, last=lambda: pl.program_id(0) == ni - 1)
    return out, extra


def _rms_bwd(h, g, dn, dres, name, tm=256):
    S, D = h.shape
    tm = min(tm, S)
    assert S % tm == 0, (S, tm)
    has_res = dres is not None

    def body(*refs):
        if has_res:
            h_ref, g_ref, dn_ref, dres_ref, dh_ref, dg_ref = refs
        else:
            h_ref, g_ref, dn_ref, dh_ref, dg_ref = refs
        hv = h_ref[...]
        r = _rstd(hv)
        xh = hv * r
        d = dn_ref[...].astype(F32)

        @pl.when(pl.program_id(0) == 0)
        def _():
            dg_ref[...] = jnp.zeros_like(dg_ref)

        dg_ref[...] += jnp.sum(d * xh, axis=0, keepdims=True)
        dh = _norm_bwd(xh, r, g_ref[...], d)
        dh_ref[...] = dres_ref[...] + dh if has_res else dh

    row = BS((tm, D), lambda i: (i, 0))
    vec = BS((1, D), lambda i: (0, 0))
    args = (h, g, dn) + ((dres,) if has_res else ())
    return pl.pallas_call(
        body, name=name, out_shape=[SDS((S, D), F32), SDS((1, D), F32)], grid=(S // tm,),
        in_specs=[row, vec, row] + ([row] if has_res else []), out_specs=[row, vec],
        compiler_params=_params(("arbitrary",), 40),
    )(*args)


def _final_loss(h, g, target, name, tm=256):
    S, D = h.shape
    tm = min(tm, S)
    assert S % tm == 0, (S, tm)

    def body(h_ref, g_ref, t_ref, dh_ref, dg_ref, loss_ref):
        hv = h_ref[...]
        r = _rstd(hv)
        xh = hv * r
        gv = g_ref[...]
        e = xh * gv - t_ref[...]

        @pl.when(pl.program_id(0) == 0)
        def _():
            dg_ref[...] = jnp.zeros_like(dg_ref)
            loss_ref[...] = jnp.zeros_like(loss_ref)

        loss_ref[...] += 0.5 * jnp.sum(jnp.mean(e * e, axis=-1, keepdims=True), axis=0, keepdims=True)
        dy = e * (1.0 / D)
        dg_ref[...] += jnp.sum(dy * xh, axis=0, keepdims=True)
        dh_ref[...] = _norm_bwd(xh, r, gv, dy)

    row = BS((tm, D), lambda i: (i, 0))
    vec = BS((1, D), lambda i: (0, 0))
    return pl.pallas_call(
        body, name=name, out_shape=[SDS((S, D), F32), SDS((1, D), F32), SDS((1, 128), F32)], grid=(S // tm,),
        in_specs=[row, vec, row], out_specs=[row, vec, BS((1, 128), lambda i: (0, 0))],
        compiler_params=_params(("arbitrary",), 40),
    )(h, g, target)


def _matmul(a, b, mode, out_dtype, name, res=None, tm=512, tn=512, tk=512):
    if mode == "tn":
        K, M = a.shape
        N = b.shape[1]
        tm, tn, tk = min(tm, M), min(tn, N), min(tk, K)
        assert M % tm == 0 and N % tn == 0 and K % tk == 0, (a.shape, b.shape, tm, tn, tk)
        nk = K // tk

        def body(a_ref, b_ref, o_ref, acc_ref):
            k = pl.program_id(2)

            @pl.when(k == 0)
            def _():
                acc_ref[...] = jnp.zeros_like(acc_ref)

            acc_ref[...] += _dot_tn(a_ref[...].astype(BF16), b_ref[...].astype(BF16))

            @pl.when(k == nk - 1)
            def _():
                o_ref[...] = acc_ref[...].astype(o_ref.dtype)

        return pl.pallas_call(
            body, name=name, out_shape=SDS((M, N), out_dtype), grid=(M // tm, N // tn, nk),
            in_specs=[BS((tk, tm), lambda i, j, k: (k, i)), BS((tk, tn), lambda i, j, k: (k, j))],
            out_specs=BS((tm, tn), lambda i, j, k: (i, j)), scratch_shapes=[pltpu.VMEM((tm, tn), F32)],
            compiler_params=_params(("parallel", "parallel", "arbitrary"), 48),
        )(a, b)

    M, K = a.shape
    N = b.shape[1] if mode == "nn" else b.shape[0]
    tm, tn = min(tm, M), min(tn, N)
    assert M % tm == 0 and N % tn == 0, (a.shape, b.shape, tm, tn)
    has_res = res is not None

    def body(*refs):
        if has_res:
            a_ref, b_ref, r_ref, o_ref = refs
        else:
            a_ref, b_ref, o_ref = refs
        av, bv = a_ref[...].astype(BF16), b_ref[...].astype(BF16)
        acc = _dot(av, bv) if mode == "nn" else _dot_nt(av, bv)
        if has_res:
            acc = acc + r_ref[...]
        o_ref[...] = acc.astype(o_ref.dtype)

    b_spec = BS((K, tn), lambda i, j: (0, j)) if mode == "nn" else BS((tn, K), lambda i, j: (j, 0))
    o_spec = BS((tm, tn), lambda i, j: (i, j))
    return pl.pallas_call(
        body, name=name, out_shape=SDS((M, N), out_dtype), grid=(M // tm, N // tn),
        in_specs=[BS((tm, K), lambda i, j: (i, 0)), b_spec] + ([o_spec] if has_res else []), out_specs=o_spec,
        compiler_params=_params(("parallel", "parallel"), 48),
    )(*((a, b) + ((res,) if has_res else ())))


def _ffn_fwd(h, g, wg, wu, wd, name, tm=512, exchange=None):
    S, D = h.shape
    nb, _, Fs = wg.shape
    tm = min(tm, S)
    sub = min(FFN_SUB_ROWS, tm)
    assert S % tm == 0 and tm % sub == 0, (S, tm, sub)

    def body(h_ref, g_ref, wg_ref, wu_ref, wd_ref, o_ref, n_ref, a_ref, b_ref):
        j = pl.program_id(1)

        @pl.when(j == 0)
        def _():
            hv = h_ref[...]
            n_ref[...] = (hv * _rstd(hv) * g_ref[...]).astype(BF16)
            o_ref[...] = jnp.zeros_like(o_ref)

        for r in range(0, tm, sub):
            rows = slice(r, r + sub)
            n = n_ref[rows, :]
            a = _dot(n, wg_ref[...]).astype(BF16)
            b = _dot(n, wu_ref[...]).astype(BF16)
            a_ref[rows, :] = a
            b_ref[rows, :] = b
            a, b = a.astype(F32), b.astype(F32)
            s = (a * jax.nn.sigmoid(a) * b).astype(BF16)
            o_ref[rows, :] += _dot(s, wd_ref[...])

        @pl.when(j == nb - 1)
        def _():
            o_ref[...] = h_ref[...] + 0.5 * o_ref[...]

    row = BS((tm, D), lambda i, j: (i, 0))
    wcol = BS((None, D, Fs), lambda i, j: (j, 0, 0))
    act = BS((None, tm, Fs), lambda i, j: (j, i, 0))
    ni = S // tm
    return _call(
        body, name=name, args=(h, g, wg, wu, wd),
        out_shape=[SDS((S, D), F32), SDS((S, D), BF16), SDS((nb, S, Fs), BF16), SDS((nb, S, Fs), BF16)],
        grid=(ni, nb),
        in_specs=[row, BS((1, D), lambda i, j: (0, 0)), wcol, wcol, BS((None, Fs, D), lambda i, j: (j, 0, 0))],
        out_specs=[row, row, act, act], sem=("parallel", "arbitrary"), vmem_mib=56, exchange=exchange,
        first=lambda: (pl.program_id(0) == 0) & (pl.program_id(1) == 0),
        last=lambda: (pl.program_id(0) == ni - 1) & (pl.program_id(1) == nb - 1))


def _ffn_bwd_dx(dh, a, b, wg, wu, wd, name, tm=512, exchange=None):
    S, D = dh.shape
    nb, _, Fs = wg.shape
    tm = min(tm, S)
    assert S % tm == 0, (S, tm)

    def body(dh_ref, a_ref, b_ref, wg_ref, wu_ref, wd_ref, dn_ref, df_ref, da_ref, db_ref, s_ref, ds_ref):
        j = pl.program_id(1)

        def first_matmul():
            ds_ref[...] = _dot_nt(df_ref[...], wd_ref[...])

        def pre_activation_cotangents():
            ds = ds_ref[...]
            av, bv = a_ref[...].astype(F32), b_ref[...].astype(F32)
            sig = jax.nn.sigmoid(av)
            sl = av * sig
            da = (ds * bv * (sig * (1.0 + av * (1.0 - sig)))).astype(BF16)
            db = (ds * sl).astype(BF16)
            da_ref[...] = da
            db_ref[...] = db
            s_ref[...] = (sl * bv).astype(BF16)
            return da, db

        def last_matmuls(da, db):
            dn_ref[...] += _dot_nt(da, wg_ref[...]) + _dot_nt(db, wu_ref[...])

        @pl.when(j == 0)
        def _():
            df_ref[...] = (0.5 * dh_ref[...]).astype(BF16)
            dn_ref[...] = jnp.zeros_like(dn_ref)
            first_matmul()

        @pl.when((j > 0) & (j < nb))
        def _():
            da, db = pre_activation_cotangents()
            first_matmul()
            last_matmuls(da, db)

        @pl.when(j == nb)
        def _():
            last_matmuls(*pre_activation_cotangents())

    row = BS((tm, D), lambda i, j: (i, 0))
    prev = BS((None, D, Fs), lambda i, j: (jnp.maximum(j - 1, 0), 0, 0))
    act = BS((None, tm, Fs), lambda i, j: (jnp.maximum(j - 1, 0), i, 0))
    ni = S // tm
    return _call(
        body, name=name, args=(dh, a, b, wg, wu, wd),
        out_shape=[SDS((S, D), F32), SDS((S, D), BF16)] + [SDS((nb, S, Fs), BF16)] * 3,
        grid=(ni, nb + 1),
        in_specs=[row, act, act, prev, prev, BS((None, Fs, D), lambda i, j: (jnp.minimum(j, nb - 1), 0, 0))],
        out_specs=[row, row, act, act, act], scratch=[pltpu.VMEM((tm, Fs), F32)],
        sem=("parallel", "arbitrary"), vmem_mib=56, exchange=exchange,
        first=lambda: (pl.program_id(0) == 0) & (pl.program_id(1) == 0),
        last=lambda: (pl.program_id(0) == ni - 1) & (pl.program_id(1) == nb))


DW_FLIPS = {0: (6, 2, 4, 0), 1: (6, 4, 2, 0)}
N_CHIPS = N_DEV // 2


def _dw_block_order():
    x, y, c, me = _mesh_pos()
    steps = [jnp.array([v for mine, sib in zip(DW_FLIPS[core], DW_FLIPS[1 - core]) for v in (sib ^ 1, mine)], jnp.int32)
             for core in (0, 1)]
    return jnp.bitwise_xor(me.astype(jnp.int32), jnp.where(c == 0, steps[0], steps[1]))


def _ffn_bwd_dw(pairs, order, name, tk=512, exchange=None):
    npair = len(pairs)
    S = pairs[0][0].shape[-2]
    nb, half = N_DEV, N_DEV // 2
    tk = min(tk, S)
    assert S % tk == 0, (S, tk)
    nk = S // tk
    shapes = [(lhs.shape[-1], rhs.shape[-1]) for lhs, rhs in pairs]
    xi = len(exchange.ins) if exchange else 0
    xo = len(exchange.out_shapes) if exchange else 0

    def body(order_ref, *rest):
        tiles, rest = rest[:2 * npair], rest[2 * npair:]
        x_in, rest = rest[:xi], rest[xi:]
        recv_bufs, rest = rest[:npair], rest[npair:]
        x_out, rest = rest[:xo], rest[xo:]
        accs, rest = rest[:npair], rest[npair:]
        out_t, rest = rest[:npair], rest[npair:]
        out_m, rest = rest[:npair], rest[npair:]
        land, rest = rest[:npair], rest[npair:]
        (send_t, recv_t, send_m, recv_m, lsem, credit), x_sem = rest[:6], rest[6:]
        t, k = pl.program_id(0), pl.program_id(1)
        x, y, c, me = _mesh_pos()
        sibling = (x, y, 1 - c)
        chip = 2 * x + y

        if exchange:
            @pl.when((t == 0) & (k == 0))
            def _():
                for cp in exchange.copies(x_in, x_out, x_sem):
                    cp.start()

        @pl.when(k == 0)
        def _():
            for acc in accs:
                acc[...] = jnp.zeros_like(acc)

        for w in range(npair):
            accs[w][...] += _dot_tn(tiles[2 * w][...], tiles[2 * w + 1][...])

        def to_sibling(w, i):
            return _remote(out_t[w], land[w], send_t.at[w, i], recv_t.at[w, i], sibling)

        def to_owner(w, i):
            dst = recv_bufs[w].at[chip]
            if i == half - 1:
                return pltpu.make_async_copy(out_m[w], dst, lsem.at[w])
            p = jnp.bitwise_xor(me, jnp.where(c == 0, DW_FLIPS[0][i], DW_FLIPS[1][i]))
            return _remote(out_m[w], dst, send_m.at[w, i], recv_m.at[w, i], (p >> 2, (p >> 1) & 1, p & 1))

        for i in range(half):
            @pl.when((t == 2 * i) & (k == nk - 1))
            def _(i=i):
                if i >= 1:
                    pl.semaphore_wait(credit, 1)
                for w in range(npair):
                    if i >= 1:
                        to_sibling(w, i - 1).wait_send()
                    out_t[w][...] = accs[w][...].astype(BF16)
                    to_sibling(w, i).start()

            @pl.when((t == 2 * i + 1) & (k == nk - 1))
            def _(i=i):
                for w in range(npair):
                    to_sibling(w, i).wait_recv()
                    if i >= 1:
                        to_owner(w, i - 1).wait_send()
                    out_m[w][...] = (accs[w][...] + land[w][...].astype(F32)).astype(BF16)
                if i < half - 1:
                    pl.semaphore_signal(credit, inc=1, device_id=sibling, device_id_type=pl.DeviceIdType.MESH)
                for w in range(npair):
                    to_owner(w, i).start()

        @pl.when((t == nb - 1) & (k == nk - 1))
        def _():
            for w in range(npair):
                to_sibling(w, half - 1).wait_send()
                to_owner(w, half - 1).wait()
                for i in range(half - 1):
                    to_owner(w, i).wait_recv()
            if exchange:
                for cp in exchange.copies(x_in, x_out, x_sem):
                    cp.wait()

    def tile_spec(arr):
        if arr.ndim == 3:
            return BS((None, tk, arr.shape[-1]), lambda t, k, o: (o[t], k, 0))
        return BS((tk, arr.shape[-1]), lambda t, k, o: (k, 0))

    flat = [a for pair in pairs for a in pair]
    grid_spec = pltpu.PrefetchScalarGridSpec(
        num_scalar_prefetch=1, grid=(nb, nk),
        in_specs=[tile_spec(a) for a in flat] + [ANY] * xi, out_specs=[ANY] * (npair + xo),
        scratch_shapes=[pltpu.VMEM(s, F32) for s in shapes] + [pltpu.VMEM(s, BF16) for s in shapes] * 3
        + [DMA_SEM((npair, half)), DMA_SEM((npair, half)), DMA_SEM((npair, half - 1)), DMA_SEM((npair, half - 1)),
           DMA_SEM((npair,)), pltpu.SemaphoreType.REGULAR]
        + (list(exchange.sems) if exchange else []))
    outs = pl.pallas_call(
        body, name=name, grid_spec=grid_spec,
        out_shape=[SDS((N_CHIPS,) + s, BF16) for s in shapes] + (list(exchange.out_shapes) if exchange else []),
        compiler_params=pltpu.CompilerParams(dimension_semantics=("arbitrary", "arbitrary"),
                                             vmem_limit_bytes=54 * MIB, has_side_effects=True),
    )(order, *flat, *(exchange.ins if exchange else ()))
    return outs[:npair], outs[npair:]


def _sgu_parts(z, gv, ws_ref, bst_ref, groups):
    da = z.shape[1] // 2
    zu, zv = z[:, :da], z[:, da:]
    u, v = _gelu(zu), _gelu(zv)
    rv = _rstd(v)
    vhat = v * rv
    vn = (vhat * gv).astype(BF16)
    tri = lax.broadcasted_iota(jnp.int32, (CHUNK, CHUNK), 0) >= lax.broadcasted_iota(jnp.int32, (CHUNK, CHUNK), 1)
    pieces = []
    for g in range(groups):
        w = jnp.where(tri, ws_ref[g], 0.0).astype(BF16)
        pieces.append(_dot(w, vn[:, g * CHUNK:(g + 1) * CHUNK]) + bst_ref[:, g:g + 1])
    sv = jnp.concatenate(pieces, axis=1)
    return dict(zu=zu, zv=zv, u=u, rv=rv, vhat=vhat, vn=vn, sv=sv, tri=tri)


def _sgu_fwd(z, g_v, w_s, b_st, g_a, name):
    S = z.shape[0]
    groups = w_s.shape[0]
    da = groups * CHUNK

    def body(z_ref, gv_ref, ws_ref, bst_ref, ga_ref, o_ref):
        p = _sgu_parts(z_ref[...], gv_ref[...], ws_ref, bst_ref, groups)
        ya = p["u"] * p["sv"]
        o_ref[...] = (ya * _rstd(ya) * ga_ref[...]).astype(BF16)

    vec = BS((1, da), lambda i: (0, 0))
    return pl.pallas_call(
        body, name=name, out_shape=SDS((S, da), BF16), grid=(S // CHUNK,),
        in_specs=[BS((CHUNK, 2 * da), lambda i: (i, 0)), vec, BS((groups, CHUNK, CHUNK), lambda i: (0, 0, 0)),
                  BS((CHUNK, groups), lambda i: (0, 0)), vec],
        out_specs=BS((CHUNK, da), lambda i: (i, 0)), compiler_params=_params(("parallel",), 32),
    )(z, g_v, w_s, b_st, g_a)


def _sgu_bwd(z, dy, g_v, w_s, w_st, b_st, g_a, name):
    S = z.shape[0]
    groups = w_s.shape[0]
    da = groups * CHUNK

    def body(z_ref, dy_ref, gv_ref, ws_ref, wst_ref, bst_ref, ga_ref, dz_ref, dws_ref, dbst_ref, dgv_ref, dga_ref):
        @pl.when(pl.program_id(0) == 0)
        def _():
            dws_ref[...] = jnp.zeros_like(dws_ref)
            dbst_ref[...] = jnp.zeros_like(dbst_ref)
            dgv_ref[...] = jnp.zeros_like(dgv_ref)
            dga_ref[...] = jnp.zeros_like(dga_ref)

        gv = gv_ref[...]
        p = _sgu_parts(z_ref[...], gv, ws_ref, bst_ref, groups)
        u, sv, tri = p["u"], p["sv"], p["tri"]
        ya = u * sv
        ra = _rstd(ya)
        yhat = ya * ra
        d = dy_ref[...]
        dga_ref[...] += jnp.sum(d * yhat, axis=0, keepdims=True)
        dya = _norm_bwd(yhat, ra, ga_ref[...], d)
        du = dya * sv
        dsv = dya * u
        dsv_b = dsv.astype(BF16)
        tri_t = (lax.broadcasted_iota(jnp.int32, (CHUNK, CHUNK), 0)
                 <= lax.broadcasted_iota(jnp.int32, (CHUNK, CHUNK), 1))
        lane = lax.broadcasted_iota(jnp.int32, (CHUNK, groups), 1)
        dvn = []
        dbs = jnp.zeros((CHUNK, groups), F32)
        for g in range(groups):
            cols = slice(g * CHUNK, (g + 1) * CHUNK)
            dbs = dbs + jnp.where(lane == g, jnp.sum(dsv[:, cols], axis=1, keepdims=True), 0.0)
            dws_ref[g] += jnp.where(tri, _dot_nt(dsv_b[:, cols], p["vn"][:, cols]), 0.0)
            wt = jnp.where(tri_t, wst_ref[g], 0.0).astype(BF16)
            dvn.append(_dot(wt, dsv_b[:, cols]))
        dbst_ref[...] += dbs
        dvn = jnp.concatenate(dvn, axis=1)
        dgv_ref[...] += jnp.sum(dvn * p["vhat"], axis=0, keepdims=True)
        dv = _norm_bwd(p["vhat"], p["rv"], gv, dvn)
        dz_ref[...] = jnp.concatenate([du * _gelu_grad(p["zu"]), dv * _gelu_grad(p["zv"])], axis=1)

    vec = BS((1, da), lambda i: (0, 0))
    wsq = BS((groups, CHUNK, CHUNK), lambda i: (0, 0, 0))
    bsq = BS((CHUNK, groups), lambda i: (0, 0))
    return pl.pallas_call(
        body, name=name,
        out_shape=[SDS((S, 2 * da), F32), SDS((groups, CHUNK, CHUNK), F32), SDS((CHUNK, groups), F32),
                   SDS((1, da), F32), SDS((1, da), F32)],
        grid=(S // CHUNK,),
        in_specs=[BS((CHUNK, 2 * da), lambda i: (i, 0)), BS((CHUNK, da), lambda i: (i, 0)), vec, wsq, wsq, bsq, vec],
        out_specs=[BS((CHUNK, 2 * da), lambda i: (i, 0)), wsq, bsq, vec, vec],
        compiler_params=_params(("arbitrary",), 32),
    )(z, dy, g_v, w_s, w_st, b_st, g_a)


def _swa_mask(i, group):
    row = lax.broadcasted_iota(jnp.int32, (group * CHUNK, 2 * CHUNK), 0) & (CHUNK - 1)
    col = lax.broadcasted_iota(jnp.int32, (group * CHUNK, 2 * CHUNK), 1)
    d = row + CHUNK - col
    return (d >= 0) & (d < CHUNK) & jnp.logical_or(i > 0, col >= CHUNK)


def _stack_heads(t, g, group):
    return jnp.concatenate([t[:, h * HEAD_DIM:(h + 1) * HEAD_DIM] for h in range(g * group, (g + 1) * group)], axis=0)


def _unstack_heads(stacked, group):
    return [stacked[h * CHUNK:(h + 1) * CHUNK] for h in range(group)]


def _swa_probs(qh, kh, sink, mask):
    s = jnp.where(mask, _dot_nt(qh, kh) * (HEAD_DIM ** -0.5), NEG)
    m = jnp.maximum(jnp.max(s, axis=-1, keepdims=True), sink)
    e = jnp.exp(s - m)
    es = jnp.exp(sink - m)
    inv = 1.0 / (jnp.sum(e, axis=-1, keepdims=True) + es)
    return e * inv, es * inv


SWA_QBLOCKS = 2


def _swa_specs(db, npair, clamp):
    kvw = 2 * KV_HEADS * HEAD_DIM
    cur = (lambda p: jnp.minimum(p, npair - 1)) if clamp else (lambda p: p)
    rows = SWA_QBLOCKS * CHUNK
    q_spec = BS((rows, db), lambda p: (cur(p), 2))
    kc_spec = BS((rows, kvw), lambda p: (cur(p), 3 * db // kvw))
    kp_spec = BS((CHUNK, kvw), lambda p: (jnp.maximum(SWA_QBLOCKS * cur(p) - 1, 0), 3 * db // kvw))
    return q_spec, kc_spec, kp_spec


def _swa_keys(kp_ref, kc_ref, qb):
    if qb == 0:
        return jnp.concatenate([kp_ref[...], kc_ref[:CHUNK, :]], axis=0).astype(BF16)
    return kc_ref[(qb - 1) * CHUNK:(qb + 1) * CHUNK, :].astype(BF16)


def _swa_fwd(z, sink_rows, g_b, name, exchange=None):
    S = z.shape[0]
    db = g_b.shape[1]
    heads = db // HEAD_DIM
    group = heads // KV_HEADS
    rows = SWA_QBLOCKS * CHUNK
    assert S % rows == 0, (S, rows)
    npair = S // rows

    def body(q_ref, kc_ref, kp_ref, sk_ref, gb_ref, yb_ref, ybn_ref):
        for qb in range(SWA_QBLOCKS):
            blk = slice(qb * CHUNK, (qb + 1) * CHUNK)
            mask = _swa_mask(SWA_QBLOCKS * pl.program_id(0) + qb, group)
            q = q_ref[blk, :].astype(BF16)
            kv = _swa_keys(kp_ref, kc_ref, qb)
            outs = []
            for g in range(KV_HEADS):
                kg = kv[:, g * HEAD_DIM:(g + 1) * HEAD_DIM]
                vg = kv[:, (KV_HEADS + g) * HEAD_DIM:(KV_HEADS + g + 1) * HEAD_DIM]
                srows = slice(g * group * CHUNK, (g + 1) * group * CHUNK)
                p, _ = _swa_probs(_stack_heads(q, g, group), kg, sk_ref[srows, :], mask)
                outs += _unstack_heads(_dot(p.astype(BF16), vg), group)
            yb = jnp.concatenate(outs, axis=1)
            yb_ref[blk, :] = yb
            ybn_ref[blk, :] = (yb * _rstd(yb) * gb_ref[...]).astype(BF16)

    q_spec, kc_spec, kp_spec = _swa_specs(db, npair, False)
    out = BS((rows, db), lambda p: (p, 0))
    return _call(
        body, name=name, args=(z, z, z, sink_rows, g_b), out_shape=[SDS((S, db), F32), SDS((S, db), BF16)],
        grid=(npair,),
        in_specs=[q_spec, kc_spec, kp_spec, BS((heads * CHUNK, 1), lambda p: (0, 0)), BS((1, db), lambda p: (0, 0))],
        out_specs=[out, out], sem=("parallel",), vmem_mib=32, exchange=exchange,
        first=lambda: pl.program_id(0) == 0, last=lambda: pl.program_id(0) == npair - 1)


def _swa_bwd(z, yb, dy, sink_rows, g_b, name):
    S = z.shape[0]
    db = g_b.shape[1]
    heads = db // HEAD_DIM
    group = heads // KV_HEADS
    rows = SWA_QBLOCKS * CHUNK
    assert SWA_QBLOCKS == 2 and S % rows == 0, (S, rows)
    npair = S // rows
    kvw = 2 * KV_HEADS * HEAD_DIM

    def body(q_ref, kc_ref, kp_ref, yb_ref, dy_ref, sk_ref, gb_ref, dq_ref, dkv_ref, dsk_ref, dgb_ref,
             done_ref, part_ref):
        p = pl.program_id(0)

        @pl.when(p == 0)
        def _():
            done_ref[...] = jnp.zeros_like(done_ref)
            part_ref[...] = jnp.zeros_like(part_ref)
            dsk_ref[...] = jnp.zeros_like(dsk_ref)
            dgb_ref[...] = jnp.zeros_like(dgb_ref)

        @pl.when(p < npair)
        def _():
            lane = lax.broadcasted_iota(jnp.int32, (1, heads), 1)
            dsinks = jnp.zeros((1, heads), F32)
            dgb = jnp.zeros((1, db), F32)
            contribs = []
            for qb in range(SWA_QBLOCKS):
                blk = slice(qb * CHUNK, (qb + 1) * CHUNK)
                mask = _swa_mask(SWA_QBLOCKS * p + qb, group)
                ybv = yb_ref[blk, :]
                rb = _rstd(ybv)
                yhat = ybv * rb
                d = dy_ref[blk, :]
                dgb = dgb + jnp.sum(d * yhat, axis=0, keepdims=True)
                do = _norm_bwd(yhat, rb, gb_ref[...], d).astype(BF16)
                q = q_ref[blk, :].astype(BF16)
                kv = _swa_keys(kp_ref, kc_ref, qb)
                dqs, dk, dv = [], [], []
                for g in range(KV_HEADS):
                    kg = kv[:, g * HEAD_DIM:(g + 1) * HEAD_DIM]
                    vg = kv[:, (KV_HEADS + g) * HEAD_DIM:(KV_HEADS + g + 1) * HEAD_DIM]
                    srows = slice(g * group * CHUNK, (g + 1) * group * CHUNK)
                    qg, dog = _stack_heads(q, g, group), _stack_heads(do, g, group)
                    pr, ps = _swa_probs(qg, kg, sk_ref[srows, :], mask)
                    dp = _dot_nt(dog, vg)
                    dr = jnp.sum(pr * dp, axis=-1, keepdims=True)
                    ds = (pr * (dp - dr) * (HEAD_DIM ** -0.5)).astype(BF16)
                    for h, t in enumerate(_unstack_heads(ps * dr, group)):
                        dsinks = dsinks - jnp.where(lane == g * group + h, jnp.sum(t, axis=0, keepdims=True), 0.0)
                    dqs += _unstack_heads(_dot(ds, kg), group)
                    dk.append(_dot_tn(ds, qg))
                    dv.append(_dot_tn(pr.astype(BF16), dog))
                dq_ref[blk, :] = jnp.concatenate(dqs, axis=1)
                contribs.append(jnp.concatenate(dk + dv, axis=1))
            dsk_ref[...] += dsinks
            dgb_ref[...] += dgb
            first, second = contribs
            dkv_ref[:CHUNK, :] = done_ref[...]
            dkv_ref[CHUNK:, :] = part_ref[...] + first[:CHUNK]
            done_ref[...] = first[CHUNK:] + second[:CHUNK]
            part_ref[...] = second[CHUNK:]

        @pl.when(p == npair)
        def _():
            dkv_ref[:CHUNK, :] = done_ref[...]
            dkv_ref[CHUNK:, :] = part_ref[...]

    q_spec, kc_spec, kp_spec = _swa_specs(db, npair, True)
    cur = BS((rows, db), lambda p: (jnp.minimum(p, npair - 1), 0))
    return pl.pallas_call(
        body, name=name,
        out_shape=[SDS((S, db), F32), SDS((S, kvw), F32), SDS((1, heads), F32), SDS((1, db), F32)],
        grid=(npair + 1,),
        in_specs=[q_spec, kc_spec, kp_spec, cur, BS((rows, db), lambda p: (jnp.minimum(p, npair - 1), 1)),
                  BS((heads * CHUNK, 1), lambda p: (0, 0)), BS((1, db), lambda p: (0, 0))],
        out_specs=[cur, BS((rows, kvw), lambda p: (jnp.maximum(p - 1, 0), 0)), BS((1, heads), lambda p: (0, 0)),
                   BS((1, db), lambda p: (0, 0))],
        scratch_shapes=[pltpu.VMEM((CHUNK, kvw), F32), pltpu.VMEM((CHUNK, kvw), F32)],
        compiler_params=_params(("arbitrary",), 32),
    )(z, z, z, yb, dy, sink_rows, g_b)


def _xattn_probs(qh, kh, hd):
    s = _dot_nt(qh, kh) * (hd ** -0.5)
    e = jnp.exp(s - jnp.max(s, axis=-1, keepdims=True))
    return e / jnp.sum(e, axis=-1, keepdims=True)


def _xattn_fwd(q, kv, name, tq=512):
    S, D = q.shape
    M = kv.shape[0]
    hd = D // X_HEADS
    tq = min(tq, S)
    assert S % tq == 0, (S, tq)

    def body(q_ref, kv_ref, o_ref):
        for h in range(X_HEADS):
            cols = slice(h * hd, (h + 1) * hd)
            p = _xattn_probs(q_ref[:, cols], kv_ref[:, cols], hd)
            o_ref[:, cols] = _dot(p.astype(BF16), kv_ref[:, D + h * hd:D + (h + 1) * hd]).astype(BF16)

    row = BS((tq, D), lambda i: (i, 0))
    return pl.pallas_call(
        body, name=name, out_shape=SDS((S, D), BF16), grid=(S // tq,),
        in_specs=[row, BS((M, 2 * D), lambda i: (0, 0))], out_specs=row, compiler_params=_params(("parallel",), 40),
    )(q, kv)


def _xattn_bwd(q, kv, do, name, tq=512):
    S, D = q.shape
    M = kv.shape[0]
    hd = D // X_HEADS
    tq = min(tq, S)
    assert S % tq == 0, (S, tq)

    def body(q_ref, kv_ref, do_ref, dq_ref, dkv_ref):
        @pl.when(pl.program_id(0) == 0)
        def _():
            dkv_ref[...] = jnp.zeros_like(dkv_ref)

        for h in range(X_HEADS):
            cols = slice(h * hd, (h + 1) * hd)
            vcols = slice(D + h * hd, D + (h + 1) * hd)
            qh, kh, vh, doh = q_ref[:, cols], kv_ref[:, cols], kv_ref[:, vcols], do_ref[:, cols]
            p = _xattn_probs(qh, kh, hd)
            dp = _dot_nt(doh, vh)
            ds = (p * (dp - jnp.sum(p * dp, axis=-1, keepdims=True)) * (hd ** -0.5)).astype(BF16)
            dq_ref[:, cols] = _dot(ds, kh).astype(BF16)
            dkv_ref[:, cols] += _dot_tn(ds, qh)
            dkv_ref[:, vcols] += _dot_tn(p.astype(BF16), doh)

    row = BS((tq, D), lambda i: (i, 0))
    full = BS((M, 2 * D), lambda i: (0, 0))
    return pl.pallas_call(
        body, name=name, out_shape=[SDS((S, D), BF16), SDS((M, 2 * D), F32)], grid=(S // tq,),
        in_specs=[row, full, row], out_specs=[row, full], compiler_params=_params(("arbitrary",), 40),
    )(q, kv, do)


def _row_tile(rows, cap):
    best = 8
    for t in range(8, min(rows, cap) + 1, 8):
        if rows % t == 0:
            best = t
    assert rows % best == 0, (rows, cap)
    return best


ADAM_TILE_ELEMS = 256 * 1024


def _adamw(w, m, v, recv, name):
    R, C = w.shape
    slots = recv.shape[0]
    tr = _row_tile(R, max(8, ADAM_TILE_ELEMS // C))

    def body(w_ref, m_ref, v_ref, r_ref, g_ref, d_ref, nm_ref, nv_ref):
        g = r_ref[0].astype(F32)
        for s in range(1, slots):
            g = g + r_ref[s].astype(F32)
        mn = ADAM_B1 * m_ref[...] + (1.0 - ADAM_B1) * g
        vn = ADAM_B2 * v_ref[...] + (1.0 - ADAM_B2) * jnp.square(g)
        m_hat = mn / (1.0 - ADAM_B1 ** ADAM_STEP)
        v_hat = vn / (1.0 - ADAM_B2 ** ADAM_STEP)
        g_ref[...] = g
        d_ref[...] = -ADAM_LR * (m_hat / (jnp.sqrt(v_hat) + ADAM_EPS) + ADAM_WD * w_ref[...])
        nm_ref[...] = mn
        nv_ref[...] = vn

    row = BS((tr, C), lambda i: (i, 0))
    return pl.pallas_call(
        body, name=name, out_shape=[SDS((R, C), F32)] * 4, grid=(R // tr,),
        in_specs=[row, row, row, BS((slots, tr, C), lambda i: (0, i, 0))], out_specs=[row] * 4,
        compiler_params=_params(("parallel",), 40),
    )(w, m, v, recv)


def _cols_to_blocks(full):
    r, c = full.shape
    return full.reshape(r, N_DEV, c // N_DEV).transpose(1, 0, 2)


def _blocks_to_cols(blocks):
    n, r, c = blocks.shape
    return blocks.transpose(1, 0, 2).reshape(r, n * c)


def _pack(parts):
    flat = jnp.concatenate([p.reshape(-1).astype(F32) for p in parts])
    pad = (-flat.shape[0]) % (128 * 128)
    return jnp.pad(flat, (0, pad)).reshape(-1, 128)


def kernel(x, mem, g_ffn1, w1_gate, w1_up, w1_down, g_mix, w_in, g_v, w_s, b_s, sinks, g_a_out, g_b_out, w_out, g_x, g_mem, w_xq, w_xkv, w_xo, g_ffn2, w2_gate, w2_up, w2_down, g_final, loss_target, m_g_ffn1, m_w1_gate, m_w1_up, m_w1_down, m_g_mix, m_w_in, m_g_v, m_w_s, m_b_s, m_sinks, m_g_a_out, m_g_b_out, m_w_out, m_g_x, m_g_mem, m_w_xq, m_w_xkv, m_w_xo, m_g_ffn2, m_w2_gate, m_w2_up, m_w2_down, m_g_final, v_g_ffn1, v_w1_gate, v_w1_up, v_w1_down, v_g_mix, v_w_in, v_g_v, v_w_s, v_b_s, v_sinks, v_g_a_out, v_g_b_out, v_w_out, v_g_x, v_g_mem, v_w_xq, v_w_xkv, v_w_xo, v_g_ffn2, v_w2_gate, v_w2_up, v_w2_down, v_g_final):
    w = dict(g_ffn1=g_ffn1, w1_gate=w1_gate, w1_up=w1_up, w1_down=w1_down, g_mix=g_mix, w_in=w_in, g_v=g_v, w_s=w_s,
             b_s=b_s, sinks=sinks, g_a_out=g_a_out, g_b_out=g_b_out, w_out=w_out, g_x=g_x, g_mem=g_mem, w_xq=w_xq,
             w_xkv=w_xkv, w_xo=w_xo, g_ffn2=g_ffn2, w2_gate=w2_gate, w2_up=w2_up, w2_down=w2_down, g_final=g_final)
    mom = dict(g_ffn1=m_g_ffn1, w1_gate=m_w1_gate, w1_up=m_w1_up, w1_down=m_w1_down, g_mix=m_g_mix, w_in=m_w_in,
               g_v=m_g_v, w_s=m_w_s, b_s=m_b_s, sinks=m_sinks, g_a_out=m_g_a_out, g_b_out=m_g_b_out, w_out=m_w_out,
               g_x=m_g_x, g_mem=m_g_mem, w_xq=m_w_xq, w_xkv=m_w_xkv, w_xo=m_w_xo, g_ffn2=m_g_ffn2,
               w2_gate=m_w2_gate, w2_up=m_w2_up, w2_down=m_w2_down, g_final=m_g_final)
    var = dict(g_ffn1=v_g_ffn1, w1_gate=v_w1_gate, w1_up=v_w1_up, w1_down=v_w1_down, g_mix=v_g_mix, w_in=v_w_in,
               g_v=v_g_v, w_s=v_w_s, b_s=v_b_s, sinks=v_sinks, g_a_out=v_g_a_out, g_b_out=v_g_b_out, w_out=v_w_out,
               g_x=v_g_x, g_mem=v_g_mem, w_xq=v_w_xq, w_xkv=v_w_xkv, w_xo=v_w_xo, g_ffn2=v_g_ffn2,
               w2_gate=v_w2_gate, w2_up=v_w2_up, w2_down=v_w2_down, g_final=v_g_final)

    xs, ms, tgt = x[0], mem[0], loss_target[0]
    D = xs.shape[1]
    d_a = w_s.shape[1] * CHUNK
    d_b = D - d_a
    kvw = 2 * KV_HEADS * HEAD_DIM

    shard = {k: w[k][0].astype(BF16) for k in BIG}
    part = _run_exchange(_gather_exchange([shard[k] for k in FFN1_W], CHIP_PEERS), "ag_ffn1_ici")
    wg1, wu1, wd1 = _run_exchange(_sibling_exchange(part), "ag_ffn1_d2d")
    gf = g_final.reshape(1, D)
    ws, ws_t, bs_t = w_s[0], jnp.swapaxes(w_s[0], 1, 2), b_s[0].T
    sink_rows = jnp.repeat(sinks.reshape(-1), CHUNK).reshape(-1, 1)

    (h1, n1, a1, b1), part = _ffn_fwd(xs, g_ffn1, wg1, wu1, wd1, "ffn1_fwd",
                                      exchange=_gather_exchange([shard[k] for k in REST_W], CHIP_PEERS))
    part = dict(zip(REST_W, part))
    n2, (win_blocks,) = _rms_fwd(h1, g_mix, "mix_norm", exchange=_sibling_exchange([part["w_in"]]))
    win = _blocks_to_cols(win_blocks)
    z = _matmul(n2, win, "nn", F32, "mm_in", tn=win.shape[1] // 2)
    ya_n = _sgu_fwd(z, g_v, ws, bs_t, g_a_out, "sgu_fwd")
    later = [k for k in REST_W if k != "w_in"]
    (yb, yb_n), handed = _swa_fwd(z, sink_rows, g_b_out, "swa_fwd",
                                  exchange=_sibling_exchange([part[k] for k in later]))
    gathered = dict(zip(later, handed))
    wg2, wu2, wd2 = gathered["w2_gate"], gathered["w2_up"], gathered["w2_down"]
    wxkv = _blocks_to_cols(gathered["w_xkv"])
    wout = gathered["w_out"].reshape(D, D)
    wxq = gathered["w_xq"].reshape(D, D)
    wxo = gathered["w_xo"].reshape(D, D)
    y = jnp.concatenate([ya_n, yb_n], axis=1)
    h2 = _matmul(y, wout, "nn", F32, "mm_out", res=h1, tn=D)
    hx, _ = _rms_fwd(h2, g_x, "x_norm")
    mn, _ = _rms_fwd(ms, g_mem, "mem_norm")
    q = _matmul(hx, wxq, "nn", BF16, "mm_xq", tn=D)
    kv = _matmul(mn, wxkv, "nn", BF16, "mm_xkv")
    o = _xattn_fwd(q, kv, "xattn_fwd")
    h3 = _matmul(o, wxo, "nn", F32, "mm_xo", res=h2, tn=D)
    (h4, n4, a2, b2), _ = _ffn_fwd(h3, g_ffn2, wg2, wu2, wd2, "ffn2_fwd")
    dh4, dg_final, loss_part = _final_loss(h4, gf, tgt, "final_loss")

    grad_big, grad_small, recv_big = {}, {"g_final": dg_final}, {}
    order = _dw_block_order()
    (dn4, df2, da2, db2, s2), _ = _ffn_bwd_dx(dh4, a2, b2, wg2, wu2, wd2, "ffn2_bwd_dx")
    dh3, grad_small["g_ffn2"] = _rms_bwd(h3, g_ffn2, dn4, dh4, "ffn2_norm_bwd")
    (recv_big["w2_gate"], recv_big["w2_up"]), _ = _ffn_bwd_dw([(n4, da2), (n4, db2)], order, "ffn2_bwd_dw_gu")
    (recv_big["w2_down"],), _ = _ffn_bwd_dw([(s2, df2)], order, "ffn2_bwd_dw_d", tk=1024)

    do = _matmul(dh3, wxo, "nt", BF16, "mm_xo_dx", tn=D)
    grad_big["w_xo"] = _matmul(o, dh3, "tn", BF16, "mm_xo_dw", tm=D, tn=1024).reshape(N_DEV, D // N_DEV, D)
    dq, dkv = _xattn_bwd(q, kv, do, "xattn_bwd")
    dhx = _matmul(dq, wxq, "nt", BF16, "mm_xq_dx", tn=D)
    grad_big["w_xq"] = _matmul(hx, dq, "tn", BF16, "mm_xq_dw", tm=D, tn=1024).reshape(N_DEV, D // N_DEV, D)
    dmn = _matmul(dkv, wxkv, "nt", F32, "mm_xkv_dx")
    grad_big["w_xkv"] = _cols_to_blocks(_matmul(mn, dkv, "tn", BF16, "mm_xkv_dw", tm=1024, tn=1024))
    dh2, grad_small["g_x"] = _rms_bwd(h2, g_x, dhx, dh3, "x_norm_bwd")
    _, grad_small["g_mem"] = _rms_bwd(ms, g_mem, dmn, None, "mem_norm_bwd")

    dy = _matmul(dh2, wout, "nt", F32, "mm_out_dx", tn=D)
    grad_big["w_out"] = _matmul(y, dh2, "tn", BF16, "mm_out_dw", tm=D, tn=1024).reshape(N_DEV, D // N_DEV, D)
    dz_uv, grad_small["w_s"], dbs_t, grad_small["g_v"], grad_small["g_a_out"] = _sgu_bwd(
        z, dy, g_v, ws, ws_t, bs_t, g_a_out, "sgu_bwd")
    grad_small["b_s"] = dbs_t.T
    dq_b, dkv_b, grad_small["sinks"], grad_small["g_b_out"] = _swa_bwd(z, yb, dy, sink_rows, g_b_out, "swa_bwd")
    dz = jnp.concatenate([dz_uv, dq_b, dkv_b], axis=1)
    dn2 = _matmul(dz, win, "nt", BF16, "mm_in_dx", tn=1024)
    grad_big["w_in"] = _cols_to_blocks(_matmul(n2, dz, "tn", BF16, "mm_in_dw", tm=D, tn=dz.shape[1] // 2))
    dh1, grad_small["g_mix"] = _rms_bwd(h1, g_mix, dn2, dh2, "mix_norm_bwd")

    (dn1, df1, da1, db1, s1), recv_mid = _ffn_bwd_dx(
        dh1, a1, b1, wg1, wu1, wd1, "ffn1_bwd_dx", exchange=_scatter_exchange([grad_big[k] for k in MID_W]))
    recv_big.update(zip(MID_W, recv_mid))
    dx, grad_small["g_ffn1"] = _rms_bwd(xs, g_ffn1, dn1, dh1, "ffn1_norm_bwd")
    (recv_big["w1_gate"], recv_big["w1_up"]), (recv_small,) = _ffn_bwd_dw(
        [(n1, da1), (n1, db1)], order, "ffn1_bwd_dw_gu",
        exchange=_gather_exchange([_pack([grad_small[k] for k in SMALL])], ALL_PEERS))
    (recv_big["w1_down"],), _ = _ffn_bwd_dw([(s1, df1)], order, "ffn1_bwd_dw_d", tk=1024)


    grads, deltas, new_m, new_v = {}, {}, {}, {}
    for k in BIG:
        shp = w[k].shape
        two_d = shp[1:]
        outs = _adamw(w[k].reshape(two_d), mom[k].reshape(two_d), var[k].reshape(two_d), recv_big[k], "adamw_" + k)
        grads[k], deltas[k], new_m[k], new_v[k] = [t.reshape(shp) for t in outs]
    packed = _adamw(_pack([w[k] for k in SMALL]), _pack([mom[k] for k in SMALL]), _pack([var[k] for k in SMALL]),
                    recv_small, "adamw_small")
    off = 0
    for k in SMALL:
        shp = w[k].shape
        size = 1
        for s in shp:
            size *= s
        for dst, src in zip((grads, deltas, new_m, new_v), packed):
            dst[k] = src.reshape(-1)[off:off + size].reshape(shp)
        off += size

    loss = lax.psum(loss_part[0, 0], AXES)
    return (loss, dx[None], *[grads[k] for k in WEIGHTS], *[deltas[k] for k in WEIGHTS],
            *[new_m[k] for k in WEIGHTS], *[new_v[k] for k in WEIGHTS])
```

```python
import jax
import jax.numpy as jnp
from jax import lax
from jax.experimental import pallas as pl
from jax.experimental.pallas import tpu as pltpu

F32 = jnp.float32
BF16 = jnp.bfloat16
SDS = jax.ShapeDtypeStruct
BS = pl.BlockSpec

N_DEV = 8
AXES = ("x", "y", "c")
EPS = 1e-5
CHUNK = 128
HEAD_DIM = 64
KV_HEADS = 2
X_HEADS = 4
NEG = -1e30
ADAM_LR = 0.001
ADAM_B1 = 0.9
ADAM_B2 = 0.999
ADAM_EPS = 1e-08
ADAM_WD = 0.01
ADAM_STEP = 10
MIB = 1 << 20
FFN_SUB_ROWS = 256
WEIGHTS = ['g_ffn1', 'w1_gate', 'w1_up', 'w1_down', 'g_mix', 'w_in', 'g_v', 'w_s', 'b_s', 'sinks', 'g_a_out',
           'g_b_out', 'w_out', 'g_x', 'g_mem', 'w_xq', 'w_xkv', 'w_xo', 'g_ffn2', 'w2_gate', 'w2_up', 'w2_down',
           'g_final']
BIG = ['w1_gate', 'w1_up', 'w1_down', 'w_in', 'w_out', 'w_xq', 'w_xkv', 'w_xo', 'w2_gate', 'w2_up', 'w2_down']
SMALL = [w for w in WEIGHTS if w not in BIG]
FFN1_W = ['w1_gate', 'w1_up', 'w1_down']
FFN2_W = ['w2_gate', 'w2_up', 'w2_down']
MID_W = ['w_in', 'w_out', 'w_xq', 'w_xkv', 'w_xo']
REST_W = MID_W + FFN2_W


def _params(sem=None, vmem_mib=48):
    return pltpu.CompilerParams(dimension_semantics=sem, vmem_limit_bytes=vmem_mib * MIB)


def _dot(a, b):
    return jnp.dot(a, b, preferred_element_type=F32)


def _dot_nt(a, b):
    return lax.dot_general(a, b, (((1,), (1,)), ((), ())), preferred_element_type=F32)


def _dot_tn(a, b):
    return lax.dot_general(a, b, (((0,), (0,)), ((), ())), preferred_element_type=F32)


def _rstd(v):
    return lax.rsqrt(jnp.mean(v * v, axis=-1, keepdims=True) + EPS)


def _norm_bwd(xhat, r, g, d):
    t = d * g
    return r * (t - xhat * jnp.mean(t * xhat, axis=-1, keepdims=True))


def _gelu(v):
    return 0.5 * v * (1.0 + lax.erf(v * 0.7071067811865476))


def _gelu_grad(v):
    return 0.5 * (1.0 + lax.erf(v * 0.7071067811865476)) + v * jnp.exp(-0.5 * v * v) * 0.3989422804014327


def _mesh_pos():
    x, y, c = lax.axis_index("x"), lax.axis_index("y"), lax.axis_index("c")
    return x, y, c, 4 * x + 2 * y + c


def _peer(x, y, c, k):
    px = 1 - x if k & 4 else x
    py = 1 - y if k & 2 else y
    pc = 1 - c if k & 1 else c
    return (px, py, pc), 4 * px + 2 * py + pc


ANY = BS(memory_space=pl.ANY)
DMA_SEM = pltpu.SemaphoreType.DMA
ALL_PEERS = (1, 2, 3, 4, 5, 6, 7)
CHIP_PEERS = (2, 4, 6)
SIBLING = 1


def _remote(src, dst, send, recv, peer):
    return pltpu.make_async_remote_copy(src_ref=src, dst_ref=dst, send_sem=send, recv_sem=recv, device_id=peer,
                                        device_id_type=pl.DeviceIdType.MESH)


class _Exchange:
    def __init__(self, ins, out_shapes, sems, copies, aliases=None):
        self.ins, self.out_shapes, self.sems, self.copies, self.aliases = ins, out_shapes, sems, copies, aliases or {}


def _gather_exchange(shards, peers):
    n, m = len(shards), len(peers)

    def copies(ins, outs, sems):
        send, recv, lsem = sems
        x, y, c, me = _mesh_pos()
        cps = [pltpu.make_async_copy(ins[w], outs[w].at[me], lsem.at[w]) for w in range(n)]
        for w in range(n):
            for j, k in enumerate(peers):
                cps.append(_remote(ins[w], outs[w].at[me], send.at[w, j], recv.at[w, j], _peer(x, y, c, k)[0]))
        return cps

    return _Exchange(shards, [SDS((N_DEV,) + s.shape, s.dtype) for s in shards],
                     [DMA_SEM((n, m)), DMA_SEM((n, m)), DMA_SEM((n,))], copies)


def _sibling_exchange(gathered):
    n = len(gathered)

    def copies(ins, outs, sems):
        send, recv = sems
        x, y, c, me = _mesh_pos()
        slots = [me] + [_peer(x, y, c, k)[1] for k in CHIP_PEERS]
        sib = _peer(x, y, c, SIBLING)[0]
        return [_remote(outs[w].at[s], outs[w].at[s], send.at[w, j], recv.at[w, j], sib)
                for w in range(n) for j, s in enumerate(slots)]

    return _Exchange(gathered, [SDS(g.shape, g.dtype) for g in gathered], [DMA_SEM((n, 4)), DMA_SEM((n, 4))], copies,
                     aliases={w: w for w in range(n)})


def _scatter_exchange(fulls):
    n, m = len(fulls), len(ALL_PEERS)

    def copies(ins, outs, sems):
        send, recv, lsem = sems
        x, y, c, me = _mesh_pos()
        cps = [pltpu.make_async_copy(ins[w].at[me], outs[w].at[me], lsem.at[w]) for w in range(n)]
        for w in range(n):
            for j, k in enumerate(ALL_PEERS):
                peer, p = _peer(x, y, c, k)
                cps.append(_remote(ins[w].at[p], outs[w].at[me], send.at[w, j], recv.at[w, j], peer))
        return cps

    return _Exchange(fulls, [SDS(f.shape, f.dtype) for f in fulls],
                     [DMA_SEM((n, m)), DMA_SEM((n, m)), DMA_SEM((n,))], copies)


def _call(body, *, name, args, in_specs, out_shape, out_specs, grid, sem, vmem_mib, scratch=(), exchange=None,
          first=None, last=None):
    if exchange is None:
        return pl.pallas_call(body, name=name, out_shape=out_shape, grid=grid, in_specs=in_specs,
                              out_specs=out_specs, scratch_shapes=list(scratch),
                              compiler_params=_params(sem, vmem_mib))(*args), []
    ni, no, ns = len(args), len(out_shape), len(scratch)
    xi, xo = len(exchange.ins), len(exchange.out_shapes)

    def hosted(*refs):
        own_in, refs = refs[:ni], refs[ni:]
        x_in, refs = refs[:xi], refs[xi:]
        own_out, refs = refs[:no], refs[no:]
        x_out, refs = refs[:xo], refs[xo:]
        own_scr, x_sem = refs[:ns], refs[ns:]

        @pl.when(first())
        def _():
            for cp in exchange.copies(x_in, x_out, x_sem):
                cp.start()

        body(*own_in, *own_out, *own_scr)

        @pl.when(last())
        def _():
            for cp in exchange.copies(x_in, x_out, x_sem):
                cp.wait()

    outs = pl.pallas_call(
        hosted, name=name, out_shape=list(out_shape) + list(exchange.out_shapes), grid=grid,
        in_specs=list(in_specs) + [ANY] * xi, out_specs=list(out_specs) + [ANY] * xo,
        scratch_shapes=list(scratch) + list(exchange.sems),
        input_output_aliases={ni + a: no + b for a, b in exchange.aliases.items()},
        compiler_params=pltpu.CompilerParams(dimension_semantics=tuple("arbitrary" for _ in grid),
                                             vmem_limit_bytes=vmem_mib * MIB, has_side_effects=True),
    )(*args, *exchange.ins)
    return outs[:no], outs[no:]


def _gather_two_level(shards, name):
    n, m = len(shards), len(CHIP_PEERS)

    def body(*refs):
        ins, outs = refs[:n], refs[n:2 * n]
        send, recv, lsem = refs[2 * n:]
        x, y, c, me = _mesh_pos()
        sib = _peer(x, y, c, SIBLING)[0]
        chips = [_peer(x, y, c, k) for k in CHIP_PEERS]

        def own(w, j):
            return _remote(ins[w], outs[w].at[me], send.at[w, j], recv.at[w, j], sib if j == 0 else chips[j - 1][0])

        def landed(w, j):
            return _remote(ins[w], outs[w].at[chips[j][1]], send.at[w, 1 + j], recv.at[w, 1 + j], chips[j][0])

        def handed(w, j, mine):
            slot = chips[j][1] if mine else jnp.bitwise_xor(chips[j][1], 1)
            return _remote(outs[w].at[slot], outs[w].at[slot], send.at[w, 1 + m + j], recv.at[w, 1 + m + j], sib)

        local = [pltpu.make_async_copy(ins[w], outs[w].at[me], lsem.at[w]) for w in range(n)]
        for w in range(n):
            local[w].start()
            for j in range(1 + m):
                own(w, j).start()
        for w in range(n):
            for j in range(m):
                landed(w, j).wait_recv()
                handed(w, j, True).start()
        for w in range(n):
            own(w, 0).wait_recv()
            for j in range(m):
                handed(w, j, False).wait_recv()
                handed(w, j, True).wait_send()
            for j in range(1 + m):
                own(w, j).wait_send()
            local[w].wait()

    k = 1 + 2 * m
    return pl.pallas_call(
        body, name=name, out_shape=[SDS((N_DEV,) + s.shape, s.dtype) for s in shards],
        in_specs=[ANY] * n, out_specs=[ANY] * n,
        scratch_shapes=[DMA_SEM((n, k)), DMA_SEM((n, k)), DMA_SEM((n,))],
        compiler_params=pltpu.CompilerParams(has_side_effects=True),
    )(*shards)


def _run_exchange(exchange, name):
    xi, xo = len(exchange.ins), len(exchange.out_shapes)

    def body(*refs):
        cps = exchange.copies(refs[:xi], refs[xi:xi + xo], refs[xi + xo:])
        for cp in cps:
            cp.start()
        for cp in cps:
            cp.wait()

    return pl.pallas_call(
        body, name=name, out_shape=list(exchange.out_shapes), in_specs=[ANY] * xi, out_specs=[ANY] * xo,
        scratch_shapes=list(exchange.sems), input_output_aliases=dict(exchange.aliases),
        compiler_params=pltpu.CompilerParams(has_side_effects=True),
    )(*exchange.ins)


def _rms_fwd(h, g, name, tm=512, exchange=None):
    S, D = h.shape
    tm = min(tm, S)
    assert S % tm == 0, (S, tm)
    ni = S // tm

    def body(h_ref, g_ref, o_ref):
        hv = h_ref[...]
        o_ref[...] = (hv * _rstd(hv) * g_ref[...]).astype(o_ref.dtype)

    (out,), extra = _call(
        body, name=name, args=(h, g), out_shape=[SDS((S, D), BF16)], grid=(ni,),
        in_specs=[BS((tm, D), lambda i: (i, 0)), BS((1, D), lambda i: (0, 0))],
        out_specs=[BS((tm, D), lambda i: (i, 0))], sem=("parallel",), vmem_mib=32, exchange=exchange,
        first=lambda: pl.program_id(0) == 0, last=lambda: pl.program_id(0) == ni - 1)
    return out, extra


def _rms_bwd(h, g, dn, dres, name, tm=256):
    S, D = h.shape
    tm = min(tm, S)
    assert S % tm == 0, (S, tm)
    has_res = dres is not None

    def body(*refs):
        if has_res:
            h_ref, g_ref, dn_ref, dres_ref, dh_ref, dg_ref = refs
        else:
            h_ref, g_ref, dn_ref, dh_ref, dg_ref = refs
        hv = h_ref[...]
        r = _rstd(hv)
        xh = hv * r
        d = dn_ref[...].astype(F32)

        @pl.when(pl.program_id(0) == 0)
        def _():
            dg_ref[...] = jnp.zeros_like(dg_ref)

        dg_ref[...] += jnp.sum(d * xh, axis=0, keepdims=True)
        dh = _norm_bwd(xh, r, g_ref[...], d)
        dh_ref[...] = dres_ref[...] + dh if has_res else dh

    row = BS((tm, D), lambda i: (i, 0))
    vec = BS((1, D), lambda i: (0, 0))
    args = (h, g, dn) + ((dres,) if has_res else ())
    return pl.pallas_call(
        body, name=name, out_shape=[SDS((S, D), F32), SDS((1, D), F32)], grid=(S // tm,),
        in_specs=[row, vec, row] + ([row] if has_res else []), out_specs=[row, vec],
        compiler_params=_params(("arbitrary",), 40),
    )(*args)


def _final_loss(h, g, target, name, tm=256):
    S, D = h.shape
    tm = min(tm, S)
    assert S % tm == 0, (S, tm)

    def body(h_ref, g_ref, t_ref, dh_ref, dg_ref, loss_ref):
        hv = h_ref[...]
        r = _rstd(hv)
        xh = hv * r
        gv = g_ref[...]
        e = xh * gv - t_ref[...]

        @pl.when(pl.program_id(0) == 0)
        def _():
            dg_ref[...] = jnp.zeros_like(dg_ref)
            loss_ref[...] = jnp.zeros_like(loss_ref)

        loss_ref[...] += 0.5 * jnp.sum(jnp.mean(e * e, axis=-1, keepdims=True), axis=0, keepdims=True)
        dy = e * (1.0 / D)
        dg_ref[...] += jnp.sum(dy * xh, axis=0, keepdims=True)
        dh_ref[...] = _norm_bwd(xh, r, gv, dy)

    row = BS((tm, D), lambda i: (i, 0))
    vec = BS((1, D), lambda i: (0, 0))
    return pl.pallas_call(
        body, name=name, out_shape=[SDS((S, D), F32), SDS((1, D), F32), SDS((1, 128), F32)], grid=(S // tm,),
        in_specs=[row, vec, row], out_specs=[row, vec, BS((1, 128), lambda i: (0, 0))],
        compiler_params=_params(("arbitrary",), 40),
    )(h, g, target)


def _matmul(a, b, mode, out_dtype, name, res=None, tm=512, tn=512, tk=512):
    if mode == "tn":
        K, M = a.shape
        N = b.shape[1]
        tm, tn, tk = min(tm, M), min(tn, N), min(tk, K)
        assert M % tm == 0 and N % tn == 0 and K % tk == 0, (a.shape, b.shape, tm, tn, tk)
        nk = K // tk

        def body(a_ref, b_ref, o_ref, acc_ref):
            k = pl.program_id(2)

            @pl.when(k == 0)
            def _():
                acc_ref[...] = jnp.zeros_like(acc_ref)

            acc_ref[...] += _dot_tn(a_ref[...].astype(BF16), b_ref[...].astype(BF16))

            @pl.when(k == nk - 1)
            def _():
                o_ref[...] = acc_ref[...].astype(o_ref.dtype)

        return pl.pallas_call(
            body, name=name, out_shape=SDS((M, N), out_dtype), grid=(M // tm, N // tn, nk),
            in_specs=[BS((tk, tm), lambda i, j, k: (k, i)), BS((tk, tn), lambda i, j, k: (k, j))],
            out_specs=BS((tm, tn), lambda i, j, k: (i, j)), scratch_shapes=[pltpu.VMEM((tm, tn), F32)],
            compiler_params=_params(("parallel", "parallel", "arbitrary"), 48),
        )(a, b)

    M, K = a.shape
    N = b.shape[1] if mode == "nn" else b.shape[0]
    tm, tn = min(tm, M), min(tn, N)
    assert M % tm == 0 and N % tn == 0, (a.shape, b.shape, tm, tn)
    has_res = res is not None

    def body(*refs):
        if has_res:
            a_ref, b_ref, r_ref, o_ref = refs
        else:
            a_ref, b_ref, o_ref = refs
        av, bv = a_ref[...].astype(BF16), b_ref[...].astype(BF16)
        acc = _dot(av, bv) if mode == "nn" else _dot_nt(av, bv)
        if has_res:
            acc = acc + r_ref[...]
        o_ref[...] = acc.astype(o_ref.dtype)

    b_spec = BS((K, tn), lambda i, j: (0, j)) if mode == "nn" else BS((tn, K), lambda i, j: (j, 0))
    o_spec = BS((tm, tn), lambda i, j: (i, j))
    return pl.pallas_call(
        body, name=name, out_shape=SDS((M, N), out_dtype), grid=(M // tm, N // tn),
        in_specs=[BS((tm, K), lambda i, j: (i, 0)), b_spec] + ([o_spec] if has_res else []), out_specs=o_spec,
        compiler_params=_params(("parallel", "parallel"), 48),
    )(*((a, b) + ((res,) if has_res else ())))


def _ffn_fwd(h, g, wg, wu, wd, name, tm=512, exchange=None):
    S, D = h.shape
    nb, _, Fs = wg.shape
    tm = min(tm, S)
    sub = min(FFN_SUB_ROWS, tm)
    assert S % tm == 0 and tm % sub == 0, (S, tm, sub)

    def body(h_ref, g_ref, wg_ref, wu_ref, wd_ref, o_ref, n_ref, a_ref, b_ref):
        j = pl.program_id(1)

        @pl.when(j == 0)
        def _():
            hv = h_ref[...]
            n_ref[...] = (hv * _rstd(hv) * g_ref[...]).astype(BF16)
            o_ref[...] = jnp.zeros_like(o_ref)

        for r in range(0, tm, sub):
            rows = slice(r, r + sub)
            n = n_ref[rows, :]
            a = _dot(n, wg_ref[...]).astype(BF16)
            b = _dot(n, wu_ref[...]).astype(BF16)
            a_ref[rows, :] = a
            b_ref[rows, :] = b
            a, b = a.astype(F32), b.astype(F32)
            s = (a * jax.nn.sigmoid(a) * b).astype(BF16)
            o_ref[rows, :] += _dot(s, wd_ref[...])

        @pl.when(j == nb - 1)
        def _():
            o_ref[...] = h_ref[...] + 0.5 * o_ref[...]

    row = BS((tm, D), lambda i, j: (i, 0))
    wcol = BS((None, D, Fs), lambda i, j: (j, 0, 0))
    act = BS((None, tm, Fs), lambda i, j: (j, i, 0))
    ni = S // tm
    return _call(
        body, name=name, args=(h, g, wg, wu, wd),
        out_shape=[SDS((S, D), F32), SDS((S, D), BF16), SDS((nb, S, Fs), BF16), SDS((nb, S, Fs), BF16)],
        grid=(ni, nb),
        in_specs=[row, BS((1, D), lambda i, j: (0, 0)), wcol, wcol, BS((None, Fs, D), lambda i, j: (j, 0, 0))],
        out_specs=[row, row, act, act], sem=("parallel", "arbitrary"), vmem_mib=56, exchange=exchange,
        first=lambda: (pl.program_id(0) == 0) & (pl.program_id(1) == 0),
        last=lambda: (pl.program_id(0) == ni - 1) & (pl.program_id(1) == nb - 1))


def _ffn_bwd_dx(dh, a, b, wg, wu, wd, name, tm=512, exchange=None):
    S, D = dh.shape
    nb, _, Fs = wg.shape
    tm = min(tm, S)
    assert S % tm == 0, (S, tm)

    def body(dh_ref, a_ref, b_ref, wg_ref, wu_ref, wd_ref, dn_ref, df_ref, da_ref, db_ref, s_ref, ds_ref):
        j = pl.program_id(1)

        def first_matmul():
            ds_ref[...] = _dot_nt(df_ref[...], wd_ref[...])

        def pre_activation_cotangents():
            ds = ds_ref[...]
            av, bv = a_ref[...].astype(F32), b_ref[...].astype(F32)
            sig = jax.nn.sigmoid(av)
            sl = av * sig
            da = (ds * bv * (sig * (1.0 + av * (1.0 - sig)))).astype(BF16)
            db = (ds * sl).astype(BF16)
            da_ref[...] = da
            db_ref[...] = db
            s_ref[...] = (sl * bv).astype(BF16)
            return da, db

        def last_matmuls(da, db):
            dn_ref[...] += _dot_nt(da, wg_ref[...]) + _dot_nt(db, wu_ref[...])

        @pl.when(j == 0)
        def _():
            df_ref[...] = (0.5 * dh_ref[...]).astype(BF16)
            dn_ref[...] = jnp.zeros_like(dn_ref)
            first_matmul()

        @pl.when((j > 0) & (j < nb))
        def _():
            da, db = pre_activation_cotangents()
            first_matmul()
            last_matmuls(da, db)

        @pl.when(j == nb)
        def _():
            last_matmuls(*pre_activation_cotangents())

    row = BS((tm, D), lambda i, j: (i, 0))
    prev = BS((None, D, Fs), lambda i, j: (jnp.maximum(j - 1, 0), 0, 0))
    act = BS((None, tm, Fs), lambda i, j: (jnp.maximum(j - 1, 0), i, 0))
    ni = S // tm
    return _call(
        body, name=name, args=(dh, a, b, wg, wu, wd),
        out_shape=[SDS((S, D), F32), SDS((S, D), BF16)] + [SDS((nb, S, Fs), BF16)] * 3,
        grid=(ni, nb + 1),
        in_specs=[row, act, act, prev, prev, BS((None, Fs, D), lambda i, j: (jnp.minimum(j, nb - 1), 0, 0))],
        out_specs=[row, row, act, act, act], scratch=[pltpu.VMEM((tm, Fs), F32)],
        sem=("parallel", "arbitrary"), vmem_mib=56, exchange=exchange,
        first=lambda: (pl.program_id(0) == 0) & (pl.program_id(1) == 0),
        last=lambda: (pl.program_id(0) == ni - 1) & (pl.program_id(1) == nb))


DW_FLIPS = {0: (6, 2, 4, 0), 1: (6, 4, 2, 0)}
N_CHIPS = N_DEV // 2


def _dw_block_order():
    x, y, c, me = _mesh_pos()
    steps = [jnp.array([v for mine, sib in zip(DW_FLIPS[core], DW_FLIPS[1 - core]) for v in (sib ^ 1, mine)], jnp.int32)
             for core in (0, 1)]
    return jnp.bitwise_xor(me.astype(jnp.int32), jnp.where(c == 0, steps[0], steps[1]))


def _ffn_bwd_dw(pairs, order, name, tk=512, exchange=None):
    npair = len(pairs)
    S = pairs[0][0].shape[-2]
    nb, half = N_DEV, N_DEV // 2
    tk = min(tk, S)
    assert S % tk == 0, (S, tk)
    nk = S // tk
    shapes = [(lhs.shape[-1], rhs.shape[-1]) for lhs, rhs in pairs]
    xi = len(exchange.ins) if exchange else 0
    xo = len(exchange.out_shapes) if exchange else 0

    def body(order_ref, *rest):
        tiles, rest = rest[:2 * npair], rest[2 * npair:]
        x_in, rest = rest[:xi], rest[xi:]
        recv_bufs, rest = rest[:npair], rest[npair:]
        x_out, rest = rest[:xo], rest[xo:]
        accs, rest = rest[:npair], rest[npair:]
        out_t, rest = rest[:npair], rest[npair:]
        out_m, rest = rest[:npair], rest[npair:]
        land, rest = rest[:npair], rest[npair:]
        (send_t, recv_t, send_m, recv_m, lsem, credit), x_sem = rest[:6], rest[6:]
        t, k = pl.program_id(0), pl.program_id(1)
        x, y, c, me = _mesh_pos()
        sibling = (x, y, 1 - c)
        chip = 2 * x + y

        if exchange:
            @pl.when((t == 0) & (k == 0))
            def _():
                for cp in exchange.copies(x_in, x_out, x_sem):
                    cp.start()

        @pl.when(k == 0)
        def _():
            for acc in accs:
                acc[...] = jnp.zeros_like(acc)

        for w in range(npair):
            accs[w][...] += _dot_tn(tiles[2 * w][...], tiles[2 * w + 1][...])

        def to_sibling(w, i):
            return _remote(out_t[w], land[w], send_t.at[w, i], recv_t.at[w, i], sibling)

        def to_owner(w, i):
            dst = recv_bufs[w].at[chip]
            if i == half - 1:
                return pltpu.make_async_copy(out_m[w], dst, lsem.at[w])
            p = jnp.bitwise_xor(me, jnp.where(c == 0, DW_FLIPS[0][i], DW_FLIPS[1][i]))
            return _remote(out_m[w], dst, send_m.at[w, i], recv_m.at[w, i], (p >> 2, (p >> 1) & 1, p & 1))

        for i in range(half):
            @pl.when((t == 2 * i) & (k == nk - 1))
            def _(i=i):
                if i >= 1:
                    pl.semaphore_wait(credit, 1)
                for w in range(npair):
                    if i >= 1:
                        to_sibling(w, i - 1).wait_send()
                    out_t[w][...] = accs[w][...].astype(BF16)
                    to_sibling(w, i).start()

            @pl.when((t == 2 * i + 1) & (k == nk - 1))
            def _(i=i):
                for w in range(npair):
                    to_sibling(w, i).wait_recv()
                    if i >= 1:
                        to_owner(w, i - 1).wait_send()
                    out_m[w][...] = (accs[w][...] + land[w][...].astype(F32)).astype(BF16)
                if i < half - 1:
                    pl.semaphore_signal(credit, inc=1, device_id=sibling, device_id_type=pl.DeviceIdType.MESH)
                for w in range(npair):
                    to_owner(w, i).start()

        @pl.when((t == nb - 1) & (k == nk - 1))
        def _():
            for w in range(npair):
                to_sibling(w, half - 1).wait_send()
                to_owner(w, half - 1).wait()
                for i in range(half - 1):
                    to_owner(w, i).wait_recv()
            if exchange:
                for cp in exchange.copies(x_in, x_out, x_sem):
                    cp.wait()

    def tile_spec(arr):
        if arr.ndim == 3:
            return BS((None, tk, arr.shape[-1]), lambda t, k, o: (o[t], k, 0))
        return BS((tk, arr.shape[-1]), lambda t, k, o: (k, 0))

    flat = [a for pair in pairs for a in pair]
    grid_spec = pltpu.PrefetchScalarGridSpec(
        num_scalar_prefetch=1, grid=(nb, nk),
        in_specs=[tile_spec(a) for a in flat] + [ANY] * xi, out_specs=[ANY] * (npair + xo),
        scratch_shapes=[pltpu.VMEM(s, F32) for s in shapes] + [pltpu.VMEM(s, BF16) for s in shapes] * 3
        + [DMA_SEM((npair, half)), DMA_SEM((npair, half)), DMA_SEM((npair, half - 1)), DMA_SEM((npair, half - 1)),
           DMA_SEM((npair,)), pltpu.SemaphoreType.REGULAR]
        + (list(exchange.sems) if exchange else []))
    outs = pl.pallas_call(
        body, name=name, grid_spec=grid_spec,
        out_shape=[SDS((N_CHIPS,) + s, BF16) for s in shapes] + (list(exchange.out_shapes) if exchange else []),
        compiler_params=pltpu.CompilerParams(dimension_semantics=("arbitrary", "arbitrary"),
                                             vmem_limit_bytes=54 * MIB, has_side_effects=True),
    )(order, *flat, *(exchange.ins if exchange else ()))
    return outs[:npair], outs[npair:]


def _sgu_parts(z, gv, ws_ref, bst_ref, groups):
    da = z.shape[1] // 2
    zu, zv = z[:, :da], z[:, da:]
    u, v = _gelu(zu), _gelu(zv)
    rv = _rstd(v)
    vhat = v * rv
    vn = (vhat * gv).astype(BF16)
    tri = lax.broadcasted_iota(jnp.int32, (CHUNK, CHUNK), 0) >= lax.broadcasted_iota(jnp.int32, (CHUNK, CHUNK), 1)
    pieces = []
    for g in range(groups):
        w = jnp.where(tri, ws_ref[g], 0.0).astype(BF16)
        pieces.append(_dot(w, vn[:, g * CHUNK:(g + 1) * CHUNK]) + bst_ref[:, g:g + 1])
    sv = jnp.concatenate(pieces, axis=1)
    return dict(zu=zu, zv=zv, u=u, rv=rv, vhat=vhat, vn=vn, sv=sv, tri=tri)


SGU_CHUNKS = 2


def _sgu_fwd(z, g_v, w_s, b_st, g_a, name):
    S = z.shape[0]
    groups = w_s.shape[0]
    da = groups * CHUNK
    rows = SGU_CHUNKS * CHUNK
    assert S % rows == 0, (S, rows)

    def body(z_ref, gv_ref, ws_ref, bst_ref, ga_ref, o_ref):
        for ch in range(SGU_CHUNKS):
            blk = slice(ch * CHUNK, (ch + 1) * CHUNK)
            p = _sgu_parts(z_ref[blk, :], gv_ref[...], ws_ref, bst_ref, groups)
            ya = p["u"] * p["sv"]
            o_ref[blk, :] = (ya * _rstd(ya) * ga_ref[...]).astype(BF16)

    vec = BS((1, da), lambda i: (0, 0))
    return pl.pallas_call(
        body, name=name, out_shape=SDS((S, da), BF16), grid=(S // rows,),
        in_specs=[BS((rows, 2 * da), lambda i: (i, 0)), vec, BS((groups, CHUNK, CHUNK), lambda i: (0, 0, 0)),
                  BS((CHUNK, groups), lambda i: (0, 0)), vec],
        out_specs=BS((rows, da), lambda i: (i, 0)), compiler_params=_params(("parallel",), 32),
    )(z, g_v, w_s, b_st, g_a)


def _sgu_bwd(z, dy, g_v, w_s, w_st, b_st, g_a, name):
    S = z.shape[0]
    groups = w_s.shape[0]
    da = groups * CHUNK
    rows = SGU_CHUNKS * CHUNK
    assert S % rows == 0, (S, rows)

    def body(z_ref, dy_ref, gv_ref, ws_ref, wst_ref, bst_ref, ga_ref, dz_ref, dws_ref, dbst_ref, dgv_ref, dga_ref):
        @pl.when(pl.program_id(0) == 0)
        def _():
            dws_ref[...] = jnp.zeros_like(dws_ref)
            dbst_ref[...] = jnp.zeros_like(dbst_ref)
            dgv_ref[...] = jnp.zeros_like(dgv_ref)
            dga_ref[...] = jnp.zeros_like(dga_ref)

        gv = gv_ref[...]
        tri_t = (lax.broadcasted_iota(jnp.int32, (CHUNK, CHUNK), 0)
                 <= lax.broadcasted_iota(jnp.int32, (CHUNK, CHUNK), 1))
        lane = lax.broadcasted_iota(jnp.int32, (CHUNK, groups), 1)
        for ch in range(SGU_CHUNKS):
            blk = slice(ch * CHUNK, (ch + 1) * CHUNK)
            p = _sgu_parts(z_ref[blk, :], gv, ws_ref, bst_ref, groups)
            u, sv, tri = p["u"], p["sv"], p["tri"]
            ya = u * sv
            ra = _rstd(ya)
            yhat = ya * ra
            d = dy_ref[blk, :]
            dga_ref[...] += jnp.sum(d * yhat, axis=0, keepdims=True)
            dya = _norm_bwd(yhat, ra, ga_ref[...], d)
            du = dya * sv
            dsv = dya * u
            dsv_b = dsv.astype(BF16)
            dvn = []
            dbs = jnp.zeros((CHUNK, groups), F32)
            for g in range(groups):
                cols = slice(g * CHUNK, (g + 1) * CHUNK)
                dbs = dbs + jnp.where(lane == g, jnp.sum(dsv[:, cols], axis=1, keepdims=True), 0.0)
                dws_ref[g] += jnp.where(tri, _dot_nt(dsv_b[:, cols], p["vn"][:, cols]), 0.0)
                wt = jnp.where(tri_t, wst_ref[g], 0.0).astype(BF16)
                dvn.append(_dot(wt, dsv_b[:, cols]))
            dbst_ref[...] += dbs
            dvn = jnp.concatenate(dvn, axis=1)
            dgv_ref[...] += jnp.sum(dvn * p["vhat"], axis=0, keepdims=True)
            dv = _norm_bwd(p["vhat"], p["rv"], gv, dvn)
            dz_ref[blk, :] = jnp.concatenate([du * _gelu_grad(p["zu"]), dv * _gelu_grad(p["zv"])], axis=1)

    vec = BS((1, da), lambda i: (0, 0))
    wsq = BS((groups, CHUNK, CHUNK), lambda i: (0, 0, 0))
    bsq = BS((CHUNK, groups), lambda i: (0, 0))
    return pl.pallas_call(
        body, name=name,
        out_shape=[SDS((S, 2 * da), F32), SDS((groups, CHUNK, CHUNK), F32), SDS((CHUNK, groups), F32),
                   SDS((1, da), F32), SDS((1, da), F32)],
        grid=(S // rows,),
        in_specs=[BS((rows, 2 * da), lambda i: (i, 0)), BS((rows, da), lambda i: (i, 0)), vec, wsq, wsq, bsq, vec],
        out_specs=[BS((rows, 2 * da), lambda i: (i, 0)), wsq, bsq, vec, vec],
        compiler_params=_params(("arbitrary",), 32),
    )(z, dy, g_v, w_s, w_st, b_st, g_a)


def _swa_mask(i, group):
    row = lax.broadcasted_iota(jnp.int32, (group * CHUNK, 2 * CHUNK), 0) & (CHUNK - 1)
    col = lax.broadcasted_iota(jnp.int32, (group * CHUNK, 2 * CHUNK), 1)
    d = row + CHUNK - col
    return (d >= 0) & (d < CHUNK) & jnp.logical_or(i > 0, col >= CHUNK)


def _stack_heads(t, g, group):
    return jnp.concatenate([t[:, h * HEAD_DIM:(h + 1) * HEAD_DIM] for h in range(g * group, (g + 1) * group)], axis=0)


def _unstack_heads(stacked, group):
    return [stacked[h * CHUNK:(h + 1) * CHUNK] for h in range(group)]


def _swa_probs(qh, kh, sink, mask):
    s = jnp.where(mask, _dot_nt(qh, kh) * (HEAD_DIM ** -0.5), NEG)
    m = jnp.maximum(jnp.max(s, axis=-1, keepdims=True), sink)
    e = jnp.exp(s - m)
    es = jnp.exp(sink - m)
    inv = 1.0 / (jnp.sum(e, axis=-1, keepdims=True) + es)
    return e * inv, es * inv


SWA_QBLOCKS = 2


def _swa_specs(db, npair, clamp):
    kvw = 2 * KV_HEADS * HEAD_DIM
    cur = (lambda p: jnp.minimum(p, npair - 1)) if clamp else (lambda p: p)
    rows = SWA_QBLOCKS * CHUNK
    q_spec = BS((rows, db), lambda p: (cur(p), 2))
    kc_spec = BS((rows, kvw), lambda p: (cur(p), 3 * db // kvw))
    kp_spec = BS((CHUNK, kvw), lambda p: (jnp.maximum(SWA_QBLOCKS * cur(p) - 1, 0), 3 * db // kvw))
    return q_spec, kc_spec, kp_spec


def _swa_keys(kp_ref, kc_ref, qb):
    if qb == 0:
        return jnp.concatenate([kp_ref[...], kc_ref[:CHUNK, :]], axis=0).astype(BF16)
    return kc_ref[(qb - 1) * CHUNK:(qb + 1) * CHUNK, :].astype(BF16)


def _swa_fwd(z, sink_rows, g_b, name, exchange=None):
    S = z.shape[0]
    db = g_b.shape[1]
    heads = db // HEAD_DIM
    group = heads // KV_HEADS
    rows = SWA_QBLOCKS * CHUNK
    assert S % rows == 0, (S, rows)
    npair = S // rows

    def body(q_ref, kc_ref, kp_ref, sk_ref, gb_ref, yb_ref, ybn_ref):
        for qb in range(SWA_QBLOCKS):
            blk = slice(qb * CHUNK, (qb + 1) * CHUNK)
            mask = _swa_mask(SWA_QBLOCKS * pl.program_id(0) + qb, group)
            q = q_ref[blk, :].astype(BF16)
            kv = _swa_keys(kp_ref, kc_ref, qb)
            outs = []
            for g in range(KV_HEADS):
                kg = kv[:, g * HEAD_DIM:(g + 1) * HEAD_DIM]
                vg = kv[:, (KV_HEADS + g) * HEAD_DIM:(KV_HEADS + g + 1) * HEAD_DIM]
                srows = slice(g * group * CHUNK, (g + 1) * group * CHUNK)
                p, _ = _swa_probs(_stack_heads(q, g, group), kg, sk_ref[srows, :], mask)
                outs += _unstack_heads(_dot(p.astype(BF16), vg), group)
            yb = jnp.concatenate(outs, axis=1)
            yb_ref[blk, :] = yb
            ybn_ref[blk, :] = (yb * _rstd(yb) * gb_ref[...]).astype(BF16)

    q_spec, kc_spec, kp_spec = _swa_specs(db, npair, False)
    out = BS((rows, db), lambda p: (p, 0))
    return _call(
        body, name=name, args=(z, z, z, sink_rows, g_b), out_shape=[SDS((S, db), F32), SDS((S, db), BF16)],
        grid=(npair,),
        in_specs=[q_spec, kc_spec, kp_spec, BS((heads * CHUNK, 1), lambda p: (0, 0)), BS((1, db), lambda p: (0, 0))],
        out_specs=[out, out], sem=("parallel",), vmem_mib=32, exchange=exchange,
        first=lambda: pl.program_id(0) == 0, last=lambda: pl.program_id(0) == npair - 1)


def _swa_bwd(z, yb, dy, sink_rows, g_b, name):
    S = z.shape[0]
    db = g_b.shape[1]
    heads = db // HEAD_DIM
    group = heads // KV_HEADS
    rows = SWA_QBLOCKS * CHUNK
    assert SWA_QBLOCKS == 2 and S % rows == 0, (S, rows)
    npair = S // rows
    kvw = 2 * KV_HEADS * HEAD_DIM

    def body(q_ref, kc_ref, kp_ref, yb_ref, dy_ref, sk_ref, gb_ref, dq_ref, dkv_ref, dsk_ref, dgb_ref,
             done_ref, part_ref):
        p = pl.program_id(0)

        @pl.when(p == 0)
        def _():
            done_ref[...] = jnp.zeros_like(done_ref)
            part_ref[...] = jnp.zeros_like(part_ref)
            dsk_ref[...] = jnp.zeros_like(dsk_ref)
            dgb_ref[...] = jnp.zeros_like(dgb_ref)

        @pl.when(p < npair)
        def _():
            lane = lax.broadcasted_iota(jnp.int32, (1, heads), 1)
            dsinks = jnp.zeros((1, heads), F32)
            dgb = jnp.zeros((1, db), F32)
            contribs = []
            for qb in range(SWA_QBLOCKS):
                blk = slice(qb * CHUNK, (qb + 1) * CHUNK)
                mask = _swa_mask(SWA_QBLOCKS * p + qb, group)
                ybv = yb_ref[blk, :]
                rb = _rstd(ybv)
                yhat = ybv * rb
                d = dy_ref[blk, :]
                dgb = dgb + jnp.sum(d * yhat, axis=0, keepdims=True)
                do = _norm_bwd(yhat, rb, gb_ref[...], d).astype(BF16)
                q = q_ref[blk, :].astype(BF16)
                kv = _swa_keys(kp_ref, kc_ref, qb)
                dqs, dk, dv = [], [], []
                for g in range(KV_HEADS):
                    kg = kv[:, g * HEAD_DIM:(g + 1) * HEAD_DIM]
                    vg = kv[:, (KV_HEADS + g) * HEAD_DIM:(KV_HEADS + g + 1) * HEAD_DIM]
                    srows = slice(g * group * CHUNK, (g + 1) * group * CHUNK)
                    qg, dog = _stack_heads(q, g, group), _stack_heads(do, g, group)
                    pr, ps = _swa_probs(qg, kg, sk_ref[srows, :], mask)
                    dp = _dot_nt(dog, vg)
                    dr = jnp.sum(pr * dp, axis=-1, keepdims=True)
                    ds = (pr * (dp - dr) * (HEAD_DIM ** -0.5)).astype(BF16)
                    for h, t in enumerate(_unstack_heads(ps * dr, group)):
                        dsinks = dsinks - jnp.where(lane == g * group + h, jnp.sum(t, axis=0, keepdims=True), 0.0)
                    dqs += _unstack_heads(_dot(ds, kg), group)
                    dk.append(_dot_tn(ds, qg))
                    dv.append(_dot_tn(pr.astype(BF16), dog))
                dq_ref[blk, :] = jnp.concatenate(dqs, axis=1)
                contribs.append(jnp.concatenate(dk + dv, axis=1))
            dsk_ref[...] += dsinks
            dgb_ref[...] += dgb
            first, second = contribs
            dkv_ref[:CHUNK, :] = done_ref[...]
            dkv_ref[CHUNK:, :] = part_ref[...] + first[:CHUNK]
            done_ref[...] = first[CHUNK:] + second[:CHUNK]
            part_ref[...] = second[CHUNK:]

        @pl.when(p == npair)
        def _():
            dkv_ref[:CHUNK, :] = done_ref[...]
            dkv_ref[CHUNK:, :] = part_ref[...]

    q_spec, kc_spec, kp_spec = _swa_specs(db, npair, True)
    cur = BS((rows, db), lambda p: (jnp.minimum(p, npair - 1), 0))
    return pl.pallas_call(
        body, name=name,
        out_shape=[SDS((S, db), F32), SDS((S, kvw), F32), SDS((1, heads), F32), SDS((1, db), F32)],
        grid=(npair + 1,),
        in_specs=[q_spec, kc_spec, kp_spec, cur, BS((rows, db), lambda p: (jnp.minimum(p, npair - 1), 1)),
                  BS((heads * CHUNK, 1), lambda p: (0, 0)), BS((1, db), lambda p: (0, 0))],
        out_specs=[cur, BS((rows, kvw), lambda p: (jnp.maximum(p - 1, 0), 0)), BS((1, heads), lambda p: (0, 0)),
                   BS((1, db), lambda p: (0, 0))],
        scratch_shapes=[pltpu.VMEM((CHUNK, kvw), F32), pltpu.VMEM((CHUNK, kvw), F32)],
        compiler_params=_params(("arbitrary",), 32),
    )(z, z, z, yb, dy, sink_rows, g_b)


def _xattn_probs(qh, kh, hd):
    s = _dot_nt(qh, kh) * (hd ** -0.5)
    e = jnp.exp(s - jnp.max(s, axis=-1, keepdims=True))
    return e / jnp.sum(e, axis=-1, keepdims=True)


def _xattn_fwd(q, kv, name, tq=512):
    S, D = q.shape
    M = kv.shape[0]
    hd = D // X_HEADS
    tq = min(tq, S)
    assert S % tq == 0, (S, tq)

    def body(q_ref, kv_ref, o_ref):
        for h in range(X_HEADS):
            cols = slice(h * hd, (h + 1) * hd)
            p = _xattn_probs(q_ref[:, cols], kv_ref[:, cols], hd)
            o_ref[:, cols] = _dot(p.astype(BF16), kv_ref[:, D + h * hd:D + (h + 1) * hd]).astype(BF16)

    row = BS((tq, D), lambda i: (i, 0))
    return pl.pallas_call(
        body, name=name, out_shape=SDS((S, D), BF16), grid=(S // tq,),
        in_specs=[row, BS((M, 2 * D), lambda i: (0, 0))], out_specs=row, compiler_params=_params(("parallel",), 40),
    )(q, kv)


def _xattn_bwd(q, kv, do, name, tq=512):
    S, D = q.shape
    M = kv.shape[0]
    hd = D // X_HEADS
    tq = min(tq, S)
    assert S % tq == 0, (S, tq)

    def body(q_ref, kv_ref, do_ref, dq_ref, dkv_ref):
        @pl.when(pl.program_id(0) == 0)
        def _():
            dkv_ref[...] = jnp.zeros_like(dkv_ref)

        for h in range(X_HEADS):
            cols = slice(h * hd, (h + 1) * hd)
            vcols = slice(D + h * hd, D + (h + 1) * hd)
            qh, kh, vh, doh = q_ref[:, cols], kv_ref[:, cols], kv_ref[:, vcols], do_ref[:, cols]
            p = _xattn_probs(qh, kh, hd)
            dp = _dot_nt(doh, vh)
            ds = (p * (dp - jnp.sum(p * dp, axis=-1, keepdims=True)) * (hd ** -0.5)).astype(BF16)
            dq_ref[:, cols] = _dot(ds, kh).astype(BF16)
            dkv_ref[:, cols] += _dot_tn(ds, qh)
            dkv_ref[:, vcols] += _dot_tn(p.astype(BF16), doh)

    row = BS((tq, D), lambda i: (i, 0))
    full = BS((M, 2 * D), lambda i: (0, 0))
    return pl.pallas_call(
        body, name=name, out_shape=[SDS((S, D), BF16), SDS((M, 2 * D), F32)], grid=(S // tq,),
        in_specs=[row, full, row], out_specs=[row, full], compiler_params=_params(("arbitrary",), 40),
    )(q, kv, do)


def _row_tile(rows, cap):
    best = 8
    for t in range(8, min(rows, cap) + 1, 8):
        if rows % t == 0:
            best = t
    assert rows % best == 0, (rows, cap)
    return best


ADAM_TILE_ELEMS = 256 * 1024


def _adamw(w, m, v, recv, name):
    R, C = w.shape
    slots = recv.shape[0]
    tr = _row_tile(R, max(8, ADAM_TILE_ELEMS // C))

    def body(w_ref, m_ref, v_ref, r_ref, g_ref, d_ref, nm_ref, nv_ref):
        g = r_ref[0].astype(F32)
        for s in range(1, slots):
            g = g + r_ref[s].astype(F32)
        mn = ADAM_B1 * m_ref[...] + (1.0 - ADAM_B1) * g
        vn = ADAM_B2 * v_ref[...] + (1.0 - ADAM_B2) * jnp.square(g)
        m_hat = mn / (1.0 - ADAM_B1 ** ADAM_STEP)
        v_hat = vn / (1.0 - ADAM_B2 ** ADAM_STEP)
        g_ref[...] = g
        d_ref[...] = -ADAM_LR * (m_hat / (jnp.sqrt(v_hat) + ADAM_EPS) + ADAM_WD * w_ref[...])
        nm_ref[...] = mn
        nv_ref[...] = vn

    row = BS((tr, C), lambda i: (i, 0))
    return pl.pallas_call(
        body, name=name, out_shape=[SDS((R, C), F32)] * 4, grid=(R // tr,),
        in_specs=[row, row, row, BS((slots, tr, C), lambda i: (0, i, 0))], out_specs=[row] * 4,
        compiler_params=_params(("parallel",), 40),
    )(w, m, v, recv)


def _cols_to_blocks(full):
    r, c = full.shape
    return full.reshape(r, N_DEV, c // N_DEV).transpose(1, 0, 2)


def _blocks_to_cols(blocks):
    n, r, c = blocks.shape
    return blocks.transpose(1, 0, 2).reshape(r, n * c)


def _pack(parts):
    flat = jnp.concatenate([p.reshape(-1).astype(F32) for p in parts])
    pad = (-flat.shape[0]) % (128 * 128)
    return jnp.pad(flat, (0, pad)).reshape(-1, 128)


def kernel(x, mem, g_ffn1, w1_gate, w1_up, w1_down, g_mix, w_in, g_v, w_s, b_s, sinks, g_a_out, g_b_out, w_out, g_x, g_mem, w_xq, w_xkv, w_xo, g_ffn2, w2_gate, w2_up, w2_down, g_final, loss_target, m_g_ffn1, m_w1_gate, m_w1_up, m_w1_down, m_g_mix, m_w_in, m_g_v, m_w_s, m_b_s, m_sinks, m_g_a_out, m_g_b_out, m_w_out, m_g_x, m_g_mem, m_w_xq, m_w_xkv, m_w_xo, m_g_ffn2, m_w2_gate, m_w2_up, m_w2_down, m_g_final, v_g_ffn1, v_w1_gate, v_w1_up, v_w1_down, v_g_mix, v_w_in, v_g_v, v_w_s, v_b_s, v_sinks, v_g_a_out, v_g_b_out, v_w_out, v_g_x, v_g_mem, v_w_xq, v_w_xkv, v_w_xo, v_g_ffn2, v_w2_gate, v_w2_up, v_w2_down, v_g_final):
    w = dict(g_ffn1=g_ffn1, w1_gate=w1_gate, w1_up=w1_up, w1_down=w1_down, g_mix=g_mix, w_in=w_in, g_v=g_v, w_s=w_s,
             b_s=b_s, sinks=sinks, g_a_out=g_a_out, g_b_out=g_b_out, w_out=w_out, g_x=g_x, g_mem=g_mem, w_xq=w_xq,
             w_xkv=w_xkv, w_xo=w_xo, g_ffn2=g_ffn2, w2_gate=w2_gate, w2_up=w2_up, w2_down=w2_down, g_final=g_final)
    mom = dict(g_ffn1=m_g_ffn1, w1_gate=m_w1_gate, w1_up=m_w1_up, w1_down=m_w1_down, g_mix=m_g_mix, w_in=m_w_in,
               g_v=m_g_v, w_s=m_w_s, b_s=m_b_s, sinks=m_sinks, g_a_out=m_g_a_out, g_b_out=m_g_b_out, w_out=m_w_out,
               g_x=m_g_x, g_mem=m_g_mem, w_xq=m_w_xq, w_xkv=m_w_xkv, w_xo=m_w_xo, g_ffn2=m_g_ffn2,
               w2_gate=m_w2_gate, w2_up=m_w2_up, w2_down=m_w2_down, g_final=m_g_final)
    var = dict(g_ffn1=v_g_ffn1, w1_gate=v_w1_gate, w1_up=v_w1_up, w1_down=v_w1_down, g_mix=v_g_mix, w_in=v_w_in,
               g_v=v_g_v, w_s=v_w_s, b_s=v_b_s, sinks=v_sinks, g_a_out=v_g_a_out, g_b_out=v_g_b_out, w_out=v_w_out,
               g_x=v_g_x, g_mem=v_g_mem, w_xq=v_w_xq, w_xkv=v_w_xkv, w_xo=v_w_xo, g_ffn2=v_g_ffn2,
               w2_gate=v_w2_gate, w2_up=v_w2_up, w2_down=v_w2_down, g_final=v_g_final)

    xs, ms, tgt = x[0], mem[0], loss_target[0]
    D = xs.shape[1]
    d_a = w_s.shape[1] * CHUNK
    d_b = D - d_a
    kvw = 2 * KV_HEADS * HEAD_DIM

    shard = {k: w[k][0].astype(BF16) for k in BIG}
    wg1, wu1, wd1 = _gather_two_level([shard[k] for k in FFN1_W], "ag_ffn1")
    gf = g_final.reshape(1, D)
    ws, ws_t, bs_t = w_s[0], jnp.swapaxes(w_s[0], 1, 2), b_s[0].T
    sink_rows = jnp.repeat(sinks.reshape(-1), CHUNK).reshape(-1, 1)

    (h1, n1, a1, b1), part = _ffn_fwd(xs, g_ffn1, wg1, wu1, wd1, "ffn1_fwd",
                                      exchange=_gather_exchange([shard[k] for k in REST_W], CHIP_PEERS))
    part = dict(zip(REST_W, part))
    n2, (win_blocks,) = _rms_fwd(h1, g_mix, "mix_norm", exchange=_sibling_exchange([part["w_in"]]))
    win = _blocks_to_cols(win_blocks)
    z = _matmul(n2, win, "nn", F32, "mm_in", tn=win.shape[1] // 2)
    ya_n = _sgu_fwd(z, g_v, ws, bs_t, g_a_out, "sgu_fwd")
    later = [k for k in REST_W if k != "w_in"]
    (yb, yb_n), handed = _swa_fwd(z, sink_rows, g_b_out, "swa_fwd",
                                  exchange=_sibling_exchange([part[k] for k in later]))
    gathered = dict(zip(later, handed))
    wg2, wu2, wd2 = gathered["w2_gate"], gathered["w2_up"], gathered["w2_down"]
    wxkv = _blocks_to_cols(gathered["w_xkv"])
    wout = gathered["w_out"].reshape(D, D)
    wxq = gathered["w_xq"].reshape(D, D)
    wxo = gathered["w_xo"].reshape(D, D)
    y = jnp.concatenate([ya_n, yb_n], axis=1)
    h2 = _matmul(y, wout, "nn", F32, "mm_out", res=h1, tn=D)
    hx, _ = _rms_fwd(h2, g_x, "x_norm")
    mn, _ = _rms_fwd(ms, g_mem, "mem_norm")
    q = _matmul(hx, wxq, "nn", BF16, "mm_xq", tn=D)
    kv = _matmul(mn, wxkv, "nn", BF16, "mm_xkv")
    o = _xattn_fwd(q, kv, "xattn_fwd")
    h3 = _matmul(o, wxo, "nn", F32, "mm_xo", res=h2, tn=D)
    (h4, n4, a2, b2), _ = _ffn_fwd(h3, g_ffn2, wg2, wu2, wd2, "ffn2_fwd")
    dh4, dg_final, loss_part = _final_loss(h4, gf, tgt, "final_loss")

    grad_big, grad_small, recv_big = {}, {"g_final": dg_final}, {}
    order = _dw_block_order()
    (dn4, df2, da2, db2, s2), _ = _ffn_bwd_dx(dh4, a2, b2, wg2, wu2, wd2, "ffn2_bwd_dx")
    dh3, grad_small["g_ffn2"] = _rms_bwd(h3, g_ffn2, dn4, dh4, "ffn2_norm_bwd")
    (recv_big["w2_gate"], recv_big["w2_up"]), _ = _ffn_bwd_dw([(n4, da2), (n4, db2)], order, "ffn2_bwd_dw_gu")
    (recv_big["w2_down"],), _ = _ffn_bwd_dw([(s2, df2)], order, "ffn2_bwd_dw_d", tk=1024)

    do = _matmul(dh3, wxo, "nt", BF16, "mm_xo_dx", tn=D)
    grad_big["w_xo"] = _matmul(o, dh3, "tn", BF16, "mm_xo_dw", tm=D, tn=1024).reshape(N_DEV, D // N_DEV, D)
    dq, dkv = _xattn_bwd(q, kv, do, "xattn_bwd")
    dhx = _matmul(dq, wxq, "nt", BF16, "mm_xq_dx", tn=D)
    grad_big["w_xq"] = _matmul(hx, dq, "tn", BF16, "mm_xq_dw", tm=D, tn=1024).reshape(N_DEV, D // N_DEV, D)
    dmn = _matmul(dkv, wxkv, "nt", F32, "mm_xkv_dx")
    grad_big["w_xkv"] = _cols_to_blocks(_matmul(mn, dkv, "tn", BF16, "mm_xkv_dw", tm=1024, tn=1024))
    dh2, grad_small["g_x"] = _rms_bwd(h2, g_x, dhx, dh3, "x_norm_bwd")
    _, grad_small["g_mem"] = _rms_bwd(ms, g_mem, dmn, None, "mem_norm_bwd")

    dy = _matmul(dh2, wout, "nt", F32, "mm_out_dx", tn=D)
    grad_big["w_out"] = _matmul(y, dh2, "tn", BF16, "mm_out_dw", tm=D, tn=1024).reshape(N_DEV, D // N_DEV, D)
    dz_uv, grad_small["w_s"], dbs_t, grad_small["g_v"], grad_small["g_a_out"] = _sgu_bwd(
        z, dy, g_v, ws, ws_t, bs_t, g_a_out, "sgu_bwd")
    grad_small["b_s"] = dbs_t.T
    dq_b, dkv_b, grad_small["sinks"], grad_small["g_b_out"] = _swa_bwd(z, yb, dy, sink_rows, g_b_out, "swa_bwd")
    dz = jnp.concatenate([dz_uv, dq_b, dkv_b], axis=1)
    dn2 = _matmul(dz, win, "nt", BF16, "mm_in_dx", tn=1024)
    grad_big["w_in"] = _cols_to_blocks(_matmul(n2, dz, "tn", BF16, "mm_in_dw", tm=D, tn=dz.shape[1] // 2))
    dh1, grad_small["g_mix"] = _rms_bwd(h1, g_mix, dn2, dh2, "mix_norm_bwd")

    (dn1, df1, da1, db1, s1), recv_mid = _ffn_bwd_dx(
        dh1, a1, b1, wg1, wu1, wd1, "ffn1_bwd_dx", exchange=_scatter_exchange([grad_big[k] for k in MID_W]))
    recv_big.update(zip(MID_W, recv_mid))
    dx, grad_small["g_ffn1"] = _rms_bwd(xs, g_ffn1, dn1, dh1, "ffn1_norm_bwd")
    (recv_big["w1_gate"], recv_big["w1_up"]), (recv_small,) = _ffn_bwd_dw(
        [(n1, da1), (n1, db1)], order, "ffn1_bwd_dw_gu",
        exchange=_gather_exchange([_pack([grad_small[k] for k in SMALL])], ALL_PEERS))
    (recv_big["w1_down"],), _ = _ffn_bwd_dw([(s1, df1)], order, "ffn1_bwd_dw_d", tk=1024)


    grads, deltas, new_m, new_v = {}, {}, {}, {}
    for k in BIG:
        shp = w[k].shape
        two_d = shp[1:]
        outs = _adamw(w[k].reshape(two_d), mom[k].reshape(two_d), var[k].reshape(two_d), recv_big[k], "adamw_" + k)
        grads[k], deltas[k], new_m[k], new_v[k] = [t.reshape(shp) for t in outs]
    packed = _adamw(_pack([w[k] for k in SMALL]), _pack([mom[k] for k in SMALL]), _pack([var[k] for k in SMALL]),
                    recv_small, "adamw_small")
    off = 0
    for k in SMALL:
        shp = w[k].shape
        size = 1
        for s in shp:
            size *= s
        for dst, src in zip((grads, deltas, new_m, new_v), packed):
            dst[k] = src.reshape(-1)[off:off + size].reshape(shp)
        off += size

    loss = lax.psum(loss_part[0, 0], AXES)
    return (loss, dx[None], *[grads[k] for k in WEIGHTS], *[deltas[k] for k in WEIGHTS],
            *[new_m[k] for k in WEIGHTS], *[new_v[k] for k in WEIGHTS])
```

```python
import jax
import jax.numpy as jnp
from jax import lax
from jax.experimental import pallas as pl
from jax.experimental.pallas import tpu as pltpu

F32 = jnp.float32
BF16 = jnp.bfloat16
SDS = jax.ShapeDtypeStruct
BS = pl.BlockSpec

N_DEV = 8
AXES = ("x", "y", "c")
EPS = 1e-5
CHUNK = 128
HEAD_DIM = 64
KV_HEADS = 2
X_HEADS = 4
NEG = -1e30
ADAM_LR = 0.001
ADAM_B1 = 0.9
ADAM_B2 = 0.999
ADAM_EPS = 1e-08
ADAM_WD = 0.01
ADAM_STEP = 10
MIB = 1 << 20
FFN_SUB_ROWS = 256
WEIGHTS = ['g_ffn1', 'w1_gate', 'w1_up', 'w1_down', 'g_mix', 'w_in', 'g_v', 'w_s', 'b_s', 'sinks', 'g_a_out',
           'g_b_out', 'w_out', 'g_x', 'g_mem', 'w_xq', 'w_xkv', 'w_xo', 'g_ffn2', 'w2_gate', 'w2_up', 'w2_down',
           'g_final']
BIG = ['w1_gate', 'w1_up', 'w1_down', 'w_in', 'w_out', 'w_xq', 'w_xkv', 'w_xo', 'w2_gate', 'w2_up', 'w2_down']
SMALL = [w for w in WEIGHTS if w not in BIG]
FFN1_W = ['w1_gate', 'w1_up', 'w1_down']
FFN2_W = ['w2_gate', 'w2_up', 'w2_down']
MID_W = ['w_in', 'w_out', 'w_xq', 'w_xkv', 'w_xo']
REST_W = MID_W + FFN2_W


def _params(sem=None, vmem_mib=48):
    return pltpu.CompilerParams(dimension_semantics=sem, vmem_limit_bytes=vmem_mib * MIB)


def _dot(a, b):
    return jnp.dot(a, b, preferred_element_type=F32)


def _dot_nt(a, b):
    return lax.dot_general(a, b, (((1,), (1,)), ((), ())), preferred_element_type=F32)


def _dot_tn(a, b):
    return lax.dot_general(a, b, (((0,), (0,)), ((), ())), preferred_element_type=F32)


def _rstd(v):
    return lax.rsqrt(jnp.mean(v * v, axis=-1, keepdims=True) + EPS)


def _norm_bwd(xhat, r, g, d):
    t = d * g
    return r * (t - xhat * jnp.mean(t * xhat, axis=-1, keepdims=True))


def _gelu(v):
    return 0.5 * v * (1.0 + lax.erf(v * 0.7071067811865476))


def _gelu_grad(v):
    return 0.5 * (1.0 + lax.erf(v * 0.7071067811865476)) + v * jnp.exp(-0.5 * v * v) * 0.3989422804014327


def _mesh_pos():
    x, y, c = lax.axis_index("x"), lax.axis_index("y"), lax.axis_index("c")
    return x, y, c, 4 * x + 2 * y + c


def _peer(x, y, c, k):
    px = 1 - x if k & 4 else x
    py = 1 - y if k & 2 else y
    pc = 1 - c if k & 1 else c
    return (px, py, pc), 4 * px + 2 * py + pc


ANY = BS(memory_space=pl.ANY)
DMA_SEM = pltpu.SemaphoreType.DMA
ALL_PEERS = (1, 2, 3, 4, 5, 6, 7)
CHIP_PEERS = (2, 4, 6)
SIBLING = 1


def _remote(src, dst, send, recv, peer):
    return pltpu.make_async_remote_copy(src_ref=src, dst_ref=dst, send_sem=send, recv_sem=recv, device_id=peer,
                                        device_id_type=pl.DeviceIdType.MESH)


class _Exchange:
    def __init__(self, ins, out_shapes, sems, copies, aliases=None):
        self.ins, self.out_shapes, self.sems, self.copies, self.aliases = ins, out_shapes, sems, copies, aliases or {}


def _gather_exchange(shards, peers):
    n, m = len(shards), len(peers)

    def copies(ins, outs, sems):
        send, recv, lsem = sems
        x, y, c, me = _mesh_pos()
        cps = [pltpu.make_async_copy(ins[w], outs[w].at[me], lsem.at[w]) for w in range(n)]
        for w in range(n):
            for j, k in enumerate(peers):
                cps.append(_remote(ins[w], outs[w].at[me], send.at[w, j], recv.at[w, j], _peer(x, y, c, k)[0]))
        return cps

    return _Exchange(shards, [SDS((N_DEV,) + s.shape, s.dtype) for s in shards],
                     [DMA_SEM((n, m)), DMA_SEM((n, m)), DMA_SEM((n,))], copies)


def _sibling_exchange(gathered):
    n = len(gathered)

    def copies(ins, outs, sems):
        send, recv = sems
        x, y, c, me = _mesh_pos()
        slots = [me] + [_peer(x, y, c, k)[1] for k in CHIP_PEERS]
        sib = _peer(x, y, c, SIBLING)[0]
        return [_remote(outs[w].at[s], outs[w].at[s], send.at[w, j], recv.at[w, j], sib)
                for w in range(n) for j, s in enumerate(slots)]

    return _Exchange(gathered, [SDS(g.shape, g.dtype) for g in gathered], [DMA_SEM((n, 4)), DMA_SEM((n, 4))], copies,
                     aliases={w: w for w in range(n)})


def _scatter_exchange(fulls):
    n, m = len(fulls), len(ALL_PEERS)

    def copies(ins, outs, sems):
        send, recv, lsem = sems
        x, y, c, me = _mesh_pos()
        cps = [pltpu.make_async_copy(ins[w].at[me], outs[w].at[me], lsem.at[w]) for w in range(n)]
        for w in range(n):
            for j, k in enumerate(ALL_PEERS):
                peer, p = _peer(x, y, c, k)
                cps.append(_remote(ins[w].at[p], outs[w].at[me], send.at[w, j], recv.at[w, j], peer))
        return cps

    return _Exchange(fulls, [SDS(f.shape, f.dtype) for f in fulls],
                     [DMA_SEM((n, m)), DMA_SEM((n, m)), DMA_SEM((n,))], copies)


def _call(body, *, name, args, in_specs, out_shape, out_specs, grid, sem, vmem_mib, scratch=(), exchange=None,
          first=None, last=None):
    if exchange is None:
        return pl.pallas_call(body, name=name, out_shape=out_shape, grid=grid, in_specs=in_specs,
                              out_specs=out_specs, scratch_shapes=list(scratch),
                              compiler_params=_params(sem, vmem_mib))(*args), []
    ni, no, ns = len(args), len(out_shape), len(scratch)
    xi, xo = len(exchange.ins), len(exchange.out_shapes)

    def hosted(*refs):
        own_in, refs = refs[:ni], refs[ni:]
        x_in, refs = refs[:xi], refs[xi:]
        own_out, refs = refs[:no], refs[no:]
        x_out, refs = refs[:xo], refs[xo:]
        own_scr, x_sem = refs[:ns], refs[ns:]

        @pl.when(first())
        def _():
            for cp in exchange.copies(x_in, x_out, x_sem):
                cp.start()

        body(*own_in, *own_out, *own_scr)

        @pl.when(last())
        def _():
            for cp in exchange.copies(x_in, x_out, x_sem):
                cp.wait()

    outs = pl.pallas_call(
        hosted, name=name, out_shape=list(out_shape) + list(exchange.out_shapes), grid=grid,
        in_specs=list(in_specs) + [ANY] * xi, out_specs=list(out_specs) + [ANY] * xo,
        scratch_shapes=list(scratch) + list(exchange.sems),
        input_output_aliases={ni + a: no + b for a, b in exchange.aliases.items()},
        compiler_params=pltpu.CompilerParams(dimension_semantics=tuple("arbitrary" for _ in grid),
                                             vmem_limit_bytes=vmem_mib * MIB, has_side_effects=True),
    )(*args, *exchange.ins)
    return outs[:no], outs[no:]


def _gather_two_level(shards, name):
    n = len(shards)
    OWN_SIB, OWN_NEAR, OWN_FAR, RELAY, HAND_NEAR, HAND_FAR, HAND_DIAG = range(7)

    def body(*refs):
        ins, outs = refs[:n], refs[n:2 * n]
        send, recv, lsem = refs[2 * n:]
        x, y, c, me = _mesh_pos()
        flip_y = c == 0
        sib = (x, y, 1 - c)
        near = (jnp.where(flip_y, x, 1 - x), jnp.where(flip_y, 1 - y, y), c)
        far = (jnp.where(flip_y, 1 - x, x), jnp.where(flip_y, y, 1 - y), c)

        def slot_of(dev):
            return 4 * dev[0] + 2 * dev[1] + dev[2]

        s_near, s_far, s_diag = slot_of(near), slot_of(far), slot_of((1 - x, 1 - y, c))

        def copy(w, kind, src, slot, to):
            return _remote(src, outs[w].at[slot], send.at[w, kind], recv.at[w, kind], to)

        def own(w, kind, to):
            return copy(w, kind, ins[w], me, to)

        def arrival(w, kind, slot):
            return copy(w, kind, outs[w].at[slot], slot, sib)

        def hand(w, kind, slot):
            return copy(w, kind, outs[w].at[slot], slot, sib)

        local = [pltpu.make_async_copy(ins[w], outs[w].at[me], lsem.at[w]) for w in range(n)]
        for w in range(n):
            local[w].start()
            own(w, OWN_SIB, sib).start()
            own(w, OWN_NEAR, near).start()
            own(w, OWN_FAR, far).start()
        for w in range(n):
            arrival(w, OWN_NEAR, s_near).wait_recv()
            copy(w, RELAY, outs[w].at[s_near], s_near, far).start()
            hand(w, HAND_NEAR, s_near).start()
        for w in range(n):
            arrival(w, OWN_FAR, s_far).wait_recv()
            hand(w, HAND_FAR, s_far).start()
            arrival(w, RELAY, s_diag).wait_recv()
            hand(w, HAND_DIAG, s_diag).start()
        for w in range(n):
            arrival(w, OWN_SIB, jnp.bitwise_xor(me, 1)).wait_recv()
            arrival(w, HAND_NEAR, jnp.bitwise_xor(s_far, 1)).wait_recv()
            arrival(w, HAND_FAR, jnp.bitwise_xor(s_near, 1)).wait_recv()
            arrival(w, HAND_DIAG, jnp.bitwise_xor(s_diag, 1)).wait_recv()
            own(w, OWN_SIB, sib).wait_send()
            own(w, OWN_NEAR, near).wait_send()
            own(w, OWN_FAR, far).wait_send()
            copy(w, RELAY, outs[w].at[s_near], s_near, far).wait_send()
            hand(w, HAND_NEAR, s_near).wait_send()
            hand(w, HAND_FAR, s_far).wait_send()
            hand(w, HAND_DIAG, s_diag).wait_send()
            local[w].wait()

    return pl.pallas_call(
        body, name=name, out_shape=[SDS((N_DEV,) + s.shape, s.dtype) for s in shards],
        in_specs=[ANY] * n, out_specs=[ANY] * n,
        scratch_shapes=[DMA_SEM((n, 7)), DMA_SEM((n, 7)), DMA_SEM((n,))],
        compiler_params=pltpu.CompilerParams(has_side_effects=True),
    )(*shards)


def _run_exchange(exchange, name):
    xi, xo = len(exchange.ins), len(exchange.out_shapes)

    def body(*refs):
        cps = exchange.copies(refs[:xi], refs[xi:xi + xo], refs[xi + xo:])
        for cp in cps:
            cp.start()
        for cp in cps:
            cp.wait()

    return pl.pallas_call(
        body, name=name, out_shape=list(exchange.out_shapes), in_specs=[ANY] * xi, out_specs=[ANY] * xo,
        scratch_shapes=list(exchange.sems), input_output_aliases=dict(exchange.aliases),
        compiler_params=pltpu.CompilerParams(has_side_effects=True),
    )(*exchange.ins)


def _rms_fwd(h, g, name, tm=512, exchange=None):
    S, D = h.shape
    tm = min(tm, S)
    assert S % tm == 0, (S, tm)
    ni = S // tm

    def body(h_ref, g_ref, o_ref):
        hv = h_ref[...]
        o_ref[...] = (hv * _rstd(hv) * g_ref[...]).astype(o_ref.dtype)

    (out,), extra = _call(
        body, name=name, args=(h, g), out_shape=[SDS((S, D), BF16)], grid=(ni,),
        in_specs=[BS((tm, D), lambda i: (i, 0)), BS((1, D), lambda i: (0, 0))],
        out_specs=[BS((tm, D), lambda i: (i, 0))], sem=("parallel",), vmem_mib=32, exchange=exchange,
        first=lambda: pl.program_id(0) == 0, last=lambda: pl.program_id(0) == ni - 1)
    return out, extra


def _rms_bwd(h, g, dn, dres, name, tm=256):
    S, D = h.shape
    tm = min(tm, S)
    assert S % tm == 0, (S, tm)
    has_res = dres is not None

    def body(*refs):
        if has_res:
            h_ref, g_ref, dn_ref, dres_ref, dh_ref, dg_ref = refs
        else:
            h_ref, g_ref, dn_ref, dh_ref, dg_ref = refs
        hv = h_ref[...]
        r = _rstd(hv)
        xh = hv * r
        d = dn_ref[...].astype(F32)

        @pl.when(pl.program_id(0) == 0)
        def _():
            dg_ref[...] = jnp.zeros_like(dg_ref)

        dg_ref[...] += jnp.sum(d * xh, axis=0, keepdims=True)
        dh = _norm_bwd(xh, r, g_ref[...], d)
        dh_ref[...] = dres_ref[...] + dh if has_res else dh

    row = BS((tm, D), lambda i: (i, 0))
    vec = BS((1, D), lambda i: (0, 0))
    args = (h, g, dn) + ((dres,) if has_res else ())
    return pl.pallas_call(
        body, name=name, out_shape=[SDS((S, D), F32), SDS((1, D), F32)], grid=(S // tm,),
        in_specs=[row, vec, row] + ([row] if has_res else []), out_specs=[row, vec],
        compiler_params=_params(("arbitrary",), 40),
    )(*args)


def _final_loss(h, g, target, name, tm=256):
    S, D = h.shape
    tm = min(tm, S)
    assert S % tm == 0, (S, tm)

    def body(h_ref, g_ref, t_ref, dh_ref, dg_ref, loss_ref):
        hv = h_ref[...]
        r = _rstd(hv)
        xh = hv * r
        gv = g_ref[...]
        e = xh * gv - t_ref[...]

        @pl.when(pl.program_id(0) == 0)
        def _():
            dg_ref[...] = jnp.zeros_like(dg_ref)
            loss_ref[...] = jnp.zeros_like(loss_ref)

        loss_ref[...] += 0.5 * jnp.sum(jnp.mean(e * e, axis=-1, keepdims=True), axis=0, keepdims=True)
        dy = e * (1.0 / D)
        dg_ref[...] += jnp.sum(dy * xh, axis=0, keepdims=True)
        dh_ref[...] = _norm_bwd(xh, r, gv, dy)

    row = BS((tm, D), lambda i: (i, 0))
    vec = BS((1, D), lambda i: (0, 0))
    return pl.pallas_call(
        body, name=name, out_shape=[SDS((S, D), F32), SDS((1, D), F32), SDS((1, 128), F32)], grid=(S // tm,),
        in_specs=[row, vec, row], out_specs=[row, vec, BS((1, 128), lambda i: (0, 0))],
        compiler_params=_params(("arbitrary",), 40),
    )(h, g, target)


def _matmul(a, b, mode, out_dtype, name, res=None, tm=512, tn=512, tk=512):
    if mode == "tn":
        K, M = a.shape
        N = b.shape[1]
        tm, tn, tk = min(tm, M), min(tn, N), min(tk, K)
        assert M % tm == 0 and N % tn == 0 and K % tk == 0, (a.shape, b.shape, tm, tn, tk)
        nk = K // tk

        def body(a_ref, b_ref, o_ref, acc_ref):
            k = pl.program_id(2)

            @pl.when(k == 0)
            def _():
                acc_ref[...] = jnp.zeros_like(acc_ref)

            acc_ref[...] += _dot_tn(a_ref[...].astype(BF16), b_ref[...].astype(BF16))

            @pl.when(k == nk - 1)
            def _():
                o_ref[...] = acc_ref[...].astype(o_ref.dtype)

        return pl.pallas_call(
            body, name=name, out_shape=SDS((M, N), out_dtype), grid=(M // tm, N // tn, nk),
            in_specs=[BS((tk, tm), lambda i, j, k: (k, i)), BS((tk, tn), lambda i, j, k: (k, j))],
            out_specs=BS((tm, tn), lambda i, j, k: (i, j)), scratch_shapes=[pltpu.VMEM((tm, tn), F32)],
            compiler_params=_params(("parallel", "parallel", "arbitrary"), 48),
        )(a, b)

    M, K = a.shape
    N = b.shape[1] if mode == "nn" else b.shape[0]
    tm, tn = min(tm, M), min(tn, N)
    assert M % tm == 0 and N % tn == 0, (a.shape, b.shape, tm, tn)
    has_res = res is not None

    def body(*refs):
        if has_res:
            a_ref, b_ref, r_ref, o_ref = refs
        else:
            a_ref, b_ref, o_ref = refs
        av, bv = a_ref[...].astype(BF16), b_ref[...].astype(BF16)
        acc = _dot(av, bv) if mode == "nn" else _dot_nt(av, bv)
        if has_res:
            acc = acc + r_ref[...]
        o_ref[...] = acc.astype(o_ref.dtype)

    b_spec = BS((K, tn), lambda i, j: (0, j)) if mode == "nn" else BS((tn, K), lambda i, j: (j, 0))
    o_spec = BS((tm, tn), lambda i, j: (i, j))
    return pl.pallas_call(
        body, name=name, out_shape=SDS((M, N), out_dtype), grid=(M // tm, N // tn),
        in_specs=[BS((tm, K), lambda i, j: (i, 0)), b_spec] + ([o_spec] if has_res else []), out_specs=o_spec,
        compiler_params=_params(("parallel", "parallel"), 48),
    )(*((a, b) + ((res,) if has_res else ())))


def _ffn_fwd(h, g, wg, wu, wd, name, tm=512, exchange=None):
    S, D = h.shape
    nb, _, Fs = wg.shape
    tm = min(tm, S)
    sub = min(FFN_SUB_ROWS, tm)
    assert S % tm == 0 and tm % sub == 0, (S, tm, sub)

    def body(h_ref, g_ref, wg_ref, wu_ref, wd_ref, o_ref, n_ref, a_ref, b_ref):
        j = pl.program_id(1)

        @pl.when(j == 0)
        def _():
            hv = h_ref[...]
            n_ref[...] = (hv * _rstd(hv) * g_ref[...]).astype(BF16)
            o_ref[...] = jnp.zeros_like(o_ref)

        for r in range(0, tm, sub):
            rows = slice(r, r + sub)
            n = n_ref[rows, :]
            a = _dot(n, wg_ref[...]).astype(BF16)
            b = _dot(n, wu_ref[...]).astype(BF16)
            a_ref[rows, :] = a
            b_ref[rows, :] = b
            a, b = a.astype(F32), b.astype(F32)
            s = (a * jax.nn.sigmoid(a) * b).astype(BF16)
            o_ref[rows, :] += _dot(s, wd_ref[...])

        @pl.when(j == nb - 1)
        def _():
            o_ref[...] = h_ref[...] + 0.5 * o_ref[...]

    row = BS((tm, D), lambda i, j: (i, 0))
    wcol = BS((None, D, Fs), lambda i, j: (j, 0, 0))
    act = BS((None, tm, Fs), lambda i, j: (j, i, 0))
    ni = S // tm
    return _call(
        body, name=name, args=(h, g, wg, wu, wd),
        out_shape=[SDS((S, D), F32), SDS((S, D), BF16), SDS((nb, S, Fs), BF16), SDS((nb, S, Fs), BF16)],
        grid=(ni, nb),
        in_specs=[row, BS((1, D), lambda i, j: (0, 0)), wcol, wcol, BS((None, Fs, D), lambda i, j: (j, 0, 0))],
        out_specs=[row, row, act, act], sem=("parallel", "arbitrary"), vmem_mib=56, exchange=exchange,
        first=lambda: (pl.program_id(0) == 0) & (pl.program_id(1) == 0),
        last=lambda: (pl.program_id(0) == ni - 1) & (pl.program_id(1) == nb - 1))


def _ffn_bwd_dx(dh, a, b, wg, wu, wd, name, tm=512, exchange=None):
    S, D = dh.shape
    nb, _, Fs = wg.shape
    tm = min(tm, S)
    assert S % tm == 0, (S, tm)

    def body(dh_ref, a_ref, b_ref, wg_ref, wu_ref, wd_ref, dn_ref, df_ref, da_ref, db_ref, s_ref, ds_ref):
        j = pl.program_id(1)

        def first_matmul():
            ds_ref[...] = _dot_nt(df_ref[...], wd_ref[...])

        def pre_activation_cotangents():
            ds = ds_ref[...]
            av, bv = a_ref[...].astype(F32), b_ref[...].astype(F32)
            sig = jax.nn.sigmoid(av)
            sl = av * sig
            da = (ds * bv * (sig * (1.0 + av * (1.0 - sig)))).astype(BF16)
            db = (ds * sl).astype(BF16)
            da_ref[...] = da
            db_ref[...] = db
            s_ref[...] = (sl * bv).astype(BF16)
            return da, db

        def last_matmuls(da, db):
            dn_ref[...] += _dot_nt(da, wg_ref[...]) + _dot_nt(db, wu_ref[...])

        @pl.when(j == 0)
        def _():
            df_ref[...] = (0.5 * dh_ref[...]).astype(BF16)
            dn_ref[...] = jnp.zeros_like(dn_ref)
            first_matmul()

        @pl.when((j > 0) & (j < nb))
        def _():
            da, db = pre_activation_cotangents()
            first_matmul()
            last_matmuls(da, db)

        @pl.when(j == nb)
        def _():
            last_matmuls(*pre_activation_cotangents())

    row = BS((tm, D), lambda i, j: (i, 0))
    prev = BS((None, D, Fs), lambda i, j: (jnp.maximum(j - 1, 0), 0, 0))
    act = BS((None, tm, Fs), lambda i, j: (jnp.maximum(j - 1, 0), i, 0))
    ni = S // tm
    return _call(
        body, name=name, args=(dh, a, b, wg, wu, wd),
        out_shape=[SDS((S, D), F32), SDS((S, D), BF16)] + [SDS((nb, S, Fs), BF16)] * 3,
        grid=(ni, nb + 1),
        in_specs=[row, act, act, prev, prev, BS((None, Fs, D), lambda i, j: (jnp.minimum(j, nb - 1), 0, 0))],
        out_specs=[row, row, act, act, act], scratch=[pltpu.VMEM((tm, Fs), F32)],
        sem=("parallel", "arbitrary"), vmem_mib=56, exchange=exchange,
        first=lambda: (pl.program_id(0) == 0) & (pl.program_id(1) == 0),
        last=lambda: (pl.program_id(0) == ni - 1) & (pl.program_id(1) == nb))


DW_FLIPS = {0: (6, 2, 4, 0), 1: (6, 4, 2, 0)}
N_CHIPS = N_DEV // 2


def _dw_block_order():
    x, y, c, me = _mesh_pos()
    steps = [jnp.array([v for mine, sib in zip(DW_FLIPS[core], DW_FLIPS[1 - core]) for v in (sib ^ 1, mine)], jnp.int32)
             for core in (0, 1)]
    return jnp.bitwise_xor(me.astype(jnp.int32), jnp.where(c == 0, steps[0], steps[1]))


def _ffn_bwd_dw(pairs, order, name, tk=512, exchange=None):
    npair = len(pairs)
    S = pairs[0][0].shape[-2]
    nb, half = N_DEV, N_DEV // 2
    tk = min(tk, S)
    assert S % tk == 0, (S, tk)
    nk = S // tk
    shapes = [(lhs.shape[-1], rhs.shape[-1]) for lhs, rhs in pairs]
    xi = len(exchange.ins) if exchange else 0
    xo = len(exchange.out_shapes) if exchange else 0

    def body(order_ref, *rest):
        tiles, rest = rest[:2 * npair], rest[2 * npair:]
        x_in, rest = rest[:xi], rest[xi:]
        recv_bufs, rest = rest[:npair], rest[npair:]
        x_out, rest = rest[:xo], rest[xo:]
        accs, rest = rest[:npair], rest[npair:]
        out_t, rest = rest[:npair], rest[npair:]
        out_m, rest = rest[:npair], rest[npair:]
        land, rest = rest[:npair], rest[npair:]
        (send_t, recv_t, send_m, recv_m, lsem, credit), x_sem = rest[:6], rest[6:]
        t, k = pl.program_id(0), pl.program_id(1)
        x, y, c, me = _mesh_pos()
        sibling = (x, y, 1 - c)
        chip = 2 * x + y

        if exchange:
            @pl.when((t == 0) & (k == 0))
            def _():
                for cp in exchange.copies(x_in, x_out, x_sem):
                    cp.start()

        @pl.when(k == 0)
        def _():
            for acc in accs:
                acc[...] = jnp.zeros_like(acc)

        for w in range(npair):
            accs[w][...] += _dot_tn(tiles[2 * w][...], tiles[2 * w + 1][...])

        def to_sibling(w, i):
            return _remote(out_t[w], land[w], send_t.at[w, i], recv_t.at[w, i], sibling)

        def to_owner(w, i):
            dst = recv_bufs[w].at[chip]
            if i == half - 1:
                return pltpu.make_async_copy(out_m[w], dst, lsem.at[w])
            p = jnp.bitwise_xor(me, jnp.where(c == 0, DW_FLIPS[0][i], DW_FLIPS[1][i]))
            return _remote(out_m[w], dst, send_m.at[w, i], recv_m.at[w, i], (p >> 2, (p >> 1) & 1, p & 1))

        for i in range(half):
            @pl.when((t == 2 * i) & (k == nk - 1))
            def _(i=i):
                if i >= 1:
                    pl.semaphore_wait(credit, 1)
                for w in range(npair):
                    if i >= 1:
                        to_sibling(w, i - 1).wait_send()
                    out_t[w][...] = accs[w][...].astype(BF16)
                    to_sibling(w, i).start()

            @pl.when((t == 2 * i + 1) & (k == nk - 1))
            def _(i=i):
                for w in range(npair):
                    to_sibling(w, i).wait_recv()
                    if i >= 1:
                        to_owner(w, i - 1).wait_send()
                    out_m[w][...] = (accs[w][...] + land[w][...].astype(F32)).astype(BF16)
                if i < half - 1:
                    pl.semaphore_signal(credit, inc=1, device_id=sibling, device_id_type=pl.DeviceIdType.MESH)
                for w in range(npair):
                    to_owner(w, i).start()

        @pl.when((t == nb - 1) & (k == nk - 1))
        def _():
            for w in range(npair):
                to_sibling(w, half - 1).wait_send()
                to_owner(w, half - 1).wait()
                for i in range(half - 1):
                    to_owner(w, i).wait_recv()
            if exchange:
                for cp in exchange.copies(x_in, x_out, x_sem):
                    cp.wait()

    def tile_spec(arr):
        if arr.ndim == 3:
            return BS((None, tk, arr.shape[-1]), lambda t, k, o: (o[t], k, 0))
        return BS((tk, arr.shape[-1]), lambda t, k, o: (k, 0))

    flat = [a for pair in pairs for a in pair]
    grid_spec = pltpu.PrefetchScalarGridSpec(
        num_scalar_prefetch=1, grid=(nb, nk),
        in_specs=[tile_spec(a) for a in flat] + [ANY] * xi, out_specs=[ANY] * (npair + xo),
        scratch_shapes=[pltpu.VMEM(s, F32) for s in shapes] + [pltpu.VMEM(s, BF16) for s in shapes] * 3
        + [DMA_SEM((npair, half)), DMA_SEM((npair, half)), DMA_SEM((npair, half - 1)), DMA_SEM((npair, half - 1)),
           DMA_SEM((npair,)), pltpu.SemaphoreType.REGULAR]
        + (list(exchange.sems) if exchange else []))
    outs = pl.pallas_call(
        body, name=name, grid_spec=grid_spec,
        out_shape=[SDS((N_CHIPS,) + s, BF16) for s in shapes] + (list(exchange.out_shapes) if exchange else []),
        compiler_params=pltpu.CompilerParams(dimension_semantics=("arbitrary", "arbitrary"),
                                             vmem_limit_bytes=54 * MIB, has_side_effects=True),
    )(order, *flat, *(exchange.ins if exchange else ()))
    return outs[:npair], outs[npair:]


def _sgu_parts(z, gv, ws_ref, bst_ref, groups):
    da = z.shape[1] // 2
    zu, zv = z[:, :da], z[:, da:]
    u, v = _gelu(zu), _gelu(zv)
    rv = _rstd(v)
    vhat = v * rv
    vn = (vhat * gv).astype(BF16)
    tri = lax.broadcasted_iota(jnp.int32, (CHUNK, CHUNK), 0) >= lax.broadcasted_iota(jnp.int32, (CHUNK, CHUNK), 1)
    pieces = []
    for g in range(groups):
        w = jnp.where(tri, ws_ref[g], 0.0).astype(BF16)
        pieces.append(_dot(w, vn[:, g * CHUNK:(g + 1) * CHUNK]) + bst_ref[:, g:g + 1])
    sv = jnp.concatenate(pieces, axis=1)
    return dict(zu=zu, zv=zv, u=u, rv=rv, vhat=vhat, vn=vn, sv=sv, tri=tri)


SGU_CHUNKS = 4


def _sgu_fwd(z, g_v, w_s, b_st, g_a, name):
    S = z.shape[0]
    groups = w_s.shape[0]
    da = groups * CHUNK
    rows = SGU_CHUNKS * CHUNK
    assert S % rows == 0, (S, rows)

    def body(z_ref, gv_ref, ws_ref, bst_ref, ga_ref, o_ref):
        for ch in range(SGU_CHUNKS):
            blk = slice(ch * CHUNK, (ch + 1) * CHUNK)
            p = _sgu_parts(z_ref[blk, :], gv_ref[...], ws_ref, bst_ref, groups)
            ya = p["u"] * p["sv"]
            o_ref[blk, :] = (ya * _rstd(ya) * ga_ref[...]).astype(BF16)

    vec = BS((1, da), lambda i: (0, 0))
    return pl.pallas_call(
        body, name=name, out_shape=SDS((S, da), BF16), grid=(S // rows,),
        in_specs=[BS((rows, 2 * da), lambda i: (i, 0)), vec, BS((groups, CHUNK, CHUNK), lambda i: (0, 0, 0)),
                  BS((CHUNK, groups), lambda i: (0, 0)), vec],
        out_specs=BS((rows, da), lambda i: (i, 0)), compiler_params=_params(("parallel",), 32),
    )(z, g_v, w_s, b_st, g_a)


def _sgu_bwd(z, dy, g_v, w_s, w_st, b_st, g_a, name):
    S = z.shape[0]
    groups = w_s.shape[0]
    da = groups * CHUNK
    rows = SGU_CHUNKS * CHUNK
    assert S % rows == 0, (S, rows)

    def body(z_ref, dy_ref, gv_ref, ws_ref, wst_ref, bst_ref, ga_ref, dz_ref, dws_ref, dbst_ref, dgv_ref, dga_ref):
        @pl.when(pl.program_id(0) == 0)
        def _():
            dws_ref[...] = jnp.zeros_like(dws_ref)
            dbst_ref[...] = jnp.zeros_like(dbst_ref)
            dgv_ref[...] = jnp.zeros_like(dgv_ref)
            dga_ref[...] = jnp.zeros_like(dga_ref)

        gv = gv_ref[...]
        tri_t = (lax.broadcasted_iota(jnp.int32, (CHUNK, CHUNK), 0)
                 <= lax.broadcasted_iota(jnp.int32, (CHUNK, CHUNK), 1))
        lane = lax.broadcasted_iota(jnp.int32, (CHUNK, groups), 1)
        for ch in range(SGU_CHUNKS):
            blk = slice(ch * CHUNK, (ch + 1) * CHUNK)
            p = _sgu_parts(z_ref[blk, :], gv, ws_ref, bst_ref, groups)
            u, sv, tri = p["u"], p["sv"], p["tri"]
            ya = u * sv
            ra = _rstd(ya)
            yhat = ya * ra
            d = dy_ref[blk, :]
            dga_ref[...] += jnp.sum(d * yhat, axis=0, keepdims=True)
            dya = _norm_bwd(yhat, ra, ga_ref[...], d)
            du = dya * sv
            dsv = dya * u
            dsv_b = dsv.astype(BF16)
            dvn = []
            dbs = jnp.zeros((CHUNK, groups), F32)
            for g in range(groups):
                cols = slice(g * CHUNK, (g + 1) * CHUNK)
                dbs = dbs + jnp.where(lane == g, jnp.sum(dsv[:, cols], axis=1, keepdims=True), 0.0)
                dws_ref[g] += jnp.where(tri, _dot_nt(dsv_b[:, cols], p["vn"][:, cols]), 0.0)
                wt = jnp.where(tri_t, wst_ref[g], 0.0).astype(BF16)
                dvn.append(_dot(wt, dsv_b[:, cols]))
            dbst_ref[...] += dbs
            dvn = jnp.concatenate(dvn, axis=1)
            dgv_ref[...] += jnp.sum(dvn * p["vhat"], axis=0, keepdims=True)
            dv = _norm_bwd(p["vhat"], p["rv"], gv, dvn)
            dz_ref[blk, :] = jnp.concatenate([du * _gelu_grad(p["zu"]), dv * _gelu_grad(p["zv"])], axis=1)

    vec = BS((1, da), lambda i: (0, 0))
    wsq = BS((groups, CHUNK, CHUNK), lambda i: (0, 0, 0))
    bsq = BS((CHUNK, groups), lambda i: (0, 0))
    return pl.pallas_call(
        body, name=name,
        out_shape=[SDS((S, 2 * da), F32), SDS((groups, CHUNK, CHUNK), F32), SDS((CHUNK, groups), F32),
                   SDS((1, da), F32), SDS((1, da), F32)],
        grid=(S // rows,),
        in_specs=[BS((rows, 2 * da), lambda i: (i, 0)), BS((rows, da), lambda i: (i, 0)), vec, wsq, wsq, bsq, vec],
        out_specs=[BS((rows, 2 * da), lambda i: (i, 0)), wsq, bsq, vec, vec],
        compiler_params=_params(("arbitrary",), 32),
    )(z, dy, g_v, w_s, w_st, b_st, g_a)


def _swa_mask(i, group):
    row = lax.broadcasted_iota(jnp.int32, (group * CHUNK, 2 * CHUNK), 0) & (CHUNK - 1)
    col = lax.broadcasted_iota(jnp.int32, (group * CHUNK, 2 * CHUNK), 1)
    d = row + CHUNK - col
    return (d >= 0) & (d < CHUNK) & jnp.logical_or(i > 0, col >= CHUNK)


def _stack_heads(t, g, group):
    return jnp.concatenate([t[:, h * HEAD_DIM:(h + 1) * HEAD_DIM] for h in range(g * group, (g + 1) * group)], axis=0)


def _unstack_heads(stacked, group):
    return [stacked[h * CHUNK:(h + 1) * CHUNK] for h in range(group)]


def _swa_probs(qh, kh, sink, mask):
    s = jnp.where(mask, _dot_nt(qh, kh) * (HEAD_DIM ** -0.5), NEG)
    m = jnp.maximum(jnp.max(s, axis=-1, keepdims=True), sink)
    e = jnp.exp(s - m)
    es = jnp.exp(sink - m)
    inv = 1.0 / (jnp.sum(e, axis=-1, keepdims=True) + es)
    return e * inv, es * inv


SWA_QBLOCKS = 2


def _swa_specs(db, npair, clamp):
    kvw = 2 * KV_HEADS * HEAD_DIM
    cur = (lambda p: jnp.minimum(p, npair - 1)) if clamp else (lambda p: p)
    rows = SWA_QBLOCKS * CHUNK
    q_spec = BS((rows, db), lambda p: (cur(p), 2))
    kc_spec = BS((rows, kvw), lambda p: (cur(p), 3 * db // kvw))
    kp_spec = BS((CHUNK, kvw), lambda p: (jnp.maximum(SWA_QBLOCKS * cur(p) - 1, 0), 3 * db // kvw))
    return q_spec, kc_spec, kp_spec


def _swa_keys(kp_ref, kc_ref, qb):
    if qb == 0:
        return jnp.concatenate([kp_ref[...], kc_ref[:CHUNK, :]], axis=0).astype(BF16)
    return kc_ref[(qb - 1) * CHUNK:(qb + 1) * CHUNK, :].astype(BF16)


def _swa_fwd(z, sink_rows, g_b, name, exchange=None):
    S = z.shape[0]
    db = g_b.shape[1]
    heads = db // HEAD_DIM
    group = heads // KV_HEADS
    rows = SWA_QBLOCKS * CHUNK
    assert S % rows == 0, (S, rows)
    npair = S // rows

    def body(q_ref, kc_ref, kp_ref, sk_ref, gb_ref, yb_ref, ybn_ref):
        for qb in range(SWA_QBLOCKS):
            blk = slice(qb * CHUNK, (qb + 1) * CHUNK)
            mask = _swa_mask(SWA_QBLOCKS * pl.program_id(0) + qb, group)
            q = q_ref[blk, :].astype(BF16)
            kv = _swa_keys(kp_ref, kc_ref, qb)
            outs = []
            for g in range(KV_HEADS):
                kg = kv[:, g * HEAD_DIM:(g + 1) * HEAD_DIM]
                vg = kv[:, (KV_HEADS + g) * HEAD_DIM:(KV_HEADS + g + 1) * HEAD_DIM]
                srows = slice(g * group * CHUNK, (g + 1) * group * CHUNK)
                p, _ = _swa_probs(_stack_heads(q, g, group), kg, sk_ref[srows, :], mask)
                outs += _unstack_heads(_dot(p.astype(BF16), vg), group)
            yb = jnp.concatenate(outs, axis=1)
            yb_ref[blk, :] = yb
            ybn_ref[blk, :] = (yb * _rstd(yb) * gb_ref[...]).astype(BF16)

    q_spec, kc_spec, kp_spec = _swa_specs(db, npair, False)
    out = BS((rows, db), lambda p: (p, 0))
    return _call(
        body, name=name, args=(z, z, z, sink_rows, g_b), out_shape=[SDS((S, db), F32), SDS((S, db), BF16)],
        grid=(npair,),
        in_specs=[q_spec, kc_spec, kp_spec, BS((heads * CHUNK, 1), lambda p: (0, 0)), BS((1, db), lambda p: (0, 0))],
        out_specs=[out, out], sem=("parallel",), vmem_mib=32, exchange=exchange,
        first=lambda: pl.program_id(0) == 0, last=lambda: pl.program_id(0) == npair - 1)


def _swa_bwd(z, yb, dy, sink_rows, g_b, name):
    S = z.shape[0]
    db = g_b.shape[1]
    heads = db // HEAD_DIM
    group = heads // KV_HEADS
    rows = SWA_QBLOCKS * CHUNK
    assert SWA_QBLOCKS == 2 and S % rows == 0, (S, rows)
    npair = S // rows
    kvw = 2 * KV_HEADS * HEAD_DIM

    def body(q_ref, kc_ref, kp_ref, yb_ref, dy_ref, sk_ref, gb_ref, dq_ref, dkv_ref, dsk_ref, dgb_ref,
             done_ref, part_ref):
        p = pl.program_id(0)

        @pl.when(p == 0)
        def _():
            done_ref[...] = jnp.zeros_like(done_ref)
            part_ref[...] = jnp.zeros_like(part_ref)
            dsk_ref[...] = jnp.zeros_like(dsk_ref)
            dgb_ref[...] = jnp.zeros_like(dgb_ref)

        @pl.when(p < npair)
        def _():
            lane = lax.broadcasted_iota(jnp.int32, (1, heads), 1)
            dsinks = jnp.zeros((1, heads), F32)
            dgb = jnp.zeros((1, db), F32)
            contribs = []
            for qb in range(SWA_QBLOCKS):
                blk = slice(qb * CHUNK, (qb + 1) * CHUNK)
                mask = _swa_mask(SWA_QBLOCKS * p + qb, group)
                ybv = yb_ref[blk, :]
                rb = _rstd(ybv)
                yhat = ybv * rb
                d = dy_ref[blk, :]
                dgb = dgb + jnp.sum(d * yhat, axis=0, keepdims=True)
                do = _norm_bwd(yhat, rb, gb_ref[...], d).astype(BF16)
                q = q_ref[blk, :].astype(BF16)
                kv = _swa_keys(kp_ref, kc_ref, qb)
                dqs, dk, dv = [], [], []
                for g in range(KV_HEADS):
                    kg = kv[:, g * HEAD_DIM:(g + 1) * HEAD_DIM]
                    vg = kv[:, (KV_HEADS + g) * HEAD_DIM:(KV_HEADS + g + 1) * HEAD_DIM]
                    srows = slice(g * group * CHUNK, (g + 1) * group * CHUNK)
                    qg, dog = _stack_heads(q, g, group), _stack_heads(do, g, group)
                    pr, ps = _swa_probs(qg, kg, sk_ref[srows, :], mask)
                    dp = _dot_nt(dog, vg)
                    dr = jnp.sum(pr * dp, axis=-1, keepdims=True)
                    ds = (pr * (dp - dr) * (HEAD_DIM ** -0.5)).astype(BF16)
                    for h, t in enumerate(_unstack_heads(ps * dr, group)):
                        dsinks = dsinks - jnp.where(lane == g * group + h, jnp.sum(t, axis=0, keepdims=True), 0.0)
                    dqs += _unstack_heads(_dot(ds, kg), group)
                    dk.append(_dot_tn(ds, qg))
                    dv.append(_dot_tn(pr.astype(BF16), dog))
                dq_ref[blk, :] = jnp.concatenate(dqs, axis=1)
                contribs.append(jnp.concatenate(dk + dv, axis=1))
            dsk_ref[...] += dsinks
            dgb_ref[...] += dgb
            first, second = contribs
            dkv_ref[:CHUNK, :] = done_ref[...]
            dkv_ref[CHUNK:, :] = part_ref[...] + first[:CHUNK]
            done_ref[...] = first[CHUNK:] + second[:CHUNK]
            part_ref[...] = second[CHUNK:]

        @pl.when(p == npair)
        def _():
            dkv_ref[:CHUNK, :] = done_ref[...]
            dkv_ref[CHUNK:, :] = part_ref[...]

    q_spec, kc_spec, kp_spec = _swa_specs(db, npair, True)
    cur = BS((rows, db), lambda p: (jnp.minimum(p, npair - 1), 0))
    return pl.pallas_call(
        body, name=name,
        out_shape=[SDS((S, db), F32), SDS((S, kvw), F32), SDS((1, heads), F32), SDS((1, db), F32)],
        grid=(npair + 1,),
        in_specs=[q_spec, kc_spec, kp_spec, cur, BS((rows, db), lambda p: (jnp.minimum(p, npair - 1), 1)),
                  BS((heads * CHUNK, 1), lambda p: (0, 0)), BS((1, db), lambda p: (0, 0))],
        out_specs=[cur, BS((rows, kvw), lambda p: (jnp.maximum(p - 1, 0), 0)), BS((1, heads), lambda p: (0, 0)),
                   BS((1, db), lambda p: (0, 0))],
        scratch_shapes=[pltpu.VMEM((CHUNK, kvw), F32), pltpu.VMEM((CHUNK, kvw), F32)],
        compiler_params=_params(("arbitrary",), 32),
    )(z, z, z, yb, dy, sink_rows, g_b)


def _xattn_probs(qh, kh, hd):
    s = _dot_nt(qh, kh) * (hd ** -0.5)
    e = jnp.exp(s - jnp.max(s, axis=-1, keepdims=True))
    return e / jnp.sum(e, axis=-1, keepdims=True)


def _xattn_fwd(q, kv, name, tq=512):
    S, D = q.shape
    M = kv.shape[0]
    hd = D // X_HEADS
    tq = min(tq, S)
    assert S % tq == 0, (S, tq)

    def body(q_ref, kv_ref, o_ref):
        for h in range(X_HEADS):
            cols = slice(h * hd, (h + 1) * hd)
            p = _xattn_probs(q_ref[:, cols], kv_ref[:, cols], hd)
            o_ref[:, cols] = _dot(p.astype(BF16), kv_ref[:, D + h * hd:D + (h + 1) * hd]).astype(BF16)

    row = BS((tq, D), lambda i: (i, 0))
    return pl.pallas_call(
        body, name=name, out_shape=SDS((S, D), BF16), grid=(S // tq,),
        in_specs=[row, BS((M, 2 * D), lambda i: (0, 0))], out_specs=row, compiler_params=_params(("parallel",), 40),
    )(q, kv)


def _xattn_bwd(q, kv, do, name, tq=512):
    S, D = q.shape
    M = kv.shape[0]
    hd = D // X_HEADS
    tq = min(tq, S)
    assert S % tq == 0, (S, tq)

    def body(q_ref, kv_ref, do_ref, dq_ref, dkv_ref):
        @pl.when(pl.program_id(0) == 0)
        def _():
            dkv_ref[...] = jnp.zeros_like(dkv_ref)

        for h in range(X_HEADS):
            cols = slice(h * hd, (h + 1) * hd)
            vcols = slice(D + h * hd, D + (h + 1) * hd)
            qh, kh, vh, doh = q_ref[:, cols], kv_ref[:, cols], kv_ref[:, vcols], do_ref[:, cols]
            p = _xattn_probs(qh, kh, hd)
            dp = _dot_nt(doh, vh)
            ds = (p * (dp - jnp.sum(p * dp, axis=-1, keepdims=True)) * (hd ** -0.5)).astype(BF16)
            dq_ref[:, cols] = _dot(ds, kh).astype(BF16)
            dkv_ref[:, cols] += _dot_tn(ds, qh)
            dkv_ref[:, vcols] += _dot_tn(p.astype(BF16), doh)

    row = BS((tq, D), lambda i: (i, 0))
    full = BS((M, 2 * D), lambda i: (0, 0))
    return pl.pallas_call(
        body, name=name, out_shape=[SDS((S, D), BF16), SDS((M, 2 * D), F32)], grid=(S // tq,),
        in_specs=[row, full, row], out_specs=[row, full], compiler_params=_params(("arbitrary",), 40),
    )(q, kv, do)


def _row_tile(rows, cap):
    best = 8
    for t in range(8, min(rows, cap) + 1, 8):
        if rows % t == 0:
            best = t
    assert rows % best == 0, (rows, cap)
    return best


ADAM_TILE_ELEMS = 256 * 1024


def _adamw(w, m, v, recv, name):
    R, C = w.shape
    slots = recv.shape[0]
    tr = _row_tile(R, max(8, ADAM_TILE_ELEMS // C))

    def body(w_ref, m_ref, v_ref, r_ref, g_ref, d_ref, nm_ref, nv_ref):
        g = r_ref[0].astype(F32)
        for s in range(1, slots):
            g = g + r_ref[s].astype(F32)
        mn = ADAM_B1 * m_ref[...] + (1.0 - ADAM_B1) * g
        vn = ADAM_B2 * v_ref[...] + (1.0 - ADAM_B2) * jnp.square(g)
        m_hat = mn / (1.0 - ADAM_B1 ** ADAM_STEP)
        v_hat = vn / (1.0 - ADAM_B2 ** ADAM_STEP)
        g_ref[...] = g
        d_ref[...] = -ADAM_LR * (m_hat / (jnp.sqrt(v_hat) + ADAM_EPS) + ADAM_WD * w_ref[...])
        nm_ref[...] = mn
        nv_ref[...] = vn

    row = BS((tr, C), lambda i: (i, 0))
    return pl.pallas_call(
        body, name=name, out_shape=[SDS((R, C), F32)] * 4, grid=(R // tr,),
        in_specs=[row, row, row, BS((slots, tr, C), lambda i: (0, i, 0))], out_specs=[row] * 4,
        compiler_params=_params(("parallel",), 40),
    )(w, m, v, recv)


def _cols_to_blocks(full):
    r, c = full.shape
    return full.reshape(r, N_DEV, c // N_DEV).transpose(1, 0, 2)


def _blocks_to_cols(blocks):
    n, r, c = blocks.shape
    return blocks.transpose(1, 0, 2).reshape(r, n * c)


def _pack(parts):
    flat = jnp.concatenate([p.reshape(-1).astype(F32) for p in parts])
    pad = (-flat.shape[0]) % (128 * 128)
    return jnp.pad(flat, (0, pad)).reshape(-1, 128)


def kernel(x, mem, g_ffn1, w1_gate, w1_up, w1_down, g_mix, w_in, g_v, w_s, b_s, sinks, g_a_out, g_b_out, w_out, g_x, g_mem, w_xq, w_xkv, w_xo, g_ffn2, w2_gate, w2_up, w2_down, g_final, loss_target, m_g_ffn1, m_w1_gate, m_w1_up, m_w1_down, m_g_mix, m_w_in, m_g_v, m_w_s, m_b_s, m_sinks, m_g_a_out, m_g_b_out, m_w_out, m_g_x, m_g_mem, m_w_xq, m_w_xkv, m_w_xo, m_g_ffn2, m_w2_gate, m_w2_up, m_w2_down, m_g_final, v_g_ffn1, v_w1_gate, v_w1_up, v_w1_down, v_g_mix, v_w_in, v_g_v, v_w_s, v_b_s, v_sinks, v_g_a_out, v_g_b_out, v_w_out, v_g_x, v_g_mem, v_w_xq, v_w_xkv, v_w_xo, v_g_ffn2, v_w2_gate, v_w2_up, v_w2_down, v_g_final):
    w = dict(g_ffn1=g_ffn1, w1_gate=w1_gate, w1_up=w1_up, w1_down=w1_down, g_mix=g_mix, w_in=w_in, g_v=g_v, w_s=w_s,
             b_s=b_s, sinks=sinks, g_a_out=g_a_out, g_b_out=g_b_out, w_out=w_out, g_x=g_x, g_mem=g_mem, w_xq=w_xq,
             w_xkv=w_xkv, w_xo=w_xo, g_ffn2=g_ffn2, w2_gate=w2_gate, w2_up=w2_up, w2_down=w2_down, g_final=g_final)
    mom = dict(g_ffn1=m_g_ffn1, w1_gate=m_w1_gate, w1_up=m_w1_up, w1_down=m_w1_down, g_mix=m_g_mix, w_in=m_w_in,
               g_v=m_g_v, w_s=m_w_s, b_s=m_b_s, sinks=m_sinks, g_a_out=m_g_a_out, g_b_out=m_g_b_out, w_out=m_w_out,
               g_x=m_g_x, g_mem=m_g_mem, w_xq=m_w_xq, w_xkv=m_w_xkv, w_xo=m_w_xo, g_ffn2=m_g_ffn2,
               w2_gate=m_w2_gate, w2_up=m_w2_up, w2_down=m_w2_down, g_final=m_g_final)
    var = dict(g_ffn1=v_g_ffn1, w1_gate=v_w1_gate, w1_up=v_w1_up, w1_down=v_w1_down, g_mix=v_g_mix, w_in=v_w_in,
               g_v=v_g_v, w_s=v_w_s, b_s=v_b_s, sinks=v_sinks, g_a_out=v_g_a_out, g_b_out=v_g_b_out, w_out=v_w_out,
               g_x=v_g_x, g_mem=v_g_mem, w_xq=v_w_xq, w_xkv=v_w_xkv, w_xo=v_w_xo, g_ffn2=v_g_ffn2,
               w2_gate=v_w2_gate, w2_up=v_w2_up, w2_down=v_w2_down, g_final=v_g_final)

    xs, ms, tgt = x[0], mem[0], loss_target[0]
    D = xs.shape[1]
    d_a = w_s.shape[1] * CHUNK
    d_b = D - d_a
    kvw = 2 * KV_HEADS * HEAD_DIM

    shard = {k: w[k][0].astype(BF16) for k in BIG}
    wg1, wu1, wd1 = _gather_two_level([shard[k] for k in FFN1_W], "ag_ffn1")
    gf = g_final.reshape(1, D)
    ws, ws_t, bs_t = w_s[0], jnp.swapaxes(w_s[0], 1, 2), b_s[0].T
    sink_rows = jnp.repeat(sinks.reshape(-1), CHUNK).reshape(-1, 1)

    (h1, n1, a1, b1), part = _ffn_fwd(xs, g_ffn1, wg1, wu1, wd1, "ffn1_fwd",
                                      exchange=_gather_exchange([shard[k] for k in REST_W], CHIP_PEERS))
    part = dict(zip(REST_W, part))
    n2, (win_blocks,) = _rms_fwd(h1, g_mix, "mix_norm", exchange=_sibling_exchange([part["w_in"]]))
    win = _blocks_to_cols(win_blocks)
    z = _matmul(n2, win, "nn", F32, "mm_in", tn=win.shape[1] // 2)
    ya_n = _sgu_fwd(z, g_v, ws, bs_t, g_a_out, "sgu_fwd")
    later = [k for k in REST_W if k != "w_in"]
    (yb, yb_n), handed = _swa_fwd(z, sink_rows, g_b_out, "swa_fwd",
                                  exchange=_sibling_exchange([part[k] for k in later]))
    gathered = dict(zip(later, handed))
    wg2, wu2, wd2 = gathered["w2_gate"], gathered["w2_up"], gathered["w2_down"]
    wxkv = _blocks_to_cols(gathered["w_xkv"])
    wout = gathered["w_out"].reshape(D, D)
    wxq = gathered["w_xq"].reshape(D, D)
    wxo = gathered["w_xo"].reshape(D, D)
    y = jnp.concatenate([ya_n, yb_n], axis=1)
    h2 = _matmul(y, wout, "nn", F32, "mm_out", res=h1, tn=D)
    hx, _ = _rms_fwd(h2, g_x, "x_norm")
    mn, _ = _rms_fwd(ms, g_mem, "mem_norm")
    q = _matmul(hx, wxq, "nn", BF16, "mm_xq", tn=D)
    kv = _matmul(mn, wxkv, "nn", BF16, "mm_xkv")
    o = _xattn_fwd(q, kv, "xattn_fwd")
    h3 = _matmul(o, wxo, "nn", F32, "mm_xo", res=h2, tn=D)
    (h4, n4, a2, b2), _ = _ffn_fwd(h3, g_ffn2, wg2, wu2, wd2, "ffn2_fwd")
    dh4, dg_final, loss_part = _final_loss(h4, gf, tgt, "final_loss")

    grad_big, grad_small, recv_big = {}, {"g_final": dg_final}, {}
    order = _dw_block_order()
    (dn4, df2, da2, db2, s2), _ = _ffn_bwd_dx(dh4, a2, b2, wg2, wu2, wd2, "ffn2_bwd_dx")
    dh3, grad_small["g_ffn2"] = _rms_bwd(h3, g_ffn2, dn4, dh4, "ffn2_norm_bwd")
    (recv_big["w2_gate"], recv_big["w2_up"]), _ = _ffn_bwd_dw([(n4, da2), (n4, db2)], order, "ffn2_bwd_dw_gu")
    (recv_big["w2_down"],), _ = _ffn_bwd_dw([(s2, df2)], order, "ffn2_bwd_dw_d", tk=1024)

    do = _matmul(dh3, wxo, "nt", BF16, "mm_xo_dx", tn=D)
    grad_big["w_xo"] = _matmul(o, dh3, "tn", BF16, "mm_xo_dw", tm=D, tn=1024).reshape(N_DEV, D // N_DEV, D)
    dq, dkv = _xattn_bwd(q, kv, do, "xattn_bwd")
    dhx = _matmul(dq, wxq, "nt", BF16, "mm_xq_dx", tn=D)
    grad_big["w_xq"] = _matmul(hx, dq, "tn", BF16, "mm_xq_dw", tm=D, tn=1024).reshape(N_DEV, D // N_DEV, D)
    dmn = _matmul(dkv, wxkv, "nt", F32, "mm_xkv_dx")
    grad_big["w_xkv"] = _cols_to_blocks(_matmul(mn, dkv, "tn", BF16, "mm_xkv_dw", tm=1024, tn=1024))
    dh2, grad_small["g_x"] = _rms_bwd(h2, g_x, dhx, dh3, "x_norm_bwd")
    _, grad_small["g_mem"] = _rms_bwd(ms, g_mem, dmn, None, "mem_norm_bwd")

    dy = _matmul(dh2, wout, "nt", F32, "mm_out_dx", tn=D)
    grad_big["w_out"] = _matmul(y, dh2, "tn", BF16, "mm_out_dw", tm=D, tn=1024).reshape(N_DEV, D // N_DEV, D)
    dz_uv, grad_small["w_s"], dbs_t, grad_small["g_v"], grad_small["g_a_out"] = _sgu_bwd(
        z, dy, g_v, ws, ws_t, bs_t, g_a_out, "sgu_bwd")
    grad_small["b_s"] = dbs_t.T
    dq_b, dkv_b, grad_small["sinks"], grad_small["g_b_out"] = _swa_bwd(z, yb, dy, sink_rows, g_b_out, "swa_bwd")
    dz = jnp.concatenate([dz_uv, dq_b, dkv_b], axis=1)
    dn2 = _matmul(dz, win, "nt", BF16, "mm_in_dx", tn=1024)
    grad_big["w_in"] = _cols_to_blocks(_matmul(n2, dz, "tn", BF16, "mm_in_dw", tm=D, tn=dz.shape[1] // 2))
    dh1, grad_small["g_mix"] = _rms_bwd(h1, g_mix, dn2, dh2, "mix_norm_bwd")

    (dn1, df1, da1, db1, s1), recv_mid = _ffn_bwd_dx(
        dh1, a1, b1, wg1, wu1, wd1, "ffn1_bwd_dx", exchange=_scatter_exchange([grad_big[k] for k in MID_W]))
    recv_big.update(zip(MID_W, recv_mid))
    dx, grad_small["g_ffn1"] = _rms_bwd(xs, g_ffn1, dn1, dh1, "ffn1_norm_bwd")
    (recv_big["w1_gate"], recv_big["w1_up"]), (recv_small,) = _ffn_bwd_dw(
        [(n1, da1), (n1, db1)], order, "ffn1_bwd_dw_gu",
        exchange=_gather_exchange([_pack([grad_small[k] for k in SMALL])], ALL_PEERS))
    (recv_big["w1_down"],), _ = _ffn_bwd_dw([(s1, df1)], order, "ffn1_bwd_dw_d", tk=1024)


    grads, deltas, new_m, new_v = {}, {}, {}, {}
    for k in BIG:
        shp = w[k].shape
        two_d = shp[1:]
        outs = _adamw(w[k].reshape(two_d), mom[k].reshape(two_d), var[k].reshape(two_d), recv_big[k], "adamw_" + k)
        grads[k], deltas[k], new_m[k], new_v[k] = [t.reshape(shp) for t in outs]
    packed = _adamw(_pack([w[k] for k in SMALL]), _pack([mom[k] for k in SMALL]), _pack([var[k] for k in SMALL]),
                    recv_small, "adamw_small")
    off = 0
    for k in SMALL:
        shp = w[k].shape
        size = 1
        for s in shp:
            size *= s
        for dst, src in zip((grads, deltas, new_m, new_v), packed):
            dst[k] = src.reshape(-1)[off:off + size].reshape(shp)
        off += size

    loss = lax.psum(loss_part[0, 0], AXES)
    return (loss, dx[None], *[grads[k] for k in WEIGHTS], *[deltas[k] for k in WEIGHTS],
            *[new_m[k] for k in WEIGHTS], *[new_v[k] for k in WEIGHTS])
```

```python
import jax
import jax.numpy as jnp
from jax import lax
from jax.experimental import pallas as pl
from jax.experimental.pallas import tpu as pltpu

F32 = jnp.float32
BF16 = jnp.bfloat16
SDS = jax.ShapeDtypeStruct
BS = pl.BlockSpec

N_DEV = 8
AXES = ("x", "y", "c")
EPS = 1e-5
CHUNK = 128
HEAD_DIM = 64
KV_HEADS = 2
X_HEADS = 4
NEG = -1e30
ADAM_LR = 0.001
ADAM_B1 = 0.9
ADAM_B2 = 0.999
ADAM_EPS = 1e-08
ADAM_WD = 0.01
ADAM_STEP = 10
MIB = 1 << 20
FFN_SUB_ROWS = 256
WEIGHTS = ['g_ffn1', 'w1_gate', 'w1_up', 'w1_down', 'g_mix', 'w_in', 'g_v', 'w_s', 'b_s', 'sinks', 'g_a_out',
           'g_b_out', 'w_out', 'g_x', 'g_mem', 'w_xq', 'w_xkv', 'w_xo', 'g_ffn2', 'w2_gate', 'w2_up', 'w2_down',
           'g_final']
BIG = ['w1_gate', 'w1_up', 'w1_down', 'w_in', 'w_out', 'w_xq', 'w_xkv', 'w_xo', 'w2_gate', 'w2_up', 'w2_down']
SMALL = [w for w in WEIGHTS if w not in BIG]
FFN1_W = ['w1_gate', 'w1_up', 'w1_down']
FFN2_W = ['w2_gate', 'w2_up', 'w2_down']
MID_W = ['w_in', 'w_out', 'w_xq', 'w_xkv', 'w_xo']
REST_W = MID_W + FFN2_W


def _params(sem=None, vmem_mib=48):
    return pltpu.CompilerParams(dimension_semantics=sem, vmem_limit_bytes=vmem_mib * MIB)


def _dot(a, b):
    return jnp.dot(a, b, preferred_element_type=F32)


def _dot_nt(a, b):
    return lax.dot_general(a, b, (((1,), (1,)), ((), ())), preferred_element_type=F32)


def _dot_tn(a, b):
    return lax.dot_general(a, b, (((0,), (0,)), ((), ())), preferred_element_type=F32)


def _rstd(v):
    return lax.rsqrt(jnp.mean(v * v, axis=-1, keepdims=True) + EPS)


def _norm_bwd(xhat, r, g, d):
    t = d * g
    return r * (t - xhat * jnp.mean(t * xhat, axis=-1, keepdims=True))


def _gelu(v):
    return 0.5 * v * (1.0 + lax.erf(v * 0.7071067811865476))


def _gelu_grad(v):
    return 0.5 * (1.0 + lax.erf(v * 0.7071067811865476)) + v * jnp.exp(-0.5 * v * v) * 0.3989422804014327


def _mesh_pos():
    x, y, c = lax.axis_index("x"), lax.axis_index("y"), lax.axis_index("c")
    return x, y, c, 4 * x + 2 * y + c


def _peer(x, y, c, k):
    px = 1 - x if k & 4 else x
    py = 1 - y if k & 2 else y
    pc = 1 - c if k & 1 else c
    return (px, py, pc), 4 * px + 2 * py + pc


ANY = BS(memory_space=pl.ANY)
DMA_SEM = pltpu.SemaphoreType.DMA
ALL_PEERS = (1, 2, 3, 4, 5, 6, 7)
CHIP_PEERS = (2, 4, 6)
SIBLING = 1


def _remote(src, dst, send, recv, peer):
    return pltpu.make_async_remote_copy(src_ref=src, dst_ref=dst, send_sem=send, recv_sem=recv, device_id=peer,
                                        device_id_type=pl.DeviceIdType.MESH)


class _Exchange:
    def __init__(self, ins, out_shapes, sems, copies, aliases=None):
        self.ins, self.out_shapes, self.sems, self.copies, self.aliases = ins, out_shapes, sems, copies, aliases or {}


def _gather_exchange(shards, peers):
    n, m = len(shards), len(peers)

    def copies(ins, outs, sems):
        send, recv, lsem = sems
        x, y, c, me = _mesh_pos()
        cps = [pltpu.make_async_copy(ins[w], outs[w].at[me], lsem.at[w]) for w in range(n)]
        for w in range(n):
            for j, k in enumerate(peers):
                cps.append(_remote(ins[w], outs[w].at[me], send.at[w, j], recv.at[w, j], _peer(x, y, c, k)[0]))
        return cps

    return _Exchange(shards, [SDS((N_DEV,) + s.shape, s.dtype) for s in shards],
                     [DMA_SEM((n, m)), DMA_SEM((n, m)), DMA_SEM((n,))], copies)


def _sibling_exchange(gathered):
    n = len(gathered)

    def copies(ins, outs, sems):
        send, recv = sems
        x, y, c, me = _mesh_pos()
        slots = [me] + [_peer(x, y, c, k)[1] for k in CHIP_PEERS]
        sib = _peer(x, y, c, SIBLING)[0]
        return [_remote(outs[w].at[s], outs[w].at[s], send.at[w, j], recv.at[w, j], sib)
                for w in range(n) for j, s in enumerate(slots)]

    return _Exchange(gathered, [SDS(g.shape, g.dtype) for g in gathered], [DMA_SEM((n, 4)), DMA_SEM((n, 4))], copies,
                     aliases={w: w for w in range(n)})


def _scatter_exchange(fulls):
    n, m = len(fulls), len(ALL_PEERS)

    def copies(ins, outs, sems):
        send, recv, lsem = sems
        x, y, c, me = _mesh_pos()
        cps = [pltpu.make_async_copy(ins[w].at[me], outs[w].at[me], lsem.at[w]) for w in range(n)]
        for w in range(n):
            for j, k in enumerate(ALL_PEERS):
                peer, p = _peer(x, y, c, k)
                cps.append(_remote(ins[w].at[p], outs[w].at[me], send.at[w, j], recv.at[w, j], peer))
        return cps

    return _Exchange(fulls, [SDS(f.shape, f.dtype) for f in fulls],
                     [DMA_SEM((n, m)), DMA_SEM((n, m)), DMA_SEM((n,))], copies)


def _call(body, *, name, args, in_specs, out_shape, out_specs, grid, sem, vmem_mib, scratch=(), exchange=None,
          first=None, last=None, aliases=None):
    aliases = dict(aliases or {})
    if exchange is None:
        return pl.pallas_call(body, name=name, out_shape=out_shape, grid=grid, in_specs=in_specs,
                              out_specs=out_specs, scratch_shapes=list(scratch), input_output_aliases=aliases,
                              compiler_params=_params(sem, vmem_mib))(*args), []
    ni, no, ns = len(args), len(out_shape), len(scratch)
    xi, xo = len(exchange.ins), len(exchange.out_shapes)

    def hosted(*refs):
        own_in, refs = refs[:ni], refs[ni:]
        x_in, refs = refs[:xi], refs[xi:]
        own_out, refs = refs[:no], refs[no:]
        x_out, refs = refs[:xo], refs[xo:]
        own_scr, x_sem = refs[:ns], refs[ns:]

        @pl.when(first())
        def _():
            for cp in exchange.copies(x_in, x_out, x_sem):
                cp.start()

        body(*own_in, *own_out, *own_scr)

        @pl.when(last())
        def _():
            for cp in exchange.copies(x_in, x_out, x_sem):
                cp.wait()

    outs = pl.pallas_call(
        hosted, name=name, out_shape=list(out_shape) + list(exchange.out_shapes), grid=grid,
        in_specs=list(in_specs) + [ANY] * xi, out_specs=list(out_specs) + [ANY] * xo,
        scratch_shapes=list(scratch) + list(exchange.sems),
        input_output_aliases={**aliases, **{ni + a: no + b for a, b in exchange.aliases.items()}},
        compiler_params=pltpu.CompilerParams(dimension_semantics=tuple("arbitrary" for _ in grid),
                                             vmem_limit_bytes=vmem_mib * MIB, has_side_effects=True),
    )(*args, *exchange.ins)
    return outs[:no], outs[no:]


def _gather_two_level(shards, name):
    n = len(shards)
    OWN_SIB, OWN_NEAR, OWN_FAR, RELAY, HAND_NEAR, HAND_FAR, HAND_DIAG = range(7)

    def body(*refs):
        ins, outs = refs[:n], refs[n:2 * n]
        send, recv, lsem = refs[2 * n:]
        x, y, c, me = _mesh_pos()
        flip_y = c == 0
        sib = (x, y, 1 - c)
        near = (jnp.where(flip_y, x, 1 - x), jnp.where(flip_y, 1 - y, y), c)
        far = (jnp.where(flip_y, 1 - x, x), jnp.where(flip_y, y, 1 - y), c)

        def slot_of(dev):
            return 4 * dev[0] + 2 * dev[1] + dev[2]

        s_near, s_far, s_diag = slot_of(near), slot_of(far), slot_of((1 - x, 1 - y, c))

        def copy(w, kind, src, slot, to):
            return _remote(src, outs[w].at[slot], send.at[w, kind], recv.at[w, kind], to)

        def own(w, kind, to):
            return copy(w, kind, ins[w], me, to)

        def arrival(w, kind, slot):
            return copy(w, kind, outs[w].at[slot], slot, sib)

        def hand(w, kind, slot):
            return copy(w, kind, outs[w].at[slot], slot, sib)

        local = [pltpu.make_async_copy(ins[w], outs[w].at[me], lsem.at[w]) for w in range(n)]
        for w in range(n):
            local[w].start()
            own(w, OWN_SIB, sib).start()
            own(w, OWN_NEAR, near).start()
            own(w, OWN_FAR, far).start()
        for w in range(n):
            arrival(w, OWN_NEAR, s_near).wait_recv()
            copy(w, RELAY, outs[w].at[s_near], s_near, far).start()
            hand(w, HAND_NEAR, s_near).start()
        for w in range(n):
            arrival(w, OWN_FAR, s_far).wait_recv()
            hand(w, HAND_FAR, s_far).start()
            arrival(w, RELAY, s_diag).wait_recv()
            hand(w, HAND_DIAG, s_diag).start()
        for w in range(n):
            arrival(w, OWN_SIB, jnp.bitwise_xor(me, 1)).wait_recv()
            arrival(w, HAND_NEAR, jnp.bitwise_xor(s_far, 1)).wait_recv()
            arrival(w, HAND_FAR, jnp.bitwise_xor(s_near, 1)).wait_recv()
            arrival(w, HAND_DIAG, jnp.bitwise_xor(s_diag, 1)).wait_recv()
            own(w, OWN_SIB, sib).wait_send()
            own(w, OWN_NEAR, near).wait_send()
            own(w, OWN_FAR, far).wait_send()
            copy(w, RELAY, outs[w].at[s_near], s_near, far).wait_send()
            hand(w, HAND_NEAR, s_near).wait_send()
            hand(w, HAND_FAR, s_far).wait_send()
            hand(w, HAND_DIAG, s_diag).wait_send()
            local[w].wait()

    return pl.pallas_call(
        body, name=name, out_shape=[SDS((N_DEV,) + s.shape, s.dtype) for s in shards],
        in_specs=[ANY] * n, out_specs=[ANY] * n,
        scratch_shapes=[DMA_SEM((n, 7)), DMA_SEM((n, 7)), DMA_SEM((n,))],
        compiler_params=pltpu.CompilerParams(has_side_effects=True),
    )(*shards)


def _run_exchange(exchange, name):
    xi, xo = len(exchange.ins), len(exchange.out_shapes)

    def body(*refs):
        cps = exchange.copies(refs[:xi], refs[xi:xi + xo], refs[xi + xo:])
        for cp in cps:
            cp.start()
        for cp in cps:
            cp.wait()

    return pl.pallas_call(
        body, name=name, out_shape=list(exchange.out_shapes), in_specs=[ANY] * xi, out_specs=[ANY] * xo,
        scratch_shapes=list(exchange.sems), input_output_aliases=dict(exchange.aliases),
        compiler_params=pltpu.CompilerParams(has_side_effects=True),
    )(*exchange.ins)


def _rms_fwd(h, g, name, tm=512, exchange=None):
    S, D = h.shape
    tm = min(tm, S)
    assert S % tm == 0, (S, tm)
    ni = S // tm

    def body(h_ref, g_ref, o_ref):
        hv = h_ref[...]
        o_ref[...] = (hv * _rstd(hv) * g_ref[...]).astype(o_ref.dtype)

    (out,), extra = _call(
        body, name=name, args=(h, g), out_shape=[SDS((S, D), BF16)], grid=(ni,),
        in_specs=[BS((tm, D), lambda i: (i, 0)), BS((1, D), lambda i: (0, 0))],
        out_specs=[BS((tm, D), lambda i: (i, 0))], sem=("parallel",), vmem_mib=32, exchange=exchange,
        first=lambda: pl.program_id(0) == 0, last=lambda: pl.program_id(0) == ni - 1)
    return out, extra


def _rms_bwd(h, g, dn, dres, name, tm=256):
    S, D = h.shape
    tm = min(tm, S)
    assert S % tm == 0, (S, tm)
    has_res = dres is not None

    def body(*refs):
        if has_res:
            h_ref, g_ref, dn_ref, dres_ref, dh_ref, dg_ref = refs
        else:
            h_ref, g_ref, dn_ref, dh_ref, dg_ref = refs
        hv = h_ref[...]
        r = _rstd(hv)
        xh = hv * r
        d = dn_ref[...].astype(F32)

        @pl.when(pl.program_id(0) == 0)
        def _():
            dg_ref[...] = jnp.zeros_like(dg_ref)

        dg_ref[...] += jnp.sum(d * xh, axis=0, keepdims=True)
        dh = _norm_bwd(xh, r, g_ref[...], d)
        dh_ref[...] = dres_ref[...] + dh if has_res else dh

    row = BS((tm, D), lambda i: (i, 0))
    vec = BS((1, D), lambda i: (0, 0))
    args = (h, g, dn) + ((dres,) if has_res else ())
    return pl.pallas_call(
        body, name=name, out_shape=[SDS((S, D), F32), SDS((1, D), F32)], grid=(S // tm,),
        in_specs=[row, vec, row] + ([row] if has_res else []), out_specs=[row, vec],
        compiler_params=_params(("arbitrary",), 40),
    )(*args)


def _final_loss(h, g, target, name, tm=256):
    S, D = h.shape
    tm = min(tm, S)
    assert S % tm == 0, (S, tm)

    def body(h_ref, g_ref, t_ref, dh_ref, dg_ref, loss_ref):
        hv = h_ref[...]
        r = _rstd(hv)
        xh = hv * r
        gv = g_ref[...]
        e = xh * gv - t_ref[...]

        @pl.when(pl.program_id(0) == 0)
        def _():
            dg_ref[...] = jnp.zeros_like(dg_ref)
            loss_ref[...] = jnp.zeros_like(loss_ref)

        loss_ref[...] += 0.5 * jnp.sum(jnp.mean(e * e, axis=-1, keepdims=True), axis=0, keepdims=True)
        dy = e * (1.0 / D)
        dg_ref[...] += jnp.sum(dy * xh, axis=0, keepdims=True)
        dh_ref[...] = _norm_bwd(xh, r, gv, dy)

    row = BS((tm, D), lambda i: (i, 0))
    vec = BS((1, D), lambda i: (0, 0))
    return pl.pallas_call(
        body, name=name, out_shape=[SDS((S, D), F32), SDS((1, D), F32), SDS((1, 128), F32)], grid=(S // tm,),
        in_specs=[row, vec, row], out_specs=[row, vec, BS((1, 128), lambda i: (0, 0))],
        compiler_params=_params(("arbitrary",), 40),
    )(h, g, target)


def _matmul(a, b, mode, out_dtype, name, res=None, tm=512, tn=512, tk=512):
    if mode == "tn":
        K, M = a.shape
        N = b.shape[1]
        tm, tn, tk = min(tm, M), min(tn, N), min(tk, K)
        assert M % tm == 0 and N % tn == 0 and K % tk == 0, (a.shape, b.shape, tm, tn, tk)
        nk = K // tk

        def body(a_ref, b_ref, o_ref, acc_ref):
            k = pl.program_id(2)

            @pl.when(k == 0)
            def _():
                acc_ref[...] = jnp.zeros_like(acc_ref)

            acc_ref[...] += _dot_tn(a_ref[...].astype(BF16), b_ref[...].astype(BF16))

            @pl.when(k == nk - 1)
            def _():
                o_ref[...] = acc_ref[...].astype(o_ref.dtype)

        return pl.pallas_call(
            body, name=name, out_shape=SDS((M, N), out_dtype), grid=(M // tm, N // tn, nk),
            in_specs=[BS((tk, tm), lambda i, j, k: (k, i)), BS((tk, tn), lambda i, j, k: (k, j))],
            out_specs=BS((tm, tn), lambda i, j, k: (i, j)), scratch_shapes=[pltpu.VMEM((tm, tn), F32)],
            compiler_params=_params(("parallel", "parallel", "arbitrary"), 48),
        )(a, b)

    M, K = a.shape
    N = b.shape[1] if mode == "nn" else b.shape[0]
    tm, tn = min(tm, M), min(tn, N)
    assert M % tm == 0 and N % tn == 0, (a.shape, b.shape, tm, tn)
    has_res = res is not None

    def body(*refs):
        if has_res:
            a_ref, b_ref, r_ref, o_ref = refs
        else:
            a_ref, b_ref, o_ref = refs
        av, bv = a_ref[...].astype(BF16), b_ref[...].astype(BF16)
        acc = _dot(av, bv) if mode == "nn" else _dot_nt(av, bv)
        if has_res:
            acc = acc + r_ref[...]
        o_ref[...] = acc.astype(o_ref.dtype)

    b_spec = BS((K, tn), lambda i, j: (0, j)) if mode == "nn" else BS((tn, K), lambda i, j: (j, 0))
    o_spec = BS((tm, tn), lambda i, j: (i, j))
    return pl.pallas_call(
        body, name=name, out_shape=SDS((M, N), out_dtype), grid=(M // tm, N // tn),
        in_specs=[BS((tm, K), lambda i, j: (i, 0)), b_spec] + ([o_spec] if has_res else []), out_specs=o_spec,
        compiler_params=_params(("parallel", "parallel"), 48),
    )(*((a, b) + ((res,) if has_res else ())))


def _ffn_fwd(h, g, wg, wu, wd, name, tm=512, exchange=None):
    S, D = h.shape
    nb, _, Fs = wg.shape
    tm = min(tm, S)
    sub = min(FFN_SUB_ROWS, tm)
    assert S % tm == 0 and tm % sub == 0, (S, tm, sub)

    def body(h_ref, g_ref, wg_ref, wu_ref, wd_ref, o_ref, n_ref, a_ref, b_ref):
        j = pl.program_id(1)

        @pl.when(j == 0)
        def _():
            hv = h_ref[...]
            n_ref[...] = (hv * _rstd(hv) * g_ref[...]).astype(BF16)
            o_ref[...] = jnp.zeros_like(o_ref)

        for r in range(0, tm, sub):
            rows = slice(r, r + sub)
            n = n_ref[rows, :]
            a = _dot(n, wg_ref[...]).astype(BF16)
            b = _dot(n, wu_ref[...]).astype(BF16)
            a_ref[rows, :] = a
            b_ref[rows, :] = b
            a, b = a.astype(F32), b.astype(F32)
            s = (a * jax.nn.sigmoid(a) * b).astype(BF16)
            o_ref[rows, :] += _dot(s, wd_ref[...])

        @pl.when(j == nb - 1)
        def _():
            o_ref[...] = h_ref[...] + 0.5 * o_ref[...]

    row = BS((tm, D), lambda i, j: (i, 0))
    wcol = BS((None, D, Fs), lambda i, j: (j, 0, 0))
    act = BS((None, tm, Fs), lambda i, j: (j, i, 0))
    ni = S // tm
    return _call(
        body, name=name, args=(h, g, wg, wu, wd),
        out_shape=[SDS((S, D), F32), SDS((S, D), BF16), SDS((nb, S, Fs), BF16), SDS((nb, S, Fs), BF16)],
        grid=(ni, nb),
        in_specs=[row, BS((1, D), lambda i, j: (0, 0)), wcol, wcol, BS((None, Fs, D), lambda i, j: (j, 0, 0))],
        out_specs=[row, row, act, act], sem=("parallel", "arbitrary"), vmem_mib=56, exchange=exchange,
        first=lambda: (pl.program_id(0) == 0) & (pl.program_id(1) == 0),
        last=lambda: (pl.program_id(0) == ni - 1) & (pl.program_id(1) == nb - 1))


def _ffn_bwd_dx(dh, a, b, wg, wu, wd, name, tm=512, exchange=None):
    S, D = dh.shape
    nb, _, Fs = wg.shape
    tm = min(tm, S)
    assert S % tm == 0, (S, tm)

    def body(dh_ref, a_ref, b_ref, wg_ref, wu_ref, wd_ref, dn_ref, df_ref, da_ref, db_ref, s_ref, ds_ref):
        j = pl.program_id(1)

        def first_matmul():
            ds_ref[...] = _dot_nt(df_ref[...], wd_ref[...])

        def pre_activation_cotangents():
            ds = ds_ref[...]
            av, bv = a_ref[...].astype(F32), b_ref[...].astype(F32)
            sig = jax.nn.sigmoid(av)
            sl = av * sig
            da = (ds * bv * (sig * (1.0 + av * (1.0 - sig)))).astype(BF16)
            db = (ds * sl).astype(BF16)
            da_ref[...] = da
            db_ref[...] = db
            s_ref[...] = (sl * bv).astype(BF16)
            return da, db

        def last_matmuls(da, db):
            dn_ref[...] += _dot_nt(da, wg_ref[...]) + _dot_nt(db, wu_ref[...])

        @pl.when(j == 0)
        def _():
            df_ref[...] = (0.5 * dh_ref[...]).astype(BF16)
            dn_ref[...] = jnp.zeros_like(dn_ref)
            first_matmul()

        @pl.when((j > 0) & (j < nb))
        def _():
            da, db = pre_activation_cotangents()
            first_matmul()
            last_matmuls(da, db)

        @pl.when(j == nb)
        def _():
            last_matmuls(*pre_activation_cotangents())

    row = BS((tm, D), lambda i, j: (i, 0))
    prev = BS((None, D, Fs), lambda i, j: (jnp.maximum(j - 1, 0), 0, 0))
    act = BS((None, tm, Fs), lambda i, j: (jnp.maximum(j - 1, 0), i, 0))
    ni = S // tm
    return _call(
        body, name=name, args=(dh, a, b, wg, wu, wd),
        out_shape=[SDS((S, D), F32), SDS((S, D), BF16)] + [SDS((nb, S, Fs), BF16)] * 3,
        grid=(ni, nb + 1),
        in_specs=[row, act, act, prev, prev, BS((None, Fs, D), lambda i, j: (jnp.minimum(j, nb - 1), 0, 0))],
        out_specs=[row, row, act, act, act], scratch=[pltpu.VMEM((tm, Fs), F32)],
        sem=("parallel", "arbitrary"), vmem_mib=56, exchange=exchange,
        first=lambda: (pl.program_id(0) == 0) & (pl.program_id(1) == 0),
        last=lambda: (pl.program_id(0) == ni - 1) & (pl.program_id(1) == nb))


DW_FLIPS = {0: (6, 2, 4, 0), 1: (6, 4, 2, 0)}
N_CHIPS = N_DEV // 2


def _dw_block_order():
    x, y, c, me = _mesh_pos()
    steps = [jnp.array([v for mine, sib in zip(DW_FLIPS[core], DW_FLIPS[1 - core]) for v in (sib ^ 1, mine)], jnp.int32)
             for core in (0, 1)]
    return jnp.bitwise_xor(me.astype(jnp.int32), jnp.where(c == 0, steps[0], steps[1]))


def _ffn_bwd_dw(pairs, order, name, tk=512, exchange=None):
    npair = len(pairs)
    S = pairs[0][0].shape[-2]
    nb, half = N_DEV, N_DEV // 2
    tk = min(tk, S)
    assert S % tk == 0, (S, tk)
    nk = S // tk
    shapes = [(lhs.shape[-1], rhs.shape[-1]) for lhs, rhs in pairs]
    xi = len(exchange.ins) if exchange else 0
    xo = len(exchange.out_shapes) if exchange else 0

    def body(order_ref, *rest):
        tiles, rest = rest[:2 * npair], rest[2 * npair:]
        x_in, rest = rest[:xi], rest[xi:]
        recv_bufs, rest = rest[:npair], rest[npair:]
        x_out, rest = rest[:xo], rest[xo:]
        accs, rest = rest[:npair], rest[npair:]
        out_t, rest = rest[:npair], rest[npair:]
        out_m, rest = rest[:npair], rest[npair:]
        land, rest = rest[:npair], rest[npair:]
        (send_t, recv_t, send_m, recv_m, lsem, credit), x_sem = rest[:6], rest[6:]
        t, k = pl.program_id(0), pl.program_id(1)
        x, y, c, me = _mesh_pos()
        sibling = (x, y, 1 - c)
        chip = 2 * x + y

        if exchange:
            @pl.when((t == 0) & (k == 0))
            def _():
                for cp in exchange.copies(x_in, x_out, x_sem):
                    cp.start()

        @pl.when(k == 0)
        def _():
            for acc in accs:
                acc[...] = jnp.zeros_like(acc)

        for w in range(npair):
            accs[w][...] += _dot_tn(tiles[2 * w][...], tiles[2 * w + 1][...])

        def to_sibling(w, i):
            return _remote(out_t[w], land[w], send_t.at[w, i], recv_t.at[w, i], sibling)

        def to_owner(w, i):
            dst = recv_bufs[w].at[chip]
            if i == half - 1:
                return pltpu.make_async_copy(out_m[w], dst, lsem.at[w])
            p = jnp.bitwise_xor(me, jnp.where(c == 0, DW_FLIPS[0][i], DW_FLIPS[1][i]))
            return _remote(out_m[w], dst, send_m.at[w, i], recv_m.at[w, i], (p >> 2, (p >> 1) & 1, p & 1))

        for i in range(half):
            @pl.when((t == 2 * i) & (k == nk - 1))
            def _(i=i):
                if i >= 1:
                    pl.semaphore_wait(credit, 1)
                for w in range(npair):
                    if i >= 1:
                        to_sibling(w, i - 1).wait_send()
                    out_t[w][...] = accs[w][...].astype(BF16)
                    to_sibling(w, i).start()

            @pl.when((t == 2 * i + 1) & (k == nk - 1))
            def _(i=i):
                for w in range(npair):
                    to_sibling(w, i).wait_recv()
                    if i >= 1:
                        to_owner(w, i - 1).wait_send()
                    out_m[w][...] = (accs[w][...] + land[w][...].astype(F32)).astype(BF16)
                if i < half - 1:
                    pl.semaphore_signal(credit, inc=1, device_id=sibling, device_id_type=pl.DeviceIdType.MESH)
                for w in range(npair):
                    to_owner(w, i).start()

        @pl.when((t == nb - 1) & (k == nk - 1))
        def _():
            for w in range(npair):
                to_sibling(w, half - 1).wait_send()
                to_owner(w, half - 1).wait()
                for i in range(half - 1):
                    to_owner(w, i).wait_recv()
            if exchange:
                for cp in exchange.copies(x_in, x_out, x_sem):
                    cp.wait()

    def tile_spec(arr):
        if arr.ndim == 3:
            return BS((None, tk, arr.shape[-1]), lambda t, k, o: (o[t], k, 0))
        return BS((tk, arr.shape[-1]), lambda t, k, o: (k, 0))

    flat = [a for pair in pairs for a in pair]
    grid_spec = pltpu.PrefetchScalarGridSpec(
        num_scalar_prefetch=1, grid=(nb, nk),
        in_specs=[tile_spec(a) for a in flat] + [ANY] * xi, out_specs=[ANY] * (npair + xo),
        scratch_shapes=[pltpu.VMEM(s, F32) for s in shapes] + [pltpu.VMEM(s, BF16) for s in shapes] * 3
        + [DMA_SEM((npair, half)), DMA_SEM((npair, half)), DMA_SEM((npair, half - 1)), DMA_SEM((npair, half - 1)),
           DMA_SEM((npair,)), pltpu.SemaphoreType.REGULAR]
        + (list(exchange.sems) if exchange else []))
    outs = pl.pallas_call(
        body, name=name, grid_spec=grid_spec,
        out_shape=[SDS((N_CHIPS,) + s, BF16) for s in shapes] + (list(exchange.out_shapes) if exchange else []),
        compiler_params=pltpu.CompilerParams(dimension_semantics=("arbitrary", "arbitrary"),
                                             vmem_limit_bytes=54 * MIB, has_side_effects=True),
    )(order, *flat, *(exchange.ins if exchange else ()))
    return outs[:npair], outs[npair:]


def _sgu_parts(z, gv, ws_ref, bst_ref, groups):
    da = z.shape[1] // 2
    zu, zv = z[:, :da], z[:, da:]
    u, v = _gelu(zu), _gelu(zv)
    rv = _rstd(v)
    vhat = v * rv
    vn = (vhat * gv).astype(BF16)
    tri = lax.broadcasted_iota(jnp.int32, (CHUNK, CHUNK), 0) >= lax.broadcasted_iota(jnp.int32, (CHUNK, CHUNK), 1)
    pieces = []
    for g in range(groups):
        w = jnp.where(tri, ws_ref[g], 0.0).astype(BF16)
        pieces.append(_dot(w, vn[:, g * CHUNK:(g + 1) * CHUNK]) + bst_ref[:, g:g + 1])
    sv = jnp.concatenate(pieces, axis=1)
    return dict(zu=zu, zv=zv, u=u, rv=rv, vhat=vhat, vn=vn, sv=sv, tri=tri)


SGU_CHUNKS = 4


def _sgu_fwd(z, g_v, w_s, b_st, g_a, width, name):
    S = z.shape[0]
    groups = w_s.shape[0]
    da = groups * CHUNK
    rows = SGU_CHUNKS * CHUNK
    assert S % rows == 0, (S, rows)

    def body(z_ref, gv_ref, ws_ref, bst_ref, ga_ref, o_ref):
        for ch in range(SGU_CHUNKS):
            blk = slice(ch * CHUNK, (ch + 1) * CHUNK)
            p = _sgu_parts(z_ref[blk, :], gv_ref[...], ws_ref, bst_ref, groups)
            ya = p["u"] * p["sv"]
            o_ref[blk, :] = (ya * _rstd(ya) * ga_ref[...]).astype(BF16)

    vec = BS((1, da), lambda i: (0, 0))
    return pl.pallas_call(
        body, name=name, out_shape=SDS((S, width), BF16), grid=(S // rows,),
        in_specs=[BS((rows, 2 * da), lambda i: (i, 0)), vec, BS((groups, CHUNK, CHUNK), lambda i: (0, 0, 0)),
                  BS((CHUNK, groups), lambda i: (0, 0)), vec],
        out_specs=BS((rows, da), lambda i: (i, 0)), compiler_params=_params(("parallel",), 32),
    )(z, g_v, w_s, b_st, g_a)


def _sgu_bwd(z, dy, g_v, w_s, w_st, b_st, g_a, width, name):
    S = z.shape[0]
    groups = w_s.shape[0]
    da = groups * CHUNK
    rows = SGU_CHUNKS * CHUNK
    assert S % rows == 0, (S, rows)

    def body(z_ref, dy_ref, gv_ref, ws_ref, wst_ref, bst_ref, ga_ref, dz_ref, dws_ref, dbst_ref, dgv_ref, dga_ref):
        @pl.when(pl.program_id(0) == 0)
        def _():
            dws_ref[...] = jnp.zeros_like(dws_ref)
            dbst_ref[...] = jnp.zeros_like(dbst_ref)
            dgv_ref[...] = jnp.zeros_like(dgv_ref)
            dga_ref[...] = jnp.zeros_like(dga_ref)

        gv = gv_ref[...]
        tri_t = (lax.broadcasted_iota(jnp.int32, (CHUNK, CHUNK), 0)
                 <= lax.broadcasted_iota(jnp.int32, (CHUNK, CHUNK), 1))
        lane = lax.broadcasted_iota(jnp.int32, (CHUNK, groups), 1)
        for ch in range(SGU_CHUNKS):
            blk = slice(ch * CHUNK, (ch + 1) * CHUNK)
            p = _sgu_parts(z_ref[blk, :], gv, ws_ref, bst_ref, groups)
            u, sv, tri = p["u"], p["sv"], p["tri"]
            ya = u * sv
            ra = _rstd(ya)
            yhat = ya * ra
            d = dy_ref[blk, :]
            dga_ref[...] += jnp.sum(d * yhat, axis=0, keepdims=True)
            dya = _norm_bwd(yhat, ra, ga_ref[...], d)
            du = dya * sv
            dsv = dya * u
            dsv_b = dsv.astype(BF16)
            dvn = []
            dbs = jnp.zeros((CHUNK, groups), F32)
            for g in range(groups):
                cols = slice(g * CHUNK, (g + 1) * CHUNK)
                dbs = dbs + jnp.where(lane == g, jnp.sum(dsv[:, cols], axis=1, keepdims=True), 0.0)
                dws_ref[g] += jnp.where(tri, _dot_nt(dsv_b[:, cols], p["vn"][:, cols]), 0.0)
                wt = jnp.where(tri_t, wst_ref[g], 0.0).astype(BF16)
                dvn.append(_dot(wt, dsv_b[:, cols]))
            dbst_ref[...] += dbs
            dvn = jnp.concatenate(dvn, axis=1)
            dgv_ref[...] += jnp.sum(dvn * p["vhat"], axis=0, keepdims=True)
            dv = _norm_bwd(p["vhat"], p["rv"], gv, dvn)
            dz_ref[blk, :] = jnp.concatenate([du * _gelu_grad(p["zu"]), dv * _gelu_grad(p["zv"])], axis=1)

    vec = BS((1, da), lambda i: (0, 0))
    wsq = BS((groups, CHUNK, CHUNK), lambda i: (0, 0, 0))
    bsq = BS((CHUNK, groups), lambda i: (0, 0))
    return pl.pallas_call(
        body, name=name,
        out_shape=[SDS((S, width), F32), SDS((groups, CHUNK, CHUNK), F32), SDS((CHUNK, groups), F32),
                   SDS((1, da), F32), SDS((1, da), F32)],
        grid=(S // rows,),
        in_specs=[BS((rows, 2 * da), lambda i: (i, 0)), BS((rows, da), lambda i: (i, 0)), vec, wsq, wsq, bsq, vec],
        out_specs=[BS((rows, 2 * da), lambda i: (i, 0)), wsq, bsq, vec, vec],
        compiler_params=_params(("arbitrary",), 32),
    )(z, dy, g_v, w_s, w_st, b_st, g_a)


def _swa_mask(i, group):
    row = lax.broadcasted_iota(jnp.int32, (group * CHUNK, 2 * CHUNK), 0) & (CHUNK - 1)
    col = lax.broadcasted_iota(jnp.int32, (group * CHUNK, 2 * CHUNK), 1)
    d = row + CHUNK - col
    return (d >= 0) & (d < CHUNK) & jnp.logical_or(i > 0, col >= CHUNK)


def _stack_heads(t, g, group):
    return jnp.concatenate([t[:, h * HEAD_DIM:(h + 1) * HEAD_DIM] for h in range(g * group, (g + 1) * group)], axis=0)


def _unstack_heads(stacked, group):
    return [stacked[h * CHUNK:(h + 1) * CHUNK] for h in range(group)]


def _swa_probs(qh, kh, sink, mask):
    s = jnp.where(mask, _dot_nt(qh, kh) * (HEAD_DIM ** -0.5), NEG)
    m = jnp.maximum(jnp.max(s, axis=-1, keepdims=True), sink)
    e = jnp.exp(s - m)
    es = jnp.exp(sink - m)
    inv = 1.0 / (jnp.sum(e, axis=-1, keepdims=True) + es)
    return e * inv, es * inv


SWA_QBLOCKS = 2


def _swa_specs(db, npair, clamp):
    kvw = 2 * KV_HEADS * HEAD_DIM
    cur = (lambda p: jnp.minimum(p, npair - 1)) if clamp else (lambda p: p)
    rows = SWA_QBLOCKS * CHUNK
    q_spec = BS((rows, db), lambda p: (cur(p), 2))
    kc_spec = BS((rows, kvw), lambda p: (cur(p), 3 * db // kvw))
    kp_spec = BS((CHUNK, kvw), lambda p: (jnp.maximum(SWA_QBLOCKS * cur(p) - 1, 0), 3 * db // kvw))
    return q_spec, kc_spec, kp_spec


def _swa_keys(kp_ref, kc_ref, qb):
    if qb == 0:
        return jnp.concatenate([kp_ref[...], kc_ref[:CHUNK, :]], axis=0).astype(BF16)
    return kc_ref[(qb - 1) * CHUNK:(qb + 1) * CHUNK, :].astype(BF16)


def _swa_fwd(z, sink_rows, g_b, y_buf, name, exchange=None):
    S = z.shape[0]
    db = g_b.shape[1]
    heads = db // HEAD_DIM
    group = heads // KV_HEADS
    rows = SWA_QBLOCKS * CHUNK
    assert S % rows == 0, (S, rows)
    npair = S // rows

    def body(q_ref, kc_ref, kp_ref, sk_ref, gb_ref, _, yb_ref, ybn_ref):
        for qb in range(SWA_QBLOCKS):
            blk = slice(qb * CHUNK, (qb + 1) * CHUNK)
            mask = _swa_mask(SWA_QBLOCKS * pl.program_id(0) + qb, group)
            q = q_ref[blk, :].astype(BF16)
            kv = _swa_keys(kp_ref, kc_ref, qb)
            outs = []
            for g in range(KV_HEADS):
                kg = kv[:, g * HEAD_DIM:(g + 1) * HEAD_DIM]
                vg = kv[:, (KV_HEADS + g) * HEAD_DIM:(KV_HEADS + g + 1) * HEAD_DIM]
                srows = slice(g * group * CHUNK, (g + 1) * group * CHUNK)
                p, _ = _swa_probs(_stack_heads(q, g, group), kg, sk_ref[srows, :], mask)
                outs += _unstack_heads(_dot(p.astype(BF16), vg), group)
            yb = jnp.concatenate(outs, axis=1)
            yb_ref[blk, :] = yb
            ybn_ref[blk, :] = (yb * _rstd(yb) * gb_ref[...]).astype(BF16)

    q_spec, kc_spec, kp_spec = _swa_specs(db, npair, False)
    assert y_buf.shape[1] == 2 * db, y_buf.shape
    return _call(
        body, name=name, args=(z, z, z, sink_rows, g_b, y_buf),
        out_shape=[SDS((S, db), F32), SDS(y_buf.shape, BF16)], grid=(npair,),
        in_specs=[q_spec, kc_spec, kp_spec, BS((heads * CHUNK, 1), lambda p: (0, 0)), BS((1, db), lambda p: (0, 0)),
                  ANY],
        out_specs=[BS((rows, db), lambda p: (p, 0)), BS((rows, db), lambda p: (p, 1))], aliases={5: 1},
        sem=("parallel",), vmem_mib=32, exchange=exchange,
        first=lambda: pl.program_id(0) == 0, last=lambda: pl.program_id(0) == npair - 1)


def _swa_bwd(z, yb, dy, sink_rows, g_b, dz_buf, name):
    S = z.shape[0]
    db = g_b.shape[1]
    heads = db // HEAD_DIM
    group = heads // KV_HEADS
    rows = SWA_QBLOCKS * CHUNK
    assert SWA_QBLOCKS == 2 and S % rows == 0, (S, rows)
    npair = S // rows
    kvw = 2 * KV_HEADS * HEAD_DIM

    def body(q_ref, kc_ref, kp_ref, yb_ref, dy_ref, sk_ref, gb_ref, _, dq_ref, dkv_ref, dsk_ref, dgb_ref,
             done_ref, part_ref):
        p = pl.program_id(0)

        @pl.when(p == 0)
        def _():
            done_ref[...] = jnp.zeros_like(done_ref)
            part_ref[...] = jnp.zeros_like(part_ref)
            dsk_ref[...] = jnp.zeros_like(dsk_ref)
            dgb_ref[...] = jnp.zeros_like(dgb_ref)

        @pl.when(p < npair)
        def _():
            lane = lax.broadcasted_iota(jnp.int32, (1, heads), 1)
            dsinks = jnp.zeros((1, heads), F32)
            dgb = jnp.zeros((1, db), F32)
            contribs = []
            for qb in range(SWA_QBLOCKS):
                blk = slice(qb * CHUNK, (qb + 1) * CHUNK)
                mask = _swa_mask(SWA_QBLOCKS * p + qb, group)
                ybv = yb_ref[blk, :]
                rb = _rstd(ybv)
                yhat = ybv * rb
                d = dy_ref[blk, :]
                dgb = dgb + jnp.sum(d * yhat, axis=0, keepdims=True)
                do = _norm_bwd(yhat, rb, gb_ref[...], d).astype(BF16)
                q = q_ref[blk, :].astype(BF16)
                kv = _swa_keys(kp_ref, kc_ref, qb)
                dqs, dk, dv = [], [], []
                for g in range(KV_HEADS):
                    kg = kv[:, g * HEAD_DIM:(g + 1) * HEAD_DIM]
                    vg = kv[:, (KV_HEADS + g) * HEAD_DIM:(KV_HEADS + g + 1) * HEAD_DIM]
                    srows = slice(g * group * CHUNK, (g + 1) * group * CHUNK)
                    qg, dog = _stack_heads(q, g, group), _stack_heads(do, g, group)
                    pr, ps = _swa_probs(qg, kg, sk_ref[srows, :], mask)
                    dp = _dot_nt(dog, vg)
                    dr = jnp.sum(pr * dp, axis=-1, keepdims=True)
                    ds = (pr * (dp - dr) * (HEAD_DIM ** -0.5)).astype(BF16)
                    for h, t in enumerate(_unstack_heads(ps * dr, group)):
                        dsinks = dsinks - jnp.where(lane == g * group + h, jnp.sum(t, axis=0, keepdims=True), 0.0)
                    dqs += _unstack_heads(_dot(ds, kg), group)
                    dk.append(_dot_tn(ds, qg))
                    dv.append(_dot_tn(pr.astype(BF16), dog))
                dq_ref[blk, :] = jnp.concatenate(dqs, axis=1)
                contribs.append(jnp.concatenate(dk + dv, axis=1))
            dsk_ref[...] += dsinks
            dgb_ref[...] += dgb
            first, second = contribs
            dkv_ref[:CHUNK, :] = done_ref[...]
            dkv_ref[CHUNK:, :] = part_ref[...] + first[:CHUNK]
            done_ref[...] = first[CHUNK:] + second[:CHUNK]
            part_ref[...] = second[CHUNK:]

        @pl.when(p == npair)
        def _():
            dkv_ref[:CHUNK, :] = done_ref[...]
            dkv_ref[CHUNK:, :] = part_ref[...]

    q_spec, kc_spec, kp_spec = _swa_specs(db, npair, True)
    cur = BS((rows, db), lambda p: (jnp.minimum(p, npair - 1), 0))
    return pl.pallas_call(
        body, name=name,
        out_shape=[SDS(dz_buf.shape, F32), SDS((S, kvw), F32), SDS((1, heads), F32), SDS((1, db), F32)],
        grid=(npair + 1,),
        in_specs=[q_spec, kc_spec, kp_spec, cur, BS((rows, db), lambda p: (jnp.minimum(p, npair - 1), 1)),
                  BS((heads * CHUNK, 1), lambda p: (0, 0)), BS((1, db), lambda p: (0, 0)), ANY],
        out_specs=[BS((rows, db), lambda p: (jnp.minimum(p, npair - 1), 2)),
                   BS((rows, kvw), lambda p: (jnp.maximum(p - 1, 0), 0)), BS((1, heads), lambda p: (0, 0)),
                   BS((1, db), lambda p: (0, 0))],
        scratch_shapes=[pltpu.VMEM((CHUNK, kvw), F32), pltpu.VMEM((CHUNK, kvw), F32)],
        input_output_aliases={7: 0}, compiler_params=_params(("arbitrary",), 32),
    )(z, z, z, yb, dy, sink_rows, g_b, dz_buf)


def _paste_columns(dst, src, name, tm=512):
    S, w = src.shape
    tm = min(tm, S)
    assert S % tm == 0 and dst.shape[1] % w == 0, (dst.shape, src.shape, tm)
    last = dst.shape[1] // w - 1

    def body(src_ref, _, out_ref):
        out_ref[...] = src_ref[...].astype(out_ref.dtype)

    return pl.pallas_call(
        body, name=name, out_shape=SDS(dst.shape, dst.dtype), grid=(S // tm,),
        in_specs=[BS((tm, w), lambda i: (i, 0)), ANY], out_specs=BS((tm, w), lambda i: (i, last)),
        input_output_aliases={1: 0}, compiler_params=_params(("parallel",), 32),
    )(src, dst)


def _xattn_probs(qh, kh, hd):
    s = _dot_nt(qh, kh) * (hd ** -0.5)
    e = jnp.exp(s - jnp.max(s, axis=-1, keepdims=True))
    return e / jnp.sum(e, axis=-1, keepdims=True)


def _xattn_fwd(q, kv, name, tq=512):
    S, D = q.shape
    M = kv.shape[0]
    hd = D // X_HEADS
    tq = min(tq, S)
    assert S % tq == 0, (S, tq)

    def body(q_ref, kv_ref, o_ref):
        for h in range(X_HEADS):
            cols = slice(h * hd, (h + 1) * hd)
            p = _xattn_probs(q_ref[:, cols], kv_ref[:, cols], hd)
            o_ref[:, cols] = _dot(p.astype(BF16), kv_ref[:, D + h * hd:D + (h + 1) * hd]).astype(BF16)

    row = BS((tq, D), lambda i: (i, 0))
    return pl.pallas_call(
        body, name=name, out_shape=SDS((S, D), BF16), grid=(S // tq,),
        in_specs=[row, BS((M, 2 * D), lambda i: (0, 0))], out_specs=row, compiler_params=_params(("parallel",), 40),
    )(q, kv)


def _xattn_bwd(q, kv, do, name, tq=512):
    S, D = q.shape
    M = kv.shape[0]
    hd = D // X_HEADS
    tq = min(tq, S)
    assert S % tq == 0, (S, tq)

    def body(q_ref, kv_ref, do_ref, dq_ref, dkv_ref):
        @pl.when(pl.program_id(0) == 0)
        def _():
            dkv_ref[...] = jnp.zeros_like(dkv_ref)

        for h in range(X_HEADS):
            cols = slice(h * hd, (h + 1) * hd)
            vcols = slice(D + h * hd, D + (h + 1) * hd)
            qh, kh, vh, doh = q_ref[:, cols], kv_ref[:, cols], kv_ref[:, vcols], do_ref[:, cols]
            p = _xattn_probs(qh, kh, hd)
            dp = _dot_nt(doh, vh)
            ds = (p * (dp - jnp.sum(p * dp, axis=-1, keepdims=True)) * (hd ** -0.5)).astype(BF16)
            dq_ref[:, cols] = _dot(ds, kh).astype(BF16)
            dkv_ref[:, cols] += _dot_tn(ds, qh)
            dkv_ref[:, vcols] += _dot_tn(p.astype(BF16), doh)

    row = BS((tq, D), lambda i: (i, 0))
    full = BS((M, 2 * D), lambda i: (0, 0))
    return pl.pallas_call(
        body, name=name, out_shape=[SDS((S, D), BF16), SDS((M, 2 * D), F32)], grid=(S // tq,),
        in_specs=[row, full, row], out_specs=[row, full], compiler_params=_params(("arbitrary",), 40),
    )(q, kv, do)


def _row_tile(rows, cap):
    best = 8
    for t in range(8, min(rows, cap) + 1, 8):
        if rows % t == 0:
            best = t
    assert rows % best == 0, (rows, cap)
    return best


ADAM_TILE_ELEMS = 256 * 1024


def _adamw(w, m, v, recv, name):
    R, C = w.shape
    slots = recv.shape[0]
    tr = _row_tile(R, max(8, ADAM_TILE_ELEMS // C))

    def body(w_ref, m_ref, v_ref, r_ref, g_ref, d_ref, nm_ref, nv_ref):
        g = r_ref[0].astype(F32)
        for s in range(1, slots):
            g = g + r_ref[s].astype(F32)
        mn = ADAM_B1 * m_ref[...] + (1.0 - ADAM_B1) * g
        vn = ADAM_B2 * v_ref[...] + (1.0 - ADAM_B2) * jnp.square(g)
        m_hat = mn / (1.0 - ADAM_B1 ** ADAM_STEP)
        v_hat = vn / (1.0 - ADAM_B2 ** ADAM_STEP)
        g_ref[...] = g
        d_ref[...] = -ADAM_LR * (m_hat / (jnp.sqrt(v_hat) + ADAM_EPS) + ADAM_WD * w_ref[...])
        nm_ref[...] = mn
        nv_ref[...] = vn

    row = BS((tr, C), lambda i: (i, 0))
    return pl.pallas_call(
        body, name=name, out_shape=[SDS((R, C), F32)] * 4, grid=(R // tr,),
        in_specs=[row, row, row, BS((slots, tr, C), lambda i: (0, i, 0))], out_specs=[row] * 4,
        compiler_params=_params(("parallel",), 40),
    )(w, m, v, recv)


def _cols_to_blocks(full):
    r, c = full.shape
    return full.reshape(r, N_DEV, c // N_DEV).transpose(1, 0, 2)


def _blocks_to_cols(blocks):
    n, r, c = blocks.shape
    return blocks.transpose(1, 0, 2).reshape(r, n * c)


def _pack(parts):
    flat = jnp.concatenate([p.reshape(-1).astype(F32) for p in parts])
    pad = (-flat.shape[0]) % (128 * 128)
    return jnp.pad(flat, (0, pad)).reshape(-1, 128)


def kernel(x, mem, g_ffn1, w1_gate, w1_up, w1_down, g_mix, w_in, g_v, w_s, b_s, sinks, g_a_out, g_b_out, w_out, g_x, g_mem, w_xq, w_xkv, w_xo, g_ffn2, w2_gate, w2_up, w2_down, g_final, loss_target, m_g_ffn1, m_w1_gate, m_w1_up, m_w1_down, m_g_mix, m_w_in, m_g_v, m_w_s, m_b_s, m_sinks, m_g_a_out, m_g_b_out, m_w_out, m_g_x, m_g_mem, m_w_xq, m_w_xkv, m_w_xo, m_g_ffn2, m_w2_gate, m_w2_up, m_w2_down, m_g_final, v_g_ffn1, v_w1_gate, v_w1_up, v_w1_down, v_g_mix, v_w_in, v_g_v, v_w_s, v_b_s, v_sinks, v_g_a_out, v_g_b_out, v_w_out, v_g_x, v_g_mem, v_w_xq, v_w_xkv, v_w_xo, v_g_ffn2, v_w2_gate, v_w2_up, v_w2_down, v_g_final):
    w = dict(g_ffn1=g_ffn1, w1_gate=w1_gate, w1_up=w1_up, w1_down=w1_down, g_mix=g_mix, w_in=w_in, g_v=g_v, w_s=w_s,
             b_s=b_s, sinks=sinks, g_a_out=g_a_out, g_b_out=g_b_out, w_out=w_out, g_x=g_x, g_mem=g_mem, w_xq=w_xq,
             w_xkv=w_xkv, w_xo=w_xo, g_ffn2=g_ffn2, w2_gate=w2_gate, w2_up=w2_up, w2_down=w2_down, g_final=g_final)
    mom = dict(g_ffn1=m_g_ffn1, w1_gate=m_w1_gate, w1_up=m_w1_up, w1_down=m_w1_down, g_mix=m_g_mix, w_in=m_w_in,
               g_v=m_g_v, w_s=m_w_s, b_s=m_b_s, sinks=m_sinks, g_a_out=m_g_a_out, g_b_out=m_g_b_out, w_out=m_w_out,
               g_x=m_g_x, g_mem=m_g_mem, w_xq=m_w_xq, w_xkv=m_w_xkv, w_xo=m_w_xo, g_ffn2=m_g_ffn2,
               w2_gate=m_w2_gate, w2_up=m_w2_up, w2_down=m_w2_down, g_final=m_g_final)
    var = dict(g_ffn1=v_g_ffn1, w1_gate=v_w1_gate, w1_up=v_w1_up, w1_down=v_w1_down, g_mix=v_g_mix, w_in=v_w_in,
               g_v=v_g_v, w_s=v_w_s, b_s=v_b_s, sinks=v_sinks, g_a_out=v_g_a_out, g_b_out=v_g_b_out, w_out=v_w_out,
               g_x=v_g_x, g_mem=v_g_mem, w_xq=v_w_xq, w_xkv=v_w_xkv, w_xo=v_w_xo, g_ffn2=v_g_ffn2,
               w2_gate=v_w2_gate, w2_up=v_w2_up, w2_down=v_w2_down, g_final=v_g_final)

    xs, ms, tgt = x[0], mem[0], loss_target[0]
    D = xs.shape[1]
    d_a = w_s.shape[1] * CHUNK
    d_b = D - d_a
    kvw = 2 * KV_HEADS * HEAD_DIM

    shard = {k: w[k][0].astype(BF16) for k in BIG}
    wg1, wu1, wd1 = _gather_two_level([shard[k] for k in FFN1_W], "ag_ffn1")
    gf = g_final.reshape(1, D)
    ws, ws_t, bs_t = w_s[0], jnp.swapaxes(w_s[0], 1, 2), b_s[0].T
    sink_rows = jnp.repeat(sinks.reshape(-1), CHUNK).reshape(-1, 1)

    (h1, n1, a1, b1), part = _ffn_fwd(xs, g_ffn1, wg1, wu1, wd1, "ffn1_fwd",
                                      exchange=_gather_exchange([shard[k] for k in REST_W], CHIP_PEERS))
    part = dict(zip(REST_W, part))
    n2, (win_blocks,) = _rms_fwd(h1, g_mix, "mix_norm", exchange=_sibling_exchange([part["w_in"]]))
    win = _blocks_to_cols(win_blocks)
    z = _matmul(n2, win, "nn", F32, "mm_in", tn=win.shape[1] // 2)
    y_a = _sgu_fwd(z, g_v, ws, bs_t, g_a_out, D, "sgu_fwd")
    later = [k for k in REST_W if k != "w_in"]
    (yb, y), handed = _swa_fwd(z, sink_rows, g_b_out, y_a, "swa_fwd",
                                  exchange=_sibling_exchange([part[k] for k in later]))
    gathered = dict(zip(later, handed))
    wg2, wu2, wd2 = gathered["w2_gate"], gathered["w2_up"], gathered["w2_down"]
    wxkv = _blocks_to_cols(gathered["w_xkv"])
    wout = gathered["w_out"].reshape(D, D)
    wxq = gathered["w_xq"].reshape(D, D)
    wxo = gathered["w_xo"].reshape(D, D)
    h2 = _matmul(y, wout, "nn", F32, "mm_out", res=h1, tn=D)
    hx, _ = _rms_fwd(h2, g_x, "x_norm")
    mn, _ = _rms_fwd(ms, g_mem, "mem_norm")
    q = _matmul(hx, wxq, "nn", BF16, "mm_xq", tn=D)
    kv = _matmul(mn, wxkv, "nn", BF16, "mm_xkv")
    o = _xattn_fwd(q, kv, "xattn_fwd")
    h3 = _matmul(o, wxo, "nn", F32, "mm_xo", res=h2, tn=D)
    (h4, n4, a2, b2), _ = _ffn_fwd(h3, g_ffn2, wg2, wu2, wd2, "ffn2_fwd")
    dh4, dg_final, loss_part = _final_loss(h4, gf, tgt, "final_loss")

    grad_big, grad_small, recv_big = {}, {"g_final": dg_final}, {}
    order = _dw_block_order()
    (dn4, df2, da2, db2, s2), _ = _ffn_bwd_dx(dh4, a2, b2, wg2, wu2, wd2, "ffn2_bwd_dx")
    dh3, grad_small["g_ffn2"] = _rms_bwd(h3, g_ffn2, dn4, dh4, "ffn2_norm_bwd")
    (recv_big["w2_gate"], recv_big["w2_up"]), _ = _ffn_bwd_dw([(n4, da2), (n4, db2)], order, "ffn2_bwd_dw_gu")
    (recv_big["w2_down"],), _ = _ffn_bwd_dw([(s2, df2)], order, "ffn2_bwd_dw_d", tk=1024)

    do = _matmul(dh3, wxo, "nt", BF16, "mm_xo_dx", tn=D)
    grad_big["w_xo"] = _matmul(o, dh3, "tn", BF16, "mm_xo_dw", tm=D, tn=1024).reshape(N_DEV, D // N_DEV, D)
    dq, dkv = _xattn_bwd(q, kv, do, "xattn_bwd")
    dhx = _matmul(dq, wxq, "nt", BF16, "mm_xq_dx", tn=D)
    grad_big["w_xq"] = _matmul(hx, dq, "tn", BF16, "mm_xq_dw", tm=D, tn=1024).reshape(N_DEV, D // N_DEV, D)
    dmn = _matmul(dkv, wxkv, "nt", F32, "mm_xkv_dx")
    grad_big["w_xkv"] = _cols_to_blocks(_matmul(mn, dkv, "tn", BF16, "mm_xkv_dw", tm=1024, tn=1024))
    dh2, grad_small["g_x"] = _rms_bwd(h2, g_x, dhx, dh3, "x_norm_bwd")
    _, grad_small["g_mem"] = _rms_bwd(ms, g_mem, dmn, None, "mem_norm_bwd")

    dy = _matmul(dh2, wout, "nt", F32, "mm_out_dx", tn=D)
    grad_big["w_out"] = _matmul(y, dh2, "tn", BF16, "mm_out_dw", tm=D, tn=1024).reshape(N_DEV, D // N_DEV, D)
    dz_a, grad_small["w_s"], dbs_t, grad_small["g_v"], grad_small["g_a_out"] = _sgu_bwd(
        z, dy, g_v, ws, ws_t, bs_t, g_a_out, z.shape[1], "sgu_bwd")
    grad_small["b_s"] = dbs_t.T
    dz_q, dkv_b, grad_small["sinks"], grad_small["g_b_out"] = _swa_bwd(
        z, yb, dy, sink_rows, g_b_out, dz_a, "swa_bwd")
    dz = _paste_columns(dz_q, dkv_b, "dz_kv")
    dn2 = _matmul(dz, win, "nt", BF16, "mm_in_dx", tn=1024)
    grad_big["w_in"] = _cols_to_blocks(_matmul(n2, dz, "tn", BF16, "mm_in_dw", tm=D, tn=dz.shape[1] // 2))
    dh1, grad_small["g_mix"] = _rms_bwd(h1, g_mix, dn2, dh2, "mix_norm_bwd")

    (dn1, df1, da1, db1, s1), recv_mid = _ffn_bwd_dx(
        dh1, a1, b1, wg1, wu1, wd1, "ffn1_bwd_dx", exchange=_scatter_exchange([grad_big[k] for k in MID_W]))
    recv_big.update(zip(MID_W, recv_mid))
    dx, grad_small["g_ffn1"] = _rms_bwd(xs, g_ffn1, dn1, dh1, "ffn1_norm_bwd")
    (recv_big["w1_gate"], recv_big["w1_up"]), (recv_small,) = _ffn_bwd_dw(
        [(n1, da1), (n1, db1)], order, "ffn1_bwd_dw_gu",
        exchange=_gather_exchange([_pack([grad_small[k] for k in SMALL])], ALL_PEERS))
    (recv_big["w1_down"],), _ = _ffn_bwd_dw([(s1, df1)], order, "ffn1_bwd_dw_d", tk=1024)


    grads, deltas, new_m, new_v = {}, {}, {}, {}
    for k in BIG:
        shp = w[k].shape
        two_d = shp[1:]
        outs = _adamw(w[k].reshape(two_d), mom[k].reshape(two_d), var[k].reshape(two_d), recv_big[k], "adamw_" + k)
        grads[k], deltas[k], new_m[k], new_v[k] = [t.reshape(shp) for t in outs]
    packed = _adamw(_pack([w[k] for k in SMALL]), _pack([mom[k] for k in SMALL]), _pack([var[k] for k in SMALL]),
                    recv_small, "adamw_small")
    off = 0
    for k in SMALL:
        shp = w[k].shape
        size = 1
        for s in shp:
            size *= s
        for dst, src in zip((grads, deltas, new_m, new_v), packed):
            dst[k] = src.reshape(-1)[off:off + size].reshape(shp)
        off += size

    loss = lax.psum(loss_part[0, 0], AXES)
    return (loss, dx[None], *[grads[k] for k in WEIGHTS], *[deltas[k] for k in WEIGHTS],
            *[new_m[k] for k in WEIGHTS], *[new_v[k] for k in WEIGHTS])
```

```python
import jax
import jax.numpy as jnp
from jax import lax
from jax.experimental import pallas as pl
from jax.experimental.pallas import tpu as pltpu

F32 = jnp.float32
BF16 = jnp.bfloat16
SDS = jax.ShapeDtypeStruct
BS = pl.BlockSpec

N_DEV = 8
AXES = ("x", "y", "c")
EPS = 1e-5
CHUNK = 128
HEAD_DIM = 64
KV_HEADS = 2
X_HEADS = 4
NEG = -1e30
ADAM_LR = 0.001
ADAM_B1 = 0.9
ADAM_B2 = 0.999
ADAM_EPS = 1e-08
ADAM_WD = 0.01
ADAM_STEP = 10
MIB = 1 << 20
FFN_SUB_ROWS = 256
WEIGHTS = ['g_ffn1', 'w1_gate', 'w1_up', 'w1_down', 'g_mix', 'w_in', 'g_v', 'w_s', 'b_s', 'sinks', 'g_a_out',
           'g_b_out', 'w_out', 'g_x', 'g_mem', 'w_xq', 'w_xkv', 'w_xo', 'g_ffn2', 'w2_gate', 'w2_up', 'w2_down',
           'g_final']
BIG = ['w1_gate', 'w1_up', 'w1_down', 'w_in', 'w_out', 'w_xq', 'w_xkv', 'w_xo', 'w2_gate', 'w2_up', 'w2_down']
SMALL = [w for w in WEIGHTS if w not in BIG]
FFN1_W = ['w1_gate', 'w1_up', 'w1_down']
FFN2_W = ['w2_gate', 'w2_up', 'w2_down']
MID_W = ['w_in', 'w_out', 'w_xq', 'w_xkv', 'w_xo']
REST_W = MID_W + FFN2_W
TRANSPOSED_W = ('w1_gate', 'w1_up', 'w2_gate', 'w2_up', 'w_in')


def _params(sem=None, vmem_mib=48):
    return pltpu.CompilerParams(dimension_semantics=sem, vmem_limit_bytes=vmem_mib * MIB)


def _dot(a, b):
    return jnp.dot(a, b, preferred_element_type=F32)


def _dot_nt(a, b):
    return lax.dot_general(a, b, (((1,), (1,)), ((), ())), preferred_element_type=F32)


def _dot_tn(a, b):
    return lax.dot_general(a, b, (((0,), (0,)), ((), ())), preferred_element_type=F32)


def _rstd(v):
    return lax.rsqrt(jnp.mean(v * v, axis=-1, keepdims=True) + EPS)


def _norm_bwd(xhat, r, g, d):
    t = d * g
    return r * (t - xhat * jnp.mean(t * xhat, axis=-1, keepdims=True))


def _gelu(v):
    return 0.5 * v * (1.0 + lax.erf(v * 0.7071067811865476))


def _gelu_grad(v):
    return 0.5 * (1.0 + lax.erf(v * 0.7071067811865476)) + v * jnp.exp(-0.5 * v * v) * 0.3989422804014327


def _mesh_pos():
    x, y, c = lax.axis_index("x"), lax.axis_index("y"), lax.axis_index("c")
    return x, y, c, 4 * x + 2 * y + c


def _peer(x, y, c, k):
    px = 1 - x if k & 4 else x
    py = 1 - y if k & 2 else y
    pc = 1 - c if k & 1 else c
    return (px, py, pc), 4 * px + 2 * py + pc


ANY = BS(memory_space=pl.ANY)
DMA_SEM = pltpu.SemaphoreType.DMA
ALL_PEERS = (1, 2, 3, 4, 5, 6, 7)
CHIP_PEERS = (2, 4, 6)
SIBLING = 1


def _remote(src, dst, send, recv, peer):
    return pltpu.make_async_remote_copy(src_ref=src, dst_ref=dst, send_sem=send, recv_sem=recv, device_id=peer,
                                        device_id_type=pl.DeviceIdType.MESH)


class _Exchange:
    def __init__(self, ins, out_shapes, sems, copies, aliases=None):
        self.ins, self.out_shapes, self.sems, self.copies, self.aliases = ins, out_shapes, sems, copies, aliases or {}


def _gather_exchange(shards, peers):
    n, m = len(shards), len(peers)

    def copies(ins, outs, sems):
        send, recv, lsem = sems
        x, y, c, me = _mesh_pos()
        cps = [pltpu.make_async_copy(ins[w], outs[w].at[me], lsem.at[w]) for w in range(n)]
        for w in range(n):
            for j, k in enumerate(peers):
                cps.append(_remote(ins[w], outs[w].at[me], send.at[w, j], recv.at[w, j], _peer(x, y, c, k)[0]))
        return cps

    return _Exchange(shards, [SDS((N_DEV,) + s.shape, s.dtype) for s in shards],
                     [DMA_SEM((n, m)), DMA_SEM((n, m)), DMA_SEM((n,))], copies)


def _sibling_exchange(gathered):
    n = len(gathered)

    def copies(ins, outs, sems):
        send, recv = sems
        x, y, c, me = _mesh_pos()
        slots = [me] + [_peer(x, y, c, k)[1] for k in CHIP_PEERS]
        sib = _peer(x, y, c, SIBLING)[0]
        return [_remote(outs[w].at[s], outs[w].at[s], send.at[w, j], recv.at[w, j], sib)
                for w in range(n) for j, s in enumerate(slots)]

    return _Exchange(gathered, [SDS(g.shape, g.dtype) for g in gathered], [DMA_SEM((n, 4)), DMA_SEM((n, 4))], copies,
                     aliases={w: w for w in range(n)})


def _scatter_exchange(fulls):
    n, m = len(fulls), len(ALL_PEERS)

    def copies(ins, outs, sems):
        send, recv, lsem = sems
        x, y, c, me = _mesh_pos()
        cps = [pltpu.make_async_copy(ins[w].at[me], outs[w].at[me], lsem.at[w]) for w in range(n)]
        for w in range(n):
            for j, k in enumerate(ALL_PEERS):
                peer, p = _peer(x, y, c, k)
                cps.append(_remote(ins[w].at[p], outs[w].at[me], send.at[w, j], recv.at[w, j], peer))
        return cps

    return _Exchange(fulls, [SDS(f.shape, f.dtype) for f in fulls],
                     [DMA_SEM((n, m)), DMA_SEM((n, m)), DMA_SEM((n,))], copies)


def _call(body, *, name, args, in_specs, out_shape, out_specs, grid, sem, vmem_mib, scratch=(), exchange=None,
          first=None, last=None, aliases=None):
    aliases = dict(aliases or {})
    if exchange is None:
        return pl.pallas_call(body, name=name, out_shape=out_shape, grid=grid, in_specs=in_specs,
                              out_specs=out_specs, scratch_shapes=list(scratch), input_output_aliases=aliases,
                              compiler_params=_params(sem, vmem_mib))(*args), []
    ni, no, ns = len(args), len(out_shape), len(scratch)
    xi, xo = len(exchange.ins), len(exchange.out_shapes)

    def hosted(*refs):
        own_in, refs = refs[:ni], refs[ni:]
        x_in, refs = refs[:xi], refs[xi:]
        own_out, refs = refs[:no], refs[no:]
        x_out, refs = refs[:xo], refs[xo:]
        own_scr, x_sem = refs[:ns], refs[ns:]

        @pl.when(first())
        def _():
            for cp in exchange.copies(x_in, x_out, x_sem):
                cp.start()

        body(*own_in, *own_out, *own_scr)

        @pl.when(last())
        def _():
            for cp in exchange.copies(x_in, x_out, x_sem):
                cp.wait()

    outs = pl.pallas_call(
        hosted, name=name, out_shape=list(out_shape) + list(exchange.out_shapes), grid=grid,
        in_specs=list(in_specs) + [ANY] * xi, out_specs=list(out_specs) + [ANY] * xo,
        scratch_shapes=list(scratch) + list(exchange.sems),
        input_output_aliases={**aliases, **{ni + a: no + b for a, b in exchange.aliases.items()}},
        compiler_params=pltpu.CompilerParams(dimension_semantics=tuple("arbitrary" for _ in grid),
                                             vmem_limit_bytes=vmem_mib * MIB, has_side_effects=True),
    )(*args, *exchange.ins)
    return outs[:no], outs[no:]


def _gather_two_level(shards, name):
    n = len(shards)
    OWN_SIB, OWN_NEAR, OWN_FAR, RELAY, HAND_NEAR, HAND_FAR, HAND_DIAG = range(7)

    def body(*refs):
        ins, outs = refs[:n], refs[n:2 * n]
        send, recv, lsem = refs[2 * n:]
        x, y, c, me = _mesh_pos()
        flip_y = c == 0
        sib = (x, y, 1 - c)
        near = (jnp.where(flip_y, x, 1 - x), jnp.where(flip_y, 1 - y, y), c)
        far = (jnp.where(flip_y, 1 - x, x), jnp.where(flip_y, y, 1 - y), c)

        def slot_of(dev):
            return 4 * dev[0] + 2 * dev[1] + dev[2]

        s_near, s_far, s_diag = slot_of(near), slot_of(far), slot_of((1 - x, 1 - y, c))

        def copy(w, kind, src, slot, to):
            return _remote(src, outs[w].at[slot], send.at[w, kind], recv.at[w, kind], to)

        def own(w, kind, to):
            return copy(w, kind, ins[w], me, to)

        def arrival(w, kind, slot):
            return copy(w, kind, outs[w].at[slot], slot, sib)

        def hand(w, kind, slot):
            return copy(w, kind, outs[w].at[slot], slot, sib)

        local = [pltpu.make_async_copy(ins[w], outs[w].at[me], lsem.at[w]) for w in range(n)]
        for w in range(n):
            local[w].start()
            own(w, OWN_SIB, sib).start()
            own(w, OWN_NEAR, near).start()
            own(w, OWN_FAR, far).start()
        for w in range(n):
            arrival(w, OWN_NEAR, s_near).wait_recv()
            copy(w, RELAY, outs[w].at[s_near], s_near, far).start()
            hand(w, HAND_NEAR, s_near).start()
        for w in range(n):
            arrival(w, OWN_FAR, s_far).wait_recv()
            hand(w, HAND_FAR, s_far).start()
            arrival(w, RELAY, s_diag).wait_recv()
            hand(w, HAND_DIAG, s_diag).start()
        for w in range(n):
            arrival(w, OWN_SIB, jnp.bitwise_xor(me, 1)).wait_recv()
            arrival(w, HAND_NEAR, jnp.bitwise_xor(s_far, 1)).wait_recv()
            arrival(w, HAND_FAR, jnp.bitwise_xor(s_near, 1)).wait_recv()
            arrival(w, HAND_DIAG, jnp.bitwise_xor(s_diag, 1)).wait_recv()
            own(w, OWN_SIB, sib).wait_send()
            own(w, OWN_NEAR, near).wait_send()
            own(w, OWN_FAR, far).wait_send()
            copy(w, RELAY, outs[w].at[s_near], s_near, far).wait_send()
            hand(w, HAND_NEAR, s_near).wait_send()
            hand(w, HAND_FAR, s_far).wait_send()
            hand(w, HAND_DIAG, s_diag).wait_send()
            local[w].wait()

    return pl.pallas_call(
        body, name=name, out_shape=[SDS((N_DEV,) + s.shape, s.dtype) for s in shards],
        in_specs=[ANY] * n, out_specs=[ANY] * n,
        scratch_shapes=[DMA_SEM((n, 7)), DMA_SEM((n, 7)), DMA_SEM((n,))],
        compiler_params=pltpu.CompilerParams(has_side_effects=True),
    )(*shards)


def _run_exchange(exchange, name):
    xi, xo = len(exchange.ins), len(exchange.out_shapes)

    def body(*refs):
        cps = exchange.copies(refs[:xi], refs[xi:xi + xo], refs[xi + xo:])
        for cp in cps:
            cp.start()
        for cp in cps:
            cp.wait()

    return pl.pallas_call(
        body, name=name, out_shape=list(exchange.out_shapes), in_specs=[ANY] * xi, out_specs=[ANY] * xo,
        scratch_shapes=list(exchange.sems), input_output_aliases=dict(exchange.aliases),
        compiler_params=pltpu.CompilerParams(has_side_effects=True),
    )(*exchange.ins)


def _rms_fwd(h, g, name, tm=512, exchange=None):
    S, D = h.shape
    tm = min(tm, S)
    assert S % tm == 0, (S, tm)
    ni = S // tm

    def body(h_ref, g_ref, o_ref):
        hv = h_ref[...]
        o_ref[...] = (hv * _rstd(hv) * g_ref[...]).astype(o_ref.dtype)

    (out,), extra = _call(
        body, name=name, args=(h, g), out_shape=[SDS((S, D), BF16)], grid=(ni,),
        in_specs=[BS((tm, D), lambda i: (i, 0)), BS((1, D), lambda i: (0, 0))],
        out_specs=[BS((tm, D), lambda i: (i, 0))], sem=("parallel",), vmem_mib=32, exchange=exchange,
        first=lambda: pl.program_id(0) == 0, last=lambda: pl.program_id(0) == ni - 1)
    return out, extra


def _rms_bwd(h, g, dn, dres, name, tm=256):
    S, D = h.shape
    tm = min(tm, S)
    assert S % tm == 0, (S, tm)
    has_res = dres is not None

    def body(*refs):
        if has_res:
            h_ref, g_ref, dn_ref, dres_ref, dh_ref, dg_ref = refs
        else:
            h_ref, g_ref, dn_ref, dh_ref, dg_ref = refs
        hv = h_ref[...]
        r = _rstd(hv)
        xh = hv * r
        d = dn_ref[...].astype(F32)

        @pl.when(pl.program_id(0) == 0)
        def _():
            dg_ref[...] = jnp.zeros_like(dg_ref)

        dg_ref[...] += jnp.sum(d * xh, axis=0, keepdims=True)
        dh = _norm_bwd(xh, r, g_ref[...], d)
        dh_ref[...] = dres_ref[...] + dh if has_res else dh

    row = BS((tm, D), lambda i: (i, 0))
    vec = BS((1, D), lambda i: (0, 0))
    args = (h, g, dn) + ((dres,) if has_res else ())
    return pl.pallas_call(
        body, name=name, out_shape=[SDS((S, D), F32), SDS((1, D), F32)], grid=(S // tm,),
        in_specs=[row, vec, row] + ([row] if has_res else []), out_specs=[row, vec],
        compiler_params=_params(("arbitrary",), 40),
    )(*args)


def _final_loss(h, g, target, name, tm=256):
    S, D = h.shape
    tm = min(tm, S)
    assert S % tm == 0, (S, tm)

    def body(h_ref, g_ref, t_ref, dh_ref, dg_ref, loss_ref):
        hv = h_ref[...]
        r = _rstd(hv)
        xh = hv * r
        gv = g_ref[...]
        e = xh * gv - t_ref[...]

        @pl.when(pl.program_id(0) == 0)
        def _():
            dg_ref[...] = jnp.zeros_like(dg_ref)
            loss_ref[...] = jnp.zeros_like(loss_ref)

        loss_ref[...] += 0.5 * jnp.sum(jnp.mean(e * e, axis=-1, keepdims=True), axis=0, keepdims=True)
        dy = e * (1.0 / D)
        dg_ref[...] += jnp.sum(dy * xh, axis=0, keepdims=True)
        dh_ref[...] = _norm_bwd(xh, r, gv, dy)

    row = BS((tm, D), lambda i: (i, 0))
    vec = BS((1, D), lambda i: (0, 0))
    return pl.pallas_call(
        body, name=name, out_shape=[SDS((S, D), F32), SDS((1, D), F32), SDS((1, 128), F32)], grid=(S // tm,),
        in_specs=[row, vec, row], out_specs=[row, vec, BS((1, 128), lambda i: (0, 0))],
        compiler_params=_params(("arbitrary",), 40),
    )(h, g, target)


def _matmul(a, b, mode, out_dtype, name, res=None, tm=512, tn=512, tk=512):
    if mode == "tn":
        K, M = a.shape
        N = b.shape[1]
        tm, tn, tk = min(tm, M), min(tn, N), min(tk, K)
        assert M % tm == 0 and N % tn == 0 and K % tk == 0, (a.shape, b.shape, tm, tn, tk)
        nk = K // tk

        def body(a_ref, b_ref, o_ref, acc_ref):
            k = pl.program_id(2)

            @pl.when(k == 0)
            def _():
                acc_ref[...] = jnp.zeros_like(acc_ref)

            acc_ref[...] += _dot_tn(a_ref[...].astype(BF16), b_ref[...].astype(BF16))

            @pl.when(k == nk - 1)
            def _():
                o_ref[...] = acc_ref[...].astype(o_ref.dtype)

        return pl.pallas_call(
            body, name=name, out_shape=SDS((M, N), out_dtype), grid=(M // tm, N // tn, nk),
            in_specs=[BS((tk, tm), lambda i, j, k: (k, i)), BS((tk, tn), lambda i, j, k: (k, j))],
            out_specs=BS((tm, tn), lambda i, j, k: (i, j)), scratch_shapes=[pltpu.VMEM((tm, tn), F32)],
            compiler_params=_params(("parallel", "parallel", "arbitrary"), 48),
        )(a, b)

    M, K = a.shape
    N = b.shape[1] if mode == "nn" else b.shape[0]
    tm, tn = min(tm, M), min(tn, N)
    assert M % tm == 0 and N % tn == 0, (a.shape, b.shape, tm, tn)
    has_res = res is not None

    def body(*refs):
        if has_res:
            a_ref, b_ref, r_ref, o_ref = refs
        else:
            a_ref, b_ref, o_ref = refs
        av, bv = a_ref[...].astype(BF16), b_ref[...].astype(BF16)
        acc = _dot(av, bv) if mode == "nn" else _dot_nt(av, bv)
        if has_res:
            acc = acc + r_ref[...]
        o_ref[...] = acc.astype(o_ref.dtype)

    b_spec = BS((K, tn), lambda i, j: (0, j)) if mode == "nn" else BS((tn, K), lambda i, j: (j, 0))
    o_spec = BS((tm, tn), lambda i, j: (i, j))
    return pl.pallas_call(
        body, name=name, out_shape=SDS((M, N), out_dtype), grid=(M // tm, N // tn),
        in_specs=[BS((tm, K), lambda i, j: (i, 0)), b_spec] + ([o_spec] if has_res else []), out_specs=o_spec,
        compiler_params=_params(("parallel", "parallel"), 48),
    )(*((a, b) + ((res,) if has_res else ())))


def _ffn_fwd(h, g, wg, wu, wd, name, tm=512, exchange=None):
    S, D = h.shape
    nb, _, Fs = wg.shape
    tm = min(tm, S)
    sub = min(FFN_SUB_ROWS, tm)
    assert S % tm == 0 and tm % sub == 0, (S, tm, sub)

    def body(h_ref, g_ref, wg_ref, wu_ref, wd_ref, o_ref, n_ref, a_ref, b_ref):
        j = pl.program_id(1)

        @pl.when(j == 0)
        def _():
            hv = h_ref[...]
            n_ref[...] = (hv * _rstd(hv) * g_ref[...]).astype(BF16)
            o_ref[...] = jnp.zeros_like(o_ref)

        for r in range(0, tm, sub):
            rows = slice(r, r + sub)
            n = n_ref[rows, :]
            a = _dot(n, wg_ref[...]).astype(BF16)
            b = _dot(n, wu_ref[...]).astype(BF16)
            a_ref[rows, :] = a
            b_ref[rows, :] = b
            a, b = a.astype(F32), b.astype(F32)
            s = (a * jax.nn.sigmoid(a) * b).astype(BF16)
            o_ref[rows, :] += _dot(s, wd_ref[...])

        @pl.when(j == nb - 1)
        def _():
            o_ref[...] = h_ref[...] + 0.5 * o_ref[...]

    row = BS((tm, D), lambda i, j: (i, 0))
    wcol = BS((None, D, Fs), lambda i, j: (j, 0, 0))
    act = BS((None, tm, Fs), lambda i, j: (j, i, 0))
    ni = S // tm
    return _call(
        body, name=name, args=(h, g, wg, wu, wd),
        out_shape=[SDS((S, D), F32), SDS((S, D), BF16), SDS((nb, S, Fs), BF16), SDS((nb, S, Fs), BF16)],
        grid=(ni, nb),
        in_specs=[row, BS((1, D), lambda i, j: (0, 0)), wcol, wcol, BS((None, Fs, D), lambda i, j: (j, 0, 0))],
        out_specs=[row, row, act, act], sem=("parallel", "arbitrary"), vmem_mib=56, exchange=exchange,
        first=lambda: (pl.program_id(0) == 0) & (pl.program_id(1) == 0),
        last=lambda: (pl.program_id(0) == ni - 1) & (pl.program_id(1) == nb - 1))


def _ffn_bwd_dx(dh, a, b, wg, wu, wd, name, tm=512, exchange=None):
    S, D = dh.shape
    nb, _, Fs = wg.shape
    tm = min(tm, S)
    assert S % tm == 0, (S, tm)

    def body(dh_ref, a_ref, b_ref, wg_ref, wu_ref, wd_ref, dn_ref, df_ref, da_ref, db_ref, s_ref, ds_ref):
        j = pl.program_id(1)

        def first_matmul():
            ds_ref[...] = _dot_nt(df_ref[...], wd_ref[...])

        def pre_activation_cotangents():
            ds = ds_ref[...]
            av, bv = a_ref[...].astype(F32), b_ref[...].astype(F32)
            sig = jax.nn.sigmoid(av)
            sl = av * sig
            da = (ds * bv * (sig * (1.0 + av * (1.0 - sig)))).astype(BF16)
            db = (ds * sl).astype(BF16)
            da_ref[...] = da
            db_ref[...] = db
            s_ref[...] = (sl * bv).astype(BF16)
            return da, db

        def last_matmuls(da, db):
            dn_ref[...] += _dot_nt(da, wg_ref[...]) + _dot_nt(db, wu_ref[...])

        @pl.when(j == 0)
        def _():
            df_ref[...] = (0.5 * dh_ref[...]).astype(BF16)
            dn_ref[...] = jnp.zeros_like(dn_ref)
            first_matmul()

        @pl.when((j > 0) & (j < nb))
        def _():
            da, db = pre_activation_cotangents()
            first_matmul()
            last_matmuls(da, db)

        @pl.when(j == nb)
        def _():
            last_matmuls(*pre_activation_cotangents())

    row = BS((tm, D), lambda i, j: (i, 0))
    prev = BS((None, D, Fs), lambda i, j: (jnp.maximum(j - 1, 0), 0, 0))
    act = BS((None, tm, Fs), lambda i, j: (jnp.maximum(j - 1, 0), i, 0))
    ni = S // tm
    return _call(
        body, name=name, args=(dh, a, b, wg, wu, wd),
        out_shape=[SDS((S, D), F32), SDS((S, D), BF16)] + [SDS((nb, S, Fs), BF16)] * 3,
        grid=(ni, nb + 1),
        in_specs=[row, act, act, prev, prev, BS((None, Fs, D), lambda i, j: (jnp.minimum(j, nb - 1), 0, 0))],
        out_specs=[row, row, act, act, act], scratch=[pltpu.VMEM((tm, Fs), F32)],
        sem=("parallel", "arbitrary"), vmem_mib=56, exchange=exchange,
        first=lambda: (pl.program_id(0) == 0) & (pl.program_id(1) == 0),
        last=lambda: (pl.program_id(0) == ni - 1) & (pl.program_id(1) == nb))


DW_FLIPS = {0: (6, 2, 4, 0), 1: (6, 4, 2, 0)}
N_CHIPS = N_DEV // 2


def _dw_block_order():
    x, y, c, me = _mesh_pos()
    steps = [jnp.array([v for mine, sib in zip(DW_FLIPS[core], DW_FLIPS[1 - core]) for v in (sib ^ 1, mine)], jnp.int32)
             for core in (0, 1)]
    return jnp.bitwise_xor(me.astype(jnp.int32), jnp.where(c == 0, steps[0], steps[1]))


def _ffn_bwd_dw(pairs, order, name, tk=512, exchange=None):
    npair = len(pairs)
    S = pairs[0][0].shape[-2]
    nb, half = N_DEV, N_DEV // 2
    tk = min(tk, S)
    assert S % tk == 0, (S, tk)
    nk = S // tk
    shapes = [(lhs.shape[-1], rhs.shape[-1]) for lhs, rhs in pairs]
    xi = len(exchange.ins) if exchange else 0
    xo = len(exchange.out_shapes) if exchange else 0

    def body(order_ref, *rest):
        tiles, rest = rest[:2 * npair], rest[2 * npair:]
        x_in, rest = rest[:xi], rest[xi:]
        recv_bufs, rest = rest[:npair], rest[npair:]
        x_out, rest = rest[:xo], rest[xo:]
        accs, rest = rest[:npair], rest[npair:]
        out_t, rest = rest[:npair], rest[npair:]
        out_m, rest = rest[:npair], rest[npair:]
        land, rest = rest[:npair], rest[npair:]
        (send_t, recv_t, send_m, recv_m, lsem, credit), x_sem = rest[:6], rest[6:]
        t, k = pl.program_id(0), pl.program_id(1)
        x, y, c, me = _mesh_pos()
        sibling = (x, y, 1 - c)
        chip = 2 * x + y

        if exchange:
            @pl.when((t == 0) & (k == 0))
            def _():
                for cp in exchange.copies(x_in, x_out, x_sem):
                    cp.start()

        @pl.when(k == 0)
        def _():
            for acc in accs:
                acc[...] = jnp.zeros_like(acc)

        for w in range(npair):
            accs[w][...] += _dot_tn(tiles[2 * w][...], tiles[2 * w + 1][...])

        def to_sibling(w, i):
            return _remote(out_t[w], land[w], send_t.at[w, i], recv_t.at[w, i], sibling)

        def to_owner(w, i):
            dst = recv_bufs[w].at[chip]
            if i == half - 1:
                return pltpu.make_async_copy(out_m[w], dst, lsem.at[w])
            p = jnp.bitwise_xor(me, jnp.where(c == 0, DW_FLIPS[0][i], DW_FLIPS[1][i]))
            return _remote(out_m[w], dst, send_m.at[w, i], recv_m.at[w, i], (p >> 2, (p >> 1) & 1, p & 1))

        for i in range(half):
            @pl.when((t == 2 * i) & (k == nk - 1))
            def _(i=i):
                if i >= 1:
                    pl.semaphore_wait(credit, 1)
                for w in range(npair):
                    if i >= 1:
                        to_sibling(w, i - 1).wait_send()
                    out_t[w][...] = accs[w][...].astype(BF16)
                    to_sibling(w, i).start()

            @pl.when((t == 2 * i + 1) & (k == nk - 1))
            def _(i=i):
                for w in range(npair):
                    to_sibling(w, i).wait_recv()
                    if i >= 1:
                        to_owner(w, i - 1).wait_send()
                    out_m[w][...] = (accs[w][...] + land[w][...].astype(F32)).astype(BF16)
                if i < half - 1:
                    pl.semaphore_signal(credit, inc=1, device_id=sibling, device_id_type=pl.DeviceIdType.MESH)
                for w in range(npair):
                    to_owner(w, i).start()

        @pl.when((t == nb - 1) & (k == nk - 1))
        def _():
            for w in range(npair):
                to_sibling(w, half - 1).wait_send()
                to_owner(w, half - 1).wait()
                for i in range(half - 1):
                    to_owner(w, i).wait_recv()
            if exchange:
                for cp in exchange.copies(x_in, x_out, x_sem):
                    cp.wait()

    def tile_spec(arr):
        if arr.ndim == 3:
            return BS((None, tk, arr.shape[-1]), lambda t, k, o: (o[t], k, 0))
        return BS((tk, arr.shape[-1]), lambda t, k, o: (k, 0))

    flat = [a for pair in pairs for a in pair]
    grid_spec = pltpu.PrefetchScalarGridSpec(
        num_scalar_prefetch=1, grid=(nb, nk),
        in_specs=[tile_spec(a) for a in flat] + [ANY] * xi, out_specs=[ANY] * (npair + xo),
        scratch_shapes=[pltpu.VMEM(s, F32) for s in shapes] + [pltpu.VMEM(s, BF16) for s in shapes] * 3
        + [DMA_SEM((npair, half)), DMA_SEM((npair, half)), DMA_SEM((npair, half - 1)), DMA_SEM((npair, half - 1)),
           DMA_SEM((npair,)), pltpu.SemaphoreType.REGULAR]
        + (list(exchange.sems) if exchange else []))
    outs = pl.pallas_call(
        body, name=name, grid_spec=grid_spec,
        out_shape=[SDS((N_CHIPS,) + s, BF16) for s in shapes] + (list(exchange.out_shapes) if exchange else []),
        compiler_params=pltpu.CompilerParams(dimension_semantics=("arbitrary", "arbitrary"),
                                             vmem_limit_bytes=54 * MIB, has_side_effects=True),
    )(order, *flat, *(exchange.ins if exchange else ()))
    return outs[:npair], outs[npair:]


def _sgu_parts(z, gv, ws_ref, bst_ref, groups):
    da = z.shape[1] // 2
    zu, zv = z[:, :da], z[:, da:]
    u, v = _gelu(zu), _gelu(zv)
    rv = _rstd(v)
    vhat = v * rv
    vn = (vhat * gv).astype(BF16)
    tri = lax.broadcasted_iota(jnp.int32, (CHUNK, CHUNK), 0) >= lax.broadcasted_iota(jnp.int32, (CHUNK, CHUNK), 1)
    pieces = []
    for g in range(groups):
        w = jnp.where(tri, ws_ref[g], 0.0).astype(BF16)
        pieces.append(_dot(w, vn[:, g * CHUNK:(g + 1) * CHUNK]) + bst_ref[:, g:g + 1])
    sv = jnp.concatenate(pieces, axis=1)
    return dict(zu=zu, zv=zv, u=u, rv=rv, vhat=vhat, vn=vn, sv=sv, tri=tri)


SGU_CHUNKS = 4


def _sgu_fwd(z, g_v, w_s, b_st, g_a, width, name):
    S = z.shape[0]
    groups = w_s.shape[0]
    da = groups * CHUNK
    rows = SGU_CHUNKS * CHUNK
    assert S % rows == 0, (S, rows)

    def body(z_ref, gv_ref, ws_ref, bst_ref, ga_ref, o_ref):
        for ch in range(SGU_CHUNKS):
            blk = slice(ch * CHUNK, (ch + 1) * CHUNK)
            p = _sgu_parts(z_ref[blk, :], gv_ref[...], ws_ref, bst_ref, groups)
            ya = p["u"] * p["sv"]
            o_ref[blk, :] = (ya * _rstd(ya) * ga_ref[...]).astype(BF16)

    vec = BS((1, da), lambda i: (0, 0))
    return pl.pallas_call(
        body, name=name, out_shape=SDS((S, width), BF16), grid=(S // rows,),
        in_specs=[BS((rows, 2 * da), lambda i: (i, 0)), vec, BS((groups, CHUNK, CHUNK), lambda i: (0, 0, 0)),
                  BS((CHUNK, groups), lambda i: (0, 0)), vec],
        out_specs=BS((rows, da), lambda i: (i, 0)), compiler_params=_params(("parallel",), 32),
    )(z, g_v, w_s, b_st, g_a)


def _sgu_bwd(z, dy, g_v, w_s, w_st, b_st, g_a, width, name):
    S = z.shape[0]
    groups = w_s.shape[0]
    da = groups * CHUNK
    rows = SGU_CHUNKS * CHUNK
    assert S % rows == 0, (S, rows)

    def body(z_ref, dy_ref, gv_ref, ws_ref, wst_ref, bst_ref, ga_ref, dz_ref, dws_ref, dbst_ref, dgv_ref, dga_ref):
        @pl.when(pl.program_id(0) == 0)
        def _():
            dws_ref[...] = jnp.zeros_like(dws_ref)
            dbst_ref[...] = jnp.zeros_like(dbst_ref)
            dgv_ref[...] = jnp.zeros_like(dgv_ref)
            dga_ref[...] = jnp.zeros_like(dga_ref)

        gv = gv_ref[...]
        tri_t = (lax.broadcasted_iota(jnp.int32, (CHUNK, CHUNK), 0)
                 <= lax.broadcasted_iota(jnp.int32, (CHUNK, CHUNK), 1))
        lane = lax.broadcasted_iota(jnp.int32, (CHUNK, groups), 1)
        for ch in range(SGU_CHUNKS):
            blk = slice(ch * CHUNK, (ch + 1) * CHUNK)
            p = _sgu_parts(z_ref[blk, :], gv, ws_ref, bst_ref, groups)
            u, sv, tri = p["u"], p["sv"], p["tri"]
            ya = u * sv
            ra = _rstd(ya)
            yhat = ya * ra
            d = dy_ref[blk, :]
            dga_ref[...] += jnp.sum(d * yhat, axis=0, keepdims=True)
            dya = _norm_bwd(yhat, ra, ga_ref[...], d)
            du = dya * sv
            dsv = dya * u
            dsv_b = dsv.astype(BF16)
            dvn = []
            dbs = jnp.zeros((CHUNK, groups), F32)
            for g in range(groups):
                cols = slice(g * CHUNK, (g + 1) * CHUNK)
                dbs = dbs + jnp.where(lane == g, jnp.sum(dsv[:, cols], axis=1, keepdims=True), 0.0)
                dws_ref[g] += jnp.where(tri, _dot_nt(dsv_b[:, cols], p["vn"][:, cols]), 0.0)
                wt = jnp.where(tri_t, wst_ref[g], 0.0).astype(BF16)
                dvn.append(_dot(wt, dsv_b[:, cols]))
            dbst_ref[...] += dbs
            dvn = jnp.concatenate(dvn, axis=1)
            dgv_ref[...] += jnp.sum(dvn * p["vhat"], axis=0, keepdims=True)
            dv = _norm_bwd(p["vhat"], p["rv"], gv, dvn)
            dz_ref[blk, :] = jnp.concatenate([du * _gelu_grad(p["zu"]), dv * _gelu_grad(p["zv"])], axis=1)

    vec = BS((1, da), lambda i: (0, 0))
    wsq = BS((groups, CHUNK, CHUNK), lambda i: (0, 0, 0))
    bsq = BS((CHUNK, groups), lambda i: (0, 0))
    return pl.pallas_call(
        body, name=name,
        out_shape=[SDS((S, width), F32), SDS((groups, CHUNK, CHUNK), F32), SDS((CHUNK, groups), F32),
                   SDS((1, da), F32), SDS((1, da), F32)],
        grid=(S // rows,),
        in_specs=[BS((rows, 2 * da), lambda i: (i, 0)), BS((rows, da), lambda i: (i, 0)), vec, wsq, wsq, bsq, vec],
        out_specs=[BS((rows, 2 * da), lambda i: (i, 0)), wsq, bsq, vec, vec],
        compiler_params=_params(("arbitrary",), 32),
    )(z, dy, g_v, w_s, w_st, b_st, g_a)


def _swa_mask(i, group):
    row = lax.broadcasted_iota(jnp.int32, (group * CHUNK, 2 * CHUNK), 0) & (CHUNK - 1)
    col = lax.broadcasted_iota(jnp.int32, (group * CHUNK, 2 * CHUNK), 1)
    d = row + CHUNK - col
    return (d >= 0) & (d < CHUNK) & jnp.logical_or(i > 0, col >= CHUNK)


def _stack_heads(t, g, group):
    return jnp.concatenate([t[:, h * HEAD_DIM:(h + 1) * HEAD_DIM] for h in range(g * group, (g + 1) * group)], axis=0)


def _unstack_heads(stacked, group):
    return [stacked[h * CHUNK:(h + 1) * CHUNK] for h in range(group)]


def _swa_probs(qh, kh, sink, mask):
    s = jnp.where(mask, _dot_nt(qh, kh) * (HEAD_DIM ** -0.5), NEG)
    m = jnp.maximum(jnp.max(s, axis=-1, keepdims=True), sink)
    e = jnp.exp(s - m)
    es = jnp.exp(sink - m)
    inv = 1.0 / (jnp.sum(e, axis=-1, keepdims=True) + es)
    return e * inv, es * inv


SWA_QBLOCKS = 2


def _swa_specs(db, npair, clamp):
    kvw = 2 * KV_HEADS * HEAD_DIM
    cur = (lambda p: jnp.minimum(p, npair - 1)) if clamp else (lambda p: p)
    rows = SWA_QBLOCKS * CHUNK
    q_spec = BS((rows, db), lambda p: (cur(p), 2))
    kc_spec = BS((rows, kvw), lambda p: (cur(p), 3 * db // kvw))
    kp_spec = BS((CHUNK, kvw), lambda p: (jnp.maximum(SWA_QBLOCKS * cur(p) - 1, 0), 3 * db // kvw))
    return q_spec, kc_spec, kp_spec


def _swa_keys(kp_ref, kc_ref, qb):
    if qb == 0:
        return jnp.concatenate([kp_ref[...], kc_ref[:CHUNK, :]], axis=0).astype(BF16)
    return kc_ref[(qb - 1) * CHUNK:(qb + 1) * CHUNK, :].astype(BF16)


def _swa_fwd(z, sink_rows, g_b, y_buf, name, exchange=None):
    S = z.shape[0]
    db = g_b.shape[1]
    heads = db // HEAD_DIM
    group = heads // KV_HEADS
    rows = SWA_QBLOCKS * CHUNK
    assert S % rows == 0, (S, rows)
    npair = S // rows

    def body(q_ref, kc_ref, kp_ref, sk_ref, gb_ref, _, yb_ref, ybn_ref):
        for qb in range(SWA_QBLOCKS):
            blk = slice(qb * CHUNK, (qb + 1) * CHUNK)
            mask = _swa_mask(SWA_QBLOCKS * pl.program_id(0) + qb, group)
            q = q_ref[blk, :].astype(BF16)
            kv = _swa_keys(kp_ref, kc_ref, qb)
            outs = []
            for g in range(KV_HEADS):
                kg = kv[:, g * HEAD_DIM:(g + 1) * HEAD_DIM]
                vg = kv[:, (KV_HEADS + g) * HEAD_DIM:(KV_HEADS + g + 1) * HEAD_DIM]
                srows = slice(g * group * CHUNK, (g + 1) * group * CHUNK)
                p, _ = _swa_probs(_stack_heads(q, g, group), kg, sk_ref[srows, :], mask)
                outs += _unstack_heads(_dot(p.astype(BF16), vg), group)
            yb = jnp.concatenate(outs, axis=1)
            yb_ref[blk, :] = yb
            ybn_ref[blk, :] = (yb * _rstd(yb) * gb_ref[...]).astype(BF16)

    q_spec, kc_spec, kp_spec = _swa_specs(db, npair, False)
    assert y_buf.shape[1] == 2 * db, y_buf.shape
    return _call(
        body, name=name, args=(z, z, z, sink_rows, g_b, y_buf),
        out_shape=[SDS((S, db), F32), SDS(y_buf.shape, BF16)], grid=(npair,),
        in_specs=[q_spec, kc_spec, kp_spec, BS((heads * CHUNK, 1), lambda p: (0, 0)), BS((1, db), lambda p: (0, 0)),
                  ANY],
        out_specs=[BS((rows, db), lambda p: (p, 0)), BS((rows, db), lambda p: (p, 1))], aliases={5: 1},
        sem=("parallel",), vmem_mib=32, exchange=exchange,
        first=lambda: pl.program_id(0) == 0, last=lambda: pl.program_id(0) == npair - 1)


def _swa_bwd(z, yb, dy, sink_rows, g_b, dz_buf, name):
    S = z.shape[0]
    db = g_b.shape[1]
    heads = db // HEAD_DIM
    group = heads // KV_HEADS
    rows = SWA_QBLOCKS * CHUNK
    assert SWA_QBLOCKS == 2 and S % rows == 0, (S, rows)
    npair = S // rows
    kvw = 2 * KV_HEADS * HEAD_DIM

    def body(q_ref, kc_ref, kp_ref, yb_ref, dy_ref, sk_ref, gb_ref, _, dq_ref, dkv_ref, dsk_ref, dgb_ref,
             done_ref, part_ref):
        p = pl.program_id(0)

        @pl.when(p == 0)
        def _():
            done_ref[...] = jnp.zeros_like(done_ref)
            part_ref[...] = jnp.zeros_like(part_ref)
            dsk_ref[...] = jnp.zeros_like(dsk_ref)
            dgb_ref[...] = jnp.zeros_like(dgb_ref)

        @pl.when(p < npair)
        def _():
            lane = lax.broadcasted_iota(jnp.int32, (1, heads), 1)
            dsinks = jnp.zeros((1, heads), F32)
            dgb = jnp.zeros((1, db), F32)
            contribs = []
            for qb in range(SWA_QBLOCKS):
                blk = slice(qb * CHUNK, (qb + 1) * CHUNK)
                mask = _swa_mask(SWA_QBLOCKS * p + qb, group)
                ybv = yb_ref[blk, :]
                rb = _rstd(ybv)
                yhat = ybv * rb
                d = dy_ref[blk, :]
                dgb = dgb + jnp.sum(d * yhat, axis=0, keepdims=True)
                do = _norm_bwd(yhat, rb, gb_ref[...], d).astype(BF16)
                q = q_ref[blk, :].astype(BF16)
                kv = _swa_keys(kp_ref, kc_ref, qb)
                dqs, dk, dv = [], [], []
                for g in range(KV_HEADS):
                    kg = kv[:, g * HEAD_DIM:(g + 1) * HEAD_DIM]
                    vg = kv[:, (KV_HEADS + g) * HEAD_DIM:(KV_HEADS + g + 1) * HEAD_DIM]
                    srows = slice(g * group * CHUNK, (g + 1) * group * CHUNK)
                    qg, dog = _stack_heads(q, g, group), _stack_heads(do, g, group)
                    pr, ps = _swa_probs(qg, kg, sk_ref[srows, :], mask)
                    dp = _dot_nt(dog, vg)
                    dr = jnp.sum(pr * dp, axis=-1, keepdims=True)
                    ds = (pr * (dp - dr) * (HEAD_DIM ** -0.5)).astype(BF16)
                    for h, t in enumerate(_unstack_heads(ps * dr, group)):
                        dsinks = dsinks - jnp.where(lane == g * group + h, jnp.sum(t, axis=0, keepdims=True), 0.0)
                    dqs += _unstack_heads(_dot(ds, kg), group)
                    dk.append(_dot_tn(ds, qg))
                    dv.append(_dot_tn(pr.astype(BF16), dog))
                dq_ref[blk, :] = jnp.concatenate(dqs, axis=1)
                contribs.append(jnp.concatenate(dk + dv, axis=1))
            dsk_ref[...] += dsinks
            dgb_ref[...] += dgb
            first, second = contribs
            dkv_ref[:CHUNK, :] = done_ref[...]
            dkv_ref[CHUNK:, :] = part_ref[...] + first[:CHUNK]
            done_ref[...] = first[CHUNK:] + second[:CHUNK]
            part_ref[...] = second[CHUNK:]

        @pl.when(p == npair)
        def _():
            dkv_ref[:CHUNK, :] = done_ref[...]
            dkv_ref[CHUNK:, :] = part_ref[...]

    q_spec, kc_spec, kp_spec = _swa_specs(db, npair, True)
    cur = BS((rows, db), lambda p: (jnp.minimum(p, npair - 1), 0))
    return pl.pallas_call(
        body, name=name,
        out_shape=[SDS(dz_buf.shape, F32), SDS((S, kvw), F32), SDS((1, heads), F32), SDS((1, db), F32)],
        grid=(npair + 1,),
        in_specs=[q_spec, kc_spec, kp_spec, cur, BS((rows, db), lambda p: (jnp.minimum(p, npair - 1), 1)),
                  BS((heads * CHUNK, 1), lambda p: (0, 0)), BS((1, db), lambda p: (0, 0)), ANY],
        out_specs=[BS((rows, db), lambda p: (jnp.minimum(p, npair - 1), 2)),
                   BS((rows, kvw), lambda p: (jnp.maximum(p - 1, 0), 0)), BS((1, heads), lambda p: (0, 0)),
                   BS((1, db), lambda p: (0, 0))],
        scratch_shapes=[pltpu.VMEM((CHUNK, kvw), F32), pltpu.VMEM((CHUNK, kvw), F32)],
        input_output_aliases={7: 0}, compiler_params=_params(("arbitrary",), 32),
    )(z, z, z, yb, dy, sink_rows, g_b, dz_buf)


def _paste_columns(dst, src, name, tm=512):
    S, w = src.shape
    tm = min(tm, S)
    assert S % tm == 0 and dst.shape[1] % w == 0, (dst.shape, src.shape, tm)
    last = dst.shape[1] // w - 1

    def body(src_ref, _, out_ref):
        out_ref[...] = src_ref[...].astype(out_ref.dtype)

    return pl.pallas_call(
        body, name=name, out_shape=SDS(dst.shape, dst.dtype), grid=(S // tm,),
        in_specs=[BS((tm, w), lambda i: (i, 0)), ANY], out_specs=BS((tm, w), lambda i: (i, last)),
        input_output_aliases={1: 0}, compiler_params=_params(("parallel",), 32),
    )(src, dst)


def _xattn_probs(qh, kh, hd):
    s = _dot_nt(qh, kh) * (hd ** -0.5)
    e = jnp.exp(s - jnp.max(s, axis=-1, keepdims=True))
    return e / jnp.sum(e, axis=-1, keepdims=True)


def _xattn_fwd(q, kv, name, tq=512):
    S, D = q.shape
    M = kv.shape[0]
    hd = D // X_HEADS
    tq = min(tq, S)
    assert S % tq == 0, (S, tq)

    def body(q_ref, kv_ref, o_ref):
        for h in range(X_HEADS):
            cols = slice(h * hd, (h + 1) * hd)
            p = _xattn_probs(q_ref[:, cols], kv_ref[:, cols], hd)
            o_ref[:, cols] = _dot(p.astype(BF16), kv_ref[:, D + h * hd:D + (h + 1) * hd]).astype(BF16)

    row = BS((tq, D), lambda i: (i, 0))
    return pl.pallas_call(
        body, name=name, out_shape=SDS((S, D), BF16), grid=(S // tq,),
        in_specs=[row, BS((M, 2 * D), lambda i: (0, 0))], out_specs=row, compiler_params=_params(("parallel",), 40),
    )(q, kv)


def _xattn_bwd(q, kv, do, name, tq=512):
    S, D = q.shape
    M = kv.shape[0]
    hd = D // X_HEADS
    tq = min(tq, S)
    assert S % tq == 0, (S, tq)

    def body(q_ref, kv_ref, do_ref, dq_ref, dkv_ref):
        @pl.when(pl.program_id(0) == 0)
        def _():
            dkv_ref[...] = jnp.zeros_like(dkv_ref)

        for h in range(X_HEADS):
            cols = slice(h * hd, (h + 1) * hd)
            vcols = slice(D + h * hd, D + (h + 1) * hd)
            qh, kh, vh, doh = q_ref[:, cols], kv_ref[:, cols], kv_ref[:, vcols], do_ref[:, cols]
            p = _xattn_probs(qh, kh, hd)
            dp = _dot_nt(doh, vh)
            ds = (p * (dp - jnp.sum(p * dp, axis=-1, keepdims=True)) * (hd ** -0.5)).astype(BF16)
            dq_ref[:, cols] = _dot(ds, kh).astype(BF16)
            dkv_ref[:, cols] += _dot_tn(ds, qh)
            dkv_ref[:, vcols] += _dot_tn(p.astype(BF16), doh)

    row = BS((tq, D), lambda i: (i, 0))
    full = BS((M, 2 * D), lambda i: (0, 0))
    return pl.pallas_call(
        body, name=name, out_shape=[SDS((S, D), BF16), SDS((M, 2 * D), F32)], grid=(S // tq,),
        in_specs=[row, full, row], out_specs=[row, full], compiler_params=_params(("arbitrary",), 40),
    )(q, kv, do)


def _row_tile(rows, cap):
    best = 8
    for t in range(8, min(rows, cap) + 1, 8):
        if rows % t == 0:
            best = t
    assert rows % best == 0, (rows, cap)
    return best


ADAM_TILE_ELEMS = 256 * 1024


def _adamw(w, m, v, recv, name):
    R, C = w.shape
    slots = recv.shape[0]
    tr = _row_tile(R, max(8, ADAM_TILE_ELEMS // C))

    def body(w_ref, m_ref, v_ref, r_ref, g_ref, d_ref, nm_ref, nv_ref):
        g = r_ref[0].astype(F32)
        for s in range(1, slots):
            g = g + r_ref[s].astype(F32)
        mn = ADAM_B1 * m_ref[...] + (1.0 - ADAM_B1) * g
        vn = ADAM_B2 * v_ref[...] + (1.0 - ADAM_B2) * jnp.square(g)
        m_hat = mn / (1.0 - ADAM_B1 ** ADAM_STEP)
        v_hat = vn / (1.0 - ADAM_B2 ** ADAM_STEP)
        g_ref[...] = g
        d_ref[...] = -ADAM_LR * (m_hat / (jnp.sqrt(v_hat) + ADAM_EPS) + ADAM_WD * w_ref[...])
        nm_ref[...] = mn
        nv_ref[...] = vn

    row = BS((tr, C), lambda i: (i, 0))
    return pl.pallas_call(
        body, name=name, out_shape=[SDS((R, C), F32)] * 4, grid=(R // tr,),
        in_specs=[row, row, row, BS((slots, tr, C), lambda i: (0, i, 0))], out_specs=[row] * 4,
        compiler_params=_params(("parallel",), 40),
    )(w, m, v, recv)


def _cols_to_blocks(full):
    r, c = full.shape
    return full.reshape(r, N_DEV, c // N_DEV).transpose(1, 0, 2)


def _blocks_to_cols(blocks):
    n, r, c = blocks.shape
    return blocks.transpose(1, 0, 2).reshape(r, n * c)


def _pack(parts):
    flat = jnp.concatenate([p.reshape(-1).astype(F32) for p in parts])
    pad = (-flat.shape[0]) % (128 * 128)
    return jnp.pad(flat, (0, pad)).reshape(-1, 128)


def kernel(x, mem, g_ffn1, w1_gate, w1_up, w1_down, g_mix, w_in, g_v, w_s, b_s, sinks, g_a_out, g_b_out, w_out, g_x, g_mem, w_xq, w_xkv, w_xo, g_ffn2, w2_gate, w2_up, w2_down, g_final, loss_target, m_g_ffn1, m_w1_gate, m_w1_up, m_w1_down, m_g_mix, m_w_in, m_g_v, m_w_s, m_b_s, m_sinks, m_g_a_out, m_g_b_out, m_w_out, m_g_x, m_g_mem, m_w_xq, m_w_xkv, m_w_xo, m_g_ffn2, m_w2_gate, m_w2_up, m_w2_down, m_g_final, v_g_ffn1, v_w1_gate, v_w1_up, v_w1_down, v_g_mix, v_w_in, v_g_v, v_w_s, v_b_s, v_sinks, v_g_a_out, v_g_b_out, v_w_out, v_g_x, v_g_mem, v_w_xq, v_w_xkv, v_w_xo, v_g_ffn2, v_w2_gate, v_w2_up, v_w2_down, v_g_final):
    w = dict(g_ffn1=g_ffn1, w1_gate=w1_gate, w1_up=w1_up, w1_down=w1_down, g_mix=g_mix, w_in=w_in, g_v=g_v, w_s=w_s,
             b_s=b_s, sinks=sinks, g_a_out=g_a_out, g_b_out=g_b_out, w_out=w_out, g_x=g_x, g_mem=g_mem, w_xq=w_xq,
             w_xkv=w_xkv, w_xo=w_xo, g_ffn2=g_ffn2, w2_gate=w2_gate, w2_up=w2_up, w2_down=w2_down, g_final=g_final)
    mom = dict(g_ffn1=m_g_ffn1, w1_gate=m_w1_gate, w1_up=m_w1_up, w1_down=m_w1_down, g_mix=m_g_mix, w_in=m_w_in,
               g_v=m_g_v, w_s=m_w_s, b_s=m_b_s, sinks=m_sinks, g_a_out=m_g_a_out, g_b_out=m_g_b_out, w_out=m_w_out,
               g_x=m_g_x, g_mem=m_g_mem, w_xq=m_w_xq, w_xkv=m_w_xkv, w_xo=m_w_xo, g_ffn2=m_g_ffn2,
               w2_gate=m_w2_gate, w2_up=m_w2_up, w2_down=m_w2_down, g_final=m_g_final)
    var = dict(g_ffn1=v_g_ffn1, w1_gate=v_w1_gate, w1_up=v_w1_up, w1_down=v_w1_down, g_mix=v_g_mix, w_in=v_w_in,
               g_v=v_g_v, w_s=v_w_s, b_s=v_b_s, sinks=v_sinks, g_a_out=v_g_a_out, g_b_out=v_g_b_out, w_out=v_w_out,
               g_x=v_g_x, g_mem=v_g_mem, w_xq=v_w_xq, w_xkv=v_w_xkv, w_xo=v_w_xo, g_ffn2=v_g_ffn2,
               w2_gate=v_w2_gate, w2_up=v_w2_up, w2_down=v_w2_down, g_final=v_g_final)

    xs, ms, tgt = x[0], mem[0], loss_target[0]
    D = xs.shape[1]
    d_a = w_s.shape[1] * CHUNK
    d_b = D - d_a
    kvw = 2 * KV_HEADS * HEAD_DIM

    shard = {k: w[k][0].astype(BF16) for k in BIG}
    wg1, wu1, wd1 = _gather_two_level([shard[k] for k in FFN1_W], "ag_ffn1")
    gf = g_final.reshape(1, D)
    ws, ws_t, bs_t = w_s[0], jnp.swapaxes(w_s[0], 1, 2), b_s[0].T
    sink_rows = jnp.repeat(sinks.reshape(-1), CHUNK).reshape(-1, 1)

    (h1, n1, a1, b1), part = _ffn_fwd(xs, g_ffn1, wg1, wu1, wd1, "ffn1_fwd",
                                      exchange=_gather_exchange([shard[k] for k in REST_W], CHIP_PEERS))
    part = dict(zip(REST_W, part))
    n2, (win_blocks,) = _rms_fwd(h1, g_mix, "mix_norm", exchange=_sibling_exchange([part["w_in"]]))
    win = _blocks_to_cols(win_blocks)
    z = _matmul(n2, win, "nn", F32, "mm_in", tn=win.shape[1] // 2)
    y_a = _sgu_fwd(z, g_v, ws, bs_t, g_a_out, D, "sgu_fwd")
    later = [k for k in REST_W if k != "w_in"]
    (yb, y), handed = _swa_fwd(z, sink_rows, g_b_out, y_a, "swa_fwd",
                                  exchange=_sibling_exchange([part[k] for k in later]))
    gathered = dict(zip(later, handed))
    wg2, wu2, wd2 = gathered["w2_gate"], gathered["w2_up"], gathered["w2_down"]
    wxkv = _blocks_to_cols(gathered["w_xkv"])
    wout = gathered["w_out"].reshape(D, D)
    wxq = gathered["w_xq"].reshape(D, D)
    wxo = gathered["w_xo"].reshape(D, D)
    h2 = _matmul(y, wout, "nn", F32, "mm_out", res=h1, tn=D)
    hx, _ = _rms_fwd(h2, g_x, "x_norm")
    mn, _ = _rms_fwd(ms, g_mem, "mem_norm")
    q = _matmul(hx, wxq, "nn", BF16, "mm_xq", tn=D)
    kv = _matmul(mn, wxkv, "nn", BF16, "mm_xkv")
    o = _xattn_fwd(q, kv, "xattn_fwd")
    h3 = _matmul(o, wxo, "nn", F32, "mm_xo", res=h2, tn=D)
    (h4, n4, a2, b2), _ = _ffn_fwd(h3, g_ffn2, wg2, wu2, wd2, "ffn2_fwd")
    dh4, dg_final, loss_part = _final_loss(h4, gf, tgt, "final_loss")

    grad_big, grad_small, recv_big = {}, {"g_final": dg_final}, {}
    order = _dw_block_order()
    (dn4, df2, da2, db2, s2), _ = _ffn_bwd_dx(dh4, a2, b2, wg2, wu2, wd2, "ffn2_bwd_dx")
    dh3, grad_small["g_ffn2"] = _rms_bwd(h3, g_ffn2, dn4, dh4, "ffn2_norm_bwd")
    (recv_big["w2_gate"], recv_big["w2_up"]), _ = _ffn_bwd_dw([(da2, n4), (db2, n4)], order, "ffn2_bwd_dw_gu")
    (recv_big["w2_down"],), _ = _ffn_bwd_dw([(s2, df2)], order, "ffn2_bwd_dw_d", tk=1024)

    do = _matmul(dh3, wxo, "nt", BF16, "mm_xo_dx", tn=D)
    grad_big["w_xo"] = _matmul(o, dh3, "tn", BF16, "mm_xo_dw", tm=D, tn=1024).reshape(N_DEV, D // N_DEV, D)
    dq, dkv = _xattn_bwd(q, kv, do, "xattn_bwd")
    dhx = _matmul(dq, wxq, "nt", BF16, "mm_xq_dx", tn=D)
    grad_big["w_xq"] = _matmul(hx, dq, "tn", BF16, "mm_xq_dw", tm=D, tn=1024).reshape(N_DEV, D // N_DEV, D)
    dmn = _matmul(dkv, wxkv, "nt", F32, "mm_xkv_dx")
    grad_big["w_xkv"] = _cols_to_blocks(_matmul(mn, dkv, "tn", BF16, "mm_xkv_dw", tm=1024, tn=1024))
    dh2, grad_small["g_x"] = _rms_bwd(h2, g_x, dhx, dh3, "x_norm_bwd")
    _, grad_small["g_mem"] = _rms_bwd(ms, g_mem, dmn, None, "mem_norm_bwd")

    dy = _matmul(dh2, wout, "nt", F32, "mm_out_dx", tn=D)
    grad_big["w_out"] = _matmul(y, dh2, "tn", BF16, "mm_out_dw", tm=D, tn=1024).reshape(N_DEV, D // N_DEV, D)
    dz_a, grad_small["w_s"], dbs_t, grad_small["g_v"], grad_small["g_a_out"] = _sgu_bwd(
        z, dy, g_v, ws, ws_t, bs_t, g_a_out, z.shape[1], "sgu_bwd")
    grad_small["b_s"] = dbs_t.T
    dz_q, dkv_b, grad_small["sinks"], grad_small["g_b_out"] = _swa_bwd(
        z, yb, dy, sink_rows, g_b_out, dz_a, "swa_bwd")
    dz = _paste_columns(dz_q, dkv_b, "dz_kv")
    dn2 = _matmul(dz, win, "nt", BF16, "mm_in_dx", tn=1024)
    grad_big["w_in"] = _matmul(dz, n2, "tn", BF16, "mm_in_dw", tm=dz.shape[1] // 2, tn=D).reshape(
        N_DEV, dz.shape[1] // N_DEV, D)
    dh1, grad_small["g_mix"] = _rms_bwd(h1, g_mix, dn2, dh2, "mix_norm_bwd")

    (dn1, df1, da1, db1, s1), recv_mid = _ffn_bwd_dx(
        dh1, a1, b1, wg1, wu1, wd1, "ffn1_bwd_dx", exchange=_scatter_exchange([grad_big[k] for k in MID_W]))
    recv_big.update(zip(MID_W, recv_mid))
    dx, grad_small["g_ffn1"] = _rms_bwd(xs, g_ffn1, dn1, dh1, "ffn1_norm_bwd")
    (recv_big["w1_gate"], recv_big["w1_up"]), (recv_small,) = _ffn_bwd_dw(
        [(da1, n1), (db1, n1)], order, "ffn1_bwd_dw_gu",
        exchange=_gather_exchange([_pack([grad_small[k] for k in SMALL])], ALL_PEERS))
    (recv_big["w1_down"],), _ = _ffn_bwd_dw([(s1, df1)], order, "ffn1_bwd_dw_d", tk=1024)


    grads, deltas, new_m, new_v = {}, {}, {}, {}
    for k in BIG:
        shp = w[k].shape
        if k in TRANSPOSED_W:
            def view(t):
                return jnp.swapaxes(t, 1, 2)[0]
            outs = _adamw(view(w[k]), view(mom[k]), view(var[k]), recv_big[k], "adamw_" + k)
            grads[k], deltas[k], new_m[k], new_v[k] = [jnp.swapaxes(t[None], 1, 2) for t in outs]
        else:
            two_d = shp[1:]
            outs = _adamw(w[k].reshape(two_d), mom[k].reshape(two_d), var[k].reshape(two_d), recv_big[k],
                          "adamw_" + k)
            grads[k], deltas[k], new_m[k], new_v[k] = [t.reshape(shp) for t in outs]
    packed = _adamw(_pack([w[k] for k in SMALL]), _pack([mom[k] for k in SMALL]), _pack([var[k] for k in SMALL]),
                    recv_small, "adamw_small")
    off = 0
    for k in SMALL:
        shp = w[k].shape
        size = 1
        for s in shp:
            size *= s
        for dst, src in zip((grads, deltas, new_m, new_v), packed):
            dst[k] = src.reshape(-1)[off:off + size].reshape(shp)
        off += size

    loss = lax.psum(loss_part[0, 0], AXES)
    return (loss, dx[None], *[grads[k] for k in WEIGHTS], *[deltas[k] for k in WEIGHTS],
            *[new_m[k] for k in WEIGHTS], *[new_v[k] for k in WEIGHTS])
```

```python
import jax
import jax.numpy as jnp
from jax import lax
from jax.experimental import pallas as pl
from jax.experimental.pallas import tpu as pltpu

F32 = jnp.float32
BF16 = jnp.bfloat16
SDS = jax.ShapeDtypeStruct
BS = pl.BlockSpec

N_DEV = 8
AXES = ("x", "y", "c")
EPS = 1e-5
CHUNK = 128
HEAD_DIM = 64
KV_HEADS = 2
X_HEADS = 4
NEG = -1e30
ADAM_LR = 0.001
ADAM_B1 = 0.9
ADAM_B2 = 0.999
ADAM_EPS = 1e-08
ADAM_WD = 0.01
ADAM_STEP = 10
MIB = 1 << 20
FFN_SUB_ROWS = 256
WEIGHTS = ['g_ffn1', 'w1_gate', 'w1_up', 'w1_down', 'g_mix', 'w_in', 'g_v', 'w_s', 'b_s', 'sinks', 'g_a_out',
           'g_b_out', 'w_out', 'g_x', 'g_mem', 'w_xq', 'w_xkv', 'w_xo', 'g_ffn2', 'w2_gate', 'w2_up', 'w2_down',
           'g_final']
BIG = ['w1_gate', 'w1_up', 'w1_down', 'w_in', 'w_out', 'w_xq', 'w_xkv', 'w_xo', 'w2_gate', 'w2_up', 'w2_down']
SMALL = [w for w in WEIGHTS if w not in BIG]
FFN1_W = ['w1_gate', 'w1_up', 'w1_down']
FFN2_W = ['w2_gate', 'w2_up', 'w2_down']
MID_W = ['w_in', 'w_out', 'w_xq', 'w_xkv', 'w_xo']
REST_W = MID_W + FFN2_W
TRANSPOSED_W = ('w1_gate', 'w1_up', 'w2_gate', 'w2_up', 'w_in')


def _params(sem=None, vmem_mib=48):
    return pltpu.CompilerParams(dimension_semantics=sem, vmem_limit_bytes=vmem_mib * MIB)


def _dot(a, b):
    return jnp.dot(a, b, preferred_element_type=F32)


def _dot_nt(a, b):
    return lax.dot_general(a, b, (((1,), (1,)), ((), ())), preferred_element_type=F32)


def _dot_tn(a, b):
    return lax.dot_general(a, b, (((0,), (0,)), ((), ())), preferred_element_type=F32)


def _rstd(v):
    return lax.rsqrt(jnp.mean(v * v, axis=-1, keepdims=True) + EPS)


def _norm_bwd(xhat, r, g, d):
    t = d * g
    return r * (t - xhat * jnp.mean(t * xhat, axis=-1, keepdims=True))


def _gelu(v):
    return 0.5 * v * (1.0 + lax.erf(v * 0.7071067811865476))


def _gelu_grad(v):
    return 0.5 * (1.0 + lax.erf(v * 0.7071067811865476)) + v * jnp.exp(-0.5 * v * v) * 0.3989422804014327


def _mesh_pos():
    x, y, c = lax.axis_index("x"), lax.axis_index("y"), lax.axis_index("c")
    return x, y, c, 4 * x + 2 * y + c


def _peer(x, y, c, k):
    px = 1 - x if k & 4 else x
    py = 1 - y if k & 2 else y
    pc = 1 - c if k & 1 else c
    return (px, py, pc), 4 * px + 2 * py + pc


ANY = BS(memory_space=pl.ANY)
DMA_SEM = pltpu.SemaphoreType.DMA
ALL_PEERS = (1, 2, 3, 4, 5, 6, 7)
CHIP_PEERS = (2, 4, 6)
SIBLING = 1


def _remote(src, dst, send, recv, peer):
    return pltpu.make_async_remote_copy(src_ref=src, dst_ref=dst, send_sem=send, recv_sem=recv, device_id=peer,
                                        device_id_type=pl.DeviceIdType.MESH)


class _Exchange:
    def __init__(self, ins, out_shapes, sems, copies, aliases=None):
        self.ins, self.out_shapes, self.sems, self.copies, self.aliases = ins, out_shapes, sems, copies, aliases or {}


def _gather_exchange(shards, peers):
    n, m = len(shards), len(peers)

    def copies(ins, outs, sems):
        send, recv, lsem = sems
        x, y, c, me = _mesh_pos()
        cps = [pltpu.make_async_copy(ins[w], outs[w].at[me], lsem.at[w]) for w in range(n)]
        for w in range(n):
            for j, k in enumerate(peers):
                cps.append(_remote(ins[w], outs[w].at[me], send.at[w, j], recv.at[w, j], _peer(x, y, c, k)[0]))
        return cps

    return _Exchange(shards, [SDS((N_DEV,) + s.shape, s.dtype) for s in shards],
                     [DMA_SEM((n, m)), DMA_SEM((n, m)), DMA_SEM((n,))], copies)


def _sibling_exchange(gathered):
    n = len(gathered)

    def copies(ins, outs, sems):
        send, recv = sems
        x, y, c, me = _mesh_pos()
        slots = [me] + [_peer(x, y, c, k)[1] for k in CHIP_PEERS]
        sib = _peer(x, y, c, SIBLING)[0]
        return [_remote(outs[w].at[s], outs[w].at[s], send.at[w, j], recv.at[w, j], sib)
                for w in range(n) for j, s in enumerate(slots)]

    return _Exchange(gathered, [SDS(g.shape, g.dtype) for g in gathered], [DMA_SEM((n, 4)), DMA_SEM((n, 4))], copies,
                     aliases={w: w for w in range(n)})


def _scatter_exchange(fulls):
    n, m = len(fulls), len(ALL_PEERS)

    def copies(ins, outs, sems):
        send, recv, lsem = sems
        x, y, c, me = _mesh_pos()
        cps = [pltpu.make_async_copy(ins[w].at[me], outs[w].at[me], lsem.at[w]) for w in range(n)]
        for w in range(n):
            for j, k in enumerate(ALL_PEERS):
                peer, p = _peer(x, y, c, k)
                cps.append(_remote(ins[w].at[p], outs[w].at[me], send.at[w, j], recv.at[w, j], peer))
        return cps

    return _Exchange(fulls, [SDS(f.shape, f.dtype) for f in fulls],
                     [DMA_SEM((n, m)), DMA_SEM((n, m)), DMA_SEM((n,))], copies)


def _call(body, *, name, args, in_specs, out_shape, out_specs, grid, sem, vmem_mib, scratch=(), exchange=None,
          first=None, last=None, aliases=None):
    aliases = dict(aliases or {})
    if exchange is None:
        return pl.pallas_call(body, name=name, out_shape=out_shape, grid=grid, in_specs=in_specs,
                              out_specs=out_specs, scratch_shapes=list(scratch), input_output_aliases=aliases,
                              compiler_params=_params(sem, vmem_mib))(*args), []
    ni, no, ns = len(args), len(out_shape), len(scratch)
    xi, xo = len(exchange.ins), len(exchange.out_shapes)

    def hosted(*refs):
        own_in, refs = refs[:ni], refs[ni:]
        x_in, refs = refs[:xi], refs[xi:]
        own_out, refs = refs[:no], refs[no:]
        x_out, refs = refs[:xo], refs[xo:]
        own_scr, x_sem = refs[:ns], refs[ns:]

        @pl.when(first())
        def _():
            for cp in exchange.copies(x_in, x_out, x_sem):
                cp.start()

        body(*own_in, *own_out, *own_scr)

        @pl.when(last())
        def _():
            for cp in exchange.copies(x_in, x_out, x_sem):
                cp.wait()

    outs = pl.pallas_call(
        hosted, name=name, out_shape=list(out_shape) + list(exchange.out_shapes), grid=grid,
        in_specs=list(in_specs) + [ANY] * xi, out_specs=list(out_specs) + [ANY] * xo,
        scratch_shapes=list(scratch) + list(exchange.sems),
        input_output_aliases={**aliases, **{ni + a: no + b for a, b in exchange.aliases.items()}},
        compiler_params=pltpu.CompilerParams(dimension_semantics=tuple("arbitrary" for _ in grid),
                                             vmem_limit_bytes=vmem_mib * MIB, has_side_effects=True),
    )(*args, *exchange.ins)
    return outs[:no], outs[no:]


def _gather_two_level(shards, name):
    n = len(shards)
    OWN_SIB, OWN_NEAR, OWN_FAR, RELAY, HAND_NEAR, HAND_FAR, HAND_DIAG = range(7)

    def body(*refs):
        ins, outs = refs[:n], refs[n:2 * n]
        send, recv, lsem = refs[2 * n:]
        x, y, c, me = _mesh_pos()
        flip_y = c == 0
        sib = (x, y, 1 - c)
        near = (jnp.where(flip_y, x, 1 - x), jnp.where(flip_y, 1 - y, y), c)
        far = (jnp.where(flip_y, 1 - x, x), jnp.where(flip_y, y, 1 - y), c)

        def slot_of(dev):
            return 4 * dev[0] + 2 * dev[1] + dev[2]

        s_near, s_far, s_diag = slot_of(near), slot_of(far), slot_of((1 - x, 1 - y, c))

        def copy(w, kind, src, slot, to):
            return _remote(src, outs[w].at[slot], send.at[w, kind], recv.at[w, kind], to)

        def own(w, kind, to):
            return copy(w, kind, ins[w], me, to)

        def arrival(w, kind, slot):
            return copy(w, kind, outs[w].at[slot], slot, sib)

        def hand(w, kind, slot):
            return copy(w, kind, outs[w].at[slot], slot, sib)

        local = [pltpu.make_async_copy(ins[w], outs[w].at[me], lsem.at[w]) for w in range(n)]
        for w in range(n):
            local[w].start()
            own(w, OWN_SIB, sib).start()
            own(w, OWN_NEAR, near).start()
            own(w, OWN_FAR, far).start()
        for w in range(n):
            arrival(w, OWN_NEAR, s_near).wait_recv()
            copy(w, RELAY, outs[w].at[s_near], s_near, far).start()
            hand(w, HAND_NEAR, s_near).start()
        for w in range(n):
            arrival(w, OWN_FAR, s_far).wait_recv()
            hand(w, HAND_FAR, s_far).start()
            arrival(w, RELAY, s_diag).wait_recv()
            hand(w, HAND_DIAG, s_diag).start()
        for w in range(n):
            arrival(w, OWN_SIB, jnp.bitwise_xor(me, 1)).wait_recv()
            arrival(w, HAND_NEAR, jnp.bitwise_xor(s_far, 1)).wait_recv()
            arrival(w, HAND_FAR, jnp.bitwise_xor(s_near, 1)).wait_recv()
            arrival(w, HAND_DIAG, jnp.bitwise_xor(s_diag, 1)).wait_recv()
            own(w, OWN_SIB, sib).wait_send()
            own(w, OWN_NEAR, near).wait_send()
            own(w, OWN_FAR, far).wait_send()
            copy(w, RELAY, outs[w].at[s_near], s_near, far).wait_send()
            hand(w, HAND_NEAR, s_near).wait_send()
            hand(w, HAND_FAR, s_far).wait_send()
            hand(w, HAND_DIAG, s_diag).wait_send()
            local[w].wait()

    return pl.pallas_call(
        body, name=name, out_shape=[SDS((N_DEV,) + s.shape, s.dtype) for s in shards],
        in_specs=[ANY] * n, out_specs=[ANY] * n,
        scratch_shapes=[DMA_SEM((n, 7)), DMA_SEM((n, 7)), DMA_SEM((n,))],
        compiler_params=pltpu.CompilerParams(has_side_effects=True),
    )(*shards)


def _run_exchange(exchange, name):
    xi, xo = len(exchange.ins), len(exchange.out_shapes)

    def body(*refs):
        cps = exchange.copies(refs[:xi], refs[xi:xi + xo], refs[xi + xo:])
        for cp in cps:
            cp.start()
        for cp in cps:
            cp.wait()

    return pl.pallas_call(
        body, name=name, out_shape=list(exchange.out_shapes), in_specs=[ANY] * xi, out_specs=[ANY] * xo,
        scratch_shapes=list(exchange.sems), input_output_aliases=dict(exchange.aliases),
        compiler_params=pltpu.CompilerParams(has_side_effects=True),
    )(*exchange.ins)


def _rms_fwd(h, g, name, tm=512, exchange=None):
    S, D = h.shape
    tm = min(tm, S)
    assert S % tm == 0, (S, tm)
    ni = S // tm

    def body(h_ref, g_ref, o_ref):
        hv = h_ref[...]
        o_ref[...] = (hv * _rstd(hv) * g_ref[...]).astype(o_ref.dtype)

    (out,), extra = _call(
        body, name=name, args=(h, g), out_shape=[SDS((S, D), BF16)], grid=(ni,),
        in_specs=[BS((tm, D), lambda i: (i, 0)), BS((1, D), lambda i: (0, 0))],
        out_specs=[BS((tm, D), lambda i: (i, 0))], sem=("parallel",), vmem_mib=32, exchange=exchange,
        first=lambda: pl.program_id(0) == 0, last=lambda: pl.program_id(0) == ni - 1)
    return out, extra


def _rms_bwd(h, g, dn, dres, name, tm=256):
    S, D = h.shape
    tm = min(tm, S)
    assert S % tm == 0, (S, tm)
    has_res = dres is not None

    def body(*refs):
        if has_res:
            h_ref, g_ref, dn_ref, dres_ref, dh_ref, dg_ref = refs
        else:
            h_ref, g_ref, dn_ref, dh_ref, dg_ref = refs
        hv = h_ref[...]
        r = _rstd(hv)
        xh = hv * r
        d = dn_ref[...].astype(F32)

        @pl.when(pl.program_id(0) == 0)
        def _():
            dg_ref[...] = jnp.zeros_like(dg_ref)

        dg_ref[...] += jnp.sum(d * xh, axis=0, keepdims=True)
        dh = _norm_bwd(xh, r, g_ref[...], d)
        dh_ref[...] = dres_ref[...] + dh if has_res else dh

    row = BS((tm, D), lambda i: (i, 0))
    vec = BS((1, D), lambda i: (0, 0))
    args = (h, g, dn) + ((dres,) if has_res else ())
    return pl.pallas_call(
        body, name=name, out_shape=[SDS((S, D), F32), SDS((1, D), F32)], grid=(S // tm,),
        in_specs=[row, vec, row] + ([row] if has_res else []), out_specs=[row, vec],
        compiler_params=_params(("arbitrary",), 40),
    )(*args)


def _final_loss(h, g, target, name, tm=256):
    S, D = h.shape
    tm = min(tm, S)
    assert S % tm == 0, (S, tm)

    def body(h_ref, g_ref, t_ref, dh_ref, dg_ref, loss_ref):
        hv = h_ref[...]
        r = _rstd(hv)
        xh = hv * r
        gv = g_ref[...]
        e = xh * gv - t_ref[...]

        @pl.when(pl.program_id(0) == 0)
        def _():
            dg_ref[...] = jnp.zeros_like(dg_ref)
            loss_ref[...] = jnp.zeros_like(loss_ref)

        loss_ref[...] += 0.5 * jnp.sum(jnp.mean(e * e, axis=-1, keepdims=True), axis=0, keepdims=True)
        dy = e * (1.0 / D)
        dg_ref[...] += jnp.sum(dy * xh, axis=0, keepdims=True)
        dh_ref[...] = _norm_bwd(xh, r, gv, dy)

    row = BS((tm, D), lambda i: (i, 0))
    vec = BS((1, D), lambda i: (0, 0))
    return pl.pallas_call(
        body, name=name, out_shape=[SDS((S, D), F32), SDS((1, D), F32), SDS((1, 128), F32)], grid=(S // tm,),
        in_specs=[row, vec, row], out_specs=[row, vec, BS((1, 128), lambda i: (0, 0))],
        compiler_params=_params(("arbitrary",), 40),
    )(h, g, target)


def _matmul(a, b, mode, out_dtype, name, res=None, tm=512, tn=512, tk=512):
    if mode == "tn":
        K, M = a.shape
        N = b.shape[1]
        tm, tn, tk = min(tm, M), min(tn, N), min(tk, K)
        assert M % tm == 0 and N % tn == 0 and K % tk == 0, (a.shape, b.shape, tm, tn, tk)
        nk = K // tk

        def body(a_ref, b_ref, o_ref, acc_ref):
            k = pl.program_id(2)

            @pl.when(k == 0)
            def _():
                acc_ref[...] = jnp.zeros_like(acc_ref)

            acc_ref[...] += _dot_tn(a_ref[...].astype(BF16), b_ref[...].astype(BF16))

            @pl.when(k == nk - 1)
            def _():
                o_ref[...] = acc_ref[...].astype(o_ref.dtype)

        return pl.pallas_call(
            body, name=name, out_shape=SDS((M, N), out_dtype), grid=(M // tm, N // tn, nk),
            in_specs=[BS((tk, tm), lambda i, j, k: (k, i)), BS((tk, tn), lambda i, j, k: (k, j))],
            out_specs=BS((tm, tn), lambda i, j, k: (i, j)), scratch_shapes=[pltpu.VMEM((tm, tn), F32)],
            compiler_params=_params(("parallel", "parallel", "arbitrary"), 48),
        )(a, b)

    M, K = a.shape
    N = b.shape[1] if mode == "nn" else b.shape[0]
    tm, tn = min(tm, M), min(tn, N)
    assert M % tm == 0 and N % tn == 0, (a.shape, b.shape, tm, tn)
    has_res = res is not None

    def body(*refs):
        if has_res:
            a_ref, b_ref, r_ref, o_ref = refs
        else:
            a_ref, b_ref, o_ref = refs
        av, bv = a_ref[...].astype(BF16), b_ref[...].astype(BF16)
        acc = _dot(av, bv) if mode == "nn" else _dot_nt(av, bv)
        if has_res:
            acc = acc + r_ref[...]
        o_ref[...] = acc.astype(o_ref.dtype)

    b_spec = BS((K, tn), lambda i, j: (0, j)) if mode == "nn" else BS((tn, K), lambda i, j: (j, 0))
    o_spec = BS((tm, tn), lambda i, j: (i, j))
    return pl.pallas_call(
        body, name=name, out_shape=SDS((M, N), out_dtype), grid=(M // tm, N // tn),
        in_specs=[BS((tm, K), lambda i, j: (i, 0)), b_spec] + ([o_spec] if has_res else []), out_specs=o_spec,
        compiler_params=_params(("parallel", "parallel"), 48),
    )(*((a, b) + ((res,) if has_res else ())))


def _ffn_fwd(h, g, wg, wu, wd, name, tm=512, exchange=None):
    S, D = h.shape
    nb, _, Fs = wg.shape
    tm = min(tm, S)
    sub = min(FFN_SUB_ROWS, tm)
    assert S % tm == 0 and tm % sub == 0, (S, tm, sub)

    def body(h_ref, g_ref, wg_ref, wu_ref, wd_ref, o_ref, n_ref, a_ref, b_ref):
        j = pl.program_id(1)

        @pl.when(j == 0)
        def _():
            hv = h_ref[...]
            n_ref[...] = (hv * _rstd(hv) * g_ref[...]).astype(BF16)
            o_ref[...] = jnp.zeros_like(o_ref)

        for r in range(0, tm, sub):
            rows = slice(r, r + sub)
            n = n_ref[rows, :]
            a = _dot(n, wg_ref[...]).astype(BF16)
            b = _dot(n, wu_ref[...]).astype(BF16)
            a_ref[rows, :] = a
            b_ref[rows, :] = b
            a, b = a.astype(F32), b.astype(F32)
            s = (a * jax.nn.sigmoid(a) * b).astype(BF16)
            o_ref[rows, :] += _dot(s, wd_ref[...])

        @pl.when(j == nb - 1)
        def _():
            o_ref[...] = h_ref[...] + 0.5 * o_ref[...]

    row = BS((tm, D), lambda i, j: (i, 0))
    wcol = BS((None, D, Fs), lambda i, j: (j, 0, 0))
    act = BS((None, tm, Fs), lambda i, j: (j, i, 0))
    ni = S // tm
    return _call(
        body, name=name, args=(h, g, wg, wu, wd),
        out_shape=[SDS((S, D), F32), SDS((S, D), BF16), SDS((nb, S, Fs), BF16), SDS((nb, S, Fs), BF16)],
        grid=(ni, nb),
        in_specs=[row, BS((1, D), lambda i, j: (0, 0)), wcol, wcol, BS((None, Fs, D), lambda i, j: (j, 0, 0))],
        out_specs=[row, row, act, act], sem=("parallel", "arbitrary"), vmem_mib=56, exchange=exchange,
        first=lambda: (pl.program_id(0) == 0) & (pl.program_id(1) == 0),
        last=lambda: (pl.program_id(0) == ni - 1) & (pl.program_id(1) == nb - 1))


def _ffn_bwd_dx(dh, a, b, wg, wu, wd, name, tm=512, exchange=None):
    S, D = dh.shape
    nb, _, Fs = wg.shape
    tm = min(tm, S)
    assert S % tm == 0, (S, tm)

    def body(dh_ref, a_ref, b_ref, wg_ref, wu_ref, wd_ref, dn_ref, df_ref, da_ref, db_ref, s_ref, ds_ref):
        j = pl.program_id(1)

        def first_matmul():
            ds_ref[...] = _dot_nt(df_ref[...], wd_ref[...])

        def pre_activation_cotangents():
            ds = ds_ref[...]
            av, bv = a_ref[...].astype(F32), b_ref[...].astype(F32)
            sig = jax.nn.sigmoid(av)
            sl = av * sig
            da = (ds * bv * (sig * (1.0 + av * (1.0 - sig)))).astype(BF16)
            db = (ds * sl).astype(BF16)
            da_ref[...] = da
            db_ref[...] = db
            s_ref[...] = (sl * bv).astype(BF16)
            return da, db

        def last_matmuls(da, db):
            dn_ref[...] += _dot_nt(da, wg_ref[...]) + _dot_nt(db, wu_ref[...])

        @pl.when(j == 0)
        def _():
            df_ref[...] = (0.5 * dh_ref[...]).astype(BF16)
            dn_ref[...] = jnp.zeros_like(dn_ref)
            first_matmul()

        @pl.when((j > 0) & (j < nb))
        def _():
            da, db = pre_activation_cotangents()
            first_matmul()
            last_matmuls(da, db)

        @pl.when(j == nb)
        def _():
            last_matmuls(*pre_activation_cotangents())

    row = BS((tm, D), lambda i, j: (i, 0))
    prev = BS((None, D, Fs), lambda i, j: (jnp.maximum(j - 1, 0), 0, 0))
    act = BS((None, tm, Fs), lambda i, j: (jnp.maximum(j - 1, 0), i, 0))
    ni = S // tm
    return _call(
        body, name=name, args=(dh, a, b, wg, wu, wd),
        out_shape=[SDS((S, D), F32), SDS((S, D), BF16)] + [SDS((nb, S, Fs), BF16)] * 3,
        grid=(ni, nb + 1),
        in_specs=[row, act, act, prev, prev, BS((None, Fs, D), lambda i, j: (jnp.minimum(j, nb - 1), 0, 0))],
        out_specs=[row, row, act, act, act], scratch=[pltpu.VMEM((tm, Fs), F32)],
        sem=("parallel", "arbitrary"), vmem_mib=56, exchange=exchange,
        first=lambda: (pl.program_id(0) == 0) & (pl.program_id(1) == 0),
        last=lambda: (pl.program_id(0) == ni - 1) & (pl.program_id(1) == nb))


DW_FLIPS = {0: (6, 2, 4, 0), 1: (6, 4, 2, 0)}
N_CHIPS = N_DEV // 2


def _dw_block_order():
    x, y, c, me = _mesh_pos()
    steps = [jnp.array([v for mine, sib in zip(DW_FLIPS[core], DW_FLIPS[1 - core]) for v in (sib ^ 1, mine)], jnp.int32)
             for core in (0, 1)]
    return jnp.bitwise_xor(me.astype(jnp.int32), jnp.where(c == 0, steps[0], steps[1]))


def _ffn_bwd_dw(pairs, order, name, tk=512, exchange=None):
    npair = len(pairs)
    S = pairs[0][0].shape[-2]
    nb, half = N_DEV, N_DEV // 2
    tk = min(tk, S)
    assert S % tk == 0, (S, tk)
    nk = S // tk
    shapes = [(lhs.shape[-1], rhs.shape[-1]) for lhs, rhs in pairs]
    flat = []
    for a in (a for pair in pairs for a in pair):
        if not any(a is f for f in flat):
            flat.append(a)
    which = [tuple(next(i for i, f in enumerate(flat) if f is a) for a in pair) for pair in pairs]
    xi = len(exchange.ins) if exchange else 0
    xo = len(exchange.out_shapes) if exchange else 0

    def body(order_ref, *rest):
        tiles, rest = rest[:len(flat)], rest[len(flat):]
        x_in, rest = rest[:xi], rest[xi:]
        recv_bufs, rest = rest[:npair], rest[npair:]
        x_out, rest = rest[:xo], rest[xo:]
        accs, rest = rest[:npair], rest[npair:]
        out_t, rest = rest[:npair], rest[npair:]
        out_m, rest = rest[:npair], rest[npair:]
        land, rest = rest[:npair], rest[npair:]
        (send_t, recv_t, send_m, recv_m, lsem, credit), x_sem = rest[:6], rest[6:]
        t, k = pl.program_id(0), pl.program_id(1)
        x, y, c, me = _mesh_pos()
        sibling = (x, y, 1 - c)
        chip = 2 * x + y

        if exchange:
            @pl.when((t == 0) & (k == 0))
            def _():
                for cp in exchange.copies(x_in, x_out, x_sem):
                    cp.start()

        @pl.when(k == 0)
        def _():
            for acc in accs:
                acc[...] = jnp.zeros_like(acc)

        for w in range(npair):
            accs[w][...] += _dot_tn(tiles[which[w][0]][...], tiles[which[w][1]][...])

        def to_sibling(w, i):
            return _remote(out_t[w], land[w], send_t.at[w, i], recv_t.at[w, i], sibling)

        def to_owner(w, i):
            dst = recv_bufs[w].at[chip]
            if i == half - 1:
                return pltpu.make_async_copy(out_m[w], dst, lsem.at[w])
            p = jnp.bitwise_xor(me, jnp.where(c == 0, DW_FLIPS[0][i], DW_FLIPS[1][i]))
            return _remote(out_m[w], dst, send_m.at[w, i], recv_m.at[w, i], (p >> 2, (p >> 1) & 1, p & 1))

        for i in range(half):
            @pl.when((t == 2 * i) & (k == nk - 1))
            def _(i=i):
                if i >= 1:
                    pl.semaphore_wait(credit, 1)
                for w in range(npair):
                    if i >= 1:
                        to_sibling(w, i - 1).wait_send()
                    out_t[w][...] = accs[w][...].astype(BF16)
                    to_sibling(w, i).start()

            @pl.when((t == 2 * i + 1) & (k == nk - 1))
            def _(i=i):
                for w in range(npair):
                    to_sibling(w, i).wait_recv()
                    if i >= 1:
                        to_owner(w, i - 1).wait_send()
                    out_m[w][...] = (accs[w][...] + land[w][...].astype(F32)).astype(BF16)
                if i < half - 1:
                    pl.semaphore_signal(credit, inc=1, device_id=sibling, device_id_type=pl.DeviceIdType.MESH)
                for w in range(npair):
                    to_owner(w, i).start()

        @pl.when((t == nb - 1) & (k == nk - 1))
        def _():
            for w in range(npair):
                to_sibling(w, half - 1).wait_send()
                to_owner(w, half - 1).wait()
                for i in range(half - 1):
                    to_owner(w, i).wait_recv()
            if exchange:
                for cp in exchange.copies(x_in, x_out, x_sem):
                    cp.wait()

    def tile_spec(arr):
        if arr.ndim == 3:
            return BS((None, tk, arr.shape[-1]), lambda t, k, o: (o[t], k, 0))
        return BS((tk, arr.shape[-1]), lambda t, k, o: (k, 0))

    grid_spec = pltpu.PrefetchScalarGridSpec(
        num_scalar_prefetch=1, grid=(nb, nk),
        in_specs=[tile_spec(a) for a in flat] + [ANY] * xi, out_specs=[ANY] * (npair + xo),
        scratch_shapes=[pltpu.VMEM(s, F32) for s in shapes] + [pltpu.VMEM(s, BF16) for s in shapes] * 3
        + [DMA_SEM((npair, half)), DMA_SEM((npair, half)), DMA_SEM((npair, half - 1)), DMA_SEM((npair, half - 1)),
           DMA_SEM((npair,)), pltpu.SemaphoreType.REGULAR]
        + (list(exchange.sems) if exchange else []))
    outs = pl.pallas_call(
        body, name=name, grid_spec=grid_spec,
        out_shape=[SDS((N_CHIPS,) + s, BF16) for s in shapes] + (list(exchange.out_shapes) if exchange else []),
        compiler_params=pltpu.CompilerParams(dimension_semantics=("arbitrary", "arbitrary"),
                                             vmem_limit_bytes=54 * MIB, has_side_effects=True),
    )(order, *flat, *(exchange.ins if exchange else ()))
    return outs[:npair], outs[npair:]


def _sgu_parts(z, gv, ws_ref, bst_ref, groups):
    da = z.shape[1] // 2
    zu, zv = z[:, :da], z[:, da:]
    u, v = _gelu(zu), _gelu(zv)
    rv = _rstd(v)
    vhat = v * rv
    vn = (vhat * gv).astype(BF16)
    tri = lax.broadcasted_iota(jnp.int32, (CHUNK, CHUNK), 0) >= lax.broadcasted_iota(jnp.int32, (CHUNK, CHUNK), 1)
    pieces = []
    for g in range(groups):
        w = jnp.where(tri, ws_ref[g], 0.0).astype(BF16)
        pieces.append(_dot(w, vn[:, g * CHUNK:(g + 1) * CHUNK]) + bst_ref[:, g:g + 1])
    sv = jnp.concatenate(pieces, axis=1)
    return dict(zu=zu, zv=zv, u=u, rv=rv, vhat=vhat, vn=vn, sv=sv, tri=tri)


SGU_CHUNKS = 4


def _sgu_fwd(z, g_v, w_s, b_st, g_a, width, name):
    S = z.shape[0]
    groups = w_s.shape[0]
    da = groups * CHUNK
    rows = SGU_CHUNKS * CHUNK
    assert S % rows == 0, (S, rows)

    def body(z_ref, gv_ref, ws_ref, bst_ref, ga_ref, o_ref):
        for ch in range(SGU_CHUNKS):
            blk = slice(ch * CHUNK, (ch + 1) * CHUNK)
            p = _sgu_parts(z_ref[blk, :], gv_ref[...], ws_ref, bst_ref, groups)
            ya = p["u"] * p["sv"]
            o_ref[blk, :] = (ya * _rstd(ya) * ga_ref[...]).astype(BF16)

    vec = BS((1, da), lambda i: (0, 0))
    return pl.pallas_call(
        body, name=name, out_shape=SDS((S, width), BF16), grid=(S // rows,),
        in_specs=[BS((rows, 2 * da), lambda i: (i, 0)), vec, BS((groups, CHUNK, CHUNK), lambda i: (0, 0, 0)),
                  BS((CHUNK, groups), lambda i: (0, 0)), vec],
        out_specs=BS((rows, da), lambda i: (i, 0)), compiler_params=_params(("parallel",), 32),
    )(z, g_v, w_s, b_st, g_a)


def _sgu_bwd(z, dy, g_v, w_s, w_st, b_st, g_a, width, name):
    S = z.shape[0]
    groups = w_s.shape[0]
    da = groups * CHUNK
    rows = SGU_CHUNKS * CHUNK
    assert S % rows == 0, (S, rows)

    def body(z_ref, dy_ref, gv_ref, ws_ref, wst_ref, bst_ref, ga_ref, dz_ref, dws_ref, dbst_ref, dgv_ref, dga_ref):
        @pl.when(pl.program_id(0) == 0)
        def _():
            dws_ref[...] = jnp.zeros_like(dws_ref)
            dbst_ref[...] = jnp.zeros_like(dbst_ref)
            dgv_ref[...] = jnp.zeros_like(dgv_ref)
            dga_ref[...] = jnp.zeros_like(dga_ref)

        gv = gv_ref[...]
        tri_t = (lax.broadcasted_iota(jnp.int32, (CHUNK, CHUNK), 0)
                 <= lax.broadcasted_iota(jnp.int32, (CHUNK, CHUNK), 1))
        lane = lax.broadcasted_iota(jnp.int32, (CHUNK, groups), 1)
        for ch in range(SGU_CHUNKS):
            blk = slice(ch * CHUNK, (ch + 1) * CHUNK)
            p = _sgu_parts(z_ref[blk, :], gv, ws_ref, bst_ref, groups)
            u, sv, tri = p["u"], p["sv"], p["tri"]
            ya = u * sv
            ra = _rstd(ya)
            yhat = ya * ra
            d = dy_ref[blk, :]
            dga_ref[...] += jnp.sum(d * yhat, axis=0, keepdims=True)
            dya = _norm_bwd(yhat, ra, ga_ref[...], d)
            du = dya * sv
            dsv = dya * u
            dsv_b = dsv.astype(BF16)
            dvn = []
            dbs = jnp.zeros((CHUNK, groups), F32)
            for g in range(groups):
                cols = slice(g * CHUNK, (g + 1) * CHUNK)
                dbs = dbs + jnp.where(lane == g, jnp.sum(dsv[:, cols], axis=1, keepdims=True), 0.0)
                dws_ref[g] += jnp.where(tri, _dot_nt(dsv_b[:, cols], p["vn"][:, cols]), 0.0)
                wt = jnp.where(tri_t, wst_ref[g], 0.0).astype(BF16)
                dvn.append(_dot(wt, dsv_b[:, cols]))
            dbst_ref[...] += dbs
            dvn = jnp.concatenate(dvn, axis=1)
            dgv_ref[...] += jnp.sum(dvn * p["vhat"], axis=0, keepdims=True)
            dv = _norm_bwd(p["vhat"], p["rv"], gv, dvn)
            dz_ref[blk, :] = jnp.concatenate([du * _gelu_grad(p["zu"]), dv * _gelu_grad(p["zv"])], axis=1)

    vec = BS((1, da), lambda i: (0, 0))
    wsq = BS((groups, CHUNK, CHUNK), lambda i: (0, 0, 0))
    bsq = BS((CHUNK, groups), lambda i: (0, 0))
    return pl.pallas_call(
        body, name=name,
        out_shape=[SDS((S, width), F32), SDS((groups, CHUNK, CHUNK), F32), SDS((CHUNK, groups), F32),
                   SDS((1, da), F32), SDS((1, da), F32)],
        grid=(S // rows,),
        in_specs=[BS((rows, 2 * da), lambda i: (i, 0)), BS((rows, da), lambda i: (i, 0)), vec, wsq, wsq, bsq, vec],
        out_specs=[BS((rows, 2 * da), lambda i: (i, 0)), wsq, bsq, vec, vec],
        compiler_params=_params(("arbitrary",), 32),
    )(z, dy, g_v, w_s, w_st, b_st, g_a)


def _swa_mask(i, group):
    row = lax.broadcasted_iota(jnp.int32, (group * CHUNK, 2 * CHUNK), 0) & (CHUNK - 1)
    col = lax.broadcasted_iota(jnp.int32, (group * CHUNK, 2 * CHUNK), 1)
    d = row + CHUNK - col
    return (d >= 0) & (d < CHUNK) & jnp.logical_or(i > 0, col >= CHUNK)


def _stack_heads(t, g, group):
    return jnp.concatenate([t[:, h * HEAD_DIM:(h + 1) * HEAD_DIM] for h in range(g * group, (g + 1) * group)], axis=0)


def _unstack_heads(stacked, group):
    return [stacked[h * CHUNK:(h + 1) * CHUNK] for h in range(group)]


def _swa_probs(qh, kh, sink, mask):
    s = jnp.where(mask, _dot_nt(qh, kh) * (HEAD_DIM ** -0.5), NEG)
    m = jnp.maximum(jnp.max(s, axis=-1, keepdims=True), sink)
    e = jnp.exp(s - m)
    es = jnp.exp(sink - m)
    inv = 1.0 / (jnp.sum(e, axis=-1, keepdims=True) + es)
    return e * inv, es * inv


SWA_QBLOCKS = 2


def _swa_specs(db, npair, clamp):
    kvw = 2 * KV_HEADS * HEAD_DIM
    cur = (lambda p: jnp.minimum(p, npair - 1)) if clamp else (lambda p: p)
    rows = SWA_QBLOCKS * CHUNK
    q_spec = BS((rows, db), lambda p: (cur(p), 2))
    kc_spec = BS((rows, kvw), lambda p: (cur(p), 3 * db // kvw))
    kp_spec = BS((CHUNK, kvw), lambda p: (jnp.maximum(SWA_QBLOCKS * cur(p) - 1, 0), 3 * db // kvw))
    return q_spec, kc_spec, kp_spec


def _swa_keys(kp_ref, kc_ref, qb):
    if qb == 0:
        return jnp.concatenate([kp_ref[...], kc_ref[:CHUNK, :]], axis=0).astype(BF16)
    return kc_ref[(qb - 1) * CHUNK:(qb + 1) * CHUNK, :].astype(BF16)


def _swa_fwd(z, sink_rows, g_b, y_buf, name, exchange=None):
    S = z.shape[0]
    db = g_b.shape[1]
    heads = db // HEAD_DIM
    group = heads // KV_HEADS
    rows = SWA_QBLOCKS * CHUNK
    assert S % rows == 0, (S, rows)
    npair = S // rows

    def body(q_ref, kc_ref, kp_ref, sk_ref, gb_ref, _, yb_ref, ybn_ref):
        for qb in range(SWA_QBLOCKS):
            blk = slice(qb * CHUNK, (qb + 1) * CHUNK)
            mask = _swa_mask(SWA_QBLOCKS * pl.program_id(0) + qb, group)
            q = q_ref[blk, :].astype(BF16)
            kv = _swa_keys(kp_ref, kc_ref, qb)
            outs = []
            for g in range(KV_HEADS):
                kg = kv[:, g * HEAD_DIM:(g + 1) * HEAD_DIM]
                vg = kv[:, (KV_HEADS + g) * HEAD_DIM:(KV_HEADS + g + 1) * HEAD_DIM]
                srows = slice(g * group * CHUNK, (g + 1) * group * CHUNK)
                p, _ = _swa_probs(_stack_heads(q, g, group), kg, sk_ref[srows, :], mask)
                outs += _unstack_heads(_dot(p.astype(BF16), vg), group)
            yb = jnp.concatenate(outs, axis=1)
            yb_ref[blk, :] = yb
            ybn_ref[blk, :] = (yb * _rstd(yb) * gb_ref[...]).astype(BF16)

    q_spec, kc_spec, kp_spec = _swa_specs(db, npair, False)
    assert y_buf.shape[1] == 2 * db, y_buf.shape
    return _call(
        body, name=name, args=(z, z, z, sink_rows, g_b, y_buf),
        out_shape=[SDS((S, db), F32), SDS(y_buf.shape, BF16)], grid=(npair,),
        in_specs=[q_spec, kc_spec, kp_spec, BS((heads * CHUNK, 1), lambda p: (0, 0)), BS((1, db), lambda p: (0, 0)),
                  ANY],
        out_specs=[BS((rows, db), lambda p: (p, 0)), BS((rows, db), lambda p: (p, 1))], aliases={5: 1},
        sem=("parallel",), vmem_mib=32, exchange=exchange,
        first=lambda: pl.program_id(0) == 0, last=lambda: pl.program_id(0) == npair - 1)


def _swa_bwd(z, yb, dy, sink_rows, g_b, dz_buf, name):
    S = z.shape[0]
    db = g_b.shape[1]
    heads = db // HEAD_DIM
    group = heads // KV_HEADS
    rows = SWA_QBLOCKS * CHUNK
    assert SWA_QBLOCKS == 2 and S % rows == 0, (S, rows)
    npair = S // rows
    kvw = 2 * KV_HEADS * HEAD_DIM

    def body(q_ref, kc_ref, kp_ref, yb_ref, dy_ref, sk_ref, gb_ref, _, dq_ref, dkv_ref, dsk_ref, dgb_ref,
             done_ref, part_ref):
        p = pl.program_id(0)

        @pl.when(p == 0)
        def _():
            done_ref[...] = jnp.zeros_like(done_ref)
            part_ref[...] = jnp.zeros_like(part_ref)
            dsk_ref[...] = jnp.zeros_like(dsk_ref)
            dgb_ref[...] = jnp.zeros_like(dgb_ref)

        @pl.when(p < npair)
        def _():
            lane = lax.broadcasted_iota(jnp.int32, (1, heads), 1)
            dsinks = jnp.zeros((1, heads), F32)
            dgb = jnp.zeros((1, db), F32)
            contribs = []
            for qb in range(SWA_QBLOCKS):
                blk = slice(qb * CHUNK, (qb + 1) * CHUNK)
                mask = _swa_mask(SWA_QBLOCKS * p + qb, group)
                ybv = yb_ref[blk, :]
                rb = _rstd(ybv)
                yhat = ybv * rb
                d = dy_ref[blk, :]
                dgb = dgb + jnp.sum(d * yhat, axis=0, keepdims=True)
                do = _norm_bwd(yhat, rb, gb_ref[...], d).astype(BF16)
                q = q_ref[blk, :].astype(BF16)
                kv = _swa_keys(kp_ref, kc_ref, qb)
                dqs, dk, dv = [], [], []
                for g in range(KV_HEADS):
                    kg = kv[:, g * HEAD_DIM:(g + 1) * HEAD_DIM]
                    vg = kv[:, (KV_HEADS + g) * HEAD_DIM:(KV_HEADS + g + 1) * HEAD_DIM]
                    srows = slice(g * group * CHUNK, (g + 1) * group * CHUNK)
                    qg, dog = _stack_heads(q, g, group), _stack_heads(do, g, group)
                    pr, ps = _swa_probs(qg, kg, sk_ref[srows, :], mask)
                    dp = _dot_nt(dog, vg)
                    dr = jnp.sum(pr * dp, axis=-1, keepdims=True)
                    ds = (pr * (dp - dr) * (HEAD_DIM ** -0.5)).astype(BF16)
                    for h, t in enumerate(_unstack_heads(ps * dr, group)):
                        dsinks = dsinks - jnp.where(lane == g * group + h, jnp.sum(t, axis=0, keepdims=True), 0.0)
                    dqs += _unstack_heads(_dot(ds, kg), group)
                    dk.append(_dot_tn(ds, qg))
                    dv.append(_dot_tn(pr.astype(BF16), dog))
                dq_ref[blk, :] = jnp.concatenate(dqs, axis=1)
                contribs.append(jnp.concatenate(dk + dv, axis=1))
            dsk_ref[...] += dsinks
            dgb_ref[...] += dgb
            first, second = contribs
            dkv_ref[:CHUNK, :] = done_ref[...]
            dkv_ref[CHUNK:, :] = part_ref[...] + first[:CHUNK]
            done_ref[...] = first[CHUNK:] + second[:CHUNK]
            part_ref[...] = second[CHUNK:]

        @pl.when(p == npair)
        def _():
            dkv_ref[:CHUNK, :] = done_ref[...]
            dkv_ref[CHUNK:, :] = part_ref[...]

    q_spec, kc_spec, kp_spec = _swa_specs(db, npair, True)
    cur = BS((rows, db), lambda p: (jnp.minimum(p, npair - 1), 0))
    return pl.pallas_call(
        body, name=name,
        out_shape=[SDS(dz_buf.shape, F32), SDS((S, kvw), F32), SDS((1, heads), F32), SDS((1, db), F32)],
        grid=(npair + 1,),
        in_specs=[q_spec, kc_spec, kp_spec, cur, BS((rows, db), lambda p: (jnp.minimum(p, npair - 1), 1)),
                  BS((heads * CHUNK, 1), lambda p: (0, 0)), BS((1, db), lambda p: (0, 0)), ANY],
        out_specs=[BS((rows, db), lambda p: (jnp.minimum(p, npair - 1), 2)),
                   BS((rows, kvw), lambda p: (jnp.maximum(p - 1, 0), 0)), BS((1, heads), lambda p: (0, 0)),
                   BS((1, db), lambda p: (0, 0))],
        scratch_shapes=[pltpu.VMEM((CHUNK, kvw), F32), pltpu.VMEM((CHUNK, kvw), F32)],
        input_output_aliases={7: 0}, compiler_params=_params(("arbitrary",), 32),
    )(z, z, z, yb, dy, sink_rows, g_b, dz_buf)


def _paste_columns(dst, src, name, tm=512):
    S, w = src.shape
    tm = min(tm, S)
    assert S % tm == 0 and dst.shape[1] % w == 0, (dst.shape, src.shape, tm)
    last = dst.shape[1] // w - 1

    def body(src_ref, _, out_ref):
        out_ref[...] = src_ref[...].astype(out_ref.dtype)

    return pl.pallas_call(
        body, name=name, out_shape=SDS(dst.shape, dst.dtype), grid=(S // tm,),
        in_specs=[BS((tm, w), lambda i: (i, 0)), ANY], out_specs=BS((tm, w), lambda i: (i, last)),
        input_output_aliases={1: 0}, compiler_params=_params(("parallel",), 32),
    )(src, dst)


def _xattn_probs(qh, kh, hd):
    s = _dot_nt(qh, kh) * (hd ** -0.5)
    e = jnp.exp(s - jnp.max(s, axis=-1, keepdims=True))
    return e / jnp.sum(e, axis=-1, keepdims=True)


def _xattn_fwd(q, kv, name, tq=512):
    S, D = q.shape
    M = kv.shape[0]
    hd = D // X_HEADS
    tq = min(tq, S)
    assert S % tq == 0, (S, tq)

    def body(q_ref, kv_ref, o_ref):
        for h in range(X_HEADS):
            cols = slice(h * hd, (h + 1) * hd)
            p = _xattn_probs(q_ref[:, cols], kv_ref[:, cols], hd)
            o_ref[:, cols] = _dot(p.astype(BF16), kv_ref[:, D + h * hd:D + (h + 1) * hd]).astype(BF16)

    row = BS((tq, D), lambda i: (i, 0))
    return pl.pallas_call(
        body, name=name, out_shape=SDS((S, D), BF16), grid=(S // tq,),
        in_specs=[row, BS((M, 2 * D), lambda i: (0, 0))], out_specs=row, compiler_params=_params(("parallel",), 40),
    )(q, kv)


def _xattn_bwd(q, kv, do, name, tq=512):
    S, D = q.shape
    M = kv.shape[0]
    hd = D // X_HEADS
    tq = min(tq, S)
    assert S % tq == 0, (S, tq)

    def body(q_ref, kv_ref, do_ref, dq_ref, dkv_ref):
        @pl.when(pl.program_id(0) == 0)
        def _():
            dkv_ref[...] = jnp.zeros_like(dkv_ref)

        for h in range(X_HEADS):
            cols = slice(h * hd, (h + 1) * hd)
            vcols = slice(D + h * hd, D + (h + 1) * hd)
            qh, kh, vh, doh = q_ref[:, cols], kv_ref[:, cols], kv_ref[:, vcols], do_ref[:, cols]
            p = _xattn_probs(qh, kh, hd)
            dp = _dot_nt(doh, vh)
            ds = (p * (dp - jnp.sum(p * dp, axis=-1, keepdims=True)) * (hd ** -0.5)).astype(BF16)
            dq_ref[:, cols] = _dot(ds, kh).astype(BF16)
            dkv_ref[:, cols] += _dot_tn(ds, qh)
            dkv_ref[:, vcols] += _dot_tn(p.astype(BF16), doh)

    row = BS((tq, D), lambda i: (i, 0))
    full = BS((M, 2 * D), lambda i: (0, 0))
    return pl.pallas_call(
        body, name=name, out_shape=[SDS((S, D), BF16), SDS((M, 2 * D), F32)], grid=(S // tq,),
        in_specs=[row, full, row], out_specs=[row, full], compiler_params=_params(("arbitrary",), 40),
    )(q, kv, do)


def _row_tile(rows, cap):
    best = 8
    for t in range(8, min(rows, cap) + 1, 8):
        if rows % t == 0:
            best = t
    assert rows % best == 0, (rows, cap)
    return best


ADAM_TILE_ELEMS = 256 * 1024


def _adamw(w, m, v, recv, name):
    R, C = w.shape
    slots = recv.shape[0]
    tr = _row_tile(R, max(8, ADAM_TILE_ELEMS // C))

    def body(w_ref, m_ref, v_ref, r_ref, g_ref, d_ref, nm_ref, nv_ref):
        g = r_ref[0].astype(F32)
        for s in range(1, slots):
            g = g + r_ref[s].astype(F32)
        mn = ADAM_B1 * m_ref[...] + (1.0 - ADAM_B1) * g
        vn = ADAM_B2 * v_ref[...] + (1.0 - ADAM_B2) * jnp.square(g)
        m_hat = mn / (1.0 - ADAM_B1 ** ADAM_STEP)
        v_hat = vn / (1.0 - ADAM_B2 ** ADAM_STEP)
        g_ref[...] = g
        d_ref[...] = -ADAM_LR * (m_hat / (jnp.sqrt(v_hat) + ADAM_EPS) + ADAM_WD * w_ref[...])
        nm_ref[...] = mn
        nv_ref[...] = vn

    row = BS((tr, C), lambda i: (i, 0))
    return pl.pallas_call(
        body, name=name, out_shape=[SDS((R, C), F32)] * 4, grid=(R // tr,),
        in_specs=[row, row, row, BS((slots, tr, C), lambda i: (0, i, 0))], out_specs=[row] * 4,
        compiler_params=_params(("parallel",), 40),
    )(w, m, v, recv)


def _cols_to_blocks(full):
    r, c = full.shape
    return full.reshape(r, N_DEV, c // N_DEV).transpose(1, 0, 2)


def _blocks_to_cols(blocks):
    n, r, c = blocks.shape
    return blocks.transpose(1, 0, 2).reshape(r, n * c)


def _pack(parts):
    flat = jnp.concatenate([p.reshape(-1).astype(F32) for p in parts])
    pad = (-flat.shape[0]) % (128 * 128)
    return jnp.pad(flat, (0, pad)).reshape(-1, 128)


def kernel(x, mem, g_ffn1, w1_gate, w1_up, w1_down, g_mix, w_in, g_v, w_s, b_s, sinks, g_a_out, g_b_out, w_out, g_x, g_mem, w_xq, w_xkv, w_xo, g_ffn2, w2_gate, w2_up, w2_down, g_final, loss_target, m_g_ffn1, m_w1_gate, m_w1_up, m_w1_down, m_g_mix, m_w_in, m_g_v, m_w_s, m_b_s, m_sinks, m_g_a_out, m_g_b_out, m_w_out, m_g_x, m_g_mem, m_w_xq, m_w_xkv, m_w_xo, m_g_ffn2, m_w2_gate, m_w2_up, m_w2_down, m_g_final, v_g_ffn1, v_w1_gate, v_w1_up, v_w1_down, v_g_mix, v_w_in, v_g_v, v_w_s, v_b_s, v_sinks, v_g_a_out, v_g_b_out, v_w_out, v_g_x, v_g_mem, v_w_xq, v_w_xkv, v_w_xo, v_g_ffn2, v_w2_gate, v_w2_up, v_w2_down, v_g_final):
    w = dict(g_ffn1=g_ffn1, w1_gate=w1_gate, w1_up=w1_up, w1_down=w1_down, g_mix=g_mix, w_in=w_in, g_v=g_v, w_s=w_s,
             b_s=b_s, sinks=sinks, g_a_out=g_a_out, g_b_out=g_b_out, w_out=w_out, g_x=g_x, g_mem=g_mem, w_xq=w_xq,
             w_xkv=w_xkv, w_xo=w_xo, g_ffn2=g_ffn2, w2_gate=w2_gate, w2_up=w2_up, w2_down=w2_down, g_final=g_final)
    mom = dict(g_ffn1=m_g_ffn1, w1_gate=m_w1_gate, w1_up=m_w1_up, w1_down=m_w1_down, g_mix=m_g_mix, w_in=m_w_in,
               g_v=m_g_v, w_s=m_w_s, b_s=m_b_s, sinks=m_sinks, g_a_out=m_g_a_out, g_b_out=m_g_b_out, w_out=m_w_out,
               g_x=m_g_x, g_mem=m_g_mem, w_xq=m_w_xq, w_xkv=m_w_xkv, w_xo=m_w_xo, g_ffn2=m_g_ffn2,
               w2_gate=m_w2_gate, w2_up=m_w2_up, w2_down=m_w2_down, g_final=m_g_final)
    var = dict(g_ffn1=v_g_ffn1, w1_gate=v_w1_gate, w1_up=v_w1_up, w1_down=v_w1_down, g_mix=v_g_mix, w_in=v_w_in,
               g_v=v_g_v, w_s=v_w_s, b_s=v_b_s, sinks=v_sinks, g_a_out=v_g_a_out, g_b_out=v_g_b_out, w_out=v_w_out,
               g_x=v_g_x, g_mem=v_g_mem, w_xq=v_w_xq, w_xkv=v_w_xkv, w_xo=v_w_xo, g_ffn2=v_g_ffn2,
               w2_gate=v_w2_gate, w2_up=v_w2_up, w2_down=v_w2_down, g_final=v_g_final)

    xs, ms, tgt = x[0], mem[0], loss_target[0]
    D = xs.shape[1]
    d_a = w_s.shape[1] * CHUNK
    d_b = D - d_a
    kvw = 2 * KV_HEADS * HEAD_DIM

    shard = {k: w[k][0].astype(BF16) for k in BIG}
    wg1, wu1, wd1 = _gather_two_level([shard[k] for k in FFN1_W], "ag_ffn1")
    gf = g_final.reshape(1, D)
    ws, ws_t, bs_t = w_s[0], jnp.swapaxes(w_s[0], 1, 2), b_s[0].T
    sink_rows = jnp.repeat(sinks.reshape(-1), CHUNK).reshape(-1, 1)

    (h1, n1, a1, b1), part = _ffn_fwd(xs, g_ffn1, wg1, wu1, wd1, "ffn1_fwd",
                                      exchange=_gather_exchange([shard[k] for k in REST_W], CHIP_PEERS))
    part = dict(zip(REST_W, part))
    n2, (win_blocks,) = _rms_fwd(h1, g_mix, "mix_norm", exchange=_sibling_exchange([part["w_in"]]))
    win = _blocks_to_cols(win_blocks)
    z = _matmul(n2, win, "nn", F32, "mm_in", tn=win.shape[1] // 2)
    y_a = _sgu_fwd(z, g_v, ws, bs_t, g_a_out, D, "sgu_fwd")
    later = [k for k in REST_W if k != "w_in"]
    (yb, y), handed = _swa_fwd(z, sink_rows, g_b_out, y_a, "swa_fwd",
                                  exchange=_sibling_exchange([part[k] for k in later]))
    gathered = dict(zip(later, handed))
    wg2, wu2, wd2 = gathered["w2_gate"], gathered["w2_up"], gathered["w2_down"]
    wxkv = _blocks_to_cols(gathered["w_xkv"])
    wout = gathered["w_out"].reshape(D, D)
    wxq = gathered["w_xq"].reshape(D, D)
    wxo = gathered["w_xo"].reshape(D, D)
    h2 = _matmul(y, wout, "nn", F32, "mm_out", res=h1, tn=D)
    hx, _ = _rms_fwd(h2, g_x, "x_norm")
    mn, _ = _rms_fwd(ms, g_mem, "mem_norm")
    q = _matmul(hx, wxq, "nn", BF16, "mm_xq", tn=D)
    kv = _matmul(mn, wxkv, "nn", BF16, "mm_xkv")
    o = _xattn_fwd(q, kv, "xattn_fwd")
    h3 = _matmul(o, wxo, "nn", F32, "mm_xo", res=h2, tn=D)
    (h4, n4, a2, b2), _ = _ffn_fwd(h3, g_ffn2, wg2, wu2, wd2, "ffn2_fwd")
    dh4, dg_final, loss_part = _final_loss(h4, gf, tgt, "final_loss")

    grad_big, grad_small, recv_big = {}, {"g_final": dg_final}, {}
    order = _dw_block_order()
    (dn4, df2, da2, db2, s2), _ = _ffn_bwd_dx(dh4, a2, b2, wg2, wu2, wd2, "ffn2_bwd_dx")
    dh3, grad_small["g_ffn2"] = _rms_bwd(h3, g_ffn2, dn4, dh4, "ffn2_norm_bwd")
    (recv_big["w2_gate"], recv_big["w2_up"]), _ = _ffn_bwd_dw(
        [(da2, n4), (db2, n4)], order, "ffn2_bwd_dw_gu", tk=1024)
    (recv_big["w2_down"],), _ = _ffn_bwd_dw([(s2, df2)], order, "ffn2_bwd_dw_d", tk=1024)

    do = _matmul(dh3, wxo, "nt", BF16, "mm_xo_dx", tn=D)
    grad_big["w_xo"] = _matmul(o, dh3, "tn", BF16, "mm_xo_dw", tm=D, tn=1024).reshape(N_DEV, D // N_DEV, D)
    dq, dkv = _xattn_bwd(q, kv, do, "xattn_bwd")
    dhx = _matmul(dq, wxq, "nt", BF16, "mm_xq_dx", tn=D)
    grad_big["w_xq"] = _matmul(hx, dq, "tn", BF16, "mm_xq_dw", tm=D, tn=1024).reshape(N_DEV, D // N_DEV, D)
    dmn = _matmul(dkv, wxkv, "nt", F32, "mm_xkv_dx")
    grad_big["w_xkv"] = _cols_to_blocks(_matmul(mn, dkv, "tn", BF16, "mm_xkv_dw", tm=1024, tn=1024))
    dh2, grad_small["g_x"] = _rms_bwd(h2, g_x, dhx, dh3, "x_norm_bwd")
    _, grad_small["g_mem"] = _rms_bwd(ms, g_mem, dmn, None, "mem_norm_bwd")

    dy = _matmul(dh2, wout, "nt", F32, "mm_out_dx", tn=D)
    grad_big["w_out"] = _matmul(y, dh2, "tn", BF16, "mm_out_dw", tm=D, tn=1024).reshape(N_DEV, D // N_DEV, D)
    dz_a, grad_small["w_s"], dbs_t, grad_small["g_v"], grad_small["g_a_out"] = _sgu_bwd(
        z, dy, g_v, ws, ws_t, bs_t, g_a_out, z.shape[1], "sgu_bwd")
    grad_small["b_s"] = dbs_t.T
    dz_q, dkv_b, grad_small["sinks"], grad_small["g_b_out"] = _swa_bwd(
        z, yb, dy, sink_rows, g_b_out, dz_a, "swa_bwd")
    dz = _paste_columns(dz_q, dkv_b, "dz_kv")
    dn2 = _matmul(dz, win, "nt", BF16, "mm_in_dx", tn=1024)
    grad_big["w_in"] = _matmul(dz, n2, "tn", BF16, "mm_in_dw", tm=dz.shape[1] // 2, tn=D).reshape(
        N_DEV, dz.shape[1] // N_DEV, D)
    dh1, grad_small["g_mix"] = _rms_bwd(h1, g_mix, dn2, dh2, "mix_norm_bwd")

    (dn1, df1, da1, db1, s1), recv_mid = _ffn_bwd_dx(
        dh1, a1, b1, wg1, wu1, wd1, "ffn1_bwd_dx", exchange=_scatter_exchange([grad_big[k] for k in MID_W]))
    recv_big.update(zip(MID_W, recv_mid))
    dx, grad_small["g_ffn1"] = _rms_bwd(xs, g_ffn1, dn1, dh1, "ffn1_norm_bwd")
    (recv_big["w1_gate"], recv_big["w1_up"]), (recv_small,) = _ffn_bwd_dw(
        [(da1, n1), (db1, n1)], order, "ffn1_bwd_dw_gu", tk=1024,
        exchange=_gather_exchange([_pack([grad_small[k] for k in SMALL])], ALL_PEERS))
    (recv_big["w1_down"],), _ = _ffn_bwd_dw([(s1, df1)], order, "ffn1_bwd_dw_d", tk=1024)


    grads, deltas, new_m, new_v = {}, {}, {}, {}
    for k in BIG:
        shp = w[k].shape
        if k in TRANSPOSED_W:
            def view(t):
                return jnp.swapaxes(t, 1, 2)[0]
            outs = _adamw(view(w[k]), view(mom[k]), view(var[k]), recv_big[k], "adamw_" + k)
            grads[k], deltas[k], new_m[k], new_v[k] = [jnp.swapaxes(t[None], 1, 2) for t in outs]
        else:
            two_d = shp[1:]
            outs = _adamw(w[k].reshape(two_d), mom[k].reshape(two_d), var[k].reshape(two_d), recv_big[k],
                          "adamw_" + k)
            grads[k], deltas[k], new_m[k], new_v[k] = [t.reshape(shp) for t in outs]
    packed = _adamw(_pack([w[k] for k in SMALL]), _pack([mom[k] for k in SMALL]), _pack([var[k] for k in SMALL]),
                    recv_small, "adamw_small")
    off = 0
    for k in SMALL:
        shp = w[k].shape
        size = 1
        for s in shp:
            size *= s
        for dst, src in zip((grads, deltas, new_m, new_v), packed):
            dst[k] = src.reshape(-1)[off:off + size].reshape(shp)
        off += size

    loss = lax.psum(loss_part[0, 0], AXES)
    return (loss, dx[None], *[grads[k] for k in WEIGHTS], *[deltas[k] for k in WEIGHTS],
            *[new_m[k] for k in WEIGHTS], *[new_v[k] for k in WEIGHTS])
```

```python
import jax
import jax.numpy as jnp
from jax import lax
from jax.experimental import pallas as pl
from jax.experimental.pallas import tpu as pltpu

F32 = jnp.float32
BF16 = jnp.bfloat16
SDS = jax.ShapeDtypeStruct
BS = pl.BlockSpec

N_DEV = 8
AXES = ("x", "y", "c")
EPS = 1e-5
CHUNK = 128
HEAD_DIM = 64
KV_HEADS = 2
X_HEADS = 4
NEG = -1e30
ADAM_LR = 0.001
ADAM_B1 = 0.9
ADAM_B2 = 0.999
ADAM_EPS = 1e-08
ADAM_WD = 0.01
ADAM_STEP = 10
MIB = 1 << 20
FFN_SUB_ROWS = 256
WEIGHTS = ['g_ffn1', 'w1_gate', 'w1_up', 'w1_down', 'g_mix', 'w_in', 'g_v', 'w_s', 'b_s', 'sinks', 'g_a_out',
           'g_b_out', 'w_out', 'g_x', 'g_mem', 'w_xq', 'w_xkv', 'w_xo', 'g_ffn2', 'w2_gate', 'w2_up', 'w2_down',
           'g_final']
BIG = ['w1_gate', 'w1_up', 'w1_down', 'w_in', 'w_out', 'w_xq', 'w_xkv', 'w_xo', 'w2_gate', 'w2_up', 'w2_down']
SMALL = [w for w in WEIGHTS if w not in BIG]
FFN1_W = ['w1_gate', 'w1_up', 'w1_down']
FFN2_W = ['w2_gate', 'w2_up', 'w2_down']
MID_W = ['w_in', 'w_out', 'w_xq', 'w_xkv', 'w_xo']
REST_W = MID_W + FFN2_W
TRANSPOSED_W = ('w1_gate', 'w1_up', 'w2_gate', 'w2_up', 'w_in')


def _params(sem=None, vmem_mib=48):
    return pltpu.CompilerParams(dimension_semantics=sem, vmem_limit_bytes=vmem_mib * MIB)


def _dot(a, b):
    return jnp.dot(a, b, preferred_element_type=F32)


def _dot_nt(a, b):
    return lax.dot_general(a, b, (((1,), (1,)), ((), ())), preferred_element_type=F32)


def _dot_tn(a, b):
    return lax.dot_general(a, b, (((0,), (0,)), ((), ())), preferred_element_type=F32)


def _rstd(v):
    return lax.rsqrt(jnp.mean(v * v, axis=-1, keepdims=True) + EPS)


def _norm_bwd(xhat, r, g, d):
    t = d * g
    return r * (t - xhat * jnp.mean(t * xhat, axis=-1, keepdims=True))


def _gelu(v):
    return 0.5 * v * (1.0 + lax.erf(v * 0.7071067811865476))


def _gelu_grad(v):
    return 0.5 * (1.0 + lax.erf(v * 0.7071067811865476)) + v * jnp.exp(-0.5 * v * v) * 0.3989422804014327


def _mesh_pos():
    x, y, c = lax.axis_index("x"), lax.axis_index("y"), lax.axis_index("c")
    return x, y, c, 4 * x + 2 * y + c


def _peer(x, y, c, k):
    px = 1 - x if k & 4 else x
    py = 1 - y if k & 2 else y
    pc = 1 - c if k & 1 else c
    return (px, py, pc), 4 * px + 2 * py + pc


ANY = BS(memory_space=pl.ANY)
DMA_SEM = pltpu.SemaphoreType.DMA
ALL_PEERS = (1, 2, 3, 4, 5, 6, 7)
CHIP_PEERS = (2, 4, 6)
SIBLING = 1


def _remote(src, dst, send, recv, peer):
    return pltpu.make_async_remote_copy(src_ref=src, dst_ref=dst, send_sem=send, recv_sem=recv, device_id=peer,
                                        device_id_type=pl.DeviceIdType.MESH)


class _Exchange:
    def __init__(self, ins, out_shapes, sems, copies, aliases=None):
        self.ins, self.out_shapes, self.sems, self.copies, self.aliases = ins, out_shapes, sems, copies, aliases or {}


def _gather_exchange(shards, peers):
    n, m = len(shards), len(peers)

    def copies(ins, outs, sems):
        send, recv, lsem = sems
        x, y, c, me = _mesh_pos()
        cps = [pltpu.make_async_copy(ins[w], outs[w].at[me], lsem.at[w]) for w in range(n)]
        for w in range(n):
            for j, k in enumerate(peers):
                cps.append(_remote(ins[w], outs[w].at[me], send.at[w, j], recv.at[w, j], _peer(x, y, c, k)[0]))
        return cps

    return _Exchange(shards, [SDS((N_DEV,) + s.shape, s.dtype) for s in shards],
                     [DMA_SEM((n, m)), DMA_SEM((n, m)), DMA_SEM((n,))], copies)


def _sibling_exchange(gathered):
    n = len(gathered)

    def copies(ins, outs, sems):
        send, recv = sems
        x, y, c, me = _mesh_pos()
        slots = [me] + [_peer(x, y, c, k)[1] for k in CHIP_PEERS]
        sib = _peer(x, y, c, SIBLING)[0]
        return [_remote(outs[w].at[s], outs[w].at[s], send.at[w, j], recv.at[w, j], sib)
                for w in range(n) for j, s in enumerate(slots)]

    return _Exchange(gathered, [SDS(g.shape, g.dtype) for g in gathered], [DMA_SEM((n, 4)), DMA_SEM((n, 4))], copies,
                     aliases={w: w for w in range(n)})


def _scatter_exchange(fulls):
    n, m = len(fulls), len(ALL_PEERS)

    def copies(ins, outs, sems):
        send, recv, lsem = sems
        x, y, c, me = _mesh_pos()
        cps = [pltpu.make_async_copy(ins[w].at[me], outs[w].at[me], lsem.at[w]) for w in range(n)]
        for w in range(n):
            for j, k in enumerate(ALL_PEERS):
                peer, p = _peer(x, y, c, k)
                cps.append(_remote(ins[w].at[p], outs[w].at[me], send.at[w, j], recv.at[w, j], peer))
        return cps

    return _Exchange(fulls, [SDS(f.shape, f.dtype) for f in fulls],
                     [DMA_SEM((n, m)), DMA_SEM((n, m)), DMA_SEM((n,))], copies)


def _call(body, *, name, args, in_specs, out_shape, out_specs, grid, sem, vmem_mib, scratch=(), exchange=None,
          first=None, last=None, aliases=None):
    aliases = dict(aliases or {})
    if exchange is None:
        return pl.pallas_call(body, name=name, out_shape=out_shape, grid=grid, in_specs=in_specs,
                              out_specs=out_specs, scratch_shapes=list(scratch), input_output_aliases=aliases,
                              compiler_params=_params(sem, vmem_mib))(*args), []
    ni, no, ns = len(args), len(out_shape), len(scratch)
    xi, xo = len(exchange.ins), len(exchange.out_shapes)

    def hosted(*refs):
        own_in, refs = refs[:ni], refs[ni:]
        x_in, refs = refs[:xi], refs[xi:]
        own_out, refs = refs[:no], refs[no:]
        x_out, refs = refs[:xo], refs[xo:]
        own_scr, x_sem = refs[:ns], refs[ns:]

        @pl.when(first())
        def _():
            for cp in exchange.copies(x_in, x_out, x_sem):
                cp.start()

        body(*own_in, *own_out, *own_scr)

        @pl.when(last())
        def _():
            for cp in exchange.copies(x_in, x_out, x_sem):
                cp.wait()

    outs = pl.pallas_call(
        hosted, name=name, out_shape=list(out_shape) + list(exchange.out_shapes), grid=grid,
        in_specs=list(in_specs) + [ANY] * xi, out_specs=list(out_specs) + [ANY] * xo,
        scratch_shapes=list(scratch) + list(exchange.sems),
        input_output_aliases={**aliases, **{ni + a: no + b for a, b in exchange.aliases.items()}},
        compiler_params=pltpu.CompilerParams(dimension_semantics=tuple("arbitrary" for _ in grid),
                                             vmem_limit_bytes=vmem_mib * MIB, has_side_effects=True),
    )(*args, *exchange.ins)
    return outs[:no], outs[no:]


def _gather_two_level(shards, name):
    n = len(shards)
    OWN_SIB, OWN_NEAR, OWN_FAR, RELAY, HAND_NEAR, HAND_FAR, HAND_DIAG = range(7)

    def body(*refs):
        ins, outs = refs[:n], refs[n:2 * n]
        send, recv, lsem = refs[2 * n:]
        x, y, c, me = _mesh_pos()
        flip_y = c == 0
        sib = (x, y, 1 - c)
        near = (jnp.where(flip_y, x, 1 - x), jnp.where(flip_y, 1 - y, y), c)
        far = (jnp.where(flip_y, 1 - x, x), jnp.where(flip_y, y, 1 - y), c)

        def slot_of(dev):
            return 4 * dev[0] + 2 * dev[1] + dev[2]

        s_near, s_far, s_diag = slot_of(near), slot_of(far), slot_of((1 - x, 1 - y, c))

        def copy(w, kind, src, slot, to):
            return _remote(src, outs[w].at[slot], send.at[w, kind], recv.at[w, kind], to)

        def own(w, kind, to):
            return copy(w, kind, ins[w], me, to)

        def arrival(w, kind, slot):
            return copy(w, kind, outs[w].at[slot], slot, sib)

        def hand(w, kind, slot):
            return copy(w, kind, outs[w].at[slot], slot, sib)

        local = [pltpu.make_async_copy(ins[w], outs[w].at[me], lsem.at[w]) for w in range(n)]
        for w in range(n):
            local[w].start()
            own(w, OWN_SIB, sib).start()
            own(w, OWN_NEAR, near).start()
            own(w, OWN_FAR, far).start()
        for w in range(n):
            arrival(w, OWN_NEAR, s_near).wait_recv()
            copy(w, RELAY, outs[w].at[s_near], s_near, far).start()
            hand(w, HAND_NEAR, s_near).start()
        for w in range(n):
            arrival(w, OWN_FAR, s_far).wait_recv()
            hand(w, HAND_FAR, s_far).start()
            arrival(w, RELAY, s_diag).wait_recv()
            hand(w, HAND_DIAG, s_diag).start()
        for w in range(n):
            arrival(w, OWN_SIB, jnp.bitwise_xor(me, 1)).wait_recv()
            arrival(w, HAND_NEAR, jnp.bitwise_xor(s_far, 1)).wait_recv()
            arrival(w, HAND_FAR, jnp.bitwise_xor(s_near, 1)).wait_recv()
            arrival(w, HAND_DIAG, jnp.bitwise_xor(s_diag, 1)).wait_recv()
            own(w, OWN_SIB, sib).wait_send()
            own(w, OWN_NEAR, near).wait_send()
            own(w, OWN_FAR, far).wait_send()
            copy(w, RELAY, outs[w].at[s_near], s_near, far).wait_send()
            hand(w, HAND_NEAR, s_near).wait_send()
            hand(w, HAND_FAR, s_far).wait_send()
            hand(w, HAND_DIAG, s_diag).wait_send()
            local[w].wait()

    return pl.pallas_call(
        body, name=name, out_shape=[SDS((N_DEV,) + s.shape, s.dtype) for s in shards],
        in_specs=[ANY] * n, out_specs=[ANY] * n,
        scratch_shapes=[DMA_SEM((n, 7)), DMA_SEM((n, 7)), DMA_SEM((n,))],
        compiler_params=pltpu.CompilerParams(has_side_effects=True),
    )(*shards)


def _run_exchange(exchange, name):
    xi, xo = len(exchange.ins), len(exchange.out_shapes)

    def body(*refs):
        cps = exchange.copies(refs[:xi], refs[xi:xi + xo], refs[xi + xo:])
        for cp in cps:
            cp.start()
        for cp in cps:
            cp.wait()

    return pl.pallas_call(
        body, name=name, out_shape=list(exchange.out_shapes), in_specs=[ANY] * xi, out_specs=[ANY] * xo,
        scratch_shapes=list(exchange.sems), input_output_aliases=dict(exchange.aliases),
        compiler_params=pltpu.CompilerParams(has_side_effects=True),
    )(*exchange.ins)


def _rms_fwd(h, g, name, tm=512, exchange=None):
    S, D = h.shape
    tm = min(tm, S)
    assert S % tm == 0, (S, tm)
    ni = S // tm

    def body(h_ref, g_ref, o_ref):
        hv = h_ref[...]
        o_ref[...] = (hv * _rstd(hv) * g_ref[...]).astype(o_ref.dtype)

    (out,), extra = _call(
        body, name=name, args=(h, g), out_shape=[SDS((S, D), BF16)], grid=(ni,),
        in_specs=[BS((tm, D), lambda i: (i, 0)), BS((1, D), lambda i: (0, 0))],
        out_specs=[BS((tm, D), lambda i: (i, 0))], sem=("parallel",), vmem_mib=32, exchange=exchange,
        first=lambda: pl.program_id(0) == 0, last=lambda: pl.program_id(0) == ni - 1)
    return out, extra


def _rms_bwd(h, g, dn, dres, name, tm=256):
    S, D = h.shape
    tm = min(tm, S)
    assert S % tm == 0, (S, tm)
    has_res = dres is not None

    def body(*refs):
        if has_res:
            h_ref, g_ref, dn_ref, dres_ref, dh_ref, dg_ref = refs
        else:
            h_ref, g_ref, dn_ref, dh_ref, dg_ref = refs
        hv = h_ref[...]
        r = _rstd(hv)
        xh = hv * r
        d = dn_ref[...].astype(F32)

        @pl.when(pl.program_id(0) == 0)
        def _():
            dg_ref[...] = jnp.zeros_like(dg_ref)

        dg_ref[...] += jnp.sum(d * xh, axis=0, keepdims=True)
        dh = _norm_bwd(xh, r, g_ref[...], d)
        dh_ref[...] = dres_ref[...] + dh if has_res else dh

    row = BS((tm, D), lambda i: (i, 0))
    vec = BS((1, D), lambda i: (0, 0))
    args = (h, g, dn) + ((dres,) if has_res else ())
    return pl.pallas_call(
        body, name=name, out_shape=[SDS((S, D), F32), SDS((1, D), F32)], grid=(S // tm,),
        in_specs=[row, vec, row] + ([row] if has_res else []), out_specs=[row, vec],
        compiler_params=_params(("arbitrary",), 40),
    )(*args)


def _matmul_norm_bwd(a, b, h, g, dres, name, tm=512):
    M, K = a.shape
    D = b.shape[0]
    tm = min(tm, M)
    assert M % tm == 0 and h.shape == (M, D), (a.shape, b.shape, h.shape, tm)

    def body(a_ref, b_ref, h_ref, g_ref, r_ref, dh_ref, dg_ref):
        d = _dot_nt(a_ref[...].astype(BF16), b_ref[...].astype(BF16))
        hv = h_ref[...]
        r = _rstd(hv)
        xh = hv * r

        @pl.when(pl.program_id(0) == 0)
        def _():
            dg_ref[...] = jnp.zeros_like(dg_ref)

        dg_ref[...] += jnp.sum(d * xh, axis=0, keepdims=True)
        dh_ref[...] = r_ref[...] + _norm_bwd(xh, r, g_ref[...], d)

    row = BS((tm, D), lambda i: (i, 0))
    vec = BS((1, D), lambda i: (0, 0))
    return pl.pallas_call(
        body, name=name, out_shape=[SDS((M, D), F32), SDS((1, D), F32)], grid=(M // tm,),
        in_specs=[BS((tm, K), lambda i: (i, 0)), BS((D, K), lambda i: (0, 0)), row, vec, row], out_specs=[row, vec],
        compiler_params=_params(("arbitrary",), 52),
    )(a, b, h, g, dres)


def _final_loss(h, g, target, name, tm=256):
    S, D = h.shape
    tm = min(tm, S)
    assert S % tm == 0, (S, tm)

    def body(h_ref, g_ref, t_ref, dh_ref, dg_ref, loss_ref):
        hv = h_ref[...]
        r = _rstd(hv)
        xh = hv * r
        gv = g_ref[...]
        e = xh * gv - t_ref[...]

        @pl.when(pl.program_id(0) == 0)
        def _():
            dg_ref[...] = jnp.zeros_like(dg_ref)
            loss_ref[...] = jnp.zeros_like(loss_ref)

        loss_ref[...] += 0.5 * jnp.sum(jnp.mean(e * e, axis=-1, keepdims=True), axis=0, keepdims=True)
        dy = e * (1.0 / D)
        dg_ref[...] += jnp.sum(dy * xh, axis=0, keepdims=True)
        dh_ref[...] = _norm_bwd(xh, r, gv, dy)

    row = BS((tm, D), lambda i: (i, 0))
    vec = BS((1, D), lambda i: (0, 0))
    return pl.pallas_call(
        body, name=name, out_shape=[SDS((S, D), F32), SDS((1, D), F32), SDS((1, 128), F32)], grid=(S // tm,),
        in_specs=[row, vec, row], out_specs=[row, vec, BS((1, 128), lambda i: (0, 0))],
        compiler_params=_params(("arbitrary",), 40),
    )(h, g, target)


def _matmul(a, b, mode, out_dtype, name, res=None, tm=512, tn=512, tk=512):
    if mode == "tn":
        K, M = a.shape
        N = b.shape[1]
        tm, tn, tk = min(tm, M), min(tn, N), min(tk, K)
        assert M % tm == 0 and N % tn == 0 and K % tk == 0, (a.shape, b.shape, tm, tn, tk)
        nk = K // tk

        def body(a_ref, b_ref, o_ref, acc_ref):
            k = pl.program_id(2)

            @pl.when(k == 0)
            def _():
                acc_ref[...] = jnp.zeros_like(acc_ref)

            acc_ref[...] += _dot_tn(a_ref[...].astype(BF16), b_ref[...].astype(BF16))

            @pl.when(k == nk - 1)
            def _():
                o_ref[...] = acc_ref[...].astype(o_ref.dtype)

        return pl.pallas_call(
            body, name=name, out_shape=SDS((M, N), out_dtype), grid=(M // tm, N // tn, nk),
            in_specs=[BS((tk, tm), lambda i, j, k: (k, i)), BS((tk, tn), lambda i, j, k: (k, j))],
            out_specs=BS((tm, tn), lambda i, j, k: (i, j)), scratch_shapes=[pltpu.VMEM((tm, tn), F32)],
            compiler_params=_params(("parallel", "parallel", "arbitrary"), 48),
        )(a, b)

    M, K = a.shape
    N = b.shape[1] if mode == "nn" else b.shape[0]
    tm, tn = min(tm, M), min(tn, N)
    assert M % tm == 0 and N % tn == 0, (a.shape, b.shape, tm, tn)
    has_res = res is not None

    def body(*refs):
        if has_res:
            a_ref, b_ref, r_ref, o_ref = refs
        else:
            a_ref, b_ref, o_ref = refs
        av, bv = a_ref[...].astype(BF16), b_ref[...].astype(BF16)
        acc = _dot(av, bv) if mode == "nn" else _dot_nt(av, bv)
        if has_res:
            acc = acc + r_ref[...]
        o_ref[...] = acc.astype(o_ref.dtype)

    b_spec = BS((K, tn), lambda i, j: (0, j)) if mode == "nn" else BS((tn, K), lambda i, j: (j, 0))
    o_spec = BS((tm, tn), lambda i, j: (i, j))
    return pl.pallas_call(
        body, name=name, out_shape=SDS((M, N), out_dtype), grid=(M // tm, N // tn),
        in_specs=[BS((tm, K), lambda i, j: (i, 0)), b_spec] + ([o_spec] if has_res else []), out_specs=o_spec,
        compiler_params=_params(("parallel", "parallel"), 48),
    )(*((a, b) + ((res,) if has_res else ())))


def _ffn_fwd(h, g, wg, wu, wd, name, tm=512, exchange=None):
    S, D = h.shape
    nb, _, Fs = wg.shape
    tm = min(tm, S)
    sub = min(FFN_SUB_ROWS, tm)
    assert S % tm == 0 and tm % sub == 0, (S, tm, sub)

    def body(h_ref, g_ref, wg_ref, wu_ref, wd_ref, o_ref, n_ref, a_ref, b_ref):
        j = pl.program_id(1)

        @pl.when(j == 0)
        def _():
            hv = h_ref[...]
            n_ref[...] = (hv * _rstd(hv) * g_ref[...]).astype(BF16)
            o_ref[...] = jnp.zeros_like(o_ref)

        for r in range(0, tm, sub):
            rows = slice(r, r + sub)
            n = n_ref[rows, :]
            a = _dot(n, wg_ref[...]).astype(BF16)
            b = _dot(n, wu_ref[...]).astype(BF16)
            a_ref[rows, :] = a
            b_ref[rows, :] = b
            a, b = a.astype(F32), b.astype(F32)
            s = (a * jax.nn.sigmoid(a) * b).astype(BF16)
            o_ref[rows, :] += _dot(s, wd_ref[...])

        @pl.when(j == nb - 1)
        def _():
            o_ref[...] = h_ref[...] + 0.5 * o_ref[...]

    row = BS((tm, D), lambda i, j: (i, 0))
    wcol = BS((None, D, Fs), lambda i, j: (j, 0, 0))
    act = BS((None, tm, Fs), lambda i, j: (j, i, 0))
    ni = S // tm
    return _call(
        body, name=name, args=(h, g, wg, wu, wd),
        out_shape=[SDS((S, D), F32), SDS((S, D), BF16), SDS((nb, S, Fs), BF16), SDS((nb, S, Fs), BF16)],
        grid=(ni, nb),
        in_specs=[row, BS((1, D), lambda i, j: (0, 0)), wcol, wcol, BS((None, Fs, D), lambda i, j: (j, 0, 0))],
        out_specs=[row, row, act, act], sem=("parallel", "arbitrary"), vmem_mib=56, exchange=exchange,
        first=lambda: (pl.program_id(0) == 0) & (pl.program_id(1) == 0),
        last=lambda: (pl.program_id(0) == ni - 1) & (pl.program_id(1) == nb - 1))


def _ffn_bwd_dx(dh, a, b, wg, wu, wd, name, tm=512, exchange=None):
    S, D = dh.shape
    nb, _, Fs = wg.shape
    tm = min(tm, S)
    assert S % tm == 0, (S, tm)

    def body(dh_ref, a_ref, b_ref, wg_ref, wu_ref, wd_ref, dn_ref, df_ref, da_ref, db_ref, s_ref, ds_ref):
        j = pl.program_id(1)

        def first_matmul():
            ds_ref[...] = _dot_nt(df_ref[...], wd_ref[...])

        def pre_activation_cotangents():
            ds = ds_ref[...]
            av, bv = a_ref[...].astype(F32), b_ref[...].astype(F32)
            sig = jax.nn.sigmoid(av)
            sl = av * sig
            da = (ds * bv * (sig * (1.0 + av * (1.0 - sig)))).astype(BF16)
            db = (ds * sl).astype(BF16)
            da_ref[...] = da
            db_ref[...] = db
            s_ref[...] = (sl * bv).astype(BF16)
            return da, db

        def last_matmuls(da, db):
            dn_ref[...] += _dot_nt(da, wg_ref[...]) + _dot_nt(db, wu_ref[...])

        @pl.when(j == 0)
        def _():
            df_ref[...] = (0.5 * dh_ref[...]).astype(BF16)
            dn_ref[...] = jnp.zeros_like(dn_ref)
            first_matmul()

        @pl.when((j > 0) & (j < nb))
        def _():
            da, db = pre_activation_cotangents()
            first_matmul()
            last_matmuls(da, db)

        @pl.when(j == nb)
        def _():
            last_matmuls(*pre_activation_cotangents())

    row = BS((tm, D), lambda i, j: (i, 0))
    prev = BS((None, D, Fs), lambda i, j: (jnp.maximum(j - 1, 0), 0, 0))
    act = BS((None, tm, Fs), lambda i, j: (jnp.maximum(j - 1, 0), i, 0))
    ni = S // tm
    return _call(
        body, name=name, args=(dh, a, b, wg, wu, wd),
        out_shape=[SDS((S, D), F32), SDS((S, D), BF16)] + [SDS((nb, S, Fs), BF16)] * 3,
        grid=(ni, nb + 1),
        in_specs=[row, act, act, prev, prev, BS((None, Fs, D), lambda i, j: (jnp.minimum(j, nb - 1), 0, 0))],
        out_specs=[row, row, act, act, act], scratch=[pltpu.VMEM((tm, Fs), F32)],
        sem=("parallel", "arbitrary"), vmem_mib=56, exchange=exchange,
        first=lambda: (pl.program_id(0) == 0) & (pl.program_id(1) == 0),
        last=lambda: (pl.program_id(0) == ni - 1) & (pl.program_id(1) == nb))


DW_FLIPS = {0: (6, 2, 4, 0), 1: (6, 4, 2, 0)}
N_CHIPS = N_DEV // 2


def _dw_block_order():
    x, y, c, me = _mesh_pos()
    steps = [jnp.array([v for mine, sib in zip(DW_FLIPS[core], DW_FLIPS[1 - core]) for v in (sib ^ 1, mine)], jnp.int32)
             for core in (0, 1)]
    return jnp.bitwise_xor(me.astype(jnp.int32), jnp.where(c == 0, steps[0], steps[1]))


def _ffn_bwd_dw(pairs, order, name, tk=512, exchange=None):
    npair = len(pairs)
    S = pairs[0][0].shape[-2]
    nb, half = N_DEV, N_DEV // 2
    tk = min(tk, S)
    assert S % tk == 0, (S, tk)
    nk = S // tk
    shapes = [(lhs.shape[-1], rhs.shape[-1]) for lhs, rhs in pairs]
    flat = []
    for a in (a for pair in pairs for a in pair):
        if not any(a is f for f in flat):
            flat.append(a)
    which = [tuple(next(i for i, f in enumerate(flat) if f is a) for a in pair) for pair in pairs]
    xi = len(exchange.ins) if exchange else 0
    xo = len(exchange.out_shapes) if exchange else 0

    def body(order_ref, *rest):
        tiles, rest = rest[:len(flat)], rest[len(flat):]
        x_in, rest = rest[:xi], rest[xi:]
        recv_bufs, rest = rest[:npair], rest[npair:]
        x_out, rest = rest[:xo], rest[xo:]
        accs, rest = rest[:npair], rest[npair:]
        out_t, rest = rest[:npair], rest[npair:]
        out_m, rest = rest[:npair], rest[npair:]
        land, rest = rest[:npair], rest[npair:]
        (send_t, recv_t, send_m, recv_m, lsem, credit), x_sem = rest[:6], rest[6:]
        t, k = pl.program_id(0), pl.program_id(1)
        x, y, c, me = _mesh_pos()
        sibling = (x, y, 1 - c)
        chip = 2 * x + y

        if exchange:
            @pl.when((t == 0) & (k == 0))
            def _():
                for cp in exchange.copies(x_in, x_out, x_sem):
                    cp.start()

        @pl.when(k == 0)
        def _():
            for acc in accs:
                acc[...] = jnp.zeros_like(acc)

        for w in range(npair):
            accs[w][...] += _dot_tn(tiles[which[w][0]][...], tiles[which[w][1]][...])

        def to_sibling(w, i):
            return _remote(out_t[w], land[w], send_t.at[w, i], recv_t.at[w, i], sibling)

        def to_owner(w, i):
            dst = recv_bufs[w].at[chip]
            if i == half - 1:
                return pltpu.make_async_copy(out_m[w], dst, lsem.at[w])
            p = jnp.bitwise_xor(me, jnp.where(c == 0, DW_FLIPS[0][i], DW_FLIPS[1][i]))
            return _remote(out_m[w], dst, send_m.at[w, i], recv_m.at[w, i], (p >> 2, (p >> 1) & 1, p & 1))

        for i in range(half):
            @pl.when((t == 2 * i) & (k == nk - 1))
            def _(i=i):
                if i >= 1:
                    pl.semaphore_wait(credit, 1)
                for w in range(npair):
                    if i >= 1:
                        to_sibling(w, i - 1).wait_send()
                    out_t[w][...] = accs[w][...].astype(BF16)
                    to_sibling(w, i).start()

            @pl.when((t == 2 * i + 1) & (k == nk - 1))
            def _(i=i):
                for w in range(npair):
                    to_sibling(w, i).wait_recv()
                    if i >= 1:
                        to_owner(w, i - 1).wait_send()
                    out_m[w][...] = (accs[w][...] + land[w][...].astype(F32)).astype(BF16)
                if i < half - 1:
                    pl.semaphore_signal(credit, inc=1, device_id=sibling, device_id_type=pl.DeviceIdType.MESH)
                for w in range(npair):
                    to_owner(w, i).start()

        @pl.when((t == nb - 1) & (k == nk - 1))
        def _():
            for w in range(npair):
                to_sibling(w, half - 1).wait_send()
                to_owner(w, half - 1).wait()
                for i in range(half - 1):
                    to_owner(w, i).wait_recv()
            if exchange:
                for cp in exchange.copies(x_in, x_out, x_sem):
                    cp.wait()

    def tile_spec(arr):
        if arr.ndim == 3:
            return BS((None, tk, arr.shape[-1]), lambda t, k, o: (o[t], k, 0))
        return BS((tk, arr.shape[-1]), lambda t, k, o: (k, 0))

    grid_spec = pltpu.PrefetchScalarGridSpec(
        num_scalar_prefetch=1, grid=(nb, nk),
        in_specs=[tile_spec(a) for a in flat] + [ANY] * xi, out_specs=[ANY] * (npair + xo),
        scratch_shapes=[pltpu.VMEM(s, F32) for s in shapes] + [pltpu.VMEM(s, BF16) for s in shapes] * 3
        + [DMA_SEM((npair, half)), DMA_SEM((npair, half)), DMA_SEM((npair, half - 1)), DMA_SEM((npair, half - 1)),
           DMA_SEM((npair,)), pltpu.SemaphoreType.REGULAR]
        + (list(exchange.sems) if exchange else []))
    outs = pl.pallas_call(
        body, name=name, grid_spec=grid_spec,
        out_shape=[SDS((N_CHIPS,) + s, BF16) for s in shapes] + (list(exchange.out_shapes) if exchange else []),
        compiler_params=pltpu.CompilerParams(dimension_semantics=("arbitrary", "arbitrary"),
                                             vmem_limit_bytes=54 * MIB, has_side_effects=True),
    )(order, *flat, *(exchange.ins if exchange else ()))
    return outs[:npair], outs[npair:]


def _sgu_parts(z, gv, ws_ref, bst_ref, groups):
    da = z.shape[1] // 2
    zu, zv = z[:, :da], z[:, da:]
    u, v = _gelu(zu), _gelu(zv)
    rv = _rstd(v)
    vhat = v * rv
    vn = (vhat * gv).astype(BF16)
    tri = lax.broadcasted_iota(jnp.int32, (CHUNK, CHUNK), 0) >= lax.broadcasted_iota(jnp.int32, (CHUNK, CHUNK), 1)
    pieces = []
    for g in range(groups):
        w = jnp.where(tri, ws_ref[g], 0.0).astype(BF16)
        pieces.append(_dot(w, vn[:, g * CHUNK:(g + 1) * CHUNK]) + bst_ref[:, g:g + 1])
    sv = jnp.concatenate(pieces, axis=1)
    return dict(zu=zu, zv=zv, u=u, rv=rv, vhat=vhat, vn=vn, sv=sv, tri=tri)


SGU_CHUNKS = 4


def _sgu_fwd(z, g_v, w_s, b_st, g_a, width, name):
    S = z.shape[0]
    groups = w_s.shape[0]
    da = groups * CHUNK
    rows = SGU_CHUNKS * CHUNK
    assert S % rows == 0, (S, rows)

    def body(z_ref, gv_ref, ws_ref, bst_ref, ga_ref, o_ref):
        for ch in range(SGU_CHUNKS):
            blk = slice(ch * CHUNK, (ch + 1) * CHUNK)
            p = _sgu_parts(z_ref[blk, :], gv_ref[...], ws_ref, bst_ref, groups)
            ya = p["u"] * p["sv"]
            o_ref[blk, :] = (ya * _rstd(ya) * ga_ref[...]).astype(BF16)

    vec = BS((1, da), lambda i: (0, 0))
    return pl.pallas_call(
        body, name=name, out_shape=SDS((S, width), BF16), grid=(S // rows,),
        in_specs=[BS((rows, 2 * da), lambda i: (i, 0)), vec, BS((groups, CHUNK, CHUNK), lambda i: (0, 0, 0)),
                  BS((CHUNK, groups), lambda i: (0, 0)), vec],
        out_specs=BS((rows, da), lambda i: (i, 0)), compiler_params=_params(("parallel",), 32),
    )(z, g_v, w_s, b_st, g_a)


def _sgu_bwd(z, dy, g_v, w_s, w_st, b_st, g_a, width, name):
    S = z.shape[0]
    groups = w_s.shape[0]
    da = groups * CHUNK
    rows = SGU_CHUNKS * CHUNK
    assert S % rows == 0, (S, rows)

    def body(z_ref, dy_ref, gv_ref, ws_ref, wst_ref, bst_ref, ga_ref, dz_ref, dws_ref, dbst_ref, dgv_ref, dga_ref):
        @pl.when(pl.program_id(0) == 0)
        def _():
            dws_ref[...] = jnp.zeros_like(dws_ref)
            dbst_ref[...] = jnp.zeros_like(dbst_ref)
            dgv_ref[...] = jnp.zeros_like(dgv_ref)
            dga_ref[...] = jnp.zeros_like(dga_ref)

        gv = gv_ref[...]
        tri_t = (lax.broadcasted_iota(jnp.int32, (CHUNK, CHUNK), 0)
                 <= lax.broadcasted_iota(jnp.int32, (CHUNK, CHUNK), 1))
        lane = lax.broadcasted_iota(jnp.int32, (CHUNK, groups), 1)
        for ch in range(SGU_CHUNKS):
            blk = slice(ch * CHUNK, (ch + 1) * CHUNK)
            p = _sgu_parts(z_ref[blk, :], gv, ws_ref, bst_ref, groups)
            u, sv, tri = p["u"], p["sv"], p["tri"]
            ya = u * sv
            ra = _rstd(ya)
            yhat = ya * ra
            d = dy_ref[blk, :]
            dga_ref[...] += jnp.sum(d * yhat, axis=0, keepdims=True)
            dya = _norm_bwd(yhat, ra, ga_ref[...], d)
            du = dya * sv
            dsv = dya * u
            dsv_b = dsv.astype(BF16)
            dvn = []
            dbs = jnp.zeros((CHUNK, groups), F32)
            for g in range(groups):
                cols = slice(g * CHUNK, (g + 1) * CHUNK)
                dbs = dbs + jnp.where(lane == g, jnp.sum(dsv[:, cols], axis=1, keepdims=True), 0.0)
                dws_ref[g] += jnp.where(tri, _dot_nt(dsv_b[:, cols], p["vn"][:, cols]), 0.0)
                wt = jnp.where(tri_t, wst_ref[g], 0.0).astype(BF16)
                dvn.append(_dot(wt, dsv_b[:, cols]))
            dbst_ref[...] += dbs
            dvn = jnp.concatenate(dvn, axis=1)
            dgv_ref[...] += jnp.sum(dvn * p["vhat"], axis=0, keepdims=True)
            dv = _norm_bwd(p["vhat"], p["rv"], gv, dvn)
            dz_ref[blk, :] = jnp.concatenate([du * _gelu_grad(p["zu"]), dv * _gelu_grad(p["zv"])], axis=1)

    vec = BS((1, da), lambda i: (0, 0))
    wsq = BS((groups, CHUNK, CHUNK), lambda i: (0, 0, 0))
    bsq = BS((CHUNK, groups), lambda i: (0, 0))
    return pl.pallas_call(
        body, name=name,
        out_shape=[SDS((S, width), F32), SDS((groups, CHUNK, CHUNK), F32), SDS((CHUNK, groups), F32),
                   SDS((1, da), F32), SDS((1, da), F32)],
        grid=(S // rows,),
        in_specs=[BS((rows, 2 * da), lambda i: (i, 0)), BS((rows, da), lambda i: (i, 0)), vec, wsq, wsq, bsq, vec],
        out_specs=[BS((rows, 2 * da), lambda i: (i, 0)), wsq, bsq, vec, vec],
        compiler_params=_params(("arbitrary",), 32),
    )(z, dy, g_v, w_s, w_st, b_st, g_a)


def _swa_mask(i, group):
    row = lax.broadcasted_iota(jnp.int32, (group * CHUNK, 2 * CHUNK), 0) & (CHUNK - 1)
    col = lax.broadcasted_iota(jnp.int32, (group * CHUNK, 2 * CHUNK), 1)
    d = row + CHUNK - col
    return (d >= 0) & (d < CHUNK) & jnp.logical_or(i > 0, col >= CHUNK)


def _stack_heads(t, g, group):
    return jnp.concatenate([t[:, h * HEAD_DIM:(h + 1) * HEAD_DIM] for h in range(g * group, (g + 1) * group)], axis=0)


def _unstack_heads(stacked, group):
    return [stacked[h * CHUNK:(h + 1) * CHUNK] for h in range(group)]


def _swa_probs(qh, kh, sink, mask):
    s = jnp.where(mask, _dot_nt(qh, kh) * (HEAD_DIM ** -0.5), NEG)
    m = jnp.maximum(jnp.max(s, axis=-1, keepdims=True), sink)
    e = jnp.exp(s - m)
    es = jnp.exp(sink - m)
    inv = 1.0 / (jnp.sum(e, axis=-1, keepdims=True) + es)
    return e * inv, es * inv


SWA_QBLOCKS = 2


def _swa_specs(db, npair, clamp):
    kvw = 2 * KV_HEADS * HEAD_DIM
    cur = (lambda p: jnp.minimum(p, npair - 1)) if clamp else (lambda p: p)
    rows = SWA_QBLOCKS * CHUNK
    q_spec = BS((rows, db), lambda p: (cur(p), 2))
    kc_spec = BS((rows, kvw), lambda p: (cur(p), 3 * db // kvw))
    kp_spec = BS((CHUNK, kvw), lambda p: (jnp.maximum(SWA_QBLOCKS * cur(p) - 1, 0), 3 * db // kvw))
    return q_spec, kc_spec, kp_spec


def _swa_keys(kp_ref, kc_ref, qb):
    if qb == 0:
        return jnp.concatenate([kp_ref[...], kc_ref[:CHUNK, :]], axis=0).astype(BF16)
    return kc_ref[(qb - 1) * CHUNK:(qb + 1) * CHUNK, :].astype(BF16)


def _swa_fwd(z, sink_rows, g_b, y_buf, name, exchange=None):
    S = z.shape[0]
    db = g_b.shape[1]
    heads = db // HEAD_DIM
    group = heads // KV_HEADS
    rows = SWA_QBLOCKS * CHUNK
    assert S % rows == 0, (S, rows)
    npair = S // rows

    def body(q_ref, kc_ref, kp_ref, sk_ref, gb_ref, _, yb_ref, ybn_ref):
        for qb in range(SWA_QBLOCKS):
            blk = slice(qb * CHUNK, (qb + 1) * CHUNK)
            mask = _swa_mask(SWA_QBLOCKS * pl.program_id(0) + qb, group)
            q = q_ref[blk, :].astype(BF16)
            kv = _swa_keys(kp_ref, kc_ref, qb)
            outs = []
            for g in range(KV_HEADS):
                kg = kv[:, g * HEAD_DIM:(g + 1) * HEAD_DIM]
                vg = kv[:, (KV_HEADS + g) * HEAD_DIM:(KV_HEADS + g + 1) * HEAD_DIM]
                srows = slice(g * group * CHUNK, (g + 1) * group * CHUNK)
                p, _ = _swa_probs(_stack_heads(q, g, group), kg, sk_ref[srows, :], mask)
                outs += _unstack_heads(_dot(p.astype(BF16), vg), group)
            yb = jnp.concatenate(outs, axis=1)
            yb_ref[blk, :] = yb
            ybn_ref[blk, :] = (yb * _rstd(yb) * gb_ref[...]).astype(BF16)

    q_spec, kc_spec, kp_spec = _swa_specs(db, npair, False)
    assert y_buf.shape[1] == 2 * db, y_buf.shape
    return _call(
        body, name=name, args=(z, z, z, sink_rows, g_b, y_buf),
        out_shape=[SDS((S, db), F32), SDS(y_buf.shape, BF16)], grid=(npair,),
        in_specs=[q_spec, kc_spec, kp_spec, BS((heads * CHUNK, 1), lambda p: (0, 0)), BS((1, db), lambda p: (0, 0)),
                  ANY],
        out_specs=[BS((rows, db), lambda p: (p, 0)), BS((rows, db), lambda p: (p, 1))], aliases={5: 1},
        sem=("parallel",), vmem_mib=32, exchange=exchange,
        first=lambda: pl.program_id(0) == 0, last=lambda: pl.program_id(0) == npair - 1)


def _swa_bwd(z, yb, dy, sink_rows, g_b, dz_buf, name):
    S = z.shape[0]
    db = g_b.shape[1]
    heads = db // HEAD_DIM
    group = heads // KV_HEADS
    rows = SWA_QBLOCKS * CHUNK
    assert SWA_QBLOCKS == 2 and S % rows == 0, (S, rows)
    npair = S // rows
    kvw = 2 * KV_HEADS * HEAD_DIM

    def body(q_ref, kc_ref, kp_ref, yb_ref, dy_ref, sk_ref, gb_ref, _, dq_ref, dkv_ref, dsk_ref, dgb_ref,
             done_ref, part_ref):
        p = pl.program_id(0)

        @pl.when(p == 0)
        def _():
            done_ref[...] = jnp.zeros_like(done_ref)
            part_ref[...] = jnp.zeros_like(part_ref)
            dsk_ref[...] = jnp.zeros_like(dsk_ref)
            dgb_ref[...] = jnp.zeros_like(dgb_ref)

        @pl.when(p < npair)
        def _():
            lane = lax.broadcasted_iota(jnp.int32, (1, heads), 1)
            dsinks = jnp.zeros((1, heads), F32)
            dgb = jnp.zeros((1, db), F32)
            contribs = []
            for qb in range(SWA_QBLOCKS):
                blk = slice(qb * CHUNK, (qb + 1) * CHUNK)
                mask = _swa_mask(SWA_QBLOCKS * p + qb, group)
                ybv = yb_ref[blk, :]
                rb = _rstd(ybv)
                yhat = ybv * rb
                d = dy_ref[blk, :]
                dgb = dgb + jnp.sum(d * yhat, axis=0, keepdims=True)
                do = _norm_bwd(yhat, rb, gb_ref[...], d).astype(BF16)
                q = q_ref[blk, :].astype(BF16)
                kv = _swa_keys(kp_ref, kc_ref, qb)
                dqs, dk, dv = [], [], []
                for g in range(KV_HEADS):
                    kg = kv[:, g * HEAD_DIM:(g + 1) * HEAD_DIM]
                    vg = kv[:, (KV_HEADS + g) * HEAD_DIM:(KV_HEADS + g + 1) * HEAD_DIM]
                    srows = slice(g * group * CHUNK, (g + 1) * group * CHUNK)
                    qg, dog = _stack_heads(q, g, group), _stack_heads(do, g, group)
                    pr, ps = _swa_probs(qg, kg, sk_ref[srows, :], mask)
                    dp = _dot_nt(dog, vg)
                    dr = jnp.sum(pr * dp, axis=-1, keepdims=True)
                    ds = (pr * (dp - dr) * (HEAD_DIM ** -0.5)).astype(BF16)
                    for h, t in enumerate(_unstack_heads(ps * dr, group)):
                        dsinks = dsinks - jnp.where(lane == g * group + h, jnp.sum(t, axis=0, keepdims=True), 0.0)
                    dqs += _unstack_heads(_dot(ds, kg), group)
                    dk.append(_dot_tn(ds, qg))
                    dv.append(_dot_tn(pr.astype(BF16), dog))
                dq_ref[blk, :] = jnp.concatenate(dqs, axis=1)
                contribs.append(jnp.concatenate(dk + dv, axis=1))
            dsk_ref[...] += dsinks
            dgb_ref[...] += dgb
            first, second = contribs
            dkv_ref[:CHUNK, :] = done_ref[...]
            dkv_ref[CHUNK:, :] = part_ref[...] + first[:CHUNK]
            done_ref[...] = first[CHUNK:] + second[:CHUNK]
            part_ref[...] = second[CHUNK:]

        @pl.when(p == npair)
        def _():
            dkv_ref[:CHUNK, :] = done_ref[...]
            dkv_ref[CHUNK:, :] = part_ref[...]

    q_spec, kc_spec, kp_spec = _swa_specs(db, npair, True)
    cur = BS((rows, db), lambda p: (jnp.minimum(p, npair - 1), 0))
    return pl.pallas_call(
        body, name=name,
        out_shape=[SDS(dz_buf.shape, F32), SDS((S, kvw), F32), SDS((1, heads), F32), SDS((1, db), F32)],
        grid=(npair + 1,),
        in_specs=[q_spec, kc_spec, kp_spec, cur, BS((rows, db), lambda p: (jnp.minimum(p, npair - 1), 1)),
                  BS((heads * CHUNK, 1), lambda p: (0, 0)), BS((1, db), lambda p: (0, 0)), ANY],
        out_specs=[BS((rows, db), lambda p: (jnp.minimum(p, npair - 1), 2)),
                   BS((rows, kvw), lambda p: (jnp.maximum(p - 1, 0), 0)), BS((1, heads), lambda p: (0, 0)),
                   BS((1, db), lambda p: (0, 0))],
        scratch_shapes=[pltpu.VMEM((CHUNK, kvw), F32), pltpu.VMEM((CHUNK, kvw), F32)],
        input_output_aliases={7: 0}, compiler_params=_params(("arbitrary",), 32),
    )(z, z, z, yb, dy, sink_rows, g_b, dz_buf)


def _paste_columns(dst, src, name, tm=512):
    S, w = src.shape
    tm = min(tm, S)
    assert S % tm == 0 and dst.shape[1] % w == 0, (dst.shape, src.shape, tm)
    last = dst.shape[1] // w - 1

    def body(src_ref, _, out_ref):
        out_ref[...] = src_ref[...].astype(out_ref.dtype)

    return pl.pallas_call(
        body, name=name, out_shape=SDS(dst.shape, dst.dtype), grid=(S // tm,),
        in_specs=[BS((tm, w), lambda i: (i, 0)), ANY], out_specs=BS((tm, w), lambda i: (i, last)),
        input_output_aliases={1: 0}, compiler_params=_params(("parallel",), 32),
    )(src, dst)


def _xattn_probs(qh, kh, hd):
    s = _dot_nt(qh, kh) * (hd ** -0.5)
    e = jnp.exp(s - jnp.max(s, axis=-1, keepdims=True))
    return e / jnp.sum(e, axis=-1, keepdims=True)


def _xattn_fwd(q, kv, name, tq=512):
    S, D = q.shape
    M = kv.shape[0]
    hd = D // X_HEADS
    tq = min(tq, S)
    assert S % tq == 0, (S, tq)

    def body(q_ref, kv_ref, o_ref):
        for h in range(X_HEADS):
            cols = slice(h * hd, (h + 1) * hd)
            p = _xattn_probs(q_ref[:, cols], kv_ref[:, cols], hd)
            o_ref[:, cols] = _dot(p.astype(BF16), kv_ref[:, D + h * hd:D + (h + 1) * hd]).astype(BF16)

    row = BS((tq, D), lambda i: (i, 0))
    return pl.pallas_call(
        body, name=name, out_shape=SDS((S, D), BF16), grid=(S // tq,),
        in_specs=[row, BS((M, 2 * D), lambda i: (0, 0))], out_specs=row, compiler_params=_params(("parallel",), 40),
    )(q, kv)


def _xattn_bwd(q, kv, do, name, tq=512):
    S, D = q.shape
    M = kv.shape[0]
    hd = D // X_HEADS
    tq = min(tq, S)
    assert S % tq == 0, (S, tq)

    def body(q_ref, kv_ref, do_ref, dq_ref, dkv_ref):
        @pl.when(pl.program_id(0) == 0)
        def _():
            dkv_ref[...] = jnp.zeros_like(dkv_ref)

        for h in range(X_HEADS):
            cols = slice(h * hd, (h + 1) * hd)
            vcols = slice(D + h * hd, D + (h + 1) * hd)
            qh, kh, vh, doh = q_ref[:, cols], kv_ref[:, cols], kv_ref[:, vcols], do_ref[:, cols]
            p = _xattn_probs(qh, kh, hd)
            dp = _dot_nt(doh, vh)
            ds = (p * (dp - jnp.sum(p * dp, axis=-1, keepdims=True)) * (hd ** -0.5)).astype(BF16)
            dq_ref[:, cols] = _dot(ds, kh).astype(BF16)
            dkv_ref[:, cols] += _dot_tn(ds, qh)
            dkv_ref[:, vcols] += _dot_tn(p.astype(BF16), doh)

    row = BS((tq, D), lambda i: (i, 0))
    full = BS((M, 2 * D), lambda i: (0, 0))
    return pl.pallas_call(
        body, name=name, out_shape=[SDS((S, D), BF16), SDS((M, 2 * D), F32)], grid=(S // tq,),
        in_specs=[row, full, row], out_specs=[row, full], compiler_params=_params(("arbitrary",), 40),
    )(q, kv, do)


def _row_tile(rows, cap):
    best = 8
    for t in range(8, min(rows, cap) + 1, 8):
        if rows % t == 0:
            best = t
    assert rows % best == 0, (rows, cap)
    return best


ADAM_TILE_ELEMS = 256 * 1024


def _adamw(w, m, v, recv, name):
    R, C = w.shape
    slots = recv.shape[0]
    tr = _row_tile(R, max(8, ADAM_TILE_ELEMS // C))

    def body(w_ref, m_ref, v_ref, r_ref, g_ref, d_ref, nm_ref, nv_ref):
        g = r_ref[0].astype(F32)
        for s in range(1, slots):
            g = g + r_ref[s].astype(F32)
        mn = ADAM_B1 * m_ref[...] + (1.0 - ADAM_B1) * g
        vn = ADAM_B2 * v_ref[...] + (1.0 - ADAM_B2) * jnp.square(g)
        m_hat = mn / (1.0 - ADAM_B1 ** ADAM_STEP)
        v_hat = vn / (1.0 - ADAM_B2 ** ADAM_STEP)
        g_ref[...] = g
        d_ref[...] = -ADAM_LR * (m_hat / (jnp.sqrt(v_hat) + ADAM_EPS) + ADAM_WD * w_ref[...])
        nm_ref[...] = mn
        nv_ref[...] = vn

    row = BS((tr, C), lambda i: (i, 0))
    return pl.pallas_call(
        body, name=name, out_shape=[SDS((R, C), F32)] * 4, grid=(R // tr,),
        in_specs=[row, row, row, BS((slots, tr, C), lambda i: (0, i, 0))], out_specs=[row] * 4,
        compiler_params=_params(("parallel",), 40),
    )(w, m, v, recv)


def _cols_to_blocks(full):
    r, c = full.shape
    return full.reshape(r, N_DEV, c // N_DEV).transpose(1, 0, 2)


def _blocks_to_cols(blocks):
    n, r, c = blocks.shape
    return blocks.transpose(1, 0, 2).reshape(r, n * c)


def _pack(parts):
    flat = jnp.concatenate([p.reshape(-1).astype(F32) for p in parts])
    pad = (-flat.shape[0]) % (128 * 128)
    return jnp.pad(flat, (0, pad)).reshape(-1, 128)


def kernel(x, mem, g_ffn1, w1_gate, w1_up, w1_down, g_mix, w_in, g_v, w_s, b_s, sinks, g_a_out, g_b_out, w_out, g_x, g_mem, w_xq, w_xkv, w_xo, g_ffn2, w2_gate, w2_up, w2_down, g_final, loss_target, m_g_ffn1, m_w1_gate, m_w1_up, m_w1_down, m_g_mix, m_w_in, m_g_v, m_w_s, m_b_s, m_sinks, m_g_a_out, m_g_b_out, m_w_out, m_g_x, m_g_mem, m_w_xq, m_w_xkv, m_w_xo, m_g_ffn2, m_w2_gate, m_w2_up, m_w2_down, m_g_final, v_g_ffn1, v_w1_gate, v_w1_up, v_w1_down, v_g_mix, v_w_in, v_g_v, v_w_s, v_b_s, v_sinks, v_g_a_out, v_g_b_out, v_w_out, v_g_x, v_g_mem, v_w_xq, v_w_xkv, v_w_xo, v_g_ffn2, v_w2_gate, v_w2_up, v_w2_down, v_g_final):
    w = dict(g_ffn1=g_ffn1, w1_gate=w1_gate, w1_up=w1_up, w1_down=w1_down, g_mix=g_mix, w_in=w_in, g_v=g_v, w_s=w_s,
             b_s=b_s, sinks=sinks, g_a_out=g_a_out, g_b_out=g_b_out, w_out=w_out, g_x=g_x, g_mem=g_mem, w_xq=w_xq,
             w_xkv=w_xkv, w_xo=w_xo, g_ffn2=g_ffn2, w2_gate=w2_gate, w2_up=w2_up, w2_down=w2_down, g_final=g_final)
    mom = dict(g_ffn1=m_g_ffn1, w1_gate=m_w1_gate, w1_up=m_w1_up, w1_down=m_w1_down, g_mix=m_g_mix, w_in=m_w_in,
               g_v=m_g_v, w_s=m_w_s, b_s=m_b_s, sinks=m_sinks, g_a_out=m_g_a_out, g_b_out=m_g_b_out, w_out=m_w_out,
               g_x=m_g_x, g_mem=m_g_mem, w_xq=m_w_xq, w_xkv=m_w_xkv, w_xo=m_w_xo, g_ffn2=m_g_ffn2,
               w2_gate=m_w2_gate, w2_up=m_w2_up, w2_down=m_w2_down, g_final=m_g_final)
    var = dict(g_ffn1=v_g_ffn1, w1_gate=v_w1_gate, w1_up=v_w1_up, w1_down=v_w1_down, g_mix=v_g_mix, w_in=v_w_in,
               g_v=v_g_v, w_s=v_w_s, b_s=v_b_s, sinks=v_sinks, g_a_out=v_g_a_out, g_b_out=v_g_b_out, w_out=v_w_out,
               g_x=v_g_x, g_mem=v_g_mem, w_xq=v_w_xq, w_xkv=v_w_xkv, w_xo=v_w_xo, g_ffn2=v_g_ffn2,
               w2_gate=v_w2_gate, w2_up=v_w2_up, w2_down=v_w2_down, g_final=v_g_final)

    xs, ms, tgt = x[0], mem[0], loss_target[0]
    D = xs.shape[1]
    d_a = w_s.shape[1] * CHUNK
    d_b = D - d_a
    kvw = 2 * KV_HEADS * HEAD_DIM

    shard = {k: w[k][0].astype(BF16) for k in BIG}
    wg1, wu1, wd1 = _gather_two_level([shard[k] for k in FFN1_W], "ag_ffn1")
    gf = g_final.reshape(1, D)
    ws, ws_t, bs_t = w_s[0], jnp.swapaxes(w_s[0], 1, 2), b_s[0].T
    sink_rows = jnp.repeat(sinks.reshape(-1), CHUNK).reshape(-1, 1)

    (h1, n1, a1, b1), part = _ffn_fwd(xs, g_ffn1, wg1, wu1, wd1, "ffn1_fwd",
                                      exchange=_gather_exchange([shard[k] for k in REST_W], CHIP_PEERS))
    part = dict(zip(REST_W, part))
    n2, (win_blocks,) = _rms_fwd(h1, g_mix, "mix_norm", exchange=_sibling_exchange([part["w_in"]]))
    win = _blocks_to_cols(win_blocks)
    z = _matmul(n2, win, "nn", F32, "mm_in", tn=win.shape[1] // 2)
    y_a = _sgu_fwd(z, g_v, ws, bs_t, g_a_out, D, "sgu_fwd")
    later = [k for k in REST_W if k != "w_in"]
    (yb, y), handed = _swa_fwd(z, sink_rows, g_b_out, y_a, "swa_fwd",
                                  exchange=_sibling_exchange([part[k] for k in later]))
    gathered = dict(zip(later, handed))
    wg2, wu2, wd2 = gathered["w2_gate"], gathered["w2_up"], gathered["w2_down"]
    wxkv = _blocks_to_cols(gathered["w_xkv"])
    wout = gathered["w_out"].reshape(D, D)
    wxq = gathered["w_xq"].reshape(D, D)
    wxo = gathered["w_xo"].reshape(D, D)
    h2 = _matmul(y, wout, "nn", F32, "mm_out", res=h1, tn=D)
    hx, _ = _rms_fwd(h2, g_x, "x_norm")
    mn, _ = _rms_fwd(ms, g_mem, "mem_norm")
    q = _matmul(hx, wxq, "nn", BF16, "mm_xq", tn=D)
    kv = _matmul(mn, wxkv, "nn", BF16, "mm_xkv")
    o = _xattn_fwd(q, kv, "xattn_fwd")
    h3 = _matmul(o, wxo, "nn", F32, "mm_xo", res=h2, tn=D)
    (h4, n4, a2, b2), _ = _ffn_fwd(h3, g_ffn2, wg2, wu2, wd2, "ffn2_fwd")
    dh4, dg_final, loss_part = _final_loss(h4, gf, tgt, "final_loss")

    grad_big, grad_small, recv_big = {}, {"g_final": dg_final}, {}
    order = _dw_block_order()
    (dn4, df2, da2, db2, s2), _ = _ffn_bwd_dx(dh4, a2, b2, wg2, wu2, wd2, "ffn2_bwd_dx")
    dh3, grad_small["g_ffn2"] = _rms_bwd(h3, g_ffn2, dn4, dh4, "ffn2_norm_bwd")
    (recv_big["w2_gate"], recv_big["w2_up"]), _ = _ffn_bwd_dw(
        [(da2, n4), (db2, n4)], order, "ffn2_bwd_dw_gu", tk=1024)
    (recv_big["w2_down"],), _ = _ffn_bwd_dw([(s2, df2)], order, "ffn2_bwd_dw_d", tk=1024)

    do = _matmul(dh3, wxo, "nt", BF16, "mm_xo_dx", tn=D)
    grad_big["w_xo"] = _matmul(o, dh3, "tn", BF16, "mm_xo_dw", tm=D, tn=1024).reshape(N_DEV, D // N_DEV, D)
    dq, dkv = _xattn_bwd(q, kv, do, "xattn_bwd")
    dh2, grad_small["g_x"] = _matmul_norm_bwd(dq, wxq, h2, g_x, dh3, "mm_xq_dx_norm_bwd")
    grad_big["w_xq"] = _matmul(hx, dq, "tn", BF16, "mm_xq_dw", tm=D, tn=1024).reshape(N_DEV, D // N_DEV, D)
    dmn = _matmul(dkv, wxkv, "nt", F32, "mm_xkv_dx")
    grad_big["w_xkv"] = _cols_to_blocks(_matmul(mn, dkv, "tn", BF16, "mm_xkv_dw", tm=1024, tn=1024))
    _, grad_small["g_mem"] = _rms_bwd(ms, g_mem, dmn, None, "mem_norm_bwd")

    dy = _matmul(dh2, wout, "nt", F32, "mm_out_dx", tn=D)
    grad_big["w_out"] = _matmul(y, dh2, "tn", BF16, "mm_out_dw", tm=D, tn=1024).reshape(N_DEV, D // N_DEV, D)
    dz_a, grad_small["w_s"], dbs_t, grad_small["g_v"], grad_small["g_a_out"] = _sgu_bwd(
        z, dy, g_v, ws, ws_t, bs_t, g_a_out, z.shape[1], "sgu_bwd")
    grad_small["b_s"] = dbs_t.T
    dz_q, dkv_b, grad_small["sinks"], grad_small["g_b_out"] = _swa_bwd(
        z, yb, dy, sink_rows, g_b_out, dz_a, "swa_bwd")
    dz = _paste_columns(dz_q, dkv_b, "dz_kv")
    dn2 = _matmul(dz, win, "nt", BF16, "mm_in_dx", tn=1024)
    grad_big["w_in"] = _matmul(dz, n2, "tn", BF16, "mm_in_dw", tm=dz.shape[1] // 2, tn=D).reshape(
        N_DEV, dz.shape[1] // N_DEV, D)
    dh1, grad_small["g_mix"] = _rms_bwd(h1, g_mix, dn2, dh2, "mix_norm_bwd")

    (dn1, df1, da1, db1, s1), recv_mid = _ffn_bwd_dx(
        dh1, a1, b1, wg1, wu1, wd1, "ffn1_bwd_dx", exchange=_scatter_exchange([grad_big[k] for k in MID_W]))
    recv_big.update(zip(MID_W, recv_mid))
    dx, grad_small["g_ffn1"] = _rms_bwd(xs, g_ffn1, dn1, dh1, "ffn1_norm_bwd")
    (recv_big["w1_gate"], recv_big["w1_up"]), (recv_small,) = _ffn_bwd_dw(
        [(da1, n1), (db1, n1)], order, "ffn1_bwd_dw_gu", tk=1024,
        exchange=_gather_exchange([_pack([grad_small[k] for k in SMALL])], ALL_PEERS))
    (recv_big["w1_down"],), _ = _ffn_bwd_dw([(s1, df1)], order, "ffn1_bwd_dw_d", tk=1024)


    grads, deltas, new_m, new_v = {}, {}, {}, {}
    for k in BIG:
        shp = w[k].shape
        if k in TRANSPOSED_W:
            def view(t):
                return jnp.swapaxes(t, 1, 2)[0]
            outs = _adamw(view(w[k]), view(mom[k]), view(var[k]), recv_big[k], "adamw_" + k)
            grads[k], deltas[k], new_m[k], new_v[k] = [jnp.swapaxes(t[None], 1, 2) for t in outs]
        else:
            two_d = shp[1:]
            outs = _adamw(w[k].reshape(two_d), mom[k].reshape(two_d), var[k].reshape(two_d), recv_big[k],
                          "adamw_" + k)
            grads[k], deltas[k], new_m[k], new_v[k] = [t.reshape(shp) for t in outs]
    packed = _adamw(_pack([w[k] for k in SMALL]), _pack([mom[k] for k in SMALL]), _pack([var[k] for k in SMALL]),
                    recv_small, "adamw_small")
    off = 0
    for k in SMALL:
        shp = w[k].shape
        size = 1
        for s in shp:
            size *= s
        for dst, src in zip((grads, deltas, new_m, new_v), packed):
            dst[k] = src.reshape(-1)[off:off + size].reshape(shp)
        off += size

    loss = lax.psum(loss_part[0, 0], AXES)
    return (loss, dx[None], *[grads[k] for k in WEIGHTS], *[deltas[k] for k in WEIGHTS],
            *[new_m[k] for k in WEIGHTS], *[new_v[k] for k in WEIGHTS])
```
